```python
import math
import jax
import jax.numpy as jnp
from jax import lax
import numpy as np

D_MODEL = 1024
BATCH = 4
SEQ = 8192
DEPTH = 4

GRID_W = 64
CTX_LEN = 256
N_MIXERS = 3
Q_BLOCK = 128
ROPE_THETA = 10000.0
NORM_EPS = 1e-6

A_HEAD_DIM = 64
A_HEADS = D_MODEL // (2 * A_HEAD_DIM)
B_HEADS = D_MODEL // 64
B_NOPE = 64
B_ROPE = 32
B_VDIM = 64
B_Q_LORA = D_MODEL // 4
B_KV_LORA = D_MODEL // 4
C_HEAD_DIM = 64
C_HEADS = D_MODEL // C_HEAD_DIM
NA_ROWS_MAX = 8
NA_COLS = 16
FFN_HIDDEN = ((8 * D_MODEL + 3 * 256 - 1) // (3 * 256)) * 256

kernel_name = 'hybrid_diff_mla_natten_dit_block'


def rmsnorm(x, g):
    xf = x.astype(jnp.float32)
    y = xf * lax.rsqrt(jnp.mean(xf * xf, axis=-1, keepdims=True) + NORM_EPS)
    return (y * g.astype(jnp.float32)).astype(x.dtype)


def swiglu(h, w_gu, w_down):
    g, u = jnp.split(h @ w_gu, 2, axis=-1)
    return (jax.nn.silu(g) * u) @ w_down


def lambda_init(layer_idx):
    return 0.8 - 0.6 * math.exp(-0.3 * layer_idx)


def axial_rope_tables(n_tokens, rot_dim):
    n_freq = rot_dim // 4
    inv = ROPE_THETA ** (-jnp.arange(n_freq, dtype=jnp.float32) / n_freq)
    t = jnp.arange(n_tokens, dtype=jnp.int32)
    row = (t // GRID_W).astype(jnp.float32)
    col = (t % GRID_W).astype(jnp.float32)
    ang = jnp.concatenate([row[:, None] * inv, col[:, None] * inv], axis=-1)
    return jnp.cos(ang), jnp.sin(ang)


def apply_rope(t, cos, sin):
    half = t.shape[-1] // 2
    tf = t.astype(jnp.float32)
    t1, t2 = tf[..., :half], tf[..., half:]
    cs, sn = cos[None, :, None, :], sin[None, :, None, :]
    return jnp.concatenate([t1 * cs - t2 * sn, t1 * sn + t2 * cs], axis=-1).astype(t.dtype)


def sweep_query_blocks(fn, qs):
    Bn, S = qs[0].shape[:2]
    nb = S // Q_BLOCK
    qb = tuple(q.reshape((Bn, nb, Q_BLOCK) + q.shape[2:]).swapaxes(0, 1) for q in qs)
    out = lax.map(lambda a: fn(*a), qb)
    return out.swapaxes(0, 1).reshape((Bn, S) + out.shape[3:])


def diff_attention_mixer(h_lat, h_ctx, w_qkv, w_o, lam_vecs, g_sub, lam_init, cos, sin, with_ctx_out):
    H, d = A_HEADS, A_HEAD_DIM
    scale = d ** -0.5

    def project(h):
        Bn, T, _ = h.shape
        q, k, v = jnp.split(h @ w_qkv, 3, axis=-1)
        q = q.reshape(Bn, T, H, 2, d)
        k = k.reshape(Bn, T, H, 2, d)
        v = v.reshape(Bn, T, H, 2 * d)
        return q[:, :, :, 0], q[:, :, :, 1], k[:, :, :, 0], k[:, :, :, 1], v

    lf = lam_vecs.astype(jnp.float32)
    lam = jnp.exp(jnp.sum(lf[0] * lf[1])) - jnp.exp(jnp.sum(lf[2] * lf[3])) + lam_init

    q1l, q2l, k1l, k2l, vl = project(h_lat)
    q1l, q2l, k1l, k2l = (apply_rope(t, cos, sin) for t in (q1l, q2l, k1l, k2l))
    q1c, q2c, k1c, k2c, vc = project(h_ctx)
    k1 = jnp.concatenate([k1c, k1l], axis=1)
    k2 = jnp.concatenate([k2c, k2l], axis=1)
    v = jnp.concatenate([vc, vl], axis=1)

    def attend(q1, q2, ka, kb, vv):
        s1 = jnp.einsum('bqhd,bkhd->bhqk', q1, ka).astype(jnp.float32) * scale
        s2 = jnp.einsum('bqhd,bkhd->bhqk', q2, kb).astype(jnp.float32) * scale
        p = jax.nn.softmax(s1, axis=-1) - lam * jax.nn.softmax(s2, axis=-1)
        return jnp.einsum('bhqk,bkhe->bqhe', p.astype(vv.dtype), vv)

    def finish(o):
        o = rmsnorm(o, g_sub) * (1.0 - lam_init)
        return o.reshape(o.shape[:2] + (H * 2 * d,)) @ w_o

    o_lat = sweep_query_blocks(lambda a, b: attend(a, b, k1, k2, v), (q1l, q2l))
    y_lat = finish(o_lat)
    y_ctx = finish(attend(q1c, q2c, k1c, k2c, vc)) if with_ctx_out else None
    return y_lat, y_ctx


def mla_mixer(h_lat, h_ctx, w_in, g_q, g_kv, w_uq, w_ukv, w_o, cos, sin, with_ctx_out):
    H = B_HEADS
    scale = (B_NOPE + B_ROPE) ** -0.5

    def project(h):
        Bn, T, _ = h.shape
        z = h @ w_in
        cq = rmsnorm(z[..., :B_Q_LORA], g_q)
        ckv = rmsnorm(z[..., B_Q_LORA:B_Q_LORA + B_KV_LORA], g_kv)
        k_rope = z[..., B_Q_LORA + B_KV_LORA:]
        q = (cq @ w_uq).reshape(Bn, T, H, B_NOPE + B_ROPE)
        kv = (ckv @ w_ukv).reshape(Bn, T, H, B_NOPE + B_VDIM)
        return q[..., :B_NOPE], q[..., B_NOPE:], kv[..., :B_NOPE], k_rope, kv[..., B_NOPE:]

    qn_l, qr_l, kn_l, kr_l, v_l = project(h_lat)
    qr_l = apply_rope(qr_l, cos, sin)
    kr_l = apply_rope(kr_l[:, :, None, :], cos, sin)[:, :, 0]
    qn_c, qr_c, kn_c, kr_c, v_c = project(h_ctx)
    kn = jnp.concatenate([kn_c, kn_l], axis=1)
    kr = jnp.concatenate([kr_c, kr_l], axis=1)
    v = jnp.concatenate([v_c, v_l], axis=1)

    def attend(qn, qr, kn_, kr_, vv):
        s = (jnp.einsum('bqhd,bkhd->bhqk', qn, kn_) + jnp.einsum('bqhr,bkr->bhqk', qr, kr_))
        p = jax.nn.softmax(s.astype(jnp.float32) * scale, axis=-1)
        return jnp.einsum('bhqk,bkhd->bqhd', p.astype(vv.dtype), vv)

    def finish(o):
        return o.reshape(o.shape[:2] + (H * B_VDIM,)) @ w_o

    o_lat = sweep_query_blocks(lambda a, b: attend(a, b, kn, kr, v), (qn_l, qr_l))
    y_lat = finish(o_lat)
    y_ctx = finish(attend(qn_c, qr_c, kn_c, kr_c, v_c)) if with_ctx_out else None
    return y_lat, y_ctx


def neighbourhood_mixer(h_lat, h_ctx, w_qkv, rpb, w_o, rows, with_ctx_out):
    H, d = C_HEADS, C_HEAD_DIM
    scale = d ** -0.5
    kr_win = min(NA_ROWS_MAX, rows)
    kc_win = NA_COLS

    def project(h):
        Bn, T, _ = h.shape
        q, k, v = jnp.split(h @ w_qkv, 3, axis=-1)
        return q.reshape(Bn, T, H, d), k.reshape(Bn, T, H, d), v.reshape(Bn, T, H, d)

    q_l, k_l, v_l = project(h_lat)
    q_c, k_c, v_c = project(h_ctx)
    Bn, S = q_l.shape[:2]
    L = k_c.shape[1]
    q_grid = q_l.reshape(Bn, rows, GRID_W, H, d)
    k_grid = k_l.reshape(Bn, rows, GRID_W, H, d)
    v_grid = v_l.reshape(Bn, rows, GRID_W, H, d)

    qcol = np.arange(GRID_W, dtype=np.int32)
    cstart = np.clip(qcol - kc_win // 2, 0, GRID_W - kc_win)
    col_idx = (cstart[:, None] + np.arange(kc_win, dtype=np.int32)[None, :]).astype(np.int32)
    dc_idx = (col_idx - qcol[:, None] + NA_COLS - 1).astype(np.int32)
    rpb_f = rpb.astype(jnp.float32)

    def row_fn(r):
        rstart = jnp.clip(r - kr_win // 2, 0, rows - kr_win)
        q_r = lax.dynamic_index_in_dim(q_grid, r, axis=1, keepdims=False)
        k_band = lax.dynamic_slice_in_dim(k_grid, rstart, kr_win, axis=1)
        v_band = lax.dynamic_slice_in_dim(v_grid, rstart, kr_win, axis=1)
        k_win = k_band[:, :, col_idx]
        v_win = v_band[:, :, col_idx]
        dr_idx = rstart + jnp.arange(kr_win, dtype=jnp.int32) - r + NA_ROWS_MAX - 1
        bias = rpb_f[:, dr_idx[:, None, None], dc_idx[None, :, :]]
        bias = jnp.transpose(bias, (0, 2, 1, 3))
        s_lat = jnp.einsum('bqhd,brqkhd->bhqrk', q_r, k_win).astype(jnp.float32) * scale + bias[None]
        s_ctx = jnp.einsum('bqhd,bkhd->bhqk', q_r, k_c).astype(jnp.float32) * scale
        s = jnp.concatenate([s_ctx, s_lat.reshape(Bn, H, GRID_W, kr_win * kc_win)], axis=-1)
        p = jax.nn.softmax(s, axis=-1).astype(v_win.dtype)
        p_lat = p[..., L:].reshape(Bn, H, GRID_W, kr_win, kc_win)
        return (jnp.einsum('bhqk,bkhd->bqhd', p[..., :L], v_c)
                + jnp.einsum('bhqrk,brqkhd->bqhd', p_lat, v_win))

    o = lax.map(row_fn, jnp.arange(rows, dtype=jnp.int32))
    o_lat = jnp.transpose(o, (1, 0, 2, 3, 4)).reshape(Bn, S, H * d)
    y_lat = o_lat @ w_o
    y_ctx = None
    if with_ctx_out:
        s = jnp.einsum('bqhd,bkhd->bhqk', q_c, k_c).astype(jnp.float32) * scale
        p = jax.nn.softmax(s, axis=-1).astype(v_c.dtype)
        o_c = jnp.einsum('bhqk,bkhd->bqhd', p, v_c)
        y_ctx = o_c.reshape(o_c.shape[:2] + (H * d,)) @ w_o
    return y_lat, y_ctx


def setup_inputs(seed: int = 0) -> dict:
    key = jax.random.key(seed)
    keys = iter(jax.random.split(key, 64))

    def nrm(shape, scale):
        return jax.random.normal(next(keys), shape, jnp.float32) * scale

    def gain(shape):
        return 1.0 + nrm(shape, 0.05)

    D = D_MODEL
    inp = {}
    inp['x'] = nrm((BATCH, SEQ, D), 1.0)
    inp['c'] = nrm((BATCH, D), 1.0)
    inp['ctx'] = nrm((BATCH, CTX_LEN, D), 1.0)
    inp['c_ctx'] = nrm((D,), 1.0)
    for i in range(DEPTH):
        p = 'l%d_' % i
        inp[p + 'w_mod'] = nrm((D, 6 * D), D ** -0.5)
        inp[p + 'b_mod'] = nrm((6 * D,), 0.02)
        inp[p + 'g_norm'] = gain((4, D))
        inp[p + 'w_gu'] = nrm((D, 2 * FFN_HIDDEN), D ** -0.5)
        inp[p + 'w_down'] = nrm((FFN_HIDDEN, D), FFN_HIDDEN ** -0.5)
        kind = i % N_MIXERS
        if kind == 0:
            wa = 2 * A_HEADS * A_HEAD_DIM
            inp[p + 'a_w_qkv'] = nrm((D, 3 * wa), D ** -0.5)
            inp[p + 'a_w_o'] = nrm((wa, D), wa ** -0.5)
            inp[p + 'a_lam'] = nrm((4, A_HEAD_DIM), 0.1)
            inp[p + 'a_g_sub'] = gain((2 * A_HEAD_DIM,))
        elif kind == 1:
            inp[p + 'b_w_in'] = nrm((D, B_Q_LORA + B_KV_LORA + B_ROPE), D ** -0.5)
            inp[p + 'b_g_q'] = gain((B_Q_LORA,))
            inp[p + 'b_g_kv'] = gain((B_KV_LORA,))
            inp[p + 'b_w_uq'] = nrm((B_Q_LORA, B_HEADS * (B_NOPE + B_ROPE)), B_Q_LORA ** -0.5)
            inp[p + 'b_w_ukv'] = nrm((B_KV_LORA, B_HEADS * (B_NOPE + B_VDIM)), B_KV_LORA ** -0.5)
            inp[p + 'b_w_o'] = nrm((B_HEADS * B_VDIM, D), (B_HEADS * B_VDIM) ** -0.5)
        else:
            wc = C_HEADS * C_HEAD_DIM
            inp[p + 'c_w_qkv'] = nrm((D, 3 * wc), D ** -0.5)
            inp[p + 'c_rpb'] = nrm((C_HEADS, 2 * NA_ROWS_MAX - 1, 2 * NA_COLS - 1), 0.5)
            inp[p + 'c_w_o'] = nrm((wc, D), wc ** -0.5)
    return inp


def reference(x, c, ctx, c_ctx,
              l0_w_mod, l0_b_mod, l0_g_norm, l0_w_gu, l0_w_down,
              l0_a_w_qkv, l0_a_w_o, l0_a_lam, l0_a_g_sub,
              l1_w_mod, l1_b_mod, l1_g_norm, l1_w_gu, l1_w_down,
              l1_b_w_in, l1_b_g_q, l1_b_g_kv, l1_b_w_uq, l1_b_w_ukv, l1_b_w_o,
              l2_w_mod, l2_b_mod, l2_g_norm, l2_w_gu, l2_w_down,
              l2_c_w_qkv, l2_c_rpb, l2_c_w_o,
              l3_w_mod, l3_b_mod, l3_g_norm, l3_w_gu, l3_w_down,
              l3_a_w_qkv, l3_a_w_o, l3_a_lam, l3_a_g_sub):
    S = x.shape[1]
    rows = S // GRID_W
    common = [
        (l0_w_mod, l0_b_mod, l0_g_norm, l0_w_gu, l0_w_down),
        (l1_w_mod, l1_b_mod, l1_g_norm, l1_w_gu, l1_w_down),
        (l2_w_mod, l2_b_mod, l2_g_norm, l2_w_gu, l2_w_down),
        (l3_w_mod, l3_b_mod, l3_g_norm, l3_w_gu, l3_w_down),
    ]
    mixer_params = [
        (l0_a_w_qkv, l0_a_w_o, l0_a_lam, l0_a_g_sub),
        (l1_b_w_in, l1_b_g_q, l1_b_g_kv, l1_b_w_uq, l1_b_w_ukv, l1_b_w_o),
        (l2_c_w_qkv, l2_c_rpb, l2_c_w_o),
        (l3_a_w_qkv, l3_a_w_o, l3_a_lam, l3_a_g_sub),
    ]
    cos_a, sin_a = axial_rope_tables(S, A_HEAD_DIM)
    cos_b, sin_b = axial_rope_tables(S, B_ROPE)

    for i in range(DEPTH):
        w_mod, b_mod, g_norm, w_gu, w_down = common[i]
        last = i == DEPTH - 1
        m_lat = (jax.nn.silu(c) @ w_mod + b_mod)[:, None, :]
        m_ctx = jax.nn.silu(c_ctx) @ w_mod + b_mod
        sh1, sc1, g1, sh2, sc2, g2 = jnp.split(m_lat, 6, axis=-1)
        csh1, csc1, cg1, csh2, csc2, cg2 = jnp.split(m_ctx, 6, axis=-1)

        h_lat = rmsnorm(x, g_norm[0]) * (1.0 + sc1) + sh1
        h_ctx = rmsnorm(ctx, g_norm[0]) * (1.0 + csc1) + csh1
        kind = i % N_MIXERS
        if kind == 0:
            y_lat, y_ctx = diff_attention_mixer(h_lat, h_ctx, *mixer_params[i], lambda_init(i),
                                                cos_a, sin_a, not last)
        elif kind == 1:
            y_lat, y_ctx = mla_mixer(h_lat, h_ctx, *mixer_params[i], cos_b, sin_b, not last)
        else:
            y_lat, y_ctx = neighbourhood_mixer(h_lat, h_ctx, *mixer_params[i], rows, not last)

        x = x + g1 * rmsnorm(y_lat, g_norm[1])
        f_lat = swiglu(rmsnorm(x, g_norm[2]) * (1.0 + sc2) + sh2, w_gu, w_down)
        x = x + g2 * rmsnorm(f_lat, g_norm[3])

        if not last:
            ctx = ctx + cg1 * rmsnorm(y_ctx, g_norm[1])
            f_ctx = swiglu(rmsnorm(ctx, g_norm[2]) * (1.0 + csc2) + csh2, w_gu, w_down)
            ctx = ctx + cg2 * rmsnorm(f_ctx, g_norm[3])
    return x
```

```python
import functools
import math

import numpy as np
import jax
import jax.numpy as jnp
from jax import lax
from jax.experimental import pallas as pl
from jax.experimental.pallas import tpu as pltpu

D_MODEL = 1024
BATCH = 4
SEQ = 8192
DEPTH = 4
GRID_W = 64
CTX_LEN = 256
ROPE_THETA = 10000.0
NORM_EPS = 1e-6

A_HEAD_DIM = 64
A_HEADS = 8
B_HEADS = 16
B_NOPE = 64
B_ROPE = 32
B_VDIM = 64
B_Q_LORA = 256
B_KV_LORA = 256
C_HEAD_DIM = 64
C_HEADS = 16
NA_ROWS = 8
NA_COLS = 16
FFN_HIDDEN = 2816

LANES = 128
ROW_TILE = 256
T_ALL = SEQ + CTX_LEN
N_TILES = T_ALL // ROW_TILE
CTX_TILE = N_TILES - 1
N_PAIRS = 8
LOG2E = math.log2(math.e)
MASK_VALUE = -1e30
VMEM_LIMIT = 60 * 1024 * 1024

F32 = jnp.float32
BF16 = jnp.bfloat16
NT_DIMS = (((1,), (1,)), ((), ()))


def _rms(x, g):
    return x * lax.rsqrt(jnp.mean(x * x, axis=-1, keepdims=True) + NORM_EPS) * g


def _sigmoid(x):
    return 1.0 / (1.0 + jnp.exp(-x))


def _params(n_axes):
    return pltpu.CompilerParams(
        dimension_semantics=("arbitrary",) * n_axes, vmem_limit_bytes=VMEM_LIMIT)


def _resident(shape):
    zeros = (0,) * len(shape)
    return pl.BlockSpec(shape, lambda *_: zeros, pipeline_mode=pl.Buffered(1))


def _mod_row(b, t):
    return jnp.where(t == CTX_TILE, BATCH, b)


def _mod_kernel(c_ref, w_ref, b_ref, o_ref):
    c = c_ref[...]
    a = (c * _sigmoid(c)).astype(BF16)
    o_ref[...] = jnp.dot(a, w_ref[...].astype(BF16), preferred_element_type=F32) + b_ref[...]


def _modulation(cc, w_mod, b_mod):
    n = w_mod.shape[1]
    blk = D_MODEL
    return pl.pallas_call(
        _mod_kernel,
        out_shape=jax.ShapeDtypeStruct((8, n), F32),
        grid=(n // blk,),
        in_specs=[pl.BlockSpec((8, D_MODEL), lambda j: (0, 0)),
                  pl.BlockSpec((D_MODEL, blk), lambda j: (0, j)),
                  pl.BlockSpec((1, blk), lambda j: (0, j))],
        out_specs=pl.BlockSpec((8, blk), lambda j: (0, j)),
        compiler_params=_params(1),
        name="modulation",
    )(cc, w_mod, b_mod.reshape(1, n))


def _rope_lanes(q, c, s, first, half):
    rot = jnp.where(first, pltpu.roll(q, LANES - half, 1), pltpu.roll(q, half, 1))
    return q * c + rot * s


def _proj_qkv_kernel(x_ref, mod_ref, g_ref, wq_ref, wkt_ref, wv_ref,
                     cq_ref, sq_ref, ckt_ref, skt_ref,
                     q_ref, kt_ref, v_ref, *, rope, qscale):
    m = mod_ref[0]
    h = (_rms(x_ref[0], g_ref[0:1]) * (1.0 + m[1:2]) + m[0:1]).astype(BF16)
    q = jnp.dot(h, wq_ref[...], preferred_element_type=F32) * qscale
    kt = lax.dot_general(wkt_ref[...], h, NT_DIMS, preferred_element_type=F32)
    v_ref[0] = jnp.dot(h, wv_ref[...], preferred_element_type=F32).astype(BF16)
    if not rope:
        q_ref[0] = q.astype(BF16)
        kt_ref[0, 0] = kt.astype(BF16)
        return
    half = A_HEAD_DIM // 2
    c, s = cq_ref[...], sq_ref[...]
    lane = lax.broadcasted_iota(jnp.int32, (ROW_TILE, LANES), 1)
    first = (lane % A_HEAD_DIM) < half
    for p in range(D_MODEL // LANES):
        sl = slice(p * LANES, (p + 1) * LANES)
        q_ref[0, :, sl] = _rope_lanes(q[:, sl], c, s, first, half).astype(BF16)
    ct, st = ckt_ref[...], skt_ref[...]
    for g in range(D_MODEL // A_HEAD_DIM):
        r0 = g * A_HEAD_DIM
        t1, t2 = kt[r0:r0 + half], kt[r0 + half:r0 + 2 * half]
        kt_ref[0, 0, r0:r0 + half, :] = (t1 * ct - t2 * st).astype(BF16)
        kt_ref[0, 0, r0 + half:r0 + 2 * half, :] = (t1 * st + t2 * ct).astype(BF16)


def _row_specs():
    return [pl.BlockSpec((1, ROW_TILE, D_MODEL), lambda b, t: (b, t, 0)),
            pl.BlockSpec((1, 6, D_MODEL), lambda b, t: (_mod_row(b, t), 0, 0))]


def _qkv_out(width):
    shapes = (jax.ShapeDtypeStruct((BATCH, T_ALL, width), BF16),
              jax.ShapeDtypeStruct((BATCH, N_TILES, width, ROW_TILE), BF16),
              jax.ShapeDtypeStruct((BATCH, T_ALL, D_MODEL), BF16))
    specs = (pl.BlockSpec((1, ROW_TILE, width), lambda b, t: (b, t, 0)),
             pl.BlockSpec((1, 1, width, ROW_TILE), lambda b, t: (b, t, 0, 0)),
             pl.BlockSpec((1, ROW_TILE, D_MODEL), lambda b, t: (b, t, 0)))
    return shapes, specs


def _proj_qkv(x, mod, g_norm, wq, wkt, wv, tabs, *, rope, qscale):
    cq, sq, ckt, skt = tabs
    half = ckt.shape[0]
    shapes, specs = _qkv_out(D_MODEL)
    return pl.pallas_call(
        functools.partial(_proj_qkv_kernel, rope=rope, qscale=qscale),
        out_shape=shapes,
        grid=(BATCH, N_TILES),
        in_specs=_row_specs() + [
            _resident((4, D_MODEL)),
            _resident((D_MODEL, D_MODEL)), _resident((D_MODEL, D_MODEL)),
            _resident((D_MODEL, D_MODEL)),
            pl.BlockSpec((ROW_TILE, LANES), lambda b, t: (t, 0)),
            pl.BlockSpec((ROW_TILE, LANES), lambda b, t: (t, 0)),
            pl.BlockSpec((half, ROW_TILE), lambda b, t: (0, t)),
            pl.BlockSpec((half, ROW_TILE), lambda b, t: (0, t))],
        out_specs=specs,
        compiler_params=_params(2),
        name="proj_qkv",
    )(x, mod, g_norm, wq, wkt, wv, cq, sq, ckt, skt)


def _proj_mla_kernel(x_ref, mod_ref, g_ref, win_ref, wkrt_ref, gq_ref, gkv_ref,
                     wuq_ref, wuknt_ref, wuv_ref, cq_ref, sq_ref, ckt_ref, skt_ref,
                     q_ref, kt_ref, v_ref, *, qscale):
    m = mod_ref[0]
    h = (_rms(x_ref[0], g_ref[0:1]) * (1.0 + m[1:2]) + m[0:1]).astype(BF16)
    z = jnp.dot(h, win_ref[...], preferred_element_type=F32)
    cq = _rms(z[:, :B_Q_LORA], gq_ref[...]).astype(BF16)
    ckv = _rms(z[:, B_Q_LORA:], gkv_ref[...]).astype(BF16)
    q = jnp.dot(cq, wuq_ref[...], preferred_element_type=F32) * qscale
    half = B_ROPE // 2
    c, s = cq_ref[...], sq_ref[...]
    lane = lax.broadcasted_iota(jnp.int32, (ROW_TILE, LANES), 1)
    first = (lane >= B_NOPE) & (lane < B_NOPE + half)
    for hh in range(B_HEADS):
        sl = slice(hh * LANES, (hh + 1) * LANES)
        q_ref[0, :, sl] = _rope_lanes(q[:, sl], c, s, first, half).astype(BF16)
    v_ref[0] = jnp.dot(ckv, wuv_ref[...], preferred_element_type=F32).astype(BF16)
    knt = lax.dot_general(wuknt_ref[...], ckv, NT_DIMS, preferred_element_type=F32)
    krt = lax.dot_general(wkrt_ref[...], h, NT_DIMS, preferred_element_type=F32)
    ct, st = ckt_ref[...], skt_ref[...]
    t1, t2 = krt[:half], krt[half:]
    kr = jnp.concatenate([t1 * ct - t2 * st, t1 * st + t2 * ct], axis=0).astype(BF16)
    pad = jnp.zeros((LANES - B_NOPE - B_ROPE, ROW_TILE), BF16)
    for hh in range(B_HEADS):
        r0 = hh * LANES
        kt_ref[0, 0, r0:r0 + B_NOPE, :] = knt[hh * B_NOPE:(hh + 1) * B_NOPE].astype(BF16)
        kt_ref[0, 0, r0 + B_NOPE:r0 + B_NOPE + B_ROPE, :] = kr
        kt_ref[0, 0, r0 + B_NOPE + B_ROPE:r0 + LANES, :] = pad


def _proj_mla(x, mod, g_norm, win, wkrt, gq, gkv, wuq, wuknt, wuv, tabs, *, qscale):
    cq, sq, ckt, skt = tabs
    half = ckt.shape[0]
    width = B_HEADS * LANES
    shapes, specs = _qkv_out(width)
    return pl.pallas_call(
        functools.partial(_proj_mla_kernel, qscale=qscale),
        out_shape=shapes,
        grid=(BATCH, N_TILES),
        in_specs=_row_specs() + [
            _resident((4, D_MODEL)),
            _resident(win.shape), _resident(wkrt.shape),
            _resident((1, B_Q_LORA)), _resident((1, B_KV_LORA)),
            _resident(wuq.shape), _resident(wuknt.shape), _resident(wuv.shape),
            pl.BlockSpec((ROW_TILE, LANES), lambda b, t: (t, 0)),
            pl.BlockSpec((ROW_TILE, LANES), lambda b, t: (t, 0)),
            pl.BlockSpec((half, ROW_TILE), lambda b, t: (0, t)),
            pl.BlockSpec((half, ROW_TILE), lambda b, t: (0, t))],
        out_specs=specs,
        compiler_params=_params(2),
        name="proj_mla",
    )(x, mod, g_norm, win, wkrt, gq, gkv, wuq, wuknt, wuv, cq, sq, ckt, skt)


def _flash_kernel(lam_ref, gsub_ref, q_ref, kt_ref, v_ref, o_ref,
                  m_ref, l_ref, acc_ref, *, diff, lam_init):
    qi = pl.program_id(2)
    dq = q_ref.shape[-1]
    q = q_ref[0]
    lane_q = lax.broadcasted_iota(jnp.int32, q.shape, 1)
    zero = jnp.zeros_like(q)
    q_slots = (jnp.where(lane_q < dq // 2, q, zero), jnp.where(lane_q >= dq // 2, q, zero))

    m_ref[...] = jnp.full(m_ref.shape, MASK_VALUE, F32)
    l_ref[...] = jnp.zeros(l_ref.shape, F32)
    acc_ref[...] = jnp.zeros(acc_ref.shape, F32)

    start = jnp.where(qi == CTX_TILE, CTX_TILE, 0)

    def body(i, carry):
        kt = kt_ref[0, i]
        v = v_ref[0, pl.ds(pl.multiple_of(i * ROW_TILE, ROW_TILE), ROW_TILE), :]
        for slot in range(2):
            s = jnp.dot(q_slots[slot], kt, preferred_element_type=F32)
            m_prev = m_ref[slot]
            m_next = jnp.maximum(m_prev, jnp.max(s, axis=1, keepdims=True))
            alpha = jnp.exp2(m_prev - m_next)
            p = jnp.exp2(s - jnp.tile(m_next, (1, ROW_TILE // LANES)))
            l_ref[slot] = alpha * l_ref[slot] + jnp.sum(p, axis=1, keepdims=True)
            acc_ref[slot] = alpha * acc_ref[slot] + jnp.dot(
                p.astype(BF16), v, preferred_element_type=F32)
            m_ref[slot] = m_next
        return carry

    lax.fori_loop(start, N_TILES, body, 0)

    o_a = acc_ref[0] * (1.0 / l_ref[0])
    o_b = acc_ref[1] * (1.0 / l_ref[1])
    if diff:
        lv = lam_ref[...]
        lam = (jnp.exp(jnp.sum(lv[0:1] * lv[1:2], axis=1, keepdims=True))
               - jnp.exp(jnp.sum(lv[2:3] * lv[3:4], axis=1, keepdims=True)) + lam_init)
        o = o_a - lam * o_b
        o = _rms(o, gsub_ref[...]) * (1.0 - lam_init)
    else:
        lane_o = lax.broadcasted_iota(jnp.int32, o_a.shape, 1)
        o = jnp.where(lane_o < LANES // 2, o_a, o_b)
    o_ref[0] = o.astype(BF16)


def _flash(q, kt, v, lam, gsub, *, diff, lam_init):
    dq = q.shape[-1] // N_PAIRS
    return pl.pallas_call(
        functools.partial(_flash_kernel, diff=diff, lam_init=lam_init),
        out_shape=jax.ShapeDtypeStruct((BATCH, T_ALL, D_MODEL), BF16),
        grid=(BATCH, N_PAIRS, N_TILES),
        in_specs=[pl.BlockSpec(lam.shape, lambda b, p, i: (0, 0)),
                  pl.BlockSpec(gsub.shape, lambda b, p, i: (0, 0)),
                  pl.BlockSpec((1, ROW_TILE, dq), lambda b, p, i: (b, i, p)),
                  pl.BlockSpec((1, N_TILES, dq, ROW_TILE), lambda b, p, i: (b, 0, p, 0)),
                  pl.BlockSpec((1, T_ALL, LANES), lambda b, p, i: (b, 0, p))],
        out_specs=pl.BlockSpec((1, ROW_TILE, LANES), lambda b, p, i: (b, i, p)),
        scratch_shapes=[pltpu.VMEM((2, ROW_TILE, LANES), F32),
                        pltpu.VMEM((2, ROW_TILE, LANES), F32),
                        pltpu.VMEM((2, ROW_TILE, LANES), F32)],
        compiler_params=_params(3),
        name="flash_diff" if diff else "flash_pair",
    )(lam, gsub, q, kt, v)


NA_Q_ROWS = ROW_TILE // GRID_W
NA_BAND_TILES = 3
NA_LAST_BAND = SEQ // ROW_TILE - NA_BAND_TILES


def _na_band_start(i):
    return jnp.clip(i - 1, 0, NA_LAST_BAND)


def _na_kernel(q_ref, k0_ref, k1_ref, k2_ref, kc_ref, v0_ref, v1_ref, v2_ref, vc_ref,
               bias_ref, o_ref):
    q = q_ref[0]
    lane_q = lax.broadcasted_iota(jnp.int32, q.shape, 1)
    zero = jnp.zeros_like(q)
    q_slots = (jnp.where(lane_q < LANES // 2, q, zero), jnp.where(lane_q >= LANES // 2, q, zero))
    k_refs = (k0_ref, k1_ref, k2_ref)
    v_refs = (v0_ref, v1_ref, v2_ref)
    outs = []
    for slot in range(2):
        qs = q_slots[slot]
        s = [jnp.dot(qs, k_refs[j][0, 0], preferred_element_type=F32)
             + bias_ref[0, slot, :, j * ROW_TILE:(j + 1) * ROW_TILE]
             for j in range(NA_BAND_TILES)]
        s.append(jnp.dot(qs, kc_ref[0, 0], preferred_element_type=F32))
        m = functools.reduce(jnp.maximum, [jnp.max(t, axis=1, keepdims=True) for t in s])
        p = [jnp.exp2(t - m) for t in s]
        l = functools.reduce(jnp.add, [jnp.sum(t, axis=1, keepdims=True) for t in p])
        vs = [r[0] for r in v_refs] + [vc_ref[0]]
        o = functools.reduce(jnp.add, [jnp.dot(t.astype(BF16), vv, preferred_element_type=F32)
                                       for t, vv in zip(p, vs)])
        outs.append(o * (1.0 / l))
    lane_o = lax.broadcasted_iota(jnp.int32, outs[0].shape, 1)
    o_ref[0] = jnp.where(lane_o < LANES // 2, outs[0], outs[1]).astype(BF16)


def _na_variant(i):
    return jnp.where(i == 0, 0, jnp.where(i < NA_LAST_BAND + NA_BAND_TILES - 1, 1,
                                          jnp.where(i < CTX_TILE, 2, 3)))


def _na_attention(q, kt, v, bias):
    def kspec(j):
        return pl.BlockSpec((1, 1, LANES, ROW_TILE),
                            lambda p, i, b: (b, _na_band_start(i) + j, p, 0))

    def vspec(j):
        return pl.BlockSpec((1, ROW_TILE, LANES), lambda p, i, b: (b, _na_band_start(i) + j, p))

    return pl.pallas_call(
        _na_kernel,
        out_shape=jax.ShapeDtypeStruct((BATCH, T_ALL, D_MODEL), BF16),
        grid=(N_PAIRS, N_TILES, BATCH),
        in_specs=[pl.BlockSpec((1, ROW_TILE, LANES), lambda p, i, b: (b, i, p)),
                  kspec(0), kspec(1), kspec(2),
                  pl.BlockSpec((1, 1, LANES, ROW_TILE), lambda p, i, b: (b, CTX_TILE, p, 0)),
                  vspec(0), vspec(1), vspec(2),
                  pl.BlockSpec((1, ROW_TILE, LANES), lambda p, i, b: (b, CTX_TILE, p)),
                  pl.BlockSpec((1, 2, ROW_TILE, NA_BAND_TILES * ROW_TILE),
                               lambda p, i, b: (_na_variant(i), p, 0, 0))],
        out_specs=pl.BlockSpec((1, ROW_TILE, LANES), lambda p, i, b: (b, i, p)),
        compiler_params=_params(3),
        name="na_attention",
    )(q, kt, kt, kt, kt, v, v, v, v, bias)


def _na_bias_indices():
    rows = SEQ // GRID_W
    n_q, n_k = ROW_TILE, NA_BAND_TILES * ROW_TILE
    dr = np.zeros((4, n_q, n_k), np.int32)
    dc = np.zeros((4, n_q, n_k), np.int32)
    ok = np.zeros((4, n_q, n_k), bool)
    qr, qc = np.divmod(np.arange(n_q), GRID_W)
    kr, kc = np.divmod(np.arange(n_k), GRID_W)
    for variant, i in enumerate((0, 1, SEQ // ROW_TILE - 1)):
        j = min(max(i - 1, 0), NA_LAST_BAND)
        r = NA_Q_ROWS * i + qr[:, None]
        k_row = NA_Q_ROWS * j + kr[None, :]
        rstart = np.clip(r - NA_ROWS // 2, 0, rows - NA_ROWS)
        cstart = np.clip(qc[:, None] - NA_COLS // 2, 0, GRID_W - NA_COLS)
        valid = ((k_row >= rstart) & (k_row < rstart + NA_ROWS)
                 & (kc[None, :] >= cstart) & (kc[None, :] < cstart + NA_COLS))
        ok[variant] = valid
        dr[variant] = np.where(valid, k_row - r + NA_ROWS - 1, 0)
        dc[variant] = np.where(valid, kc[None, :] - qc[:, None] + NA_COLS - 1, 0)
    return dr, dc, ok


def _post_kernel(o_ref, x_ref, mod_ref, g_ref, wo_ref, wgu_ref, wd_ref, xo_ref):
    m = mod_ref[0]
    y = jnp.dot(o_ref[0], wo_ref[...], preferred_element_type=F32)
    x1 = x_ref[0] + m[2:3] * _rms(y, g_ref[1:2])
    h2 = (_rms(x1, g_ref[2:3]) * (1.0 + m[4:5]) + m[3:4]).astype(BF16)
    gu = jnp.dot(h2, wgu_ref[...], preferred_element_type=F32)
    gate, up = gu[:, :FFN_HIDDEN], gu[:, FFN_HIDDEN:]
    a = (gate * _sigmoid(gate) * up).astype(BF16)
    f = jnp.dot(a, wd_ref[...], preferred_element_type=F32)
    xo_ref[0] = x1 + m[5:6] * _rms(f, g_ref[3:4])


def _post(o, x, mod, g_norm, wo, wgu, wd, *, n_tiles):
    return pl.pallas_call(
        _post_kernel,
        out_shape=jax.ShapeDtypeStruct((BATCH, n_tiles * ROW_TILE, D_MODEL), F32),
        grid=(BATCH, n_tiles),
        in_specs=[pl.BlockSpec((1, ROW_TILE, D_MODEL), lambda b, t: (b, t, 0))] + _row_specs() + [
            _resident((4, D_MODEL)), _resident(wo.shape), _resident(wgu.shape),
            _resident(wd.shape)],
        out_specs=pl.BlockSpec((1, ROW_TILE, D_MODEL), lambda b, t: (b, t, 0)),
        compiler_params=_params(2),
        name="post_attention",
    )(o, x, mod, g_norm, wo, wgu, wd)


def _rope_angles(rot_dim):
    n_freq = rot_dim // 4
    inv = ROPE_THETA ** (-jnp.arange(n_freq, dtype=F32) / n_freq)
    t = jnp.arange(SEQ, dtype=jnp.int32)
    row = (t // GRID_W).astype(F32)
    col = (t % GRID_W).astype(F32)
    ang = jnp.concatenate([row[:, None] * inv, col[:, None] * inv], axis=-1)
    cos = jnp.concatenate([jnp.cos(ang), jnp.ones((CTX_LEN, rot_dim // 2), F32)], axis=0)
    sin = jnp.concatenate([jnp.sin(ang), jnp.zeros((CTX_LEN, rot_dim // 2), F32)], axis=0)
    return cos, sin


def _rope_tables_diff():
    cos, sin = _rope_angles(A_HEAD_DIM)
    cq = jnp.tile(cos, (1, 4))
    sq = jnp.tile(jnp.concatenate([-sin, sin], axis=1), (1, 2))
    return cq, sq, cos.T, sin.T


def _rope_tables_mla():
    cos, sin = _rope_angles(B_ROPE)
    ones = jnp.ones((T_ALL, B_NOPE), F32)
    zeros = jnp.zeros((T_ALL, B_NOPE), F32)
    tail = LANES - B_NOPE - B_ROPE
    cq = jnp.concatenate([ones, cos, cos, ones[:, :tail]], axis=1)
    sq = jnp.concatenate([zeros, -sin, sin, zeros[:, :tail]], axis=1)
    return cq, sq, cos.T, sin.T


def kernel(x, c, ctx, c_ctx, l0_w_mod, l0_b_mod, l0_g_norm, l0_w_gu, l0_w_down, l0_a_w_qkv, l0_a_w_o, l0_a_lam, l0_a_g_sub, l1_w_mod, l1_b_mod, l1_g_norm, l1_w_gu, l1_w_down, l1_b_w_in, l1_b_g_q, l1_b_g_kv, l1_b_w_uq, l1_b_w_ukv, l1_b_w_o, l2_w_mod, l2_b_mod, l2_g_norm, l2_w_gu, l2_w_down, l2_c_w_qkv, l2_c_rpb, l2_c_w_o, l3_w_mod, l3_b_mod, l3_g_norm, l3_w_gu, l3_w_down, l3_a_w_qkv, l3_a_w_o, l3_a_lam, l3_a_g_sub):
    common = [
        (l0_w_mod, l0_b_mod, l0_g_norm, l0_w_gu, l0_w_down),
        (l1_w_mod, l1_b_mod, l1_g_norm, l1_w_gu, l1_w_down),
        (l2_w_mod, l2_b_mod, l2_g_norm, l2_w_gu, l2_w_down),
        (l3_w_mod, l3_b_mod, l3_g_norm, l3_w_gu, l3_w_down),
    ]
    diff_params = {0: (l0_a_w_qkv, l0_a_w_o, l0_a_lam, l0_a_g_sub),
                   3: (l3_a_w_qkv, l3_a_w_o, l3_a_lam, l3_a_g_sub)}

    xs = jnp.concatenate([x, ctx], axis=1)
    cc = jnp.concatenate([c, c_ctx[None, :], jnp.zeros((8 - BATCH - 1, D_MODEL), F32)], axis=0)
    tabs_a = _rope_tables_diff()
    tabs_b = _rope_tables_mla()
    dummy_lam = jnp.zeros((4, A_HEAD_DIM), F32)
    dummy_gsub = jnp.ones((1, LANES), F32)

    for i in range(DEPTH):
        w_mod, b_mod, g_norm, w_gu, w_down = common[i]
        last = i == DEPTH - 1
        mod = _modulation(cc, w_mod, b_mod).reshape(8, 6, D_MODEL)
        kind = i % 3
        if kind == 0:
            w_qkv, w_o, lam, g_sub = diff_params[i]
            lam_init = 0.8 - 0.6 * math.exp(-0.3 * i)
            wq = w_qkv[:, :D_MODEL].astype(BF16)
            wkt = w_qkv[:, D_MODEL:2 * D_MODEL].T.astype(BF16)
            wv = w_qkv[:, 2 * D_MODEL:].astype(BF16)
            q, kt, v = _proj_qkv(xs, mod, g_norm, wq, wkt, wv, tabs_a, rope=True,
                                 qscale=A_HEAD_DIM ** -0.5 * LOG2E)
            o = _flash(q, kt, v, lam, g_sub.reshape(1, LANES), diff=True, lam_init=lam_init)
        elif kind == 1:
            w_in, g_q, g_kv, w_uq, w_ukv, w_o = (l1_b_w_in, l1_b_g_q, l1_b_g_kv, l1_b_w_uq,
                                                 l1_b_w_ukv, l1_b_w_o)
            n_lora = B_Q_LORA + B_KV_LORA
            win = w_in[:, :n_lora].astype(BF16)
            wkrt = w_in[:, n_lora:].T.astype(BF16)
            wuq = w_uq.reshape(B_Q_LORA, B_HEADS, B_NOPE + B_ROPE)
            wuq = jnp.pad(wuq, ((0, 0), (0, 0), (0, LANES - B_NOPE - B_ROPE)))
            wuq = wuq.reshape(B_Q_LORA, B_HEADS * LANES).astype(BF16)
            wukv = w_ukv.reshape(B_KV_LORA, B_HEADS, B_NOPE + B_VDIM)
            wuknt = wukv[:, :, :B_NOPE].reshape(B_KV_LORA, B_HEADS * B_NOPE).T.astype(BF16)
            wuv = wukv[:, :, B_NOPE:].reshape(B_KV_LORA, B_HEADS * B_VDIM).astype(BF16)
            q, kt, v = _proj_mla(xs, mod, g_norm, win, wkrt, g_q.reshape(1, -1),
                                 g_kv.reshape(1, -1), wuq, wuknt, wuv, tabs_b,
                                 qscale=(B_NOPE + B_ROPE) ** -0.5 * LOG2E)
            o = _flash(q, kt, v, dummy_lam, dummy_gsub, diff=False, lam_init=0.0)
        else:
            w_qkv, rpb, w_o = l2_c_w_qkv, l2_c_rpb, l2_c_w_o
            wq = w_qkv[:, :D_MODEL].astype(BF16)
            wkt = w_qkv[:, D_MODEL:2 * D_MODEL].T.astype(BF16)
            wv = w_qkv[:, 2 * D_MODEL:].astype(BF16)
            q, kt, v = _proj_qkv(xs, mod, g_norm, wq, wkt, wv, tabs_a, rope=False,
                                 qscale=C_HEAD_DIM ** -0.5 * LOG2E)
            dr, dc, ok = _na_bias_indices()
            bias = jnp.where(ok[:, None], rpb.astype(F32)[:, dr, dc].transpose(1, 0, 2, 3) * LOG2E,
                             MASK_VALUE)
            o = _na_attention(q, kt, v, bias)
        xs = _post(o, xs, mod, g_norm, w_o.astype(BF16), w_gu.astype(BF16),
                   w_down.astype(BF16), n_tiles=N_TILES - 1 if last else N_TILES)
    return xs
```

```python
import functools
import math

import numpy as np
import jax
import jax.numpy as jnp
from jax import lax
from jax.experimental import pallas as pl
from jax.experimental.pallas import tpu as pltpu

D_MODEL = 1024
BATCH = 4
SEQ = 8192
DEPTH = 4
GRID_W = 64
CTX_LEN = 256
ROPE_THETA = 10000.0
NORM_EPS = 1e-6

A_HEAD_DIM = 64
A_HEADS = 8
B_HEADS = 16
B_NOPE = 64
B_ROPE = 32
B_VDIM = 64
B_Q_LORA = 256
B_KV_LORA = 256
C_HEAD_DIM = 64
C_HEADS = 16
NA_ROWS = 8
NA_COLS = 16
FFN_HIDDEN = 2816

LANES = 128
ROW_TILE = 256
T_ALL = SEQ + CTX_LEN
N_TILES = T_ALL // ROW_TILE
CTX_TILE = N_TILES - 1
N_PAIRS = 8
KEY_TILES = 3
N_GROUPS = N_TILES // KEY_TILES
LOG2E = math.log2(math.e)
MASK_VALUE = -1e30
VMEM_LIMIT = 60 * 1024 * 1024

F32 = jnp.float32
BF16 = jnp.bfloat16
NT_DIMS = (((1,), (1,)), ((), ()))


def _rms(x, g):
    return x * lax.rsqrt(jnp.mean(x * x, axis=-1, keepdims=True) + NORM_EPS) * g


def _sigmoid(x):
    return 1.0 / (1.0 + jnp.exp(-x))


def _params(n_axes):
    return pltpu.CompilerParams(
        dimension_semantics=("arbitrary",) * n_axes, vmem_limit_bytes=VMEM_LIMIT)


def _resident(shape):
    zeros = (0,) * len(shape)
    return pl.BlockSpec(shape, lambda *_: zeros, pipeline_mode=pl.Buffered(1))


def _mod_row(b, t):
    return jnp.where(t == CTX_TILE, BATCH, b)


def _mod_kernel(c_ref, w_ref, b_ref, o_ref):
    c = c_ref[...]
    a = (c * _sigmoid(c)).astype(BF16)
    o_ref[...] = jnp.dot(a, w_ref[...].astype(BF16), preferred_element_type=F32) + b_ref[...]


def _modulation(cc, w_mod, b_mod):
    n = w_mod.shape[1]
    blk = D_MODEL
    return pl.pallas_call(
        _mod_kernel,
        out_shape=jax.ShapeDtypeStruct((8, n), F32),
        grid=(n // blk,),
        in_specs=[pl.BlockSpec((8, D_MODEL), lambda j: (0, 0)),
                  pl.BlockSpec((D_MODEL, blk), lambda j: (0, j)),
                  pl.BlockSpec((1, blk), lambda j: (0, j))],
        out_specs=pl.BlockSpec((8, blk), lambda j: (0, j)),
        compiler_params=_params(1),
        name="modulation",
    )(cc, w_mod, b_mod.reshape(1, n))


def _rope_lanes(q, c, s, first, half):
    rot = jnp.where(first, pltpu.roll(q, LANES - half, 1), pltpu.roll(q, half, 1))
    return q * c + rot * s


def _proj_qkv_kernel(x_ref, mod_ref, g_ref, wq_ref, wkt_ref, wv_ref,
                     cq_ref, sq_ref, ckt_ref, skt_ref,
                     q_ref, kt_ref, v_ref, *, rope, qscale):
    m = mod_ref[0]
    h = (_rms(x_ref[0], g_ref[0:1]) * (1.0 + m[1:2]) + m[0:1]).astype(BF16)
    q = jnp.dot(h, wq_ref[...], preferred_element_type=F32) * qscale
    kt = lax.dot_general(wkt_ref[...], h, NT_DIMS, preferred_element_type=F32)
    v_ref[0] = jnp.dot(h, wv_ref[...], preferred_element_type=F32).astype(BF16)
    if not rope:
        q_ref[0] = q.astype(BF16)
        kt_ref[0, 0] = kt.astype(BF16)
        return
    half = A_HEAD_DIM // 2
    c, s = cq_ref[...], sq_ref[...]
    lane = lax.broadcasted_iota(jnp.int32, (ROW_TILE, LANES), 1)
    first = (lane % A_HEAD_DIM) < half
    for p in range(D_MODEL // LANES):
        sl = slice(p * LANES, (p + 1) * LANES)
        q_ref[0, :, sl] = _rope_lanes(q[:, sl], c, s, first, half).astype(BF16)
    ct, st = ckt_ref[...], skt_ref[...]
    for g in range(D_MODEL // A_HEAD_DIM):
        r0 = g * A_HEAD_DIM
        t1, t2 = kt[r0:r0 + half], kt[r0 + half:r0 + 2 * half]
        kt_ref[0, 0, r0:r0 + half, :] = (t1 * ct - t2 * st).astype(BF16)
        kt_ref[0, 0, r0 + half:r0 + 2 * half, :] = (t1 * st + t2 * ct).astype(BF16)


def _row_specs():
    return [pl.BlockSpec((1, ROW_TILE, D_MODEL), lambda b, t: (b, t, 0)),
            pl.BlockSpec((1, 6, D_MODEL), lambda b, t: (_mod_row(b, t), 0, 0))]


def _qkv_out(width):
    shapes = (jax.ShapeDtypeStruct((BATCH, T_ALL, width), BF16),
              jax.ShapeDtypeStruct((BATCH, N_TILES, width, ROW_TILE), BF16),
              jax.ShapeDtypeStruct((BATCH, T_ALL, D_MODEL), BF16))
    specs = (pl.BlockSpec((1, ROW_TILE, width), lambda b, t: (b, t, 0)),
             pl.BlockSpec((1, 1, width, ROW_TILE), lambda b, t: (b, t, 0, 0)),
             pl.BlockSpec((1, ROW_TILE, D_MODEL), lambda b, t: (b, t, 0)))
    return shapes, specs


def _proj_qkv(x, mod, g_norm, wq, wkt, wv, tabs, *, rope, qscale):
    cq, sq, ckt, skt = tabs
    half = ckt.shape[0]
    shapes, specs = _qkv_out(D_MODEL)
    return pl.pallas_call(
        functools.partial(_proj_qkv_kernel, rope=rope, qscale=qscale),
        out_shape=shapes,
        grid=(BATCH, N_TILES),
        in_specs=_row_specs() + [
            _resident((4, D_MODEL)),
            _resident((D_MODEL, D_MODEL)), _resident((D_MODEL, D_MODEL)),
            _resident((D_MODEL, D_MODEL)),
            pl.BlockSpec((ROW_TILE, LANES), lambda b, t: (t, 0)),
            pl.BlockSpec((ROW_TILE, LANES), lambda b, t: (t, 0)),
            pl.BlockSpec((half, ROW_TILE), lambda b, t: (0, t)),
            pl.BlockSpec((half, ROW_TILE), lambda b, t: (0, t))],
        out_specs=specs,
        compiler_params=_params(2),
        name="proj_qkv",
    )(x, mod, g_norm, wq, wkt, wv, cq, sq, ckt, skt)


def _proj_mla_kernel(x_ref, mod_ref, g_ref, win_ref, wkrt_ref, gq_ref, gkv_ref,
                     wuq_ref, wuknt_ref, wuv_ref, cq_ref, sq_ref, ckt_ref, skt_ref,
                     q_ref, kt_ref, v_ref, *, qscale):
    m = mod_ref[0]
    h = (_rms(x_ref[0], g_ref[0:1]) * (1.0 + m[1:2]) + m[0:1]).astype(BF16)
    z = jnp.dot(h, win_ref[...], preferred_element_type=F32)
    cq = _rms(z[:, :B_Q_LORA], gq_ref[...]).astype(BF16)
    ckv = _rms(z[:, B_Q_LORA:], gkv_ref[...]).astype(BF16)
    q = jnp.dot(cq, wuq_ref[...], preferred_element_type=F32) * qscale
    half = B_ROPE // 2
    c, s = cq_ref[...], sq_ref[...]
    lane = lax.broadcasted_iota(jnp.int32, (ROW_TILE, LANES), 1)
    first = (lane >= B_NOPE) & (lane < B_NOPE + half)
    for hh in range(B_HEADS):
        sl = slice(hh * LANES, (hh + 1) * LANES)
        q_ref[0, :, sl] = _rope_lanes(q[:, sl], c, s, first, half).astype(BF16)
    v_ref[0] = jnp.dot(ckv, wuv_ref[...], preferred_element_type=F32).astype(BF16)
    knt = lax.dot_general(wuknt_ref[...], ckv, NT_DIMS, preferred_element_type=F32)
    krt = lax.dot_general(wkrt_ref[...], h, NT_DIMS, preferred_element_type=F32)
    ct, st = ckt_ref[...], skt_ref[...]
    t1, t2 = krt[:half], krt[half:]
    kr = jnp.concatenate([t1 * ct - t2 * st, t1 * st + t2 * ct], axis=0).astype(BF16)
    pad = jnp.zeros((LANES - B_NOPE - B_ROPE, ROW_TILE), BF16)
    for hh in range(B_HEADS):
        r0 = hh * LANES
        kt_ref[0, 0, r0:r0 + B_NOPE, :] = knt[hh * B_NOPE:(hh + 1) * B_NOPE].astype(BF16)
        kt_ref[0, 0, r0 + B_NOPE:r0 + B_NOPE + B_ROPE, :] = kr
        kt_ref[0, 0, r0 + B_NOPE + B_ROPE:r0 + LANES, :] = pad


def _proj_mla(x, mod, g_norm, win, wkrt, gq, gkv, wuq, wuknt, wuv, tabs, *, qscale):
    cq, sq, ckt, skt = tabs
    half = ckt.shape[0]
    width = B_HEADS * LANES
    shapes, specs = _qkv_out(width)
    return pl.pallas_call(
        functools.partial(_proj_mla_kernel, qscale=qscale),
        out_shape=shapes,
        grid=(BATCH, N_TILES),
        in_specs=_row_specs() + [
            _resident((4, D_MODEL)),
            _resident(win.shape), _resident(wkrt.shape),
            _resident((1, B_Q_LORA)), _resident((1, B_KV_LORA)),
            _resident(wuq.shape), _resident(wuknt.shape), _resident(wuv.shape),
            pl.BlockSpec((ROW_TILE, LANES), lambda b, t: (t, 0)),
            pl.BlockSpec((ROW_TILE, LANES), lambda b, t: (t, 0)),
            pl.BlockSpec((half, ROW_TILE), lambda b, t: (0, t)),
            pl.BlockSpec((half, ROW_TILE), lambda b, t: (0, t))],
        out_specs=specs,
        compiler_params=_params(2),
        name="proj_mla",
    )(x, mod, g_norm, win, wkrt, gq, gkv, wuq, wuknt, wuv, cq, sq, ckt, skt)


def _flash_kernel(lam_ref, gsub_ref, q_ref, kt_ref, v_ref, o_ref,
                  qs_ref, s0_ref, s1_ref, m_ref, l_ref, acc_ref, *, diff, lam_init):
    qi = pl.program_id(2)
    dq = q_ref.shape[-1]
    q = q_ref[0]
    lane_q = lax.broadcasted_iota(jnp.int32, q.shape, 1)
    zero = jnp.zeros_like(q)
    qs_ref[0] = jnp.where(lane_q < dq // 2, q, zero)
    qs_ref[1] = jnp.where(lane_q >= dq // 2, q, zero)

    m_ref[...] = jnp.full(m_ref.shape, MASK_VALUE, F32)
    l_ref[...] = jnp.zeros(l_ref.shape, F32)
    acc_ref[...] = jnp.zeros(acc_ref.shape, F32)

    def scores(tiles, s_ref):
        for slot in range(2):
            for j, t in enumerate(tiles):
                s_ref[slot, j] = jnp.dot(qs_ref[slot], kt_ref[0, t], preferred_element_type=F32)

    def update(tiles, s_ref):
        vs = [v_ref[0, pl.ds(pl.multiple_of(t * ROW_TILE, ROW_TILE), ROW_TILE), :] for t in tiles]
        for slot in range(2):
            s = [s_ref[slot, j] for j in range(len(tiles))]
            m_prev = m_ref[slot]
            m_cur = jnp.max(functools.reduce(jnp.maximum, s), axis=1, keepdims=True)
            m_next = jnp.maximum(m_prev, m_cur)
            alpha = jnp.exp2(m_prev - m_next)
            m_wide = jnp.tile(m_next, (1, ROW_TILE // LANES))
            p = [jnp.exp2(t - m_wide) for t in s]
            p_sum = functools.reduce(jnp.add, p)
            l_ref[slot] = alpha * l_ref[slot] + functools.reduce(
                jnp.add, [p_sum[:, j * LANES:(j + 1) * LANES] for j in range(ROW_TILE // LANES)])
            pv = functools.reduce(jnp.add, [jnp.dot(t.astype(BF16), v, preferred_element_type=F32)
                                            for t, v in zip(p, vs)])
            acc_ref[slot] = alpha * acc_ref[slot] + pv
            m_ref[slot] = m_next

    def group(g):
        return [g * KEY_TILES + j for j in range(KEY_TILES)]

    @pl.when(qi != CTX_TILE)
    def _():
        scores(group(0), s0_ref)

        def body(k, carry):
            update(group(2 * k), s0_ref)
            scores(group(2 * k + 1), s1_ref)
            update(group(2 * k + 1), s1_ref)
            scores(group(2 * k + 2), s0_ref)
            return carry
        lax.fori_loop(0, (N_GROUPS - 1) // 2, body, 0)
        update(group(N_GROUPS - 1), s0_ref)

    @pl.when(qi == CTX_TILE)
    def _():
        scores([CTX_TILE], s0_ref)
        update([CTX_TILE], s0_ref)

    o_a = acc_ref[0] * (1.0 / jnp.sum(l_ref[0], axis=1, keepdims=True))
    o_b = acc_ref[1] * (1.0 / jnp.sum(l_ref[1], axis=1, keepdims=True))
    if diff:
        lv = lam_ref[...]
        lam = (jnp.exp(jnp.sum(lv[0:1] * lv[1:2], axis=1, keepdims=True))
               - jnp.exp(jnp.sum(lv[2:3] * lv[3:4], axis=1, keepdims=True)) + lam_init)
        o = o_a - lam * o_b
        o = _rms(o, gsub_ref[...]) * (1.0 - lam_init)
    else:
        lane_o = lax.broadcasted_iota(jnp.int32, o_a.shape, 1)
        o = jnp.where(lane_o < LANES // 2, o_a, o_b)
    o_ref[0] = o.astype(BF16)


def _flash(q, kt, v, lam, gsub, *, diff, lam_init):
    dq = q.shape[-1] // N_PAIRS
    return pl.pallas_call(
        functools.partial(_flash_kernel, diff=diff, lam_init=lam_init),
        out_shape=jax.ShapeDtypeStruct((BATCH, T_ALL, D_MODEL), BF16),
        grid=(BATCH, N_PAIRS, N_TILES),
        in_specs=[pl.BlockSpec(lam.shape, lambda b, p, i: (0, 0)),
                  pl.BlockSpec(gsub.shape, lambda b, p, i: (0, 0)),
                  pl.BlockSpec((1, ROW_TILE, dq), lambda b, p, i: (b, i, p)),
                  pl.BlockSpec((1, N_TILES, dq, ROW_TILE), lambda b, p, i: (b, 0, p, 0)),
                  pl.BlockSpec((1, T_ALL, LANES), lambda b, p, i: (b, 0, p))],
        out_specs=pl.BlockSpec((1, ROW_TILE, LANES), lambda b, p, i: (b, i, p)),
        scratch_shapes=[pltpu.VMEM((2, ROW_TILE, dq), BF16),
                        pltpu.VMEM((2, KEY_TILES, ROW_TILE, ROW_TILE), F32),
                        pltpu.VMEM((2, KEY_TILES, ROW_TILE, ROW_TILE), F32),
                        pltpu.VMEM((2, ROW_TILE, LANES), F32),
                        pltpu.VMEM((2, ROW_TILE, LANES), F32),
                        pltpu.VMEM((2, ROW_TILE, LANES), F32)],
        compiler_params=_params(3),
        name="flash_diff" if diff else "flash_pair",
    )(lam, gsub, q, kt, v)


NA_Q_ROWS = ROW_TILE // GRID_W
NA_BAND_TILES = 3
NA_LAST_BAND = SEQ // ROW_TILE - NA_BAND_TILES


def _na_band_start(i):
    return jnp.clip(i - 1, 0, NA_LAST_BAND)


def _na_kernel(q_ref, k0_ref, k1_ref, k2_ref, kc_ref, v0_ref, v1_ref, v2_ref, vc_ref,
               bias_ref, o_ref):
    q = q_ref[0]
    lane_q = lax.broadcasted_iota(jnp.int32, q.shape, 1)
    zero = jnp.zeros_like(q)
    q_slots = (jnp.where(lane_q < LANES // 2, q, zero), jnp.where(lane_q >= LANES // 2, q, zero))
    k_refs = (k0_ref, k1_ref, k2_ref)
    v_refs = (v0_ref, v1_ref, v2_ref)
    outs = []
    for slot in range(2):
        qs = q_slots[slot]
        s = [jnp.dot(qs, k_refs[j][0, 0], preferred_element_type=F32)
             + bias_ref[0, slot, :, j * ROW_TILE:(j + 1) * ROW_TILE]
             for j in range(NA_BAND_TILES)]
        s.append(jnp.dot(qs, kc_ref[0, 0], preferred_element_type=F32))
        m = functools.reduce(jnp.maximum, [jnp.max(t, axis=1, keepdims=True) for t in s])
        p = [jnp.exp2(t - m) for t in s]
        l = functools.reduce(jnp.add, [jnp.sum(t, axis=1, keepdims=True) for t in p])
        vs = [r[0] for r in v_refs] + [vc_ref[0]]
        o = functools.reduce(jnp.add, [jnp.dot(t.astype(BF16), vv, preferred_element_type=F32)
                                       for t, vv in zip(p, vs)])
        outs.append(o * (1.0 / l))
    lane_o = lax.broadcasted_iota(jnp.int32, outs[0].shape, 1)
    o_ref[0] = jnp.where(lane_o < LANES // 2, outs[0], outs[1]).astype(BF16)


def _na_variant(i):
    return jnp.where(i == 0, 0, jnp.where(i < NA_LAST_BAND + NA_BAND_TILES - 1, 1,
                                          jnp.where(i < CTX_TILE, 2, 3)))


def _na_attention(q, kt, v, bias):
    def kspec(j):
        return pl.BlockSpec((1, 1, LANES, ROW_TILE),
                            lambda p, i, b: (b, _na_band_start(i) + j, p, 0))

    def vspec(j):
        return pl.BlockSpec((1, ROW_TILE, LANES), lambda p, i, b: (b, _na_band_start(i) + j, p))

    return pl.pallas_call(
        _na_kernel,
        out_shape=jax.ShapeDtypeStruct((BATCH, T_ALL, D_MODEL), BF16),
        grid=(N_PAIRS, N_TILES, BATCH),
        in_specs=[pl.BlockSpec((1, ROW_TILE, LANES), lambda p, i, b: (b, i, p)),
                  kspec(0), kspec(1), kspec(2),
                  pl.BlockSpec((1, 1, LANES, ROW_TILE), lambda p, i, b: (b, CTX_TILE, p, 0)),
                  vspec(0), vspec(1), vspec(2),
                  pl.BlockSpec((1, ROW_TILE, LANES), lambda p, i, b: (b, CTX_TILE, p)),
                  pl.BlockSpec((1, 2, ROW_TILE, NA_BAND_TILES * ROW_TILE),
                               lambda p, i, b: (_na_variant(i), p, 0, 0))],
        out_specs=pl.BlockSpec((1, ROW_TILE, LANES), lambda p, i, b: (b, i, p)),
        compiler_params=_params(3),
        name="na_attention",
    )(q, kt, kt, kt, kt, v, v, v, v, bias)


def _na_bias_table(rpb):
    rows = SEQ // GRID_W
    band_rows = NA_BAND_TILES * NA_Q_ROWS
    col = np.arange(GRID_W)
    cstart = np.clip(col - NA_COLS // 2, 0, GRID_W - NA_COLS)
    ok_c = (col[None, :] >= cstart[:, None]) & (col[None, :] < cstart[:, None] + NA_COLS)
    dc = col[None, :] - col[:, None] + NA_COLS - 1
    onehot_c = ((dc[:, :, None] == np.arange(2 * NA_COLS - 1)) & ok_c[:, :, None]).astype(np.float32)
    dr = np.zeros((4, NA_Q_ROWS, band_rows), np.int32)
    ok_r = np.zeros((4, NA_Q_ROWS, band_rows), bool)
    for variant, i in enumerate((0, 1, SEQ // ROW_TILE - 1)):
        j = min(max(i - 1, 0), NA_LAST_BAND)
        r = NA_Q_ROWS * i + np.arange(NA_Q_ROWS)[:, None]
        k_row = NA_Q_ROWS * j + np.arange(band_rows)[None, :]
        rstart = np.clip(r - NA_ROWS // 2, 0, rows - NA_ROWS)
        ok_r[variant] = (k_row >= rstart) & (k_row < rstart + NA_ROWS)
        dr[variant] = np.where(ok_r[variant], k_row - r + NA_ROWS - 1, 0)
    by_col = jnp.einsum("hab,ckb->hack", rpb.astype(F32) * LOG2E, jnp.asarray(onehot_c),
                        precision=lax.Precision.HIGHEST)
    by_col = jnp.where(ok_c[None, None], by_col, MASK_VALUE)
    full = jnp.take(by_col, jnp.asarray(dr.reshape(-1)), axis=1)
    full = full.reshape(C_HEADS, 4, NA_Q_ROWS, band_rows, GRID_W, GRID_W)
    full = jnp.where(ok_r[None, :, :, :, None, None], full, MASK_VALUE)
    full = full.transpose(1, 0, 2, 4, 3, 5)
    return full.reshape(4, C_HEADS, ROW_TILE, NA_BAND_TILES * ROW_TILE)


def _post_kernel(o_ref, x_ref, mod_ref, g_ref, wo_ref, wgu_ref, wd_ref, xo_ref):
    m = mod_ref[0]
    y = jnp.dot(o_ref[0], wo_ref[...], preferred_element_type=F32)
    x1 = x_ref[0] + m[2:3] * _rms(y, g_ref[1:2])
    h2 = (_rms(x1, g_ref[2:3]) * (1.0 + m[4:5]) + m[3:4]).astype(BF16)
    gu = jnp.dot(h2, wgu_ref[...], preferred_element_type=F32)
    gate, up = gu[:, :FFN_HIDDEN], gu[:, FFN_HIDDEN:]
    a = (gate * _sigmoid(gate) * up).astype(BF16)
    f = jnp.dot(a, wd_ref[...], preferred_element_type=F32)
    xo_ref[0] = x1 + m[5:6] * _rms(f, g_ref[3:4])


def _post(o, x, mod, g_norm, wo, wgu, wd, *, n_tiles):
    return pl.pallas_call(
        _post_kernel,
        out_shape=jax.ShapeDtypeStruct((BATCH, n_tiles * ROW_TILE, D_MODEL), F32),
        grid=(BATCH, n_tiles),
        in_specs=[pl.BlockSpec((1, ROW_TILE, D_MODEL), lambda b, t: (b, t, 0))] + _row_specs() + [
            _resident((4, D_MODEL)), _resident(wo.shape), _resident(wgu.shape),
            _resident(wd.shape)],
        out_specs=pl.BlockSpec((1, ROW_TILE, D_MODEL), lambda b, t: (b, t, 0)),
        compiler_params=_params(2),
        name="post_attention",
    )(o, x, mod, g_norm, wo, wgu, wd)


def _rope_angles(rot_dim):
    n_freq = rot_dim // 4
    inv = ROPE_THETA ** (-jnp.arange(n_freq, dtype=F32) / n_freq)
    t = jnp.arange(SEQ, dtype=jnp.int32)
    row = (t // GRID_W).astype(F32)
    col = (t % GRID_W).astype(F32)
    ang = jnp.concatenate([row[:, None] * inv, col[:, None] * inv], axis=-1)
    cos = jnp.concatenate([jnp.cos(ang), jnp.ones((CTX_LEN, rot_dim // 2), F32)], axis=0)
    sin = jnp.concatenate([jnp.sin(ang), jnp.zeros((CTX_LEN, rot_dim // 2), F32)], axis=0)
    return cos, sin


def _rope_tables_diff():
    cos, sin = _rope_angles(A_HEAD_DIM)
    cq = jnp.tile(cos, (1, 4))
    sq = jnp.tile(jnp.concatenate([-sin, sin], axis=1), (1, 2))
    return cq, sq, cos.T, sin.T


def _rope_tables_mla():
    cos, sin = _rope_angles(B_ROPE)
    ones = jnp.ones((T_ALL, B_NOPE), F32)
    zeros = jnp.zeros((T_ALL, B_NOPE), F32)
    tail = LANES - B_NOPE - B_ROPE
    cq = jnp.concatenate([ones, cos, cos, ones[:, :tail]], axis=1)
    sq = jnp.concatenate([zeros, -sin, sin, zeros[:, :tail]], axis=1)
    return cq, sq, cos.T, sin.T


def kernel(x, c, ctx, c_ctx, l0_w_mod, l0_b_mod, l0_g_norm, l0_w_gu, l0_w_down, l0_a_w_qkv, l0_a_w_o, l0_a_lam, l0_a_g_sub, l1_w_mod, l1_b_mod, l1_g_norm, l1_w_gu, l1_w_down, l1_b_w_in, l1_b_g_q, l1_b_g_kv, l1_b_w_uq, l1_b_w_ukv, l1_b_w_o, l2_w_mod, l2_b_mod, l2_g_norm, l2_w_gu, l2_w_down, l2_c_w_qkv, l2_c_rpb, l2_c_w_o, l3_w_mod, l3_b_mod, l3_g_norm, l3_w_gu, l3_w_down, l3_a_w_qkv, l3_a_w_o, l3_a_lam, l3_a_g_sub):
    common = [
        (l0_w_mod, l0_b_mod, l0_g_norm, l0_w_gu, l0_w_down),
        (l1_w_mod, l1_b_mod, l1_g_norm, l1_w_gu, l1_w_down),
        (l2_w_mod, l2_b_mod, l2_g_norm, l2_w_gu, l2_w_down),
        (l3_w_mod, l3_b_mod, l3_g_norm, l3_w_gu, l3_w_down),
    ]
    diff_params = {0: (l0_a_w_qkv, l0_a_w_o, l0_a_lam, l0_a_g_sub),
                   3: (l3_a_w_qkv, l3_a_w_o, l3_a_lam, l3_a_g_sub)}

    xs = jnp.concatenate([x, ctx], axis=1)
    cc = jnp.concatenate([c, c_ctx[None, :], jnp.zeros((8 - BATCH - 1, D_MODEL), F32)], axis=0)
    tabs_a = _rope_tables_diff()
    tabs_b = _rope_tables_mla()
    dummy_lam = jnp.zeros((4, A_HEAD_DIM), F32)
    dummy_gsub = jnp.ones((1, LANES), F32)

    for i in range(DEPTH):
        w_mod, b_mod, g_norm, w_gu, w_down = common[i]
        last = i == DEPTH - 1
        mod = _modulation(cc, w_mod, b_mod).reshape(8, 6, D_MODEL)
        kind = i % 3
        if kind == 0:
            w_qkv, w_o, lam, g_sub = diff_params[i]
            lam_init = 0.8 - 0.6 * math.exp(-0.3 * i)
            wq = w_qkv[:, :D_MODEL].astype(BF16)
            wkt = w_qkv[:, D_MODEL:2 * D_MODEL].T.astype(BF16)
            wv = w_qkv[:, 2 * D_MODEL:].astype(BF16)
            q, kt, v = _proj_qkv(xs, mod, g_norm, wq, wkt, wv, tabs_a, rope=True,
                                 qscale=A_HEAD_DIM ** -0.5 * LOG2E)
            o = _flash(q, kt, v, lam, g_sub.reshape(1, LANES), diff=True, lam_init=lam_init)
        elif kind == 1:
            w_in, g_q, g_kv, w_uq, w_ukv, w_o = (l1_b_w_in, l1_b_g_q, l1_b_g_kv, l1_b_w_uq,
                                                 l1_b_w_ukv, l1_b_w_o)
            n_lora = B_Q_LORA + B_KV_LORA
            win = w_in[:, :n_lora].astype(BF16)
            wkrt = w_in[:, n_lora:].T.astype(BF16)
            wuq = w_uq.reshape(B_Q_LORA, B_HEADS, B_NOPE + B_ROPE)
            wuq = jnp.pad(wuq, ((0, 0), (0, 0), (0, LANES - B_NOPE - B_ROPE)))
            wuq = wuq.reshape(B_Q_LORA, B_HEADS * LANES).astype(BF16)
            wukv = w_ukv.reshape(B_KV_LORA, B_HEADS, B_NOPE + B_VDIM)
            wuknt = wukv[:, :, :B_NOPE].reshape(B_KV_LORA, B_HEADS * B_NOPE).T.astype(BF16)
            wuv = wukv[:, :, B_NOPE:].reshape(B_KV_LORA, B_HEADS * B_VDIM).astype(BF16)
            q, kt, v = _proj_mla(xs, mod, g_norm, win, wkrt, g_q.reshape(1, -1),
                                 g_kv.reshape(1, -1), wuq, wuknt, wuv, tabs_b,
                                 qscale=(B_NOPE + B_ROPE) ** -0.5 * LOG2E)
            o = _flash(q, kt, v, dummy_lam, dummy_gsub, diff=False, lam_init=0.0)
        else:
            w_qkv, rpb, w_o = l2_c_w_qkv, l2_c_rpb, l2_c_w_o
            wq = w_qkv[:, :D_MODEL].astype(BF16)
            wkt = w_qkv[:, D_MODEL:2 * D_MODEL].T.astype(BF16)
            wv = w_qkv[:, 2 * D_MODEL:].astype(BF16)
            q, kt, v = _proj_qkv(xs, mod, g_norm, wq, wkt, wv, tabs_a, rope=False,
                                 qscale=C_HEAD_DIM ** -0.5 * LOG2E)
            o = _na_attention(q, kt, v, _na_bias_table(rpb))
        xs = _post(o, xs, mod, g_norm, w_o.astype(BF16), w_gu.astype(BF16),
                   w_down.astype(BF16), n_tiles=N_TILES - 1 if last else N_TILES)
    return xs
```

```python
import functools
import math

import numpy as np
import jax
import jax.numpy as jnp
from jax import lax
from jax.experimental import pallas as pl
from jax.experimental.pallas import tpu as pltpu

D_MODEL = 1024
BATCH = 4
SEQ = 8192
DEPTH = 4
GRID_W = 64
CTX_LEN = 256
ROPE_THETA = 10000.0
NORM_EPS = 1e-6

A_HEAD_DIM = 64
A_HEADS = 8
B_HEADS = 16
B_NOPE = 64
B_ROPE = 32
B_VDIM = 64
B_Q_LORA = 256
B_KV_LORA = 256
C_HEAD_DIM = 64
C_HEADS = 16
NA_ROWS = 8
NA_COLS = 16
FFN_HIDDEN = 2816

LANES = 128
ROW_TILE = 256
T_ALL = SEQ + CTX_LEN
N_TILES = T_ALL // ROW_TILE
CTX_TILE = N_TILES - 1
N_PAIRS = 8
KEY_TILES = 3
N_GROUPS = N_TILES // KEY_TILES
V_ONES = 16
LOG2E = math.log2(math.e)
MASK_VALUE = -1e30
VMEM_LIMIT = 60 * 1024 * 1024

F32 = jnp.float32
BF16 = jnp.bfloat16
NT_DIMS = (((1,), (1,)), ((), ()))


def _rms(x, g):
    return x * lax.rsqrt(jnp.mean(x * x, axis=-1, keepdims=True) + NORM_EPS) * g


def _sigmoid(x):
    return 1.0 / (1.0 + jnp.exp(-x))


def _params(n_axes):
    return pltpu.CompilerParams(
        dimension_semantics=("arbitrary",) * n_axes, vmem_limit_bytes=VMEM_LIMIT)


def _resident(shape):
    zeros = (0,) * len(shape)
    return pl.BlockSpec(shape, lambda *_: zeros, pipeline_mode=pl.Buffered(1))


def _mod_row(b, t):
    return jnp.where(t == CTX_TILE, BATCH, b)


def _mod_kernel(c_ref, w_ref, b_ref, o_ref):
    c = c_ref[...]
    a = (c * _sigmoid(c)).astype(BF16)
    o_ref[...] = jnp.dot(a, w_ref[...].astype(BF16), preferred_element_type=F32) + b_ref[...]


def _modulation(cc, w_mod, b_mod):
    n = w_mod.shape[1]
    blk = D_MODEL
    return pl.pallas_call(
        _mod_kernel,
        out_shape=jax.ShapeDtypeStruct((8, n), F32),
        grid=(n // blk,),
        in_specs=[pl.BlockSpec((8, D_MODEL), lambda j: (0, 0)),
                  pl.BlockSpec((D_MODEL, blk), lambda j: (0, j)),
                  pl.BlockSpec((1, blk), lambda j: (0, j))],
        out_specs=pl.BlockSpec((8, blk), lambda j: (0, j)),
        compiler_params=_params(1),
        name="modulation",
    )(cc, w_mod, b_mod.reshape(1, n))


def _rope_lanes(k, c, s, first, half):
    rot = jnp.where(first, pltpu.roll(k, LANES - half, 1), pltpu.roll(k, half, 1))
    return k * c + rot * s


def _store_values(vt_ref, vt, rows):
    ones = jnp.ones((V_ONES, ROW_TILE), BF16)
    for n in range(vt.shape[0] // rows):
        r0 = n * (rows + V_ONES)
        vt_ref[0, 0, r0:r0 + rows, :] = vt[n * rows:(n + 1) * rows].astype(BF16)
        vt_ref[0, 0, r0 + rows:r0 + rows + V_ONES, :] = ones


def _proj_qkv_kernel(x_ref, mod_ref, g_ref, wqt_ref, wk_ref, wvt_ref,
                     ck_ref, sk_ref, cqt_ref, sqt_ref,
                     qt_ref, k_ref, vt_ref, *, rope, qscale, v_rows):
    m = mod_ref[0]
    h = (_rms(x_ref[0], g_ref[0:1]) * (1.0 + m[1:2]) + m[0:1]).astype(BF16)
    qt = lax.dot_general(wqt_ref[...], h, NT_DIMS, preferred_element_type=F32) * qscale
    k = jnp.dot(h, wk_ref[...], preferred_element_type=F32)
    vt = lax.dot_general(wvt_ref[...], h, NT_DIMS, preferred_element_type=F32)
    _store_values(vt_ref, vt, v_rows)
    if not rope:
        qt_ref[0, 0] = qt.astype(BF16)
        k_ref[0] = k.astype(BF16)
        return
    half = A_HEAD_DIM // 2
    c, s = ck_ref[...], sk_ref[...]
    lane = lax.broadcasted_iota(jnp.int32, (ROW_TILE, LANES), 1)
    first = (lane % A_HEAD_DIM) < half
    for p in range(D_MODEL // LANES):
        sl = slice(p * LANES, (p + 1) * LANES)
        k_ref[0, :, sl] = _rope_lanes(k[:, sl], c, s, first, half).astype(BF16)
    ct, st = cqt_ref[...], sqt_ref[...]
    for g in range(D_MODEL // A_HEAD_DIM):
        r0 = g * A_HEAD_DIM
        t1, t2 = qt[r0:r0 + half], qt[r0 + half:r0 + 2 * half]
        qt_ref[0, 0, r0:r0 + half, :] = (t1 * ct - t2 * st).astype(BF16)
        qt_ref[0, 0, r0 + half:r0 + 2 * half, :] = (t1 * st + t2 * ct).astype(BF16)


def _row_specs():
    return [pl.BlockSpec((1, ROW_TILE, D_MODEL), lambda b, t: (b, t, 0)),
            pl.BlockSpec((1, 6, D_MODEL), lambda b, t: (_mod_row(b, t), 0, 0))]


def _rope_specs(half):
    return [pl.BlockSpec((ROW_TILE, LANES), lambda b, t: (t, 0)),
            pl.BlockSpec((ROW_TILE, LANES), lambda b, t: (t, 0)),
            pl.BlockSpec((half, ROW_TILE), lambda b, t: (0, t)),
            pl.BlockSpec((half, ROW_TILE), lambda b, t: (0, t))]


def _qkv_out(width, v_rows):
    vt_rows = D_MODEL // v_rows * (v_rows + V_ONES)
    shapes = (jax.ShapeDtypeStruct((BATCH, N_TILES, width, ROW_TILE), BF16),
              jax.ShapeDtypeStruct((BATCH, T_ALL, width), BF16),
              jax.ShapeDtypeStruct((BATCH, N_TILES, vt_rows, ROW_TILE), BF16))
    specs = (pl.BlockSpec((1, 1, width, ROW_TILE), lambda b, t: (b, t, 0, 0)),
             pl.BlockSpec((1, ROW_TILE, width), lambda b, t: (b, t, 0)),
             pl.BlockSpec((1, 1, vt_rows, ROW_TILE), lambda b, t: (b, t, 0, 0)))
    return shapes, specs


def _proj_qkv(x, mod, g_norm, wqt, wk, wvt, tabs, *, rope, qscale, v_rows):
    shapes, specs = _qkv_out(D_MODEL, v_rows)
    return pl.pallas_call(
        functools.partial(_proj_qkv_kernel, rope=rope, qscale=qscale, v_rows=v_rows),
        out_shape=shapes,
        grid=(BATCH, N_TILES),
        in_specs=_row_specs() + [
            _resident((4, D_MODEL)),
            _resident((D_MODEL, D_MODEL)), _resident((D_MODEL, D_MODEL)),
            _resident((D_MODEL, D_MODEL))] + _rope_specs(tabs[2].shape[0]),
        out_specs=specs,
        compiler_params=_params(2),
        name="proj_qkv",
    )(x, mod, g_norm, wqt, wk, wvt, *tabs)


def _proj_mla_kernel(x_ref, mod_ref, g_ref, win_ref, wkr_ref, gq_ref, gkv_ref,
                     wuqt_ref, wukn_ref, wuvt_ref, ck_ref, sk_ref, cqt_ref, sqt_ref,
                     qt_ref, k_ref, vt_ref, *, qscale):
    m = mod_ref[0]
    h = (_rms(x_ref[0], g_ref[0:1]) * (1.0 + m[1:2]) + m[0:1]).astype(BF16)
    z = jnp.dot(h, win_ref[...], preferred_element_type=F32)
    cq = _rms(z[:, :B_Q_LORA], gq_ref[...]).astype(BF16)
    ckv = _rms(z[:, B_Q_LORA:], gkv_ref[...]).astype(BF16)
    half = B_ROPE // 2

    qt = lax.dot_general(wuqt_ref[...], cq, NT_DIMS, preferred_element_type=F32) * qscale
    ct, st = cqt_ref[...], sqt_ref[...]
    for hh in range(B_HEADS):
        r0, r1 = hh * LANES, hh * LANES + B_NOPE
        qt_ref[0, 0, r0:r1, :] = qt[r0:r1].astype(BF16)
        t1, t2 = qt[r1:r1 + half], qt[r1 + half:r1 + 2 * half]
        qt_ref[0, 0, r1:r1 + half, :] = (t1 * ct - t2 * st).astype(BF16)
        qt_ref[0, 0, r1 + half:r1 + 2 * half, :] = (t1 * st + t2 * ct).astype(BF16)
        qt_ref[0, 0, r1 + B_ROPE:r0 + LANES, :] = qt[r1 + B_ROPE:r0 + LANES].astype(BF16)

    kr = jnp.dot(h, wkr_ref[...], preferred_element_type=F32)
    lane = lax.broadcasted_iota(jnp.int32, (ROW_TILE, LANES), 1)
    first = (lane >= B_NOPE) & (lane < B_NOPE + half)
    kr = _rope_lanes(kr, ck_ref[...], sk_ref[...], first, half)
    kn = jnp.dot(ckv, wukn_ref[...], preferred_element_type=F32)
    for hh in range(B_HEADS):
        sl = slice(hh * LANES, (hh + 1) * LANES)
        k_ref[0, :, sl] = (kn[:, sl] + kr).astype(BF16)

    vt = lax.dot_general(wuvt_ref[...], ckv, NT_DIMS, preferred_element_type=F32)
    _store_values(vt_ref, vt, B_VDIM)


def _proj_mla(x, mod, g_norm, win, wkr, gq, gkv, wuqt, wukn, wuvt, tabs, *, qscale):
    shapes, specs = _qkv_out(B_HEADS * LANES, B_VDIM)
    return pl.pallas_call(
        functools.partial(_proj_mla_kernel, qscale=qscale),
        out_shape=shapes,
        grid=(BATCH, N_TILES),
        in_specs=_row_specs() + [
            _resident((4, D_MODEL)),
            _resident(win.shape), _resident(wkr.shape),
            _resident((1, B_Q_LORA)), _resident((1, B_KV_LORA)),
            _resident(wuqt.shape), _resident(wukn.shape), _resident(wuvt.shape)]
            + _rope_specs(tabs[2].shape[0]),
        out_specs=specs,
        compiler_params=_params(2),
        name="proj_mla",
    )(x, mod, g_norm, win, wkr, gq, gkv, wuqt, wukn, wuvt, *tabs)


def _value_rows(diff):
    if diff:
        rows = 2 * A_HEAD_DIM
        return ((0, rows + V_ONES), (0, rows + V_ONES)), rows
    rows = LANES // 2
    return ((0, rows + V_ONES), (rows + V_ONES, 2 * (rows + V_ONES))), rows


def _split_queries(qt, qs_ref):
    dq = qt.shape[0]
    row = lax.broadcasted_iota(jnp.int32, qt.shape, 0)
    zero = jnp.zeros_like(qt)
    qs_ref[0] = jnp.where(row < dq // 2, qt, zero)
    qs_ref[1] = jnp.where(row >= dq // 2, qt, zero)


def _probabilities(s, m):
    return jnp.exp2((s - m).astype(BF16))


def _flash_kernel(lam_ref, gsub_ref, qt_ref, k_ref, vt_ref, o_ref,
                  qs_ref, s0_ref, s1_ref, m_ref, acc_ref, *, diff, lam_init):
    qi = pl.program_id(2)
    _split_queries(qt_ref[0, 0], qs_ref)
    m_ref[...] = jnp.full(m_ref.shape, MASK_VALUE, F32)
    acc_ref[...] = jnp.zeros(acc_ref.shape, F32)
    v_rows, n_val = _value_rows(diff)

    def scores(tiles, s_ref):
        for slot in range(2):
            for j, t in enumerate(tiles):
                k = k_ref[0, pl.ds(pl.multiple_of(t * ROW_TILE, ROW_TILE), ROW_TILE), :]
                s_ref[slot, j] = jnp.dot(k, qs_ref[slot], preferred_element_type=F32)

    def update(tiles, s_ref):
        for slot in range(2):
            lo, hi = v_rows[slot]
            s = [s_ref[slot, j] for j in range(len(tiles))]
            m_prev = m_ref[slot]
            m_next = jnp.maximum(
                m_prev, jnp.max(functools.reduce(jnp.maximum, s), axis=0, keepdims=True))
            alpha = jnp.exp2(m_prev - m_next)
            pv = functools.reduce(jnp.add, [
                jnp.dot(vt_ref[0, t, lo:hi, :], _probabilities(sj, m_next),
                        preferred_element_type=F32) for t, sj in zip(tiles, s)])
            acc_ref[slot] = alpha * acc_ref[slot] + pv
            m_ref[slot] = m_next

    def group(g):
        return [g * KEY_TILES + j for j in range(KEY_TILES)]

    @pl.when(qi != CTX_TILE)
    def _():
        scores(group(0), s0_ref)

        def body(k, carry):
            scores(group(2 * k + 1), s1_ref)
            update(group(2 * k), s0_ref)
            scores(group(2 * k + 2), s0_ref)
            update(group(2 * k + 1), s1_ref)
            return carry
        lax.fori_loop(0, (N_GROUPS - 1) // 2, body, 0, unroll=True)
        update(group(N_GROUPS - 1), s0_ref)

    @pl.when(qi == CTX_TILE)
    def _():
        scores([CTX_TILE], s0_ref)
        update([CTX_TILE], s0_ref)

    acc_a, acc_b = acc_ref[0], acc_ref[1]
    ot_a = acc_a[:n_val] * (1.0 / acc_a[n_val:n_val + 1])
    ot_b = acc_b[:n_val] * (1.0 / acc_b[n_val:n_val + 1])
    if diff:
        lv = lam_ref[...]
        lam = (jnp.exp(jnp.sum(lv[0:1] * lv[1:2], axis=1, keepdims=True))
               - jnp.exp(jnp.sum(lv[2:3] * lv[3:4], axis=1, keepdims=True)) + lam_init)
        o = (ot_a - lam * ot_b).T
        o = _rms(o, gsub_ref[...]) * (1.0 - lam_init)
    else:
        o = jnp.concatenate([ot_a, ot_b], axis=0).T
    o_ref[0] = o.astype(BF16)


def _flash(qt, k, vt, lam, gsub, *, diff, lam_init):
    dq = k.shape[-1] // N_PAIRS
    vt_rows = vt.shape[2] // N_PAIRS
    acc_rows = _value_rows(diff)[0][0][1]
    return pl.pallas_call(
        functools.partial(_flash_kernel, diff=diff, lam_init=lam_init),
        out_shape=jax.ShapeDtypeStruct((BATCH, T_ALL, D_MODEL), BF16),
        grid=(BATCH, N_PAIRS, N_TILES),
        in_specs=[pl.BlockSpec(lam.shape, lambda b, p, i: (0, 0)),
                  pl.BlockSpec(gsub.shape, lambda b, p, i: (0, 0)),
                  pl.BlockSpec((1, 1, dq, ROW_TILE), lambda b, p, i: (b, i, p, 0)),
                  pl.BlockSpec((1, T_ALL, dq), lambda b, p, i: (b, 0, p)),
                  pl.BlockSpec((1, N_TILES, vt_rows, ROW_TILE), lambda b, p, i: (b, 0, p, 0))],
        out_specs=pl.BlockSpec((1, ROW_TILE, LANES), lambda b, p, i: (b, i, p)),
        scratch_shapes=[pltpu.VMEM((2, dq, ROW_TILE), BF16),
                        pltpu.VMEM((2, KEY_TILES, ROW_TILE, ROW_TILE), F32),
                        pltpu.VMEM((2, KEY_TILES, ROW_TILE, ROW_TILE), F32),
                        pltpu.VMEM((2, 1, ROW_TILE), F32),
                        pltpu.VMEM((2, acc_rows, ROW_TILE), F32)],
        compiler_params=_params(3),
        name="flash_diff" if diff else "flash_pair",
    )(lam, gsub, qt, k, vt)


NA_Q_ROWS = ROW_TILE // GRID_W
NA_BAND_TILES = 3
NA_LAST_BAND = SEQ // ROW_TILE - NA_BAND_TILES


def _na_band_start(i):
    return jnp.clip(i - 1, 0, NA_LAST_BAND)


def _na_kernel(qt_ref, k0_ref, k1_ref, k2_ref, kc_ref, v0_ref, v1_ref, v2_ref, vc_ref,
               bias_ref, o_ref, qs_ref):
    _split_queries(qt_ref[0, 0], qs_ref)
    k_refs = (k0_ref, k1_ref, k2_ref)
    v_refs = (v0_ref, v1_ref, v2_ref, vc_ref)
    v_rows, n_val = _value_rows(False)
    outs = []
    for slot in range(2):
        lo, hi = v_rows[slot]
        qs = qs_ref[slot]
        s = [jnp.dot(k_refs[j][0], qs, preferred_element_type=F32)
             + bias_ref[0, slot, j * ROW_TILE:(j + 1) * ROW_TILE, :]
             for j in range(NA_BAND_TILES)]
        s.append(jnp.dot(kc_ref[0], qs, preferred_element_type=F32))
        m = jnp.max(functools.reduce(jnp.maximum, s), axis=0, keepdims=True)
        acc = functools.reduce(jnp.add, [
            jnp.dot(r[0, 0, lo:hi, :], _probabilities(sj, m), preferred_element_type=F32)
            for r, sj in zip(v_refs, s)])
        outs.append(acc[:n_val] * (1.0 / acc[n_val:n_val + 1]))
    o_ref[0] = jnp.concatenate(outs, axis=0).T.astype(BF16)


def _na_variant(i):
    return jnp.where(i == 0, 0, jnp.where(i < NA_LAST_BAND + NA_BAND_TILES - 1, 1,
                                          jnp.where(i < CTX_TILE, 2, 3)))


def _na_attention(qt, k, vt, bias):
    vt_rows = vt.shape[2] // N_PAIRS

    def kspec(j):
        return pl.BlockSpec((1, ROW_TILE, LANES), lambda p, i, b: (b, _na_band_start(i) + j, p))

    def vspec(j):
        return pl.BlockSpec((1, 1, vt_rows, ROW_TILE),
                            lambda p, i, b: (b, _na_band_start(i) + j, p, 0))

    return pl.pallas_call(
        _na_kernel,
        out_shape=jax.ShapeDtypeStruct((BATCH, T_ALL, D_MODEL), BF16),
        grid=(N_PAIRS, N_TILES, BATCH),
        in_specs=[pl.BlockSpec((1, 1, LANES, ROW_TILE), lambda p, i, b: (b, i, p, 0)),
                  kspec(0), kspec(1), kspec(2),
                  pl.BlockSpec((1, ROW_TILE, LANES), lambda p, i, b: (b, CTX_TILE, p)),
                  vspec(0), vspec(1), vspec(2),
                  pl.BlockSpec((1, 1, vt_rows, ROW_TILE), lambda p, i, b: (b, CTX_TILE, p, 0)),
                  pl.BlockSpec((1, 2, NA_BAND_TILES * ROW_TILE, ROW_TILE),
                               lambda p, i, b: (_na_variant(i), p, 0, 0))],
        out_specs=pl.BlockSpec((1, ROW_TILE, LANES), lambda p, i, b: (b, i, p)),
        scratch_shapes=[pltpu.VMEM((2, LANES, ROW_TILE), BF16)],
        compiler_params=_params(3),
        name="na_attention",
    )(qt, k, k, k, k, vt, vt, vt, vt, bias)


def _na_bias_table(rpb):
    rows = SEQ // GRID_W
    band_rows = NA_BAND_TILES * NA_Q_ROWS
    col = np.arange(GRID_W)
    cstart = np.clip(col - NA_COLS // 2, 0, GRID_W - NA_COLS)
    ok_c = (col[None, :] >= cstart[:, None]) & (col[None, :] < cstart[:, None] + NA_COLS)
    dc = col[None, :] - col[:, None] + NA_COLS - 1
    onehot_c = ((dc[:, :, None] == np.arange(2 * NA_COLS - 1)) & ok_c[:, :, None]).astype(np.float32)
    dr = np.zeros((4, NA_Q_ROWS, band_rows), np.int32)
    ok_r = np.zeros((4, NA_Q_ROWS, band_rows), bool)
    for variant, i in enumerate((0, 1, SEQ // ROW_TILE - 1)):
        j = min(max(i - 1, 0), NA_LAST_BAND)
        r = NA_Q_ROWS * i + np.arange(NA_Q_ROWS)[:, None]
        k_row = NA_Q_ROWS * j + np.arange(band_rows)[None, :]
        rstart = np.clip(r - NA_ROWS // 2, 0, rows - NA_ROWS)
        ok_r[variant] = (k_row >= rstart) & (k_row < rstart + NA_ROWS)
        dr[variant] = np.where(ok_r[variant], k_row - r + NA_ROWS - 1, 0)
    by_col = jnp.einsum("hab,ckb->hack", rpb.astype(F32) * LOG2E, jnp.asarray(onehot_c),
                        precision=lax.Precision.HIGHEST)
    by_col = jnp.where(ok_c[None, None], by_col, MASK_VALUE)
    full = jnp.take(by_col, jnp.asarray(dr.reshape(-1)), axis=1)
    full = full.reshape(C_HEADS, 4, NA_Q_ROWS, band_rows, GRID_W, GRID_W)
    full = jnp.where(ok_r[None, :, :, :, None, None], full, MASK_VALUE)
    full = full.transpose(1, 0, 3, 5, 2, 4)
    return full.reshape(4, C_HEADS, NA_BAND_TILES * ROW_TILE, ROW_TILE)


def _post_kernel(o_ref, x_ref, mod_ref, g_ref, wo_ref, wgu_ref, wd_ref, xo_ref):
    m = mod_ref[0]
    y = jnp.dot(o_ref[0], wo_ref[...], preferred_element_type=F32)
    x1 = x_ref[0] + m[2:3] * _rms(y, g_ref[1:2])
    h2 = (_rms(x1, g_ref[2:3]) * (1.0 + m[4:5]) + m[3:4]).astype(BF16)
    gu = jnp.dot(h2, wgu_ref[...], preferred_element_type=F32)
    gate, up = gu[:, :FFN_HIDDEN], gu[:, FFN_HIDDEN:]
    a = (gate * _sigmoid(gate) * up).astype(BF16)
    f = jnp.dot(a, wd_ref[...], preferred_element_type=F32)
    xo_ref[0] = x1 + m[5:6] * _rms(f, g_ref[3:4])


def _post(o, x, mod, g_norm, wo, wgu, wd, *, n_tiles):
    return pl.pallas_call(
        _post_kernel,
        out_shape=jax.ShapeDtypeStruct((BATCH, n_tiles * ROW_TILE, D_MODEL), F32),
        grid=(BATCH, n_tiles),
        in_specs=[pl.BlockSpec((1, ROW_TILE, D_MODEL), lambda b, t: (b, t, 0))] + _row_specs() + [
            _resident((4, D_MODEL)), _resident(wo.shape), _resident(wgu.shape),
            _resident(wd.shape)],
        out_specs=pl.BlockSpec((1, ROW_TILE, D_MODEL), lambda b, t: (b, t, 0)),
        compiler_params=_params(2),
        name="post_attention",
    )(o, x, mod, g_norm, wo, wgu, wd)


def _rope_angles(rot_dim):
    n_freq = rot_dim // 4
    inv = ROPE_THETA ** (-jnp.arange(n_freq, dtype=F32) / n_freq)
    t = jnp.arange(SEQ, dtype=jnp.int32)
    row = (t // GRID_W).astype(F32)
    col = (t % GRID_W).astype(F32)
    ang = jnp.concatenate([row[:, None] * inv, col[:, None] * inv], axis=-1)
    cos = jnp.concatenate([jnp.cos(ang), jnp.ones((CTX_LEN, rot_dim // 2), F32)], axis=0)
    sin = jnp.concatenate([jnp.sin(ang), jnp.zeros((CTX_LEN, rot_dim // 2), F32)], axis=0)
    return cos, sin


def _rope_tables_diff():
    cos, sin = _rope_angles(A_HEAD_DIM)
    c_lane = jnp.tile(cos, (1, 4))
    s_lane = jnp.tile(jnp.concatenate([-sin, sin], axis=1), (1, 2))
    return c_lane, s_lane, cos.T, sin.T


def _rope_tables_mla():
    cos, sin = _rope_angles(B_ROPE)
    ones = jnp.ones((T_ALL, B_NOPE), F32)
    zeros = jnp.zeros((T_ALL, B_NOPE), F32)
    tail = LANES - B_NOPE - B_ROPE
    c_lane = jnp.concatenate([ones, cos, cos, ones[:, :tail]], axis=1)
    s_lane = jnp.concatenate([zeros, -sin, sin, zeros[:, :tail]], axis=1)
    return c_lane, s_lane, cos.T, sin.T


def kernel(x, c, ctx, c_ctx, l0_w_mod, l0_b_mod, l0_g_norm, l0_w_gu, l0_w_down, l0_a_w_qkv, l0_a_w_o, l0_a_lam, l0_a_g_sub, l1_w_mod, l1_b_mod, l1_g_norm, l1_w_gu, l1_w_down, l1_b_w_in, l1_b_g_q, l1_b_g_kv, l1_b_w_uq, l1_b_w_ukv, l1_b_w_o, l2_w_mod, l2_b_mod, l2_g_norm, l2_w_gu, l2_w_down, l2_c_w_qkv, l2_c_rpb, l2_c_w_o, l3_w_mod, l3_b_mod, l3_g_norm, l3_w_gu, l3_w_down, l3_a_w_qkv, l3_a_w_o, l3_a_lam, l3_a_g_sub):
    common = [
        (l0_w_mod, l0_b_mod, l0_g_norm, l0_w_gu, l0_w_down),
        (l1_w_mod, l1_b_mod, l1_g_norm, l1_w_gu, l1_w_down),
        (l2_w_mod, l2_b_mod, l2_g_norm, l2_w_gu, l2_w_down),
        (l3_w_mod, l3_b_mod, l3_g_norm, l3_w_gu, l3_w_down),
    ]
    diff_params = {0: (l0_a_w_qkv, l0_a_w_o, l0_a_lam, l0_a_g_sub),
                   3: (l3_a_w_qkv, l3_a_w_o, l3_a_lam, l3_a_g_sub)}

    xs = jnp.concatenate([x, ctx], axis=1)
    cc = jnp.concatenate([c, c_ctx[None, :], jnp.zeros((8 - BATCH - 1, D_MODEL), F32)], axis=0)
    tabs_a = _rope_tables_diff()
    tabs_b = _rope_tables_mla()
    dummy_lam = jnp.zeros((4, A_HEAD_DIM), F32)
    dummy_gsub = jnp.ones((1, LANES), F32)

    for i in range(DEPTH):
        w_mod, b_mod, g_norm, w_gu, w_down = common[i]
        last = i == DEPTH - 1
        mod = _modulation(cc, w_mod, b_mod).reshape(8, 6, D_MODEL)
        kind = i % 3
        if kind == 0:
            w_qkv, w_o, lam, g_sub = diff_params[i]
            lam_init = 0.8 - 0.6 * math.exp(-0.3 * i)
            wqt = w_qkv[:, :D_MODEL].T.astype(BF16)
            wk = w_qkv[:, D_MODEL:2 * D_MODEL].astype(BF16)
            wvt = w_qkv[:, 2 * D_MODEL:].T.astype(BF16)
            qt, k, vt = _proj_qkv(xs, mod, g_norm, wqt, wk, wvt, tabs_a, rope=True,
                                  qscale=A_HEAD_DIM ** -0.5 * LOG2E, v_rows=2 * A_HEAD_DIM)
            o = _flash(qt, k, vt, lam, g_sub.reshape(1, LANES), diff=True, lam_init=lam_init)
        elif kind == 1:
            w_in, g_q, g_kv, w_uq, w_ukv, w_o = (l1_b_w_in, l1_b_g_q, l1_b_g_kv, l1_b_w_uq,
                                                 l1_b_w_ukv, l1_b_w_o)
            n_lora = B_Q_LORA + B_KV_LORA
            pad = LANES - B_NOPE - B_ROPE
            win = w_in[:, :n_lora].astype(BF16)
            wkr = jnp.pad(w_in[:, n_lora:], ((0, 0), (B_NOPE, pad))).astype(BF16)
            wuq = w_uq.reshape(B_Q_LORA, B_HEADS, B_NOPE + B_ROPE)
            wuq = jnp.pad(wuq, ((0, 0), (0, 0), (0, pad)))
            wuqt = wuq.reshape(B_Q_LORA, B_HEADS * LANES).T.astype(BF16)
            wukv = w_ukv.reshape(B_KV_LORA, B_HEADS, B_NOPE + B_VDIM)
            wukn = jnp.pad(wukv[:, :, :B_NOPE], ((0, 0), (0, 0), (0, LANES - B_NOPE)))
            wukn = wukn.reshape(B_KV_LORA, B_HEADS * LANES).astype(BF16)
            wuvt = wukv[:, :, B_NOPE:].reshape(B_KV_LORA, B_HEADS * B_VDIM).T.astype(BF16)
            qt, k, vt = _proj_mla(xs, mod, g_norm, win, wkr, g_q.reshape(1, -1),
                                  g_kv.reshape(1, -1), wuqt, wukn, wuvt, tabs_b,
                                  qscale=(B_NOPE + B_ROPE) ** -0.5 * LOG2E)
            o = _flash(qt, k, vt, dummy_lam, dummy_gsub, diff=False, lam_init=0.0)
        else:
            w_qkv, rpb, w_o = l2_c_w_qkv, l2_c_rpb, l2_c_w_o
            wqt = w_qkv[:, :D_MODEL].T.astype(BF16)
            wk = w_qkv[:, D_MODEL:2 * D_MODEL].astype(BF16)
            wvt = w_qkv[:, 2 * D_MODEL:].T.astype(BF16)
            qt, k, vt = _proj_qkv(xs, mod, g_norm, wqt, wk, wvt, tabs_a, rope=False,
                                  qscale=C_HEAD_DIM ** -0.5 * LOG2E, v_rows=C_HEAD_DIM)
            o = _na_attention(qt, k, vt, _na_bias_table(rpb))
        xs = _post(o, xs, mod, g_norm, w_o.astype(BF16), w_gu.astype(BF16),
                   w_down.astype(BF16), n_tiles=N_TILES - 1 if last else N_TILES)
    return xs
```

```python
import functools
import math

import numpy as np
import jax
import jax.numpy as jnp
from jax import lax
from jax.experimental import pallas as pl
from jax.experimental.pallas import tpu as pltpu

D_MODEL = 1024
BATCH = 4
SEQ = 8192
DEPTH = 4
GRID_W = 64
CTX_LEN = 256
ROPE_THETA = 10000.0
NORM_EPS = 1e-6

A_HEAD_DIM = 64
A_HEADS = 8
B_HEADS = 16
B_NOPE = 64
B_ROPE = 32
B_VDIM = 64
B_Q_LORA = 256
B_KV_LORA = 256
C_HEAD_DIM = 64
C_HEADS = 16
NA_ROWS = 8
NA_COLS = 16
FFN_HIDDEN = 2816

LANES = 128
SUBLANES = 8
STRIP = 64
ROW_TILE = 256
T_ALL = SEQ + CTX_LEN
N_TILES = T_ALL // ROW_TILE
CTX_TILE = N_TILES - 1
N_PAIRS = 8
KEY_GROUPS = tuple((3 * g, 3) for g in range(9)) + tuple((27 + 2 * g, 2) for g in range(3))
V_ONES = 16
LOG2E = math.log2(math.e)
MASK_VALUE = -1e30
VMEM_LIMIT = 60 * 1024 * 1024

F32 = jnp.float32
BF16 = jnp.bfloat16
NT_DIMS = (((1,), (1,)), ((), ()))


def _rms(x, g):
    return x * lax.rsqrt(jnp.mean(x * x, axis=-1, keepdims=True) + NORM_EPS) * g


def _sigmoid(x):
    return 1.0 / (1.0 + jnp.exp(-x))


def _params(n_axes):
    return pltpu.CompilerParams(
        dimension_semantics=("arbitrary",) * n_axes, vmem_limit_bytes=VMEM_LIMIT)


def _resident(shape):
    zeros = (0,) * len(shape)
    return pl.BlockSpec(shape, lambda *_: zeros, pipeline_mode=pl.Buffered(1))


def _mod_row(b, t):
    return jnp.where(t == CTX_TILE, BATCH, b)


def _mod_kernel(c_ref, w_ref, b_ref, o_ref):
    c = c_ref[...]
    a = (c * _sigmoid(c)).astype(BF16)
    o_ref[...] = jnp.dot(a, w_ref[...].astype(BF16), preferred_element_type=F32) + b_ref[...]


def _modulation(cc, w_mod, b_mod):
    n = w_mod.shape[1]
    blk = D_MODEL
    return pl.pallas_call(
        _mod_kernel,
        out_shape=jax.ShapeDtypeStruct((8, n), F32),
        grid=(n // blk,),
        in_specs=[pl.BlockSpec((8, D_MODEL), lambda j: (0, 0)),
                  pl.BlockSpec((D_MODEL, blk), lambda j: (0, j)),
                  pl.BlockSpec((1, blk), lambda j: (0, j))],
        out_specs=pl.BlockSpec((8, blk), lambda j: (0, j)),
        compiler_params=_params(1),
        name="modulation",
    )(cc, w_mod, b_mod.reshape(1, n))


def _rope_lanes(k, c, s, first, half):
    rot = jnp.where(first, pltpu.roll(k, LANES - half, 1), pltpu.roll(k, half, 1))
    return k * c + rot * s


def _store_values(vt_ref, vt, rows):
    ones = jnp.ones((V_ONES, ROW_TILE), BF16)
    for n in range(vt.shape[0] // rows):
        r0 = n * (rows + V_ONES)
        vt_ref[0, 0, r0:r0 + rows, :] = vt[n * rows:(n + 1) * rows].astype(BF16)
        vt_ref[0, 0, r0 + rows:r0 + rows + V_ONES, :] = ones


def _proj_qkv_kernel(x_ref, mod_ref, g_ref, wqt_ref, wk_ref, wvt_ref,
                     ck_ref, sk_ref, cqt_ref, sqt_ref,
                     qt_ref, k_ref, vt_ref, *, rope, qscale, v_rows):
    m = mod_ref[0]
    h = (_rms(x_ref[0], g_ref[0:1]) * (1.0 + m[1:2]) + m[0:1]).astype(BF16)
    qt = lax.dot_general(wqt_ref[...], h, NT_DIMS, preferred_element_type=F32) * qscale
    k = jnp.dot(h, wk_ref[...], preferred_element_type=F32)
    vt = lax.dot_general(wvt_ref[...], h, NT_DIMS, preferred_element_type=F32)
    _store_values(vt_ref, vt, v_rows)
    if not rope:
        qt_ref[0, 0] = qt.astype(BF16)
        k_ref[0] = k.astype(BF16)
        return
    half = A_HEAD_DIM // 2
    c, s = ck_ref[...], sk_ref[...]
    lane = lax.broadcasted_iota(jnp.int32, (ROW_TILE, LANES), 1)
    first = (lane % A_HEAD_DIM) < half
    for p in range(D_MODEL // LANES):
        sl = slice(p * LANES, (p + 1) * LANES)
        k_ref[0, :, sl] = _rope_lanes(k[:, sl], c, s, first, half).astype(BF16)
    ct, st = cqt_ref[...], sqt_ref[...]
    for g in range(D_MODEL // A_HEAD_DIM):
        r0 = g * A_HEAD_DIM
        t1, t2 = qt[r0:r0 + half], qt[r0 + half:r0 + 2 * half]
        qt_ref[0, 0, r0:r0 + half, :] = (t1 * ct - t2 * st).astype(BF16)
        qt_ref[0, 0, r0 + half:r0 + 2 * half, :] = (t1 * st + t2 * ct).astype(BF16)


def _row_specs():
    return [pl.BlockSpec((1, ROW_TILE, D_MODEL), lambda b, t: (b, t, 0)),
            pl.BlockSpec((1, 6, D_MODEL), lambda b, t: (_mod_row(b, t), 0, 0))]


def _rope_specs(half):
    return [pl.BlockSpec((ROW_TILE, LANES), lambda b, t: (t, 0)),
            pl.BlockSpec((ROW_TILE, LANES), lambda b, t: (t, 0)),
            pl.BlockSpec((half, ROW_TILE), lambda b, t: (0, t)),
            pl.BlockSpec((half, ROW_TILE), lambda b, t: (0, t))]


def _qkv_out(width, v_rows):
    vt_rows = D_MODEL // v_rows * (v_rows + V_ONES)
    shapes = (jax.ShapeDtypeStruct((BATCH, N_TILES, width, ROW_TILE), BF16),
              jax.ShapeDtypeStruct((BATCH, T_ALL, width), BF16),
              jax.ShapeDtypeStruct((BATCH, N_TILES, vt_rows, ROW_TILE), BF16))
    specs = (pl.BlockSpec((1, 1, width, ROW_TILE), lambda b, t: (b, t, 0, 0)),
             pl.BlockSpec((1, ROW_TILE, width), lambda b, t: (b, t, 0)),
             pl.BlockSpec((1, 1, vt_rows, ROW_TILE), lambda b, t: (b, t, 0, 0)))
    return shapes, specs


def _proj_qkv(x, mod, g_norm, wqt, wk, wvt, tabs, *, rope, qscale, v_rows):
    shapes, specs = _qkv_out(D_MODEL, v_rows)
    return pl.pallas_call(
        functools.partial(_proj_qkv_kernel, rope=rope, qscale=qscale, v_rows=v_rows),
        out_shape=shapes,
        grid=(BATCH, N_TILES),
        in_specs=_row_specs() + [
            _resident((4, D_MODEL)),
            _resident((D_MODEL, D_MODEL)), _resident((D_MODEL, D_MODEL)),
            _resident((D_MODEL, D_MODEL))] + _rope_specs(tabs[2].shape[0]),
        out_specs=specs,
        compiler_params=_params(2),
        name="proj_qkv",
    )(x, mod, g_norm, wqt, wk, wvt, *tabs)


def _proj_mla_kernel(x_ref, mod_ref, g_ref, win_ref, wkr_ref, gq_ref, gkv_ref,
                     wuqt_ref, wukn_ref, wuvt_ref, ck_ref, sk_ref, cqt_ref, sqt_ref,
                     qt_ref, k_ref, vt_ref, *, qscale):
    m = mod_ref[0]
    h = (_rms(x_ref[0], g_ref[0:1]) * (1.0 + m[1:2]) + m[0:1]).astype(BF16)
    z = jnp.dot(h, win_ref[...], preferred_element_type=F32)
    cq = _rms(z[:, :B_Q_LORA], gq_ref[...]).astype(BF16)
    ckv = _rms(z[:, B_Q_LORA:], gkv_ref[...]).astype(BF16)
    half = B_ROPE // 2

    qt = lax.dot_general(wuqt_ref[...], cq, NT_DIMS, preferred_element_type=F32) * qscale
    ct, st = cqt_ref[...], sqt_ref[...]
    for hh in range(B_HEADS):
        r0, r1 = hh * LANES, hh * LANES + B_NOPE
        qt_ref[0, 0, r0:r1, :] = qt[r0:r1].astype(BF16)
        t1, t2 = qt[r1:r1 + half], qt[r1 + half:r1 + 2 * half]
        qt_ref[0, 0, r1:r1 + half, :] = (t1 * ct - t2 * st).astype(BF16)
        qt_ref[0, 0, r1 + half:r1 + 2 * half, :] = (t1 * st + t2 * ct).astype(BF16)
        qt_ref[0, 0, r1 + B_ROPE:r0 + LANES, :] = qt[r1 + B_ROPE:r0 + LANES].astype(BF16)

    kr = jnp.dot(h, wkr_ref[...], preferred_element_type=F32)
    lane = lax.broadcasted_iota(jnp.int32, (ROW_TILE, LANES), 1)
    first = (lane >= B_NOPE) & (lane < B_NOPE + half)
    kr = _rope_lanes(kr, ck_ref[...], sk_ref[...], first, half)
    kn = jnp.dot(ckv, wukn_ref[...], preferred_element_type=F32)
    for hh in range(B_HEADS):
        sl = slice(hh * LANES, (hh + 1) * LANES)
        k_ref[0, :, sl] = (kn[:, sl] + kr).astype(BF16)

    vt = lax.dot_general(wuvt_ref[...], ckv, NT_DIMS, preferred_element_type=F32)
    _store_values(vt_ref, vt, B_VDIM)


def _proj_mla(x, mod, g_norm, win, wkr, gq, gkv, wuqt, wukn, wuvt, tabs, *, qscale):
    shapes, specs = _qkv_out(B_HEADS * LANES, B_VDIM)
    return pl.pallas_call(
        functools.partial(_proj_mla_kernel, qscale=qscale),
        out_shape=shapes,
        grid=(BATCH, N_TILES),
        in_specs=_row_specs() + [
            _resident((4, D_MODEL)),
            _resident(win.shape), _resident(wkr.shape),
            _resident((1, B_Q_LORA)), _resident((1, B_KV_LORA)),
            _resident(wuqt.shape), _resident(wukn.shape), _resident(wuvt.shape)]
            + _rope_specs(tabs[2].shape[0]),
        out_specs=specs,
        compiler_params=_params(2),
        name="proj_mla",
    )(x, mod, g_norm, win, wkr, gq, gkv, wuqt, wukn, wuvt, *tabs)


def _value_rows(diff):
    if diff:
        rows = 2 * A_HEAD_DIM
        return ((0, rows + V_ONES), (0, rows + V_ONES)), rows
    rows = LANES // 2
    return ((0, rows + V_ONES), (rows + V_ONES, 2 * (rows + V_ONES))), rows


def _split_queries(qt, qs_ref):
    dq = qt.shape[0]
    row = lax.broadcasted_iota(jnp.int32, qt.shape, 0)
    zero = jnp.zeros_like(qt)
    qs_ref[0] = jnp.where(row < dq // 2, qt, zero)
    qs_ref[1] = jnp.where(row >= dq // 2, qt, zero)


def _probabilities(s, m):
    return jnp.exp2((s - m).astype(BF16))


def _flash_kernel(lam_ref, gsub_ref, qt_ref, k_ref, vt_ref, o_ref,
                  qs_ref, s0_ref, s1_ref, x0_ref, x1_ref, p0_ref, p1_ref, a0_ref, a1_ref,
                  m_ref, acc_ref, *, diff, lam_init):
    v_rows, n_val = _value_rows(diff)
    bufs = ((s0_ref, x0_ref, p0_ref, a0_ref), (s1_ref, x1_ref, p1_ref, a1_ref))
    n_groups = len(KEY_GROUPS)

    def tile_rows(j):
        return slice(j * ROW_TILE, (j + 1) * ROW_TILE)

    def scores_tile(t, j, s_ref, x_ref):
        k = k_ref[0, t * ROW_TILE:(t + 1) * ROW_TILE, :]
        for slot in range(2):
            s = jnp.dot(k, qs_ref[slot], preferred_element_type=F32)
            s_ref[slot, tile_rows(j), :] = s
            x_ref[slot, j:j + 1, :] = jnp.max(s, axis=0, keepdims=True)

    def stats(n, x_ref, a_ref, first):
        for slot in range(2):
            m_prev = jnp.full((1, ROW_TILE), MASK_VALUE, F32) if first else m_ref[slot]
            m_next = jnp.maximum(m_prev, jnp.max(x_ref[slot, 0:n, :], axis=0, keepdims=True))
            a_ref[slot] = jnp.exp2(m_prev - m_next)
            m_ref[slot] = m_next

    def probs_tile(j, s_ref, p_ref):
        for slot in range(2):
            m = m_ref[slot]
            for c in range(ROW_TILE // STRIP):
                rows = slice(j * ROW_TILE + c * STRIP, j * ROW_TILE + (c + 1) * STRIP)
                p_ref[slot, rows, :] = _probabilities(s_ref[slot, rows, :], m)

    def values_tile(t, j, p_ref, pv):
        for slot in range(2):
            lo, hi = v_rows[slot]
            pv[slot].append(jnp.dot(vt_ref[0, t, lo:hi, :], p_ref[slot, tile_rows(j), :],
                                    preferred_element_type=F32))

    def accumulate(pv, a_ref):
        for slot in range(2):
            acc_ref[slot] = a_ref[slot] * acc_ref[slot] + functools.reduce(jnp.add, pv[slot])

    def finalize(i):
        acc_a, acc_b = acc_ref[0], acc_ref[1]
        ot_a = acc_a[:n_val] * (1.0 / acc_a[n_val:n_val + 1])
        ot_b = acc_b[:n_val] * (1.0 / acc_b[n_val:n_val + 1])
        if diff:
            lv = lam_ref[...]
            lam = (jnp.exp(jnp.sum(lv[0:1] * lv[1:2], axis=1, keepdims=True))
                   - jnp.exp(jnp.sum(lv[2:3] * lv[3:4], axis=1, keepdims=True)) + lam_init)
            o = (ot_a - lam * ot_b).T
            o = _rms(o, gsub_ref[...]) * (1.0 - lam_init)
        else:
            o = jnp.concatenate([ot_a, ot_b], axis=0).T
        o_ref[0, pl.ds(pl.multiple_of(i * ROW_TILE, ROW_TILE), ROW_TILE), :] = o.astype(BF16)

    acc_ref[...] = jnp.zeros(acc_ref.shape, F32)

    _split_queries(qt_ref[0, 0], qs_ref)
    for j in range(KEY_GROUPS[0][1]):
        scores_tile(KEY_GROUPS[0][0] + j, j, s0_ref, x0_ref)
    stats(KEY_GROUPS[0][1], x0_ref, a0_ref, True)
    for j in range(KEY_GROUPS[1][1]):
        scores_tile(KEY_GROUPS[1][0] + j, j, s1_ref, x1_ref)
    for j in range(KEY_GROUPS[0][1]):
        probs_tile(j, s0_ref, p0_ref)

    def body(i, carry):
        for g in range(n_groups):
            s_ref, x_ref, p_ref, a_ref = bufs[g % 2]
            s_nxt, x_nxt, p_nxt, a_nxt = bufs[(g + 1) % 2]
            t_val, n_val_tiles = KEY_GROUPS[g]
            t_sco, n_sco = KEY_GROUPS[(g + 2) % n_groups]
            n_prob = KEY_GROUPS[(g + 1) % n_groups][1]
            stats(n_prob, x_nxt, a_nxt, g + 1 == n_groups)
            if g + 2 == n_groups:
                _split_queries(qt_ref[0, i + 1], qs_ref)
            pv = ([], [])
            for j in range(max(n_val_tiles, n_sco, n_prob)):
                if j < n_val_tiles:
                    values_tile(t_val + j, j, p_ref, pv)
                if j < n_sco:
                    scores_tile(t_sco + j, j, s_ref, x_ref)
                if j < n_prob:
                    probs_tile(j, s_nxt, p_nxt)
            accumulate(pv, a_ref)
        finalize(i)
        return carry
    lax.fori_loop(0, CTX_TILE, body, 0)

    scores_tile(CTX_TILE, 0, s0_ref, x0_ref)
    stats(1, x0_ref, a0_ref, True)
    probs_tile(0, s0_ref, p0_ref)
    pv = ([], [])
    values_tile(CTX_TILE, 0, p0_ref, pv)
    accumulate(pv, a0_ref)
    finalize(CTX_TILE)


def _flash(qt, k, vt, lam, gsub, *, diff, lam_init):
    dq = k.shape[-1] // N_PAIRS
    vt_rows = vt.shape[2] // N_PAIRS
    acc_rows = _value_rows(diff)[0][0][1]
    group_rows = max(n for _, n in KEY_GROUPS) * ROW_TILE
    return pl.pallas_call(
        functools.partial(_flash_kernel, diff=diff, lam_init=lam_init),
        out_shape=jax.ShapeDtypeStruct((BATCH, T_ALL, D_MODEL), BF16),
        grid=(BATCH, N_PAIRS),
        in_specs=[pl.BlockSpec(lam.shape, lambda b, p: (0, 0)),
                  pl.BlockSpec(gsub.shape, lambda b, p: (0, 0)),
                  pl.BlockSpec((1, N_TILES, dq, ROW_TILE), lambda b, p: (b, 0, p, 0)),
                  pl.BlockSpec((1, T_ALL, dq), lambda b, p: (b, 0, p)),
                  pl.BlockSpec((1, N_TILES, vt_rows, ROW_TILE), lambda b, p: (b, 0, p, 0))],
        out_specs=pl.BlockSpec((1, T_ALL, LANES), lambda b, p: (b, 0, p)),
        scratch_shapes=[pltpu.VMEM((2, dq, ROW_TILE), BF16),
                        pltpu.VMEM((2, group_rows, ROW_TILE), F32),
                        pltpu.VMEM((2, group_rows, ROW_TILE), F32),
                        pltpu.VMEM((2, SUBLANES, ROW_TILE), F32),
                        pltpu.VMEM((2, SUBLANES, ROW_TILE), F32),
                        pltpu.VMEM((2, group_rows, ROW_TILE), BF16),
                        pltpu.VMEM((2, group_rows, ROW_TILE), BF16),
                        pltpu.VMEM((2, 1, ROW_TILE), F32),
                        pltpu.VMEM((2, 1, ROW_TILE), F32),
                        pltpu.VMEM((2, 1, ROW_TILE), F32),
                        pltpu.VMEM((2, acc_rows, ROW_TILE), F32)],
        compiler_params=_params(2),
        name="flash_diff" if diff else "flash_pair",
    )(lam, gsub, qt, k, vt)


NA_Q_ROWS = ROW_TILE // GRID_W
NA_BAND_TILES = 3
NA_LAST_BAND = SEQ // ROW_TILE - NA_BAND_TILES


def _na_band_start(i):
    return jnp.clip(i - 1, 0, NA_LAST_BAND)


def _na_kernel(qt_ref, k0_ref, k1_ref, k2_ref, kc_ref, v0_ref, v1_ref, v2_ref, vc_ref,
               bias_ref, o_ref, qs_ref):
    _split_queries(qt_ref[0, 0], qs_ref)
    k_refs = (k0_ref, k1_ref, k2_ref)
    v_refs = (v0_ref, v1_ref, v2_ref, vc_ref)
    v_rows, n_val = _value_rows(False)
    outs = []
    for slot in range(2):
        lo, hi = v_rows[slot]
        qs = qs_ref[slot]
        s = [jnp.dot(k_refs[j][0], qs, preferred_element_type=F32)
             + bias_ref[0, slot, j * ROW_TILE:(j + 1) * ROW_TILE, :]
             for j in range(NA_BAND_TILES)]
        s.append(jnp.dot(kc_ref[0], qs, preferred_element_type=F32))
        m = jnp.max(functools.reduce(jnp.maximum, s), axis=0, keepdims=True)
        acc = functools.reduce(jnp.add, [
            jnp.dot(r[0, 0, lo:hi, :], _probabilities(sj, m), preferred_element_type=F32)
            for r, sj in zip(v_refs, s)])
        outs.append(acc[:n_val] * (1.0 / acc[n_val:n_val + 1]))
    o_ref[0] = jnp.concatenate(outs, axis=0).T.astype(BF16)


def _na_variant(i):
    return jnp.where(i == 0, 0, jnp.where(i < NA_LAST_BAND + NA_BAND_TILES - 1, 1,
                                          jnp.where(i < CTX_TILE, 2, 3)))


def _na_attention(qt, k, vt, bias):
    vt_rows = vt.shape[2] // N_PAIRS

    def kspec(j):
        return pl.BlockSpec((1, ROW_TILE, LANES), lambda p, i, b: (b, _na_band_start(i) + j, p))

    def vspec(j):
        return pl.BlockSpec((1, 1, vt_rows, ROW_TILE),
                            lambda p, i, b: (b, _na_band_start(i) + j, p, 0))

    return pl.pallas_call(
        _na_kernel,
        out_shape=jax.ShapeDtypeStruct((BATCH, T_ALL, D_MODEL), BF16),
        grid=(N_PAIRS, N_TILES, BATCH),
        in_specs=[pl.BlockSpec((1, 1, LANES, ROW_TILE), lambda p, i, b: (b, i, p, 0)),
                  kspec(0), kspec(1), kspec(2),
                  pl.BlockSpec((1, ROW_TILE, LANES), lambda p, i, b: (b, CTX_TILE, p)),
                  vspec(0), vspec(1), vspec(2),
                  pl.BlockSpec((1, 1, vt_rows, ROW_TILE), lambda p, i, b: (b, CTX_TILE, p, 0)),
                  pl.BlockSpec((1, 2, NA_BAND_TILES * ROW_TILE, ROW_TILE),
                               lambda p, i, b: (_na_variant(i), p, 0, 0))],
        out_specs=pl.BlockSpec((1, ROW_TILE, LANES), lambda p, i, b: (b, i, p)),
        scratch_shapes=[pltpu.VMEM((2, LANES, ROW_TILE), BF16)],
        compiler_params=_params(3),
        name="na_attention",
    )(qt, k, k, k, k, vt, vt, vt, vt, bias)


def _na_bias_table(rpb):
    rows = SEQ // GRID_W
    band_rows = NA_BAND_TILES * NA_Q_ROWS
    col = np.arange(GRID_W)
    cstart = np.clip(col - NA_COLS // 2, 0, GRID_W - NA_COLS)
    ok_c = (col[None, :] >= cstart[:, None]) & (col[None, :] < cstart[:, None] + NA_COLS)
    dc = col[None, :] - col[:, None] + NA_COLS - 1
    onehot_c = ((dc[:, :, None] == np.arange(2 * NA_COLS - 1)) & ok_c[:, :, None]).astype(np.float32)
    dr = np.zeros((4, NA_Q_ROWS, band_rows), np.int32)
    ok_r = np.zeros((4, NA_Q_ROWS, band_rows), bool)
    for variant, i in enumerate((0, 1, SEQ // ROW_TILE - 1)):
        j = min(max(i - 1, 0), NA_LAST_BAND)
        r = NA_Q_ROWS * i + np.arange(NA_Q_ROWS)[:, None]
        k_row = NA_Q_ROWS * j + np.arange(band_rows)[None, :]
        rstart = np.clip(r - NA_ROWS // 2, 0, rows - NA_ROWS)
        ok_r[variant] = (k_row >= rstart) & (k_row < rstart + NA_ROWS)
        dr[variant] = np.where(ok_r[variant], k_row - r + NA_ROWS - 1, 0)
    by_col = jnp.einsum("hab,ckb->hack", rpb.astype(F32) * LOG2E, jnp.asarray(onehot_c),
                        precision=lax.Precision.HIGHEST)
    by_col = jnp.where(ok_c[None, None], by_col, MASK_VALUE)
    full = jnp.take(by_col, jnp.asarray(dr.reshape(-1)), axis=1)
    full = full.reshape(C_HEADS, 4, NA_Q_ROWS, band_rows, GRID_W, GRID_W)
    full = jnp.where(ok_r[None, :, :, :, None, None], full, MASK_VALUE)
    full = full.transpose(1, 0, 3, 5, 2, 4)
    return full.reshape(4, C_HEADS, NA_BAND_TILES * ROW_TILE, ROW_TILE)


def _post_kernel(o_ref, x_ref, mod_ref, g_ref, wo_ref, wgu_ref, wd_ref, xo_ref):
    m = mod_ref[0]
    y = jnp.dot(o_ref[0], wo_ref[...], preferred_element_type=F32)
    x1 = x_ref[0] + m[2:3] * _rms(y, g_ref[1:2])
    h2 = (_rms(x1, g_ref[2:3]) * (1.0 + m[4:5]) + m[3:4]).astype(BF16)
    gu = jnp.dot(h2, wgu_ref[...], preferred_element_type=F32)
    gate, up = gu[:, :FFN_HIDDEN], gu[:, FFN_HIDDEN:]
    a = (gate * _sigmoid(gate) * up).astype(BF16)
    f = jnp.dot(a, wd_ref[...], preferred_element_type=F32)
    xo_ref[0] = x1 + m[5:6] * _rms(f, g_ref[3:4])


def _post(o, x, mod, g_norm, wo, wgu, wd, *, n_tiles):
    return pl.pallas_call(
        _post_kernel,
        out_shape=jax.ShapeDtypeStruct((BATCH, n_tiles * ROW_TILE, D_MODEL), F32),
        grid=(BATCH, n_tiles),
        in_specs=[pl.BlockSpec((1, ROW_TILE, D_MODEL), lambda b, t: (b, t, 0))] + _row_specs() + [
            _resident((4, D_MODEL)), _resident(wo.shape), _resident(wgu.shape),
            _resident(wd.shape)],
        out_specs=pl.BlockSpec((1, ROW_TILE, D_MODEL), lambda b, t: (b, t, 0)),
        compiler_params=_params(2),
        name="post_attention",
    )(o, x, mod, g_norm, wo, wgu, wd)


def _rope_angles(rot_dim):
    n_freq = rot_dim // 4
    inv = ROPE_THETA ** (-jnp.arange(n_freq, dtype=F32) / n_freq)
    t = jnp.arange(SEQ, dtype=jnp.int32)
    row = (t // GRID_W).astype(F32)
    col = (t % GRID_W).astype(F32)
    ang = jnp.concatenate([row[:, None] * inv, col[:, None] * inv], axis=-1)
    cos = jnp.concatenate([jnp.cos(ang), jnp.ones((CTX_LEN, rot_dim // 2), F32)], axis=0)
    sin = jnp.concatenate([jnp.sin(ang), jnp.zeros((CTX_LEN, rot_dim // 2), F32)], axis=0)
    return cos, sin


def _rope_tables_diff():
    cos, sin = _rope_angles(A_HEAD_DIM)
    c_lane = jnp.tile(cos, (1, 4))
    s_lane = jnp.tile(jnp.concatenate([-sin, sin], axis=1), (1, 2))
    return c_lane, s_lane, cos.T, sin.T


def _rope_tables_mla():
    cos, sin = _rope_angles(B_ROPE)
    ones = jnp.ones((T_ALL, B_NOPE), F32)
    zeros = jnp.zeros((T_ALL, B_NOPE), F32)
    tail = LANES - B_NOPE - B_ROPE
    c_lane = jnp.concatenate([ones, cos, cos, ones[:, :tail]], axis=1)
    s_lane = jnp.concatenate([zeros, -sin, sin, zeros[:, :tail]], axis=1)
    return c_lane, s_lane, cos.T, sin.T


def kernel(x, c, ctx, c_ctx, l0_w_mod, l0_b_mod, l0_g_norm, l0_w_gu, l0_w_down, l0_a_w_qkv, l0_a_w_o, l0_a_lam, l0_a_g_sub, l1_w_mod, l1_b_mod, l1_g_norm, l1_w_gu, l1_w_down, l1_b_w_in, l1_b_g_q, l1_b_g_kv, l1_b_w_uq, l1_b_w_ukv, l1_b_w_o, l2_w_mod, l2_b_mod, l2_g_norm, l2_w_gu, l2_w_down, l2_c_w_qkv, l2_c_rpb, l2_c_w_o, l3_w_mod, l3_b_mod, l3_g_norm, l3_w_gu, l3_w_down, l3_a_w_qkv, l3_a_w_o, l3_a_lam, l3_a_g_sub):
    common = [
        (l0_w_mod, l0_b_mod, l0_g_norm, l0_w_gu, l0_w_down),
        (l1_w_mod, l1_b_mod, l1_g_norm, l1_w_gu, l1_w_down),
        (l2_w_mod, l2_b_mod, l2_g_norm, l2_w_gu, l2_w_down),
        (l3_w_mod, l3_b_mod, l3_g_norm, l3_w_gu, l3_w_down),
    ]
    diff_params = {0: (l0_a_w_qkv, l0_a_w_o, l0_a_lam, l0_a_g_sub),
                   3: (l3_a_w_qkv, l3_a_w_o, l3_a_lam, l3_a_g_sub)}

    xs = jnp.concatenate([x, ctx], axis=1)
    cc = jnp.concatenate([c, c_ctx[None, :], jnp.zeros((8 - BATCH - 1, D_MODEL), F32)], axis=0)
    tabs_a = _rope_tables_diff()
    tabs_b = _rope_tables_mla()
    dummy_lam = jnp.zeros((4, A_HEAD_DIM), F32)
    dummy_gsub = jnp.ones((1, LANES), F32)

    for i in range(DEPTH):
        w_mod, b_mod, g_norm, w_gu, w_down = common[i]
        last = i == DEPTH - 1
        mod = _modulation(cc, w_mod, b_mod).reshape(8, 6, D_MODEL)
        kind = i % 3
        if kind == 0:
            w_qkv, w_o, lam, g_sub = diff_params[i]
            lam_init = 0.8 - 0.6 * math.exp(-0.3 * i)
            wqt = w_qkv[:, :D_MODEL].T.astype(BF16)
            wk = w_qkv[:, D_MODEL:2 * D_MODEL].astype(BF16)
            wvt = w_qkv[:, 2 * D_MODEL:].T.astype(BF16)
            qt, k, vt = _proj_qkv(xs, mod, g_norm, wqt, wk, wvt, tabs_a, rope=True,
                                  qscale=A_HEAD_DIM ** -0.5 * LOG2E, v_rows=2 * A_HEAD_DIM)
            o = _flash(qt, k, vt, lam, g_sub.reshape(1, LANES), diff=True, lam_init=lam_init)
        elif kind == 1:
            w_in, g_q, g_kv, w_uq, w_ukv, w_o = (l1_b_w_in, l1_b_g_q, l1_b_g_kv, l1_b_w_uq,
                                                 l1_b_w_ukv, l1_b_w_o)
            n_lora = B_Q_LORA + B_KV_LORA
            pad = LANES - B_NOPE - B_ROPE
            win = w_in[:, :n_lora].astype(BF16)
            wkr = jnp.pad(w_in[:, n_lora:], ((0, 0), (B_NOPE, pad))).astype(BF16)
            wuq = w_uq.reshape(B_Q_LORA, B_HEADS, B_NOPE + B_ROPE)
            wuq = jnp.pad(wuq, ((0, 0), (0, 0), (0, pad)))
            wuqt = wuq.reshape(B_Q_LORA, B_HEADS * LANES).T.astype(BF16)
            wukv = w_ukv.reshape(B_KV_LORA, B_HEADS, B_NOPE + B_VDIM)
            wukn = jnp.pad(wukv[:, :, :B_NOPE], ((0, 0), (0, 0), (0, LANES - B_NOPE)))
            wukn = wukn.reshape(B_KV_LORA, B_HEADS * LANES).astype(BF16)
            wuvt = wukv[:, :, B_NOPE:].reshape(B_KV_LORA, B_HEADS * B_VDIM).T.astype(BF16)
            qt, k, vt = _proj_mla(xs, mod, g_norm, win, wkr, g_q.reshape(1, -1),
                                  g_kv.reshape(1, -1), wuqt, wukn, wuvt, tabs_b,
                                  qscale=(B_NOPE + B_ROPE) ** -0.5 * LOG2E)
            o = _flash(qt, k, vt, dummy_lam, dummy_gsub, diff=False, lam_init=0.0)
        else:
            w_qkv, rpb, w_o = l2_c_w_qkv, l2_c_rpb, l2_c_w_o
            wqt = w_qkv[:, :D_MODEL].T.astype(BF16)
            wk = w_qkv[:, D_MODEL:2 * D_MODEL].astype(BF16)
            wvt = w_qkv[:, 2 * D_MODEL:].T.astype(BF16)
            qt, k, vt = _proj_qkv(xs, mod, g_norm, wqt, wk, wvt, tabs_a, rope=False,
                                  qscale=C_HEAD_DIM ** -0.5 * LOG2E, v_rows=C_HEAD_DIM)
            o = _na_attention(qt, k, vt, _na_bias_table(rpb))
        xs = _post(o, xs, mod, g_norm, w_o.astype(BF16), w_gu.astype(BF16),
                   w_down.astype(BF16), n_tiles=N_TILES - 1 if last else N_TILES)
    return xs
```

```python
import functools
import math

import numpy as np
import jax
import jax.numpy as jnp
from jax import lax
from jax.experimental import pallas as pl
from jax.experimental.pallas import tpu as pltpu

D_MODEL = 1024
BATCH = 4
SEQ = 8192
DEPTH = 4
GRID_W = 64
CTX_LEN = 256
ROPE_THETA = 10000.0
NORM_EPS = 1e-6

A_HEAD_DIM = 64
A_HEADS = 8
B_HEADS = 16
B_NOPE = 64
B_ROPE = 32
B_VDIM = 64
B_Q_LORA = 256
B_KV_LORA = 256
C_HEAD_DIM = 64
C_HEADS = 16
NA_ROWS = 8
NA_COLS = 16
FFN_HIDDEN = 2816

LANES = 128
STRIP = 64
ROW_TILE = 256
T_ALL = SEQ + CTX_LEN
N_TILES = T_ALL // ROW_TILE
CTX_TILE = N_TILES - 1
N_PAIRS = 8
KEY_GROUPS = tuple((3 * g, 3) for g in range(9)) + tuple((27 + 2 * g, 2) for g in range(3))
PROB_LEAD = 1
SCORE_LEAD = 2
V_ONES = 16
LOG2E = math.log2(math.e)
MASK_VALUE = -1e30
VMEM_LIMIT = 60 * 1024 * 1024

F32 = jnp.float32
BF16 = jnp.bfloat16
NT_DIMS = (((1,), (1,)), ((), ()))


def _rms(x, g):
    return x * lax.rsqrt(jnp.mean(x * x, axis=-1, keepdims=True) + NORM_EPS) * g


def _sigmoid(x):
    return 1.0 / (1.0 + jnp.exp(-x))


def _params(n_axes):
    return pltpu.CompilerParams(
        dimension_semantics=("arbitrary",) * n_axes, vmem_limit_bytes=VMEM_LIMIT)


def _resident(shape):
    zeros = (0,) * len(shape)
    return pl.BlockSpec(shape, lambda *_: zeros, pipeline_mode=pl.Buffered(1))


def _mod_row(b, t):
    return jnp.where(t == CTX_TILE, BATCH, b)


def _mod_kernel(c_ref, w_ref, b_ref, o_ref):
    c = c_ref[...]
    a = (c * _sigmoid(c)).astype(BF16)
    o_ref[...] = jnp.dot(a, w_ref[...].astype(BF16), preferred_element_type=F32) + b_ref[...]


def _modulation(cc, w_mod, b_mod):
    n = w_mod.shape[1]
    blk = D_MODEL
    return pl.pallas_call(
        _mod_kernel,
        out_shape=jax.ShapeDtypeStruct((8, n), F32),
        grid=(n // blk,),
        in_specs=[pl.BlockSpec((8, D_MODEL), lambda j: (0, 0)),
                  pl.BlockSpec((D_MODEL, blk), lambda j: (0, j)),
                  pl.BlockSpec((1, blk), lambda j: (0, j))],
        out_specs=pl.BlockSpec((8, blk), lambda j: (0, j)),
        compiler_params=_params(1),
        name="modulation",
    )(cc, w_mod, b_mod.reshape(1, n))


def _rope_lanes(k, c, s, first, half):
    rot = jnp.where(first, pltpu.roll(k, LANES - half, 1), pltpu.roll(k, half, 1))
    return k * c + rot * s


def _store_values(vt_ref, vt, rows):
    ones = jnp.ones((V_ONES, ROW_TILE), BF16)
    for n in range(vt.shape[0] // rows):
        r0 = n * (rows + V_ONES)
        vt_ref[0, 0, r0:r0 + rows, :] = vt[n * rows:(n + 1) * rows].astype(BF16)
        vt_ref[0, 0, r0 + rows:r0 + rows + V_ONES, :] = ones


def _proj_qkv_kernel(x_ref, mod_ref, g_ref, wqt_ref, wk_ref, wvt_ref,
                     ck_ref, sk_ref, cqt_ref, sqt_ref,
                     qt_ref, k_ref, vt_ref, *, rope, qscale, v_rows):
    m = mod_ref[0]
    h = (_rms(x_ref[0], g_ref[0:1]) * (1.0 + m[1:2]) + m[0:1]).astype(BF16)
    qt = lax.dot_general(wqt_ref[...], h, NT_DIMS, preferred_element_type=F32) * qscale
    k = jnp.dot(h, wk_ref[...], preferred_element_type=F32)
    vt = lax.dot_general(wvt_ref[...], h, NT_DIMS, preferred_element_type=F32)
    _store_values(vt_ref, vt, v_rows)
    if not rope:
        qt_ref[0, 0] = qt.astype(BF16)
        k_ref[0] = k.astype(BF16)
        return
    half = A_HEAD_DIM // 2
    c, s = ck_ref[...], sk_ref[...]
    lane = lax.broadcasted_iota(jnp.int32, (ROW_TILE, LANES), 1)
    first = (lane % A_HEAD_DIM) < half
    for p in range(D_MODEL // LANES):
        sl = slice(p * LANES, (p + 1) * LANES)
        k_ref[0, :, sl] = _rope_lanes(k[:, sl], c, s, first, half).astype(BF16)
    ct, st = cqt_ref[...], sqt_ref[...]
    for g in range(D_MODEL // A_HEAD_DIM):
        r0 = g * A_HEAD_DIM
        t1, t2 = qt[r0:r0 + half], qt[r0 + half:r0 + 2 * half]
        qt_ref[0, 0, r0:r0 + half, :] = (t1 * ct - t2 * st).astype(BF16)
        qt_ref[0, 0, r0 + half:r0 + 2 * half, :] = (t1 * st + t2 * ct).astype(BF16)


def _row_specs():
    return [pl.BlockSpec((1, ROW_TILE, D_MODEL), lambda b, t: (b, t, 0)),
            pl.BlockSpec((1, 6, D_MODEL), lambda b, t: (_mod_row(b, t), 0, 0))]


def _rope_specs(half):
    return [pl.BlockSpec((ROW_TILE, LANES), lambda b, t: (t, 0)),
            pl.BlockSpec((ROW_TILE, LANES), lambda b, t: (t, 0)),
            pl.BlockSpec((half, ROW_TILE), lambda b, t: (0, t)),
            pl.BlockSpec((half, ROW_TILE), lambda b, t: (0, t))]


def _qkv_out(width, v_rows):
    vt_rows = D_MODEL // v_rows * (v_rows + V_ONES)
    shapes = (jax.ShapeDtypeStruct((BATCH, N_TILES, width, ROW_TILE), BF16),
              jax.ShapeDtypeStruct((BATCH, T_ALL, width), BF16),
              jax.ShapeDtypeStruct((BATCH, N_TILES, vt_rows, ROW_TILE), BF16))
    specs = (pl.BlockSpec((1, 1, width, ROW_TILE), lambda b, t: (b, t, 0, 0)),
             pl.BlockSpec((1, ROW_TILE, width), lambda b, t: (b, t, 0)),
             pl.BlockSpec((1, 1, vt_rows, ROW_TILE), lambda b, t: (b, t, 0, 0)))
    return shapes, specs


def _proj_qkv(x, mod, g_norm, wqt, wk, wvt, tabs, *, rope, qscale, v_rows):
    shapes, specs = _qkv_out(D_MODEL, v_rows)
    return pl.pallas_call(
        functools.partial(_proj_qkv_kernel, rope=rope, qscale=qscale, v_rows=v_rows),
        out_shape=shapes,
        grid=(BATCH, N_TILES),
        in_specs=_row_specs() + [
            _resident((4, D_MODEL)),
            _resident((D_MODEL, D_MODEL)), _resident((D_MODEL, D_MODEL)),
            _resident((D_MODEL, D_MODEL))] + _rope_specs(tabs[2].shape[0]),
        out_specs=specs,
        compiler_params=_params(2),
        name="proj_qkv",
    )(x, mod, g_norm, wqt, wk, wvt, *tabs)


def _proj_mla_kernel(x_ref, mod_ref, g_ref, win_ref, wkr_ref, gq_ref, gkv_ref,
                     wuqt_ref, wukn_ref, wuvt_ref, ck_ref, sk_ref, cqt_ref, sqt_ref,
                     qt_ref, k_ref, vt_ref, *, qscale):
    m = mod_ref[0]
    h = (_rms(x_ref[0], g_ref[0:1]) * (1.0 + m[1:2]) + m[0:1]).astype(BF16)
    z = jnp.dot(h, win_ref[...], preferred_element_type=F32)
    cq = _rms(z[:, :B_Q_LORA], gq_ref[...]).astype(BF16)
    ckv = _rms(z[:, B_Q_LORA:], gkv_ref[...]).astype(BF16)
    half = B_ROPE // 2

    qt = lax.dot_general(wuqt_ref[...], cq, NT_DIMS, preferred_element_type=F32) * qscale
    ct, st = cqt_ref[...], sqt_ref[...]
    for hh in range(B_HEADS):
        r0, r1 = hh * LANES, hh * LANES + B_NOPE
        qt_ref[0, 0, r0:r1, :] = qt[r0:r1].astype(BF16)
        t1, t2 = qt[r1:r1 + half], qt[r1 + half:r1 + 2 * half]
        qt_ref[0, 0, r1:r1 + half, :] = (t1 * ct - t2 * st).astype(BF16)
        qt_ref[0, 0, r1 + half:r1 + 2 * half, :] = (t1 * st + t2 * ct).astype(BF16)
        qt_ref[0, 0, r1 + B_ROPE:r0 + LANES, :] = qt[r1 + B_ROPE:r0 + LANES].astype(BF16)

    kr = jnp.dot(h, wkr_ref[...], preferred_element_type=F32)
    lane = lax.broadcasted_iota(jnp.int32, (ROW_TILE, LANES), 1)
    first = (lane >= B_NOPE) & (lane < B_NOPE + half)
    kr = _rope_lanes(kr, ck_ref[...], sk_ref[...], first, half)
    kn = jnp.dot(ckv, wukn_ref[...], preferred_element_type=F32)
    for hh in range(B_HEADS):
        sl = slice(hh * LANES, (hh + 1) * LANES)
        k_ref[0, :, sl] = (kn[:, sl] + kr).astype(BF16)

    vt = lax.dot_general(wuvt_ref[...], ckv, NT_DIMS, preferred_element_type=F32)
    _store_values(vt_ref, vt, B_VDIM)


def _proj_mla(x, mod, g_norm, win, wkr, gq, gkv, wuqt, wukn, wuvt, tabs, *, qscale):
    shapes, specs = _qkv_out(B_HEADS * LANES, B_VDIM)
    return pl.pallas_call(
        functools.partial(_proj_mla_kernel, qscale=qscale),
        out_shape=shapes,
        grid=(BATCH, N_TILES),
        in_specs=_row_specs() + [
            _resident((4, D_MODEL)),
            _resident(win.shape), _resident(wkr.shape),
            _resident((1, B_Q_LORA)), _resident((1, B_KV_LORA)),
            _resident(wuqt.shape), _resident(wukn.shape), _resident(wuvt.shape)]
            + _rope_specs(tabs[2].shape[0]),
        out_specs=specs,
        compiler_params=_params(2),
        name="proj_mla",
    )(x, mod, g_norm, win, wkr, gq, gkv, wuqt, wukn, wuvt, *tabs)


def _value_rows(diff):
    if diff:
        rows = 2 * A_HEAD_DIM
        return ((0, rows + V_ONES), (0, rows + V_ONES)), rows
    rows = LANES // 2
    return ((0, rows + V_ONES), (rows + V_ONES, 2 * (rows + V_ONES))), rows


def _split_queries(qt, qs_ref):
    dq = qt.shape[0]
    row = lax.broadcasted_iota(jnp.int32, qt.shape, 0)
    zero = jnp.zeros_like(qt)
    qs_ref[0] = jnp.where(row < dq // 2, qt, zero)
    qs_ref[1] = jnp.where(row >= dq // 2, qt, zero)


def _probabilities(s, m):
    return jnp.exp2((s - m).astype(BF16))


def _flash_kernel(lam_ref, gsub_ref, qt_ref, k_ref, vt_ref, o_ref,
                  qs_ref, s0_ref, s1_ref, p0_ref, p1_ref, p2_ref, a0_ref, a1_ref, a2_ref,
                  m_ref, acc_ref, *, diff, lam_init):
    v_rows, n_val = _value_rows(diff)
    s_bufs, p_bufs, a_bufs = (s0_ref, s1_ref), (p0_ref, p1_ref, p2_ref), (a0_ref, a1_ref, a2_ref)
    n_groups = len(KEY_GROUPS)

    def tile_rows(j):
        return slice(j * ROW_TILE, (j + 1) * ROW_TILE)

    def scores_tile(t, j, s_ref):
        k = k_ref[0, t * ROW_TILE:(t + 1) * ROW_TILE, :]
        for slot in range(2):
            s_ref[slot, tile_rows(j), :] = jnp.dot(k, qs_ref[slot], preferred_element_type=F32)

    def stats(n, s_ref, a_ref, first):
        for slot in range(2):
            m_prev = jnp.full((1, ROW_TILE), MASK_VALUE, F32) if first else m_ref[slot]
            strips = [s_ref[slot, c * STRIP:(c + 1) * STRIP, :] for c in range(n * ROW_TILE // STRIP)]
            m_next = jnp.maximum(
                m_prev, jnp.max(functools.reduce(jnp.maximum, strips), axis=0, keepdims=True))
            a_ref[slot] = jnp.exp2(m_prev - m_next)
            m_ref[slot] = m_next

    def probs_tile(j, s_ref, p_ref):
        for slot in range(2):
            m = m_ref[slot]
            for c in range(ROW_TILE // STRIP):
                rows = slice(j * ROW_TILE + c * STRIP, j * ROW_TILE + (c + 1) * STRIP)
                p_ref[slot, rows, :] = _probabilities(s_ref[slot, rows, :], m)

    def values_tile(t, j, p_ref, pv):
        for slot in range(2):
            lo, hi = v_rows[slot]
            pv[slot].append(jnp.dot(vt_ref[0, t, lo:hi, :], p_ref[slot, tile_rows(j), :],
                                    preferred_element_type=F32))

    def accumulate(pv, a_ref):
        for slot in range(2):
            acc_ref[slot] = a_ref[slot] * acc_ref[slot] + functools.reduce(jnp.add, pv[slot])

    def finalize(i):
        acc_a, acc_b = acc_ref[0], acc_ref[1]
        ot_a = acc_a[:n_val] * (1.0 / acc_a[n_val:n_val + 1])
        ot_b = acc_b[:n_val] * (1.0 / acc_b[n_val:n_val + 1])
        if diff:
            lv = lam_ref[...]
            lam = (jnp.exp(jnp.sum(lv[0:1] * lv[1:2], axis=1, keepdims=True))
                   - jnp.exp(jnp.sum(lv[2:3] * lv[3:4], axis=1, keepdims=True)) + lam_init)
            o = (ot_a - lam * ot_b).T
            o = _rms(o, gsub_ref[...]) * (1.0 - lam_init)
        else:
            o = jnp.concatenate([ot_a, ot_b], axis=0).T
        o_ref[0, pl.ds(pl.multiple_of(i * ROW_TILE, ROW_TILE), ROW_TILE), :] = o.astype(BF16)

    acc_ref[...] = jnp.zeros(acc_ref.shape, F32)

    def scores_group(g, s_ref):
        for j in range(KEY_GROUPS[g][1]):
            scores_tile(KEY_GROUPS[g][0] + j, j, s_ref)

    def probs_group(g, s_ref, p_ref, a_ref, first):
        stats(KEY_GROUPS[g][1], s_ref, a_ref, first)
        for j in range(KEY_GROUPS[g][1]):
            probs_tile(j, s_ref, p_ref)

    _split_queries(qt_ref[0, 0], qs_ref)
    for fill in range(-SCORE_LEAD, 0):
        g_prob, g_sco = fill + PROB_LEAD, fill + SCORE_LEAD
        if g_prob >= 0:
            probs_group(g_prob, s_bufs[g_prob % 2], p_bufs[g_prob % 3], a_bufs[g_prob % 3],
                        g_prob == 0)
        scores_group(g_sco, s_bufs[g_sco % 2])

    def body(i, carry):
        for g in range(n_groups):
            g_prob, g_sco = (g + PROB_LEAD) % n_groups, (g + SCORE_LEAD) % n_groups
            t_val, n_val_tiles = KEY_GROUPS[g]
            t_sco, n_sco = KEY_GROUPS[g_sco]
            n_prob = KEY_GROUPS[g_prob][1]
            s_prob, p_prob, a_prob = s_bufs[g_prob % 2], p_bufs[g_prob % 3], a_bufs[g_prob % 3]
            stats(n_prob, s_prob, a_prob, g_prob == 0)
            if g + SCORE_LEAD == n_groups:
                _split_queries(qt_ref[0, i + 1], qs_ref)
            pv = ([], [])
            for j in range(max(n_val_tiles, n_sco, n_prob)):
                if j < n_prob:
                    probs_tile(j, s_prob, p_prob)
                if j < n_val_tiles:
                    values_tile(t_val + j, j, p_bufs[g % 3], pv)
                if j < n_sco:
                    scores_tile(t_sco + j, j, s_bufs[g_sco % 2])
            accumulate(pv, a_bufs[g % 3])
        finalize(i)
        return carry
    lax.fori_loop(0, CTX_TILE, body, 0)

    scores_tile(CTX_TILE, 0, s0_ref)
    stats(1, s0_ref, a0_ref, True)
    probs_tile(0, s0_ref, p0_ref)
    pv = ([], [])
    values_tile(CTX_TILE, 0, p0_ref, pv)
    accumulate(pv, a0_ref)
    finalize(CTX_TILE)


def _flash(qt, k, vt, lam, gsub, *, diff, lam_init):
    dq = k.shape[-1] // N_PAIRS
    vt_rows = vt.shape[2] // N_PAIRS
    acc_rows = _value_rows(diff)[0][0][1]
    group_rows = max(n for _, n in KEY_GROUPS) * ROW_TILE
    return pl.pallas_call(
        functools.partial(_flash_kernel, diff=diff, lam_init=lam_init),
        out_shape=jax.ShapeDtypeStruct((BATCH, T_ALL, D_MODEL), BF16),
        grid=(BATCH, N_PAIRS),
        in_specs=[pl.BlockSpec(lam.shape, lambda b, p: (0, 0)),
                  pl.BlockSpec(gsub.shape, lambda b, p: (0, 0)),
                  pl.BlockSpec((1, N_TILES, dq, ROW_TILE), lambda b, p: (b, 0, p, 0)),
                  pl.BlockSpec((1, T_ALL, dq), lambda b, p: (b, 0, p)),
                  pl.BlockSpec((1, N_TILES, vt_rows, ROW_TILE), lambda b, p: (b, 0, p, 0))],
        out_specs=pl.BlockSpec((1, T_ALL, LANES), lambda b, p: (b, 0, p)),
        scratch_shapes=[pltpu.VMEM((2, dq, ROW_TILE), BF16),
                        pltpu.VMEM((2, group_rows, ROW_TILE), F32),
                        pltpu.VMEM((2, group_rows, ROW_TILE), F32),
                        pltpu.VMEM((2, group_rows, ROW_TILE), BF16),
                        pltpu.VMEM((2, group_rows, ROW_TILE), BF16),
                        pltpu.VMEM((2, group_rows, ROW_TILE), BF16),
                        pltpu.VMEM((2, 1, ROW_TILE), F32),
                        pltpu.VMEM((2, 1, ROW_TILE), F32),
                        pltpu.VMEM((2, 1, ROW_TILE), F32),
                        pltpu.VMEM((2, 1, ROW_TILE), F32),
                        pltpu.VMEM((2, acc_rows, ROW_TILE), F32)],
        compiler_params=_params(2),
        name="flash_diff" if diff else "flash_pair",
    )(lam, gsub, qt, k, vt)


NA_Q_ROWS = ROW_TILE // GRID_W
NA_BAND_TILES = 3
NA_LAST_BAND = SEQ // ROW_TILE - NA_BAND_TILES


def _na_band_start(i):
    return jnp.clip(i - 1, 0, NA_LAST_BAND)


def _na_kernel(qt_ref, k0_ref, k1_ref, k2_ref, kc_ref, v0_ref, v1_ref, v2_ref, vc_ref,
               bias_ref, o_ref, qs_ref):
    k_refs = (k0_ref, k1_ref, k2_ref)
    v_refs = (v0_ref, v1_ref, v2_ref, vc_ref)
    v_rows, n_val = _value_rows(False)
    chains = [(b, slot) for b in range(BATCH) for slot in range(2)]
    for b in range(BATCH):
        _split_queries(qt_ref[b, 0], qs_ref.at[b])
    scores = {}
    for b, slot in chains:
        qs = qs_ref[b, slot]
        s = [jnp.dot(k_refs[j][b], qs, preferred_element_type=F32)
             + bias_ref[0, slot, j * ROW_TILE:(j + 1) * ROW_TILE, :]
             for j in range(NA_BAND_TILES)]
        s.append(jnp.dot(kc_ref[b], qs, preferred_element_type=F32))
        scores[b, slot] = s
    outs = {}
    for b, slot in chains:
        lo, hi = v_rows[slot]
        s = scores[b, slot]
        m = jnp.max(functools.reduce(jnp.maximum, s), axis=0, keepdims=True)
        acc = functools.reduce(jnp.add, [
            jnp.dot(r[b, 0, lo:hi, :], _probabilities(sj, m), preferred_element_type=F32)
            for r, sj in zip(v_refs, s)])
        outs[b, slot] = acc[:n_val] * (1.0 / acc[n_val:n_val + 1])
    for b in range(BATCH):
        o_ref[b] = jnp.concatenate([outs[b, 0], outs[b, 1]], axis=0).T.astype(BF16)


def _na_variant(i):
    return jnp.where(i == 0, 0, jnp.where(i < NA_LAST_BAND + NA_BAND_TILES - 1, 1,
                                          jnp.where(i < CTX_TILE, 2, 3)))


def _na_attention(qt, k, vt, bias):
    vt_rows = vt.shape[2] // N_PAIRS

    def kspec(j):
        return pl.BlockSpec((BATCH, ROW_TILE, LANES), lambda p, i: (0, _na_band_start(i) + j, p))

    def vspec(j):
        return pl.BlockSpec((BATCH, 1, vt_rows, ROW_TILE),
                            lambda p, i: (0, _na_band_start(i) + j, p, 0))

    return pl.pallas_call(
        _na_kernel,
        out_shape=jax.ShapeDtypeStruct((BATCH, T_ALL, D_MODEL), BF16),
        grid=(N_PAIRS, N_TILES),
        in_specs=[pl.BlockSpec((BATCH, 1, LANES, ROW_TILE), lambda p, i: (0, i, p, 0)),
                  kspec(0), kspec(1), kspec(2),
                  pl.BlockSpec((BATCH, ROW_TILE, LANES), lambda p, i: (0, CTX_TILE, p)),
                  vspec(0), vspec(1), vspec(2),
                  pl.BlockSpec((BATCH, 1, vt_rows, ROW_TILE), lambda p, i: (0, CTX_TILE, p, 0)),
                  pl.BlockSpec((1, 2, NA_BAND_TILES * ROW_TILE, ROW_TILE),
                               lambda p, i: (_na_variant(i), p, 0, 0))],
        out_specs=pl.BlockSpec((BATCH, ROW_TILE, LANES), lambda p, i: (0, i, p)),
        scratch_shapes=[pltpu.VMEM((BATCH, 2, LANES, ROW_TILE), BF16)],
        compiler_params=_params(2),
        name="na_attention",
    )(qt, k, k, k, k, vt, vt, vt, vt, bias)


def _na_bias_table(rpb):
    rows = SEQ // GRID_W
    band_rows = NA_BAND_TILES * NA_Q_ROWS
    col = np.arange(GRID_W)
    cstart = np.clip(col - NA_COLS // 2, 0, GRID_W - NA_COLS)
    ok_c = (col[None, :] >= cstart[:, None]) & (col[None, :] < cstart[:, None] + NA_COLS)
    dc = col[None, :] - col[:, None] + NA_COLS - 1
    onehot_c = ((dc[:, :, None] == np.arange(2 * NA_COLS - 1)) & ok_c[:, :, None]).astype(np.float32)
    dr = np.zeros((4, NA_Q_ROWS, band_rows), np.int32)
    ok_r = np.zeros((4, NA_Q_ROWS, band_rows), bool)
    for variant, i in enumerate((0, 1, SEQ // ROW_TILE - 1)):
        j = min(max(i - 1, 0), NA_LAST_BAND)
        r = NA_Q_ROWS * i + np.arange(NA_Q_ROWS)[:, None]
        k_row = NA_Q_ROWS * j + np.arange(band_rows)[None, :]
        rstart = np.clip(r - NA_ROWS // 2, 0, rows - NA_ROWS)
        ok_r[variant] = (k_row >= rstart) & (k_row < rstart + NA_ROWS)
        dr[variant] = np.where(ok_r[variant], k_row - r + NA_ROWS - 1, 0)
    by_col = jnp.einsum("hab,ckb->hack", rpb.astype(F32) * LOG2E, jnp.asarray(onehot_c),
                        precision=lax.Precision.HIGHEST)
    by_col = jnp.where(ok_c[None, None], by_col, MASK_VALUE)
    full = jnp.take(by_col, jnp.asarray(dr.reshape(-1)), axis=1)
    full = full.reshape(C_HEADS, 4, NA_Q_ROWS, band_rows, GRID_W, GRID_W)
    full = jnp.where(ok_r[None, :, :, :, None, None], full, MASK_VALUE)
    full = full.transpose(1, 0, 3, 5, 2, 4)
    return full.reshape(4, C_HEADS, NA_BAND_TILES * ROW_TILE, ROW_TILE)


def _post_kernel(o_ref, x_ref, mod_ref, g_ref, wo_ref, wgu_ref, wd_ref, xo_ref):
    m = mod_ref[0]
    y = jnp.dot(o_ref[0], wo_ref[...], preferred_element_type=F32)
    x1 = x_ref[0] + m[2:3] * _rms(y, g_ref[1:2])
    h2 = (_rms(x1, g_ref[2:3]) * (1.0 + m[4:5]) + m[3:4]).astype(BF16)
    gu = jnp.dot(h2, wgu_ref[...], preferred_element_type=F32)
    gate, up = gu[:, :FFN_HIDDEN], gu[:, FFN_HIDDEN:]
    a = (gate * _sigmoid(gate) * up).astype(BF16)
    f = jnp.dot(a, wd_ref[...], preferred_element_type=F32)
    xo_ref[0] = x1 + m[5:6] * _rms(f, g_ref[3:4])


def _post(o, x, mod, g_norm, wo, wgu, wd, *, n_tiles):
    return pl.pallas_call(
        _post_kernel,
        out_shape=jax.ShapeDtypeStruct((BATCH, n_tiles * ROW_TILE, D_MODEL), F32),
        grid=(BATCH, n_tiles),
        in_specs=[pl.BlockSpec((1, ROW_TILE, D_MODEL), lambda b, t: (b, t, 0))] + _row_specs() + [
            _resident((4, D_MODEL)), _resident(wo.shape), _resident(wgu.shape),
            _resident(wd.shape)],
        out_specs=pl.BlockSpec((1, ROW_TILE, D_MODEL), lambda b, t: (b, t, 0)),
        compiler_params=_params(2),
        name="post_attention",
    )(o, x, mod, g_norm, wo, wgu, wd)


def _rope_angles(rot_dim):
    n_freq = rot_dim // 4
    inv = ROPE_THETA ** (-jnp.arange(n_freq, dtype=F32) / n_freq)
    t = jnp.arange(SEQ, dtype=jnp.int32)
    row = (t // GRID_W).astype(F32)
    col = (t % GRID_W).astype(F32)
    ang = jnp.concatenate([row[:, None] * inv, col[:, None] * inv], axis=-1)
    cos = jnp.concatenate([jnp.cos(ang), jnp.ones((CTX_LEN, rot_dim // 2), F32)], axis=0)
    sin = jnp.concatenate([jnp.sin(ang), jnp.zeros((CTX_LEN, rot_dim // 2), F32)], axis=0)
    return cos, sin


def _rope_tables_diff():
    cos, sin = _rope_angles(A_HEAD_DIM)
    c_lane = jnp.tile(cos, (1, 4))
    s_lane = jnp.tile(jnp.concatenate([-sin, sin], axis=1), (1, 2))
    return c_lane, s_lane, cos.T, sin.T


def _rope_tables_mla():
    cos, sin = _rope_angles(B_ROPE)
    ones = jnp.ones((T_ALL, B_NOPE), F32)
    zeros = jnp.zeros((T_ALL, B_NOPE), F32)
    tail = LANES - B_NOPE - B_ROPE
    c_lane = jnp.concatenate([ones, cos, cos, ones[:, :tail]], axis=1)
    s_lane = jnp.concatenate([zeros, -sin, sin, zeros[:, :tail]], axis=1)
    return c_lane, s_lane, cos.T, sin.T


def kernel(x, c, ctx, c_ctx, l0_w_mod, l0_b_mod, l0_g_norm, l0_w_gu, l0_w_down, l0_a_w_qkv, l0_a_w_o, l0_a_lam, l0_a_g_sub, l1_w_mod, l1_b_mod, l1_g_norm, l1_w_gu, l1_w_down, l1_b_w_in, l1_b_g_q, l1_b_g_kv, l1_b_w_uq, l1_b_w_ukv, l1_b_w_o, l2_w_mod, l2_b_mod, l2_g_norm, l2_w_gu, l2_w_down, l2_c_w_qkv, l2_c_rpb, l2_c_w_o, l3_w_mod, l3_b_mod, l3_g_norm, l3_w_gu, l3_w_down, l3_a_w_qkv, l3_a_w_o, l3_a_lam, l3_a_g_sub):
    common = [
        (l0_w_mod, l0_b_mod, l0_g_norm, l0_w_gu, l0_w_down),
        (l1_w_mod, l1_b_mod, l1_g_norm, l1_w_gu, l1_w_down),
        (l2_w_mod, l2_b_mod, l2_g_norm, l2_w_gu, l2_w_down),
        (l3_w_mod, l3_b_mod, l3_g_norm, l3_w_gu, l3_w_down),
    ]
    diff_params = {0: (l0_a_w_qkv, l0_a_w_o, l0_a_lam, l0_a_g_sub),
                   3: (l3_a_w_qkv, l3_a_w_o, l3_a_lam, l3_a_g_sub)}

    xs = jnp.concatenate([x, ctx], axis=1)
    cc = jnp.concatenate([c, c_ctx[None, :], jnp.zeros((8 - BATCH - 1, D_MODEL), F32)], axis=0)
    tabs_a = _rope_tables_diff()
    tabs_b = _rope_tables_mla()
    dummy_lam = jnp.zeros((4, A_HEAD_DIM), F32)
    dummy_gsub = jnp.ones((1, LANES), F32)

    for i in range(DEPTH):
        w_mod, b_mod, g_norm, w_gu, w_down = common[i]
        last = i == DEPTH - 1
        mod = _modulation(cc, w_mod, b_mod).reshape(8, 6, D_MODEL)
        kind = i % 3
        if kind == 0:
            w_qkv, w_o, lam, g_sub = diff_params[i]
            lam_init = 0.8 - 0.6 * math.exp(-0.3 * i)
            wqt = w_qkv[:, :D_MODEL].T.astype(BF16)
            wk = w_qkv[:, D_MODEL:2 * D_MODEL].astype(BF16)
            wvt = w_qkv[:, 2 * D_MODEL:].T.astype(BF16)
            qt, k, vt = _proj_qkv(xs, mod, g_norm, wqt, wk, wvt, tabs_a, rope=True,
                                  qscale=A_HEAD_DIM ** -0.5 * LOG2E, v_rows=2 * A_HEAD_DIM)
            o = _flash(qt, k, vt, lam, g_sub.reshape(1, LANES), diff=True, lam_init=lam_init)
        elif kind == 1:
            w_in, g_q, g_kv, w_uq, w_ukv, w_o = (l1_b_w_in, l1_b_g_q, l1_b_g_kv, l1_b_w_uq,
                                                 l1_b_w_ukv, l1_b_w_o)
            n_lora = B_Q_LORA + B_KV_LORA
            pad = LANES - B_NOPE - B_ROPE
            win = w_in[:, :n_lora].astype(BF16)
            wkr = jnp.pad(w_in[:, n_lora:], ((0, 0), (B_NOPE, pad))).astype(BF16)
            wuq = w_uq.reshape(B_Q_LORA, B_HEADS, B_NOPE + B_ROPE)
            wuq = jnp.pad(wuq, ((0, 0), (0, 0), (0, pad)))
            wuqt = wuq.reshape(B_Q_LORA, B_HEADS * LANES).T.astype(BF16)
            wukv = w_ukv.reshape(B_KV_LORA, B_HEADS, B_NOPE + B_VDIM)
            wukn = jnp.pad(wukv[:, :, :B_NOPE], ((0, 0), (0, 0), (0, LANES - B_NOPE)))
            wukn = wukn.reshape(B_KV_LORA, B_HEADS * LANES).astype(BF16)
            wuvt = wukv[:, :, B_NOPE:].reshape(B_KV_LORA, B_HEADS * B_VDIM).T.astype(BF16)
            qt, k, vt = _proj_mla(xs, mod, g_norm, win, wkr, g_q.reshape(1, -1),
                                  g_kv.reshape(1, -1), wuqt, wukn, wuvt, tabs_b,
                                  qscale=(B_NOPE + B_ROPE) ** -0.5 * LOG2E)
            o = _flash(qt, k, vt, dummy_lam, dummy_gsub, diff=False, lam_init=0.0)
        else:
            w_qkv, rpb, w_o = l2_c_w_qkv, l2_c_rpb, l2_c_w_o
            wqt = w_qkv[:, :D_MODEL].T.astype(BF16)
            wk = w_qkv[:, D_MODEL:2 * D_MODEL].astype(BF16)
            wvt = w_qkv[:, 2 * D_MODEL:].T.astype(BF16)
            qt, k, vt = _proj_qkv(xs, mod, g_norm, wqt, wk, wvt, tabs_a, rope=False,
                                  qscale=C_HEAD_DIM ** -0.5 * LOG2E, v_rows=C_HEAD_DIM)
            o = _na_attention(qt, k, vt, _na_bias_table(rpb))
        xs = _post(o, xs, mod, g_norm, w_o.astype(BF16), w_gu.astype(BF16),
                   w_down.astype(BF16), n_tiles=N_TILES - 1 if last else N_TILES)
    return xs
```

```python
import functools
import math

import numpy as np
import jax
import jax.numpy as jnp
from jax import lax
from jax.experimental import pallas as pl
from jax.experimental.pallas import tpu as pltpu

D_MODEL = 1024
BATCH = 4
SEQ = 8192
DEPTH = 4
GRID_W = 64
CTX_LEN = 256
ROPE_THETA = 10000.0
NORM_EPS = 1e-6

A_HEAD_DIM = 64
A_HEADS = 8
B_HEADS = 16
B_NOPE = 64
B_ROPE = 32
B_VDIM = 64
B_Q_LORA = 256
B_KV_LORA = 256
C_HEAD_DIM = 64
C_HEADS = 16
NA_ROWS = 8
NA_COLS = 16
FFN_HIDDEN = 2816

LANES = 128
STRIP = 64
ROW_TILE = 256
T_ALL = SEQ + CTX_LEN
N_TILES = T_ALL // ROW_TILE
CTX_TILE = N_TILES - 1
N_PAIRS = 8
KEY_GROUPS = tuple((3 * g, 3) for g in range(9)) + tuple((27 + 2 * g, 2) for g in range(3))
PROB_LEAD = 1
SCORE_LEAD = 2
V_ONES = 16
LOG2E = math.log2(math.e)
MASK_VALUE = -1e30
VMEM_LIMIT = 60 * 1024 * 1024

F32 = jnp.float32
BF16 = jnp.bfloat16
NT_DIMS = (((1,), (1,)), ((), ()))


def _rms(x, g):
    return x * lax.rsqrt(jnp.mean(x * x, axis=-1, keepdims=True) + NORM_EPS) * g


def _sigmoid(x):
    return 1.0 / (1.0 + jnp.exp(-x))


def _params(n_axes):
    return pltpu.CompilerParams(
        dimension_semantics=("arbitrary",) * n_axes, vmem_limit_bytes=VMEM_LIMIT)


def _resident(shape):
    zeros = (0,) * len(shape)
    return pl.BlockSpec(shape, lambda *_: zeros, pipeline_mode=pl.Buffered(1))


def _mod_row(b, t):
    return jnp.where(t == CTX_TILE, BATCH, b)


def _mod_kernel(c_ref, w_ref, b_ref, o_ref):
    c = c_ref[...]
    a = (c * _sigmoid(c)).astype(BF16)
    o_ref[...] = jnp.dot(a, w_ref[...].astype(BF16), preferred_element_type=F32) + b_ref[...]


def _modulation(cc, w_mod, b_mod):
    n = w_mod.shape[1]
    blk = D_MODEL
    return pl.pallas_call(
        _mod_kernel,
        out_shape=jax.ShapeDtypeStruct((8, n), F32),
        grid=(n // blk,),
        in_specs=[pl.BlockSpec((8, D_MODEL), lambda j: (0, 0)),
                  pl.BlockSpec((D_MODEL, blk), lambda j: (0, j)),
                  pl.BlockSpec((1, blk), lambda j: (0, j))],
        out_specs=pl.BlockSpec((8, blk), lambda j: (0, j)),
        compiler_params=_params(1),
        name="modulation",
    )(cc, w_mod, b_mod.reshape(1, n))


def _rope_lanes(k, c, s, first, half):
    rot = jnp.where(first, pltpu.roll(k, LANES - half, 1), pltpu.roll(k, half, 1))
    return k * c + rot * s


def _store_values(vt_ref, vt, rows):
    ones = jnp.ones((V_ONES, ROW_TILE), BF16)
    for n in range(vt.shape[0] // rows):
        r0 = n * (rows + V_ONES)
        vt_ref[0, 0, r0:r0 + rows, :] = vt[n * rows:(n + 1) * rows].astype(BF16)
        vt_ref[0, 0, r0 + rows:r0 + rows + V_ONES, :] = ones


def _proj_qkv_kernel(x_ref, mod_ref, g_ref, wqt_ref, wk_ref, wvt_ref,
                     ck_ref, sk_ref, cqt_ref, sqt_ref,
                     qt_ref, k_ref, vt_ref, *, rope, qscale, v_rows):
    m = mod_ref[0]
    h = (_rms(x_ref[0], g_ref[0:1]) * (1.0 + m[1:2]) + m[0:1]).astype(BF16)
    qt = lax.dot_general(wqt_ref[...], h, NT_DIMS, preferred_element_type=F32) * qscale
    k = jnp.dot(h, wk_ref[...], preferred_element_type=F32)
    vt = lax.dot_general(wvt_ref[...], h, NT_DIMS, preferred_element_type=F32)
    _store_values(vt_ref, vt, v_rows)
    if not rope:
        qt_ref[0, 0] = qt.astype(BF16)
        k_ref[0] = k.astype(BF16)
        return
    half = A_HEAD_DIM // 2
    c, s = ck_ref[...], sk_ref[...]
    lane = lax.broadcasted_iota(jnp.int32, (ROW_TILE, LANES), 1)
    first = (lane % A_HEAD_DIM) < half
    for p in range(D_MODEL // LANES):
        sl = slice(p * LANES, (p + 1) * LANES)
        k_ref[0, :, sl] = _rope_lanes(k[:, sl], c, s, first, half).astype(BF16)
    ct, st = cqt_ref[...], sqt_ref[...]
    for g in range(D_MODEL // A_HEAD_DIM):
        r0 = g * A_HEAD_DIM
        t1, t2 = qt[r0:r0 + half], qt[r0 + half:r0 + 2 * half]
        qt_ref[0, 0, r0:r0 + half, :] = (t1 * ct - t2 * st).astype(BF16)
        qt_ref[0, 0, r0 + half:r0 + 2 * half, :] = (t1 * st + t2 * ct).astype(BF16)


def _row_specs():
    return [pl.BlockSpec((1, ROW_TILE, D_MODEL), lambda b, t: (b, t, 0)),
            pl.BlockSpec((1, 6, D_MODEL), lambda b, t: (_mod_row(b, t), 0, 0))]


def _rope_specs(half):
    return [pl.BlockSpec((ROW_TILE, LANES), lambda b, t: (t, 0)),
            pl.BlockSpec((ROW_TILE, LANES), lambda b, t: (t, 0)),
            pl.BlockSpec((half, ROW_TILE), lambda b, t: (0, t)),
            pl.BlockSpec((half, ROW_TILE), lambda b, t: (0, t))]


def _qkv_out(width, v_rows):
    vt_rows = D_MODEL // v_rows * (v_rows + V_ONES)
    shapes = (jax.ShapeDtypeStruct((BATCH, N_TILES, width, ROW_TILE), BF16),
              jax.ShapeDtypeStruct((BATCH, T_ALL, width), BF16),
              jax.ShapeDtypeStruct((BATCH, N_TILES, vt_rows, ROW_TILE), BF16))
    specs = (pl.BlockSpec((1, 1, width, ROW_TILE), lambda b, t: (b, t, 0, 0)),
             pl.BlockSpec((1, ROW_TILE, width), lambda b, t: (b, t, 0)),
             pl.BlockSpec((1, 1, vt_rows, ROW_TILE), lambda b, t: (b, t, 0, 0)))
    return shapes, specs


def _proj_qkv(x, mod, g_norm, wqt, wk, wvt, tabs, *, rope, qscale, v_rows):
    shapes, specs = _qkv_out(D_MODEL, v_rows)
    return pl.pallas_call(
        functools.partial(_proj_qkv_kernel, rope=rope, qscale=qscale, v_rows=v_rows),
        out_shape=shapes,
        grid=(BATCH, N_TILES),
        in_specs=_row_specs() + [
            _resident((4, D_MODEL)),
            _resident((D_MODEL, D_MODEL)), _resident((D_MODEL, D_MODEL)),
            _resident((D_MODEL, D_MODEL))] + _rope_specs(tabs[2].shape[0]),
        out_specs=specs,
        compiler_params=_params(2),
        name="proj_qkv",
    )(x, mod, g_norm, wqt, wk, wvt, *tabs)


def _proj_mla_kernel(x_ref, mod_ref, g_ref, win_ref, wkr_ref, gq_ref, gkv_ref,
                     wuqt_ref, wukn_ref, wuvt_ref, ck_ref, sk_ref, cqt_ref, sqt_ref,
                     qt_ref, k_ref, vt_ref, *, qscale):
    m = mod_ref[0]
    h = (_rms(x_ref[0], g_ref[0:1]) * (1.0 + m[1:2]) + m[0:1]).astype(BF16)
    z = jnp.dot(h, win_ref[...], preferred_element_type=F32)
    cq = _rms(z[:, :B_Q_LORA], gq_ref[...]).astype(BF16)
    ckv = _rms(z[:, B_Q_LORA:], gkv_ref[...]).astype(BF16)
    half = B_ROPE // 2

    qt = lax.dot_general(wuqt_ref[...], cq, NT_DIMS, preferred_element_type=F32) * qscale
    ct, st = cqt_ref[...], sqt_ref[...]
    for hh in range(B_HEADS):
        r0, r1 = hh * LANES, hh * LANES + B_NOPE
        qt_ref[0, 0, r0:r1, :] = qt[r0:r1].astype(BF16)
        t1, t2 = qt[r1:r1 + half], qt[r1 + half:r1 + 2 * half]
        qt_ref[0, 0, r1:r1 + half, :] = (t1 * ct - t2 * st).astype(BF16)
        qt_ref[0, 0, r1 + half:r1 + 2 * half, :] = (t1 * st + t2 * ct).astype(BF16)
        qt_ref[0, 0, r1 + B_ROPE:r0 + LANES, :] = qt[r1 + B_ROPE:r0 + LANES].astype(BF16)

    kr = jnp.dot(h, wkr_ref[...], preferred_element_type=F32)
    lane = lax.broadcasted_iota(jnp.int32, (ROW_TILE, LANES), 1)
    first = (lane >= B_NOPE) & (lane < B_NOPE + half)
    kr = _rope_lanes(kr, ck_ref[...], sk_ref[...], first, half)
    kn = jnp.dot(ckv, wukn_ref[...], preferred_element_type=F32)
    for hh in range(B_HEADS):
        sl = slice(hh * LANES, (hh + 1) * LANES)
        k_ref[0, :, sl] = (kn[:, sl] + kr).astype(BF16)

    vt = lax.dot_general(wuvt_ref[...], ckv, NT_DIMS, preferred_element_type=F32)
    _store_values(vt_ref, vt, B_VDIM)


def _proj_mla(x, mod, g_norm, win, wkr, gq, gkv, wuqt, wukn, wuvt, tabs, *, qscale):
    shapes, specs = _qkv_out(B_HEADS * LANES, B_VDIM)
    return pl.pallas_call(
        functools.partial(_proj_mla_kernel, qscale=qscale),
        out_shape=shapes,
        grid=(BATCH, N_TILES),
        in_specs=_row_specs() + [
            _resident((4, D_MODEL)),
            _resident(win.shape), _resident(wkr.shape),
            _resident((1, B_Q_LORA)), _resident((1, B_KV_LORA)),
            _resident(wuqt.shape), _resident(wukn.shape), _resident(wuvt.shape)]
            + _rope_specs(tabs[2].shape[0]),
        out_specs=specs,
        compiler_params=_params(2),
        name="proj_mla",
    )(x, mod, g_norm, win, wkr, gq, gkv, wuqt, wukn, wuvt, *tabs)


def _value_rows(diff):
    if diff:
        rows = 2 * A_HEAD_DIM
        return ((0, rows + V_ONES), (0, rows + V_ONES)), rows
    rows = LANES // 2
    return ((0, rows + V_ONES), (rows + V_ONES, 2 * (rows + V_ONES))), rows


def _split_queries(qt, qs_ref):
    dq = qt.shape[0]
    row = lax.broadcasted_iota(jnp.int32, qt.shape, 0)
    zero = jnp.zeros_like(qt)
    qs_ref[0] = jnp.where(row < dq // 2, qt, zero)
    qs_ref[1] = jnp.where(row >= dq // 2, qt, zero)


def _probabilities(s, m):
    return jnp.exp2((s - m).astype(BF16))


def _flash_kernel(lam_ref, gsub_ref, qt_ref, k_ref, vt_ref, o_ref,
                  qs_ref, s0_ref, s1_ref, p0_ref, p1_ref, p2_ref, a0_ref, a1_ref, a2_ref,
                  m_ref, acc_ref, fin_ref, *, diff, lam_init):
    v_rows, n_val = _value_rows(diff)
    s_bufs, p_bufs, a_bufs = (s0_ref, s1_ref), (p0_ref, p1_ref, p2_ref), (a0_ref, a1_ref, a2_ref)
    n_groups = len(KEY_GROUPS)

    def tile_rows(j):
        return slice(j * ROW_TILE, (j + 1) * ROW_TILE)

    def scores(t0, n, slot, s_ref):
        for j in range(n):
            k = k_ref[0, (t0 + j) * ROW_TILE:(t0 + j + 1) * ROW_TILE, :]
            s_ref[slot, tile_rows(j), :] = jnp.dot(k, qs_ref[slot], preferred_element_type=F32)

    def stats(n, s_ref, a_ref, first):
        for slot in range(2):
            m_prev = jnp.full((1, ROW_TILE), MASK_VALUE, F32) if first else m_ref[slot]
            strips = [s_ref[slot, c * STRIP:(c + 1) * STRIP, :] for c in range(n * ROW_TILE // STRIP)]
            m_next = jnp.maximum(
                m_prev, jnp.max(functools.reduce(jnp.maximum, strips), axis=0, keepdims=True))
            a_ref[slot] = jnp.exp2(m_prev - m_next)
            m_ref[slot] = m_next

    def probs_tile(j, s_ref, p_ref):
        for slot in range(2):
            m = m_ref[slot]
            for c in range(ROW_TILE // STRIP):
                rows = slice(j * ROW_TILE + c * STRIP, j * ROW_TILE + (c + 1) * STRIP)
                p_ref[slot, rows, :] = _probabilities(s_ref[slot, rows, :], m)

    def values_tile(t, j, p_ref, pv):
        for slot in range(2):
            lo, hi = v_rows[slot]
            pv[slot].append(jnp.dot(vt_ref[0, t, lo:hi, :], p_ref[slot, tile_rows(j), :],
                                    preferred_element_type=F32))

    def accumulate(pv, a_ref):
        for slot in range(2):
            acc_ref[slot] = a_ref[slot] * acc_ref[slot] + functools.reduce(jnp.add, pv[slot])

    def finalize(i, src_ref):
        acc_a, acc_b = src_ref[0], src_ref[1]
        ot_a = acc_a[:n_val] * (1.0 / acc_a[n_val:n_val + 1])
        ot_b = acc_b[:n_val] * (1.0 / acc_b[n_val:n_val + 1])
        if diff:
            lv = lam_ref[...]
            lam = (jnp.exp(jnp.sum(lv[0:1] * lv[1:2], axis=1, keepdims=True))
                   - jnp.exp(jnp.sum(lv[2:3] * lv[3:4], axis=1, keepdims=True)) + lam_init)
            o = (ot_a - lam * ot_b).T
            o = _rms(o, gsub_ref[...]) * (1.0 - lam_init)
        else:
            o = jnp.concatenate([ot_a, ot_b], axis=0).T
        o_ref[0, pl.ds(pl.multiple_of(i * ROW_TILE, ROW_TILE), ROW_TILE), :] = o.astype(BF16)

    acc_ref[...] = jnp.zeros(acc_ref.shape, F32)
    fin_ref[...] = jnp.ones(fin_ref.shape, F32)

    def scores_group(g, s_ref):
        for slot in range(2):
            scores(KEY_GROUPS[g][0], KEY_GROUPS[g][1], slot, s_ref)

    def probs_group(g, s_ref, p_ref, a_ref, first):
        stats(KEY_GROUPS[g][1], s_ref, a_ref, first)
        for j in range(KEY_GROUPS[g][1]):
            probs_tile(j, s_ref, p_ref)

    _split_queries(qt_ref[0, 0], qs_ref)
    for fill in range(-SCORE_LEAD, 0):
        g_prob, g_sco = fill + PROB_LEAD, fill + SCORE_LEAD
        if g_prob >= 0:
            probs_group(g_prob, s_bufs[g_prob % 2], p_bufs[g_prob % 3], a_bufs[g_prob % 3],
                        g_prob == 0)
        scores_group(g_sco, s_bufs[g_sco % 2])

    def body(i, carry):
        for g in range(n_groups):
            g_prob, g_sco = (g + PROB_LEAD) % n_groups, (g + SCORE_LEAD) % n_groups
            t_val, n_val_tiles = KEY_GROUPS[g]
            t_sco, n_sco = KEY_GROUPS[g_sco]
            n_prob = KEY_GROUPS[g_prob][1]
            s_prob, p_prob, a_prob = s_bufs[g_prob % 2], p_bufs[g_prob % 3], a_bufs[g_prob % 3]
            stats(n_prob, s_prob, a_prob, g_prob == 0)
            if g + SCORE_LEAD == n_groups:
                _split_queries(qt_ref[0, i + 1], qs_ref)
            pv = ([], [])
            for j in range(max(n_val_tiles, n_prob)):
                if j < n_prob:
                    probs_tile(j, s_prob, p_prob)
                if j < n_val_tiles:
                    values_tile(t_val + j, j, p_bufs[g % 3], pv)
                if j < 2:
                    scores(t_sco, n_sco, j, s_bufs[g_sco % 2])
            accumulate(pv, a_bufs[g % 3])
            if g == 0:
                finalize(jnp.maximum(i - 1, 0), fin_ref)
        fin_ref[...] = acc_ref[...]
        return carry
    lax.fori_loop(0, CTX_TILE, body, 0)
    finalize(CTX_TILE - 1, fin_ref)

    for slot in range(2):
        scores(CTX_TILE, 1, slot, s0_ref)
    stats(1, s0_ref, a0_ref, True)
    probs_tile(0, s0_ref, p0_ref)
    pv = ([], [])
    values_tile(CTX_TILE, 0, p0_ref, pv)
    accumulate(pv, a0_ref)
    finalize(CTX_TILE, acc_ref)


def _flash(qt, k, vt, lam, gsub, *, diff, lam_init):
    dq = k.shape[-1] // N_PAIRS
    vt_rows = vt.shape[2] // N_PAIRS
    acc_rows = _value_rows(diff)[0][0][1]
    group_rows = max(n for _, n in KEY_GROUPS) * ROW_TILE
    return pl.pallas_call(
        functools.partial(_flash_kernel, diff=diff, lam_init=lam_init),
        out_shape=jax.ShapeDtypeStruct((BATCH, T_ALL, D_MODEL), BF16),
        grid=(BATCH, N_PAIRS),
        in_specs=[pl.BlockSpec(lam.shape, lambda b, p: (0, 0)),
                  pl.BlockSpec(gsub.shape, lambda b, p: (0, 0)),
                  pl.BlockSpec((1, N_TILES, dq, ROW_TILE), lambda b, p: (b, 0, p, 0)),
                  pl.BlockSpec((1, T_ALL, dq), lambda b, p: (b, 0, p)),
                  pl.BlockSpec((1, N_TILES, vt_rows, ROW_TILE), lambda b, p: (b, 0, p, 0))],
        out_specs=pl.BlockSpec((1, T_ALL, LANES), lambda b, p: (b, 0, p)),
        scratch_shapes=[pltpu.VMEM((2, dq, ROW_TILE), BF16),
                        pltpu.VMEM((2, group_rows, ROW_TILE), F32),
                        pltpu.VMEM((2, group_rows, ROW_TILE), F32),
                        pltpu.VMEM((2, group_rows, ROW_TILE), BF16),
                        pltpu.VMEM((2, group_rows, ROW_TILE), BF16),
                        pltpu.VMEM((2, group_rows, ROW_TILE), BF16),
                        pltpu.VMEM((2, 1, ROW_TILE), F32),
                        pltpu.VMEM((2, 1, ROW_TILE), F32),
                        pltpu.VMEM((2, 1, ROW_TILE), F32),
                        pltpu.VMEM((2, 1, ROW_TILE), F32),
                        pltpu.VMEM((2, acc_rows, ROW_TILE), F32),
                        pltpu.VMEM((2, acc_rows, ROW_TILE), F32)],
        compiler_params=_params(2),
        name="flash_diff" if diff else "flash_pair",
    )(lam, gsub, qt, k, vt)


NA_Q_ROWS = ROW_TILE // GRID_W
NA_BAND_TILES = 3
NA_LAST_BAND = SEQ // ROW_TILE - NA_BAND_TILES


def _na_band_start(i):
    return jnp.clip(i - 1, 0, NA_LAST_BAND)


def _na_kernel(qt_ref, k0_ref, k1_ref, k2_ref, kc_ref, v0_ref, v1_ref, v2_ref, vc_ref,
               bias_ref, o_ref, qs_ref):
    k_refs = (k0_ref, k1_ref, k2_ref)
    v_refs = (v0_ref, v1_ref, v2_ref, vc_ref)
    v_rows, n_val = _value_rows(False)
    chains = [(b, slot) for b in range(BATCH) for slot in range(2)]
    for b in range(BATCH):
        _split_queries(qt_ref[b, 0], qs_ref.at[b])
    scores = {}
    for b, slot in chains:
        qs = qs_ref[b, slot]
        s = [jnp.dot(k_refs[j][b], qs, preferred_element_type=F32)
             + bias_ref[0, slot, j * ROW_TILE:(j + 1) * ROW_TILE, :]
             for j in range(NA_BAND_TILES)]
        s.append(jnp.dot(kc_ref[b], qs, preferred_element_type=F32))
        scores[b, slot] = s
    outs = {}
    for b, slot in chains:
        lo, hi = v_rows[slot]
        s = scores[b, slot]
        m = jnp.max(functools.reduce(jnp.maximum, s), axis=0, keepdims=True)
        acc = functools.reduce(jnp.add, [
            jnp.dot(r[b, 0, lo:hi, :], _probabilities(sj, m), preferred_element_type=F32)
            for r, sj in zip(v_refs, s)])
        outs[b, slot] = acc[:n_val] * (1.0 / acc[n_val:n_val + 1])
    for b in range(BATCH):
        o_ref[b] = jnp.concatenate([outs[b, 0], outs[b, 1]], axis=0).T.astype(BF16)


def _na_variant(i):
    return jnp.where(i == 0, 0, jnp.where(i < NA_LAST_BAND + NA_BAND_TILES - 1, 1,
                                          jnp.where(i < CTX_TILE, 2, 3)))


def _na_attention(qt, k, vt, bias):
    vt_rows = vt.shape[2] // N_PAIRS

    def kspec(j):
        return pl.BlockSpec((BATCH, ROW_TILE, LANES), lambda p, i: (0, _na_band_start(i) + j, p))

    def vspec(j):
        return pl.BlockSpec((BATCH, 1, vt_rows, ROW_TILE),
                            lambda p, i: (0, _na_band_start(i) + j, p, 0))

    return pl.pallas_call(
        _na_kernel,
        out_shape=jax.ShapeDtypeStruct((BATCH, T_ALL, D_MODEL), BF16),
        grid=(N_PAIRS, N_TILES),
        in_specs=[pl.BlockSpec((BATCH, 1, LANES, ROW_TILE), lambda p, i: (0, i, p, 0)),
                  kspec(0), kspec(1), kspec(2),
                  pl.BlockSpec((BATCH, ROW_TILE, LANES), lambda p, i: (0, CTX_TILE, p)),
                  vspec(0), vspec(1), vspec(2),
                  pl.BlockSpec((BATCH, 1, vt_rows, ROW_TILE), lambda p, i: (0, CTX_TILE, p, 0)),
                  pl.BlockSpec((1, 2, NA_BAND_TILES * ROW_TILE, ROW_TILE),
                               lambda p, i: (_na_variant(i), p, 0, 0))],
        out_specs=pl.BlockSpec((BATCH, ROW_TILE, LANES), lambda p, i: (0, i, p)),
        scratch_shapes=[pltpu.VMEM((BATCH, 2, LANES, ROW_TILE), BF16)],
        compiler_params=_params(2),
        name="na_attention",
    )(qt, k, k, k, k, vt, vt, vt, vt, bias)


def _na_bias_table(rpb):
    rows = SEQ // GRID_W
    band_rows = NA_BAND_TILES * NA_Q_ROWS
    col = np.arange(GRID_W)
    cstart = np.clip(col - NA_COLS // 2, 0, GRID_W - NA_COLS)
    ok_c = (col[None, :] >= cstart[:, None]) & (col[None, :] < cstart[:, None] + NA_COLS)
    dc = col[None, :] - col[:, None] + NA_COLS - 1
    onehot_c = ((dc[:, :, None] == np.arange(2 * NA_COLS - 1)) & ok_c[:, :, None]).astype(np.float32)
    dr = np.zeros((4, NA_Q_ROWS, band_rows), np.int32)
    ok_r = np.zeros((4, NA_Q_ROWS, band_rows), bool)
    for variant, i in enumerate((0, 1, SEQ // ROW_TILE - 1)):
        j = min(max(i - 1, 0), NA_LAST_BAND)
        r = NA_Q_ROWS * i + np.arange(NA_Q_ROWS)[:, None]
        k_row = NA_Q_ROWS * j + np.arange(band_rows)[None, :]
        rstart = np.clip(r - NA_ROWS // 2, 0, rows - NA_ROWS)
        ok_r[variant] = (k_row >= rstart) & (k_row < rstart + NA_ROWS)
        dr[variant] = np.where(ok_r[variant], k_row - r + NA_ROWS - 1, 0)
    by_col = jnp.einsum("hab,ckb->hack", rpb.astype(F32) * LOG2E, jnp.asarray(onehot_c),
                        precision=lax.Precision.HIGHEST)
    by_col = jnp.where(ok_c[None, None], by_col, MASK_VALUE)
    full = jnp.take(by_col, jnp.asarray(dr.reshape(-1)), axis=1)
    full = full.reshape(C_HEADS, 4, NA_Q_ROWS, band_rows, GRID_W, GRID_W)
    full = jnp.where(ok_r[None, :, :, :, None, None], full, MASK_VALUE)
    full = full.transpose(1, 0, 3, 5, 2, 4)
    return full.reshape(4, C_HEADS, NA_BAND_TILES * ROW_TILE, ROW_TILE)


def _post_kernel(o_ref, x_ref, mod_ref, g_ref, wo_ref, wgu_ref, wd_ref, xo_ref):
    m = mod_ref[0]
    y = jnp.dot(o_ref[0], wo_ref[...], preferred_element_type=F32)
    x1 = x_ref[0] + m[2:3] * _rms(y, g_ref[1:2])
    h2 = (_rms(x1, g_ref[2:3]) * (1.0 + m[4:5]) + m[3:4]).astype(BF16)
    gu = jnp.dot(h2, wgu_ref[...], preferred_element_type=F32)
    gate, up = gu[:, :FFN_HIDDEN], gu[:, FFN_HIDDEN:]
    a = (gate * _sigmoid(gate) * up).astype(BF16)
    f = jnp.dot(a, wd_ref[...], preferred_element_type=F32)
    xo_ref[0] = x1 + m[5:6] * _rms(f, g_ref[3:4])


def _post(o, x, mod, g_norm, wo, wgu, wd, *, n_tiles):
    return pl.pallas_call(
        _post_kernel,
        out_shape=jax.ShapeDtypeStruct((BATCH, n_tiles * ROW_TILE, D_MODEL), F32),
        grid=(BATCH, n_tiles),
        in_specs=[pl.BlockSpec((1, ROW_TILE, D_MODEL), lambda b, t: (b, t, 0))] + _row_specs() + [
            _resident((4, D_MODEL)), _resident(wo.shape), _resident(wgu.shape),
            _resident(wd.shape)],
        out_specs=pl.BlockSpec((1, ROW_TILE, D_MODEL), lambda b, t: (b, t, 0)),
        compiler_params=_params(2),
        name="post_attention",
    )(o, x, mod, g_norm, wo, wgu, wd)


def _rope_angles(rot_dim):
    n_freq = rot_dim // 4
    inv = ROPE_THETA ** (-jnp.arange(n_freq, dtype=F32) / n_freq)
    t = jnp.arange(SEQ, dtype=jnp.int32)
    row = (t // GRID_W).astype(F32)
    col = (t % GRID_W).astype(F32)
    ang = jnp.concatenate([row[:, None] * inv, col[:, None] * inv], axis=-1)
    cos = jnp.concatenate([jnp.cos(ang), jnp.ones((CTX_LEN, rot_dim // 2), F32)], axis=0)
    sin = jnp.concatenate([jnp.sin(ang), jnp.zeros((CTX_LEN, rot_dim // 2), F32)], axis=0)
    return cos, sin


def _rope_tables_diff():
    cos, sin = _rope_angles(A_HEAD_DIM)
    c_lane = jnp.tile(cos, (1, 4))
    s_lane = jnp.tile(jnp.concatenate([-sin, sin], axis=1), (1, 2))
    return c_lane, s_lane, cos.T, sin.T


def _rope_tables_mla():
    cos, sin = _rope_angles(B_ROPE)
    ones = jnp.ones((T_ALL, B_NOPE), F32)
    zeros = jnp.zeros((T_ALL, B_NOPE), F32)
    tail = LANES - B_NOPE - B_ROPE
    c_lane = jnp.concatenate([ones, cos, cos, ones[:, :tail]], axis=1)
    s_lane = jnp.concatenate([zeros, -sin, sin, zeros[:, :tail]], axis=1)
    return c_lane, s_lane, cos.T, sin.T


def kernel(x, c, ctx, c_ctx, l0_w_mod, l0_b_mod, l0_g_norm, l0_w_gu, l0_w_down, l0_a_w_qkv, l0_a_w_o, l0_a_lam, l0_a_g_sub, l1_w_mod, l1_b_mod, l1_g_norm, l1_w_gu, l1_w_down, l1_b_w_in, l1_b_g_q, l1_b_g_kv, l1_b_w_uq, l1_b_w_ukv, l1_b_w_o, l2_w_mod, l2_b_mod, l2_g_norm, l2_w_gu, l2_w_down, l2_c_w_qkv, l2_c_rpb, l2_c_w_o, l3_w_mod, l3_b_mod, l3_g_norm, l3_w_gu, l3_w_down, l3_a_w_qkv, l3_a_w_o, l3_a_lam, l3_a_g_sub):
    common = [
        (l0_w_mod, l0_b_mod, l0_g_norm, l0_w_gu, l0_w_down),
        (l1_w_mod, l1_b_mod, l1_g_norm, l1_w_gu, l1_w_down),
        (l2_w_mod, l2_b_mod, l2_g_norm, l2_w_gu, l2_w_down),
        (l3_w_mod, l3_b_mod, l3_g_norm, l3_w_gu, l3_w_down),
    ]
    diff_params = {0: (l0_a_w_qkv, l0_a_w_o, l0_a_lam, l0_a_g_sub),
                   3: (l3_a_w_qkv, l3_a_w_o, l3_a_lam, l3_a_g_sub)}

    xs = jnp.concatenate([x, ctx], axis=1)
    cc = jnp.concatenate([c, c_ctx[None, :], jnp.zeros((8 - BATCH - 1, D_MODEL), F32)], axis=0)
    tabs_a = _rope_tables_diff()
    tabs_b = _rope_tables_mla()
    dummy_lam = jnp.zeros((4, A_HEAD_DIM), F32)
    dummy_gsub = jnp.ones((1, LANES), F32)

    for i in range(DEPTH):
        w_mod, b_mod, g_norm, w_gu, w_down = common[i]
        last = i == DEPTH - 1
        mod = _modulation(cc, w_mod, b_mod).reshape(8, 6, D_MODEL)
        kind = i % 3
        if kind == 0:
            w_qkv, w_o, lam, g_sub = diff_params[i]
            lam_init = 0.8 - 0.6 * math.exp(-0.3 * i)
            wqt = w_qkv[:, :D_MODEL].T.astype(BF16)
            wk = w_qkv[:, D_MODEL:2 * D_MODEL].astype(BF16)
            wvt = w_qkv[:, 2 * D_MODEL:].T.astype(BF16)
            qt, k, vt = _proj_qkv(xs, mod, g_norm, wqt, wk, wvt, tabs_a, rope=True,
                                  qscale=A_HEAD_DIM ** -0.5 * LOG2E, v_rows=2 * A_HEAD_DIM)
            o = _flash(qt, k, vt, lam, g_sub.reshape(1, LANES), diff=True, lam_init=lam_init)
        elif kind == 1:
            w_in, g_q, g_kv, w_uq, w_ukv, w_o = (l1_b_w_in, l1_b_g_q, l1_b_g_kv, l1_b_w_uq,
                                                 l1_b_w_ukv, l1_b_w_o)
            n_lora = B_Q_LORA + B_KV_LORA
            pad = LANES - B_NOPE - B_ROPE
            win = w_in[:, :n_lora].astype(BF16)
            wkr = jnp.pad(w_in[:, n_lora:], ((0, 0), (B_NOPE, pad))).astype(BF16)
            wuq = w_uq.reshape(B_Q_LORA, B_HEADS, B_NOPE + B_ROPE)
            wuq = jnp.pad(wuq, ((0, 0), (0, 0), (0, pad)))
            wuqt = wuq.reshape(B_Q_LORA, B_HEADS * LANES).T.astype(BF16)
            wukv = w_ukv.reshape(B_KV_LORA, B_HEADS, B_NOPE + B_VDIM)
            wukn = jnp.pad(wukv[:, :, :B_NOPE], ((0, 0), (0, 0), (0, LANES - B_NOPE)))
            wukn = wukn.reshape(B_KV_LORA, B_HEADS * LANES).astype(BF16)
            wuvt = wukv[:, :, B_NOPE:].reshape(B_KV_LORA, B_HEADS * B_VDIM).T.astype(BF16)
            qt, k, vt = _proj_mla(xs, mod, g_norm, win, wkr, g_q.reshape(1, -1),
                                  g_kv.reshape(1, -1), wuqt, wukn, wuvt, tabs_b,
                                  qscale=(B_NOPE + B_ROPE) ** -0.5 * LOG2E)
            o = _flash(qt, k, vt, dummy_lam, dummy_gsub, diff=False, lam_init=0.0)
        else:
            w_qkv, rpb, w_o = l2_c_w_qkv, l2_c_rpb, l2_c_w_o
            wqt = w_qkv[:, :D_MODEL].T.astype(BF16)
            wk = w_qkv[:, D_MODEL:2 * D_MODEL].astype(BF16)
            wvt = w_qkv[:, 2 * D_MODEL:].T.astype(BF16)
            qt, k, vt = _proj_qkv(xs, mod, g_norm, wqt, wk, wvt, tabs_a, rope=False,
                                  qscale=C_HEAD_DIM ** -0.5 * LOG2E, v_rows=C_HEAD_DIM)
            o = _na_attention(qt, k, vt, _na_bias_table(rpb))
        xs = _post(o, xs, mod, g_norm, w_o.astype(BF16), w_gu.astype(BF16),
                   w_down.astype(BF16), n_tiles=N_TILES - 1 if last else N_TILES)
    return xs
```

```python
import functools
import math

import numpy as np
import jax
import jax.numpy as jnp
from jax import lax
from jax.experimental import pallas as pl
from jax.experimental.pallas import tpu as pltpu

D_MODEL = 1024
BATCH = 4
SEQ = 8192
DEPTH = 4
GRID_W = 64
CTX_LEN = 256
ROPE_THETA = 10000.0
NORM_EPS = 1e-6

A_HEAD_DIM = 64
A_HEADS = 8
B_HEADS = 16
B_NOPE = 64
B_ROPE = 32
B_VDIM = 64
B_Q_LORA = 256
B_KV_LORA = 256
C_HEAD_DIM = 64
C_HEADS = 16
NA_ROWS = 8
NA_COLS = 16
FFN_HIDDEN = 2816

LANES = 128
STRIP = 64
ROW_TILE = 256
T_ALL = SEQ + CTX_LEN
N_TILES = T_ALL // ROW_TILE
CTX_TILE = N_TILES - 1
N_PAIRS = 8
KEY_GROUPS = tuple((3 * g, 3) for g in range(9)) + tuple((27 + 2 * g, 2) for g in range(3))
SCORE_TILES = 2
PROB_LEAD = 1
SCORE_LEAD = 2
V_ONES = 16
LOG2E = math.log2(math.e)
MASK_VALUE = -1e30
VMEM_LIMIT = 60 * 1024 * 1024

F32 = jnp.float32
BF16 = jnp.bfloat16
NT_DIMS = (((1,), (1,)), ((), ()))


def _rms(x, g):
    return x * lax.rsqrt(jnp.mean(x * x, axis=-1, keepdims=True) + NORM_EPS) * g


def _sigmoid(x):
    return 1.0 / (1.0 + jnp.exp(-x))


def _params(n_axes):
    return pltpu.CompilerParams(
        dimension_semantics=("arbitrary",) * n_axes, vmem_limit_bytes=VMEM_LIMIT)


def _resident(shape):
    zeros = (0,) * len(shape)
    return pl.BlockSpec(shape, lambda *_: zeros, pipeline_mode=pl.Buffered(1))


def _mod_row(b, t):
    return jnp.where(t == CTX_TILE, BATCH, b)


def _mod_kernel(c_ref, w_ref, b_ref, o_ref):
    c = c_ref[...]
    a = (c * _sigmoid(c)).astype(BF16)
    o_ref[...] = jnp.dot(a, w_ref[...].astype(BF16), preferred_element_type=F32) + b_ref[...]


def _modulation(cc, w_mod, b_mod):
    n = w_mod.shape[1]
    blk = D_MODEL
    return pl.pallas_call(
        _mod_kernel,
        out_shape=jax.ShapeDtypeStruct((8, n), F32),
        grid=(n // blk,),
        in_specs=[pl.BlockSpec((8, D_MODEL), lambda j: (0, 0)),
                  pl.BlockSpec((D_MODEL, blk), lambda j: (0, j)),
                  pl.BlockSpec((1, blk), lambda j: (0, j))],
        out_specs=pl.BlockSpec((8, blk), lambda j: (0, j)),
        compiler_params=_params(1),
        name="modulation",
    )(cc, w_mod, b_mod.reshape(1, n))


def _rope_lanes(k, c, s, first, half):
    rot = jnp.where(first, pltpu.roll(k, LANES - half, 1), pltpu.roll(k, half, 1))
    return k * c + rot * s


def _store_values(vt_ref, vt, rows):
    ones = jnp.ones((V_ONES, ROW_TILE), BF16)
    for n in range(vt.shape[0] // rows):
        r0 = n * (rows + V_ONES)
        vt_ref[0, 0, r0:r0 + rows, :] = vt[n * rows:(n + 1) * rows].astype(BF16)
        vt_ref[0, 0, r0 + rows:r0 + rows + V_ONES, :] = ones


def _proj_qkv_kernel(x_ref, mod_ref, g_ref, wqt_ref, wk_ref, wvt_ref,
                     ck_ref, sk_ref, cqt_ref, sqt_ref,
                     qt_ref, k_ref, vt_ref, *, rope, qscale, v_rows):
    m = mod_ref[0]
    h = (_rms(x_ref[0], g_ref[0:1]) * (1.0 + m[1:2]) + m[0:1]).astype(BF16)
    qt = lax.dot_general(wqt_ref[...], h, NT_DIMS, preferred_element_type=F32) * qscale
    k = jnp.dot(h, wk_ref[...], preferred_element_type=F32)
    vt = lax.dot_general(wvt_ref[...], h, NT_DIMS, preferred_element_type=F32)
    _store_values(vt_ref, vt, v_rows)
    if not rope:
        qt_ref[0, 0] = qt.astype(BF16)
        k_ref[0] = k.astype(BF16)
        return
    half = A_HEAD_DIM // 2
    c, s = ck_ref[...], sk_ref[...]
    lane = lax.broadcasted_iota(jnp.int32, (ROW_TILE, LANES), 1)
    first = (lane % A_HEAD_DIM) < half
    for p in range(D_MODEL // LANES):
        sl = slice(p * LANES, (p + 1) * LANES)
        k_ref[0, :, sl] = _rope_lanes(k[:, sl], c, s, first, half).astype(BF16)
    ct, st = cqt_ref[...], sqt_ref[...]
    for g in range(D_MODEL // A_HEAD_DIM):
        r0 = g * A_HEAD_DIM
        t1, t2 = qt[r0:r0 + half], qt[r0 + half:r0 + 2 * half]
        qt_ref[0, 0, r0:r0 + half, :] = (t1 * ct - t2 * st).astype(BF16)
        qt_ref[0, 0, r0 + half:r0 + 2 * half, :] = (t1 * st + t2 * ct).astype(BF16)


def _row_specs():
    return [pl.BlockSpec((1, ROW_TILE, D_MODEL), lambda b, t: (b, t, 0)),
            pl.BlockSpec((1, 6, D_MODEL), lambda b, t: (_mod_row(b, t), 0, 0))]


def _rope_specs(half):
    return [pl.BlockSpec((ROW_TILE, LANES), lambda b, t: (t, 0)),
            pl.BlockSpec((ROW_TILE, LANES), lambda b, t: (t, 0)),
            pl.BlockSpec((half, ROW_TILE), lambda b, t: (0, t)),
            pl.BlockSpec((half, ROW_TILE), lambda b, t: (0, t))]


def _qkv_out(width, v_rows):
    vt_rows = D_MODEL // v_rows * (v_rows + V_ONES)
    shapes = (jax.ShapeDtypeStruct((BATCH, N_TILES, width, ROW_TILE), BF16),
              jax.ShapeDtypeStruct((BATCH, T_ALL, width), BF16),
              jax.ShapeDtypeStruct((BATCH, N_TILES, vt_rows, ROW_TILE), BF16))
    specs = (pl.BlockSpec((1, 1, width, ROW_TILE), lambda b, t: (b, t, 0, 0)),
             pl.BlockSpec((1, ROW_TILE, width), lambda b, t: (b, t, 0)),
             pl.BlockSpec((1, 1, vt_rows, ROW_TILE), lambda b, t: (b, t, 0, 0)))
    return shapes, specs


def _proj_qkv(x, mod, g_norm, wqt, wk, wvt, tabs, *, rope, qscale, v_rows):
    shapes, specs = _qkv_out(D_MODEL, v_rows)
    return pl.pallas_call(
        functools.partial(_proj_qkv_kernel, rope=rope, qscale=qscale, v_rows=v_rows),
        out_shape=shapes,
        grid=(BATCH, N_TILES),
        in_specs=_row_specs() + [
            _resident((4, D_MODEL)),
            _resident((D_MODEL, D_MODEL)), _resident((D_MODEL, D_MODEL)),
            _resident((D_MODEL, D_MODEL))] + _rope_specs(tabs[2].shape[0]),
        out_specs=specs,
        compiler_params=_params(2),
        name="proj_qkv",
    )(x, mod, g_norm, wqt, wk, wvt, *tabs)


def _proj_mla_kernel(x_ref, mod_ref, g_ref, win_ref, wkr_ref, gq_ref, gkv_ref,
                     wuqt_ref, wukn_ref, wuvt_ref, ck_ref, sk_ref, cqt_ref, sqt_ref,
                     qt_ref, k_ref, vt_ref, *, qscale):
    m = mod_ref[0]
    h = (_rms(x_ref[0], g_ref[0:1]) * (1.0 + m[1:2]) + m[0:1]).astype(BF16)
    z = jnp.dot(h, win_ref[...], preferred_element_type=F32)
    cq = _rms(z[:, :B_Q_LORA], gq_ref[...]).astype(BF16)
    ckv = _rms(z[:, B_Q_LORA:], gkv_ref[...]).astype(BF16)
    half = B_ROPE // 2

    qt = lax.dot_general(wuqt_ref[...], cq, NT_DIMS, preferred_element_type=F32) * qscale
    ct, st = cqt_ref[...], sqt_ref[...]
    for hh in range(B_HEADS):
        r0, r1 = hh * LANES, hh * LANES + B_NOPE
        qt_ref[0, 0, r0:r1, :] = qt[r0:r1].astype(BF16)
        t1, t2 = qt[r1:r1 + half], qt[r1 + half:r1 + 2 * half]
        qt_ref[0, 0, r1:r1 + half, :] = (t1 * ct - t2 * st).astype(BF16)
        qt_ref[0, 0, r1 + half:r1 + 2 * half, :] = (t1 * st + t2 * ct).astype(BF16)
        qt_ref[0, 0, r1 + B_ROPE:r0 + LANES, :] = qt[r1 + B_ROPE:r0 + LANES].astype(BF16)

    kr = jnp.dot(h, wkr_ref[...], preferred_element_type=F32)
    lane = lax.broadcasted_iota(jnp.int32, (ROW_TILE, LANES), 1)
    first = (lane >= B_NOPE) & (lane < B_NOPE + half)
    kr = _rope_lanes(kr, ck_ref[...], sk_ref[...], first, half)
    kn = jnp.dot(ckv, wukn_ref[...], preferred_element_type=F32)
    for hh in range(B_HEADS):
        sl = slice(hh * LANES, (hh + 1) * LANES)
        k_ref[0, :, sl] = (kn[:, sl] + kr).astype(BF16)

    vt = lax.dot_general(wuvt_ref[...], ckv, NT_DIMS, preferred_element_type=F32)
    _store_values(vt_ref, vt, B_VDIM)


def _proj_mla(x, mod, g_norm, win, wkr, gq, gkv, wuqt, wukn, wuvt, tabs, *, qscale):
    shapes, specs = _qkv_out(B_HEADS * LANES, B_VDIM)
    return pl.pallas_call(
        functools.partial(_proj_mla_kernel, qscale=qscale),
        out_shape=shapes,
        grid=(BATCH, N_TILES),
        in_specs=_row_specs() + [
            _resident((4, D_MODEL)),
            _resident(win.shape), _resident(wkr.shape),
            _resident((1, B_Q_LORA)), _resident((1, B_KV_LORA)),
            _resident(wuqt.shape), _resident(wukn.shape), _resident(wuvt.shape)]
            + _rope_specs(tabs[2].shape[0]),
        out_specs=specs,
        compiler_params=_params(2),
        name="proj_mla",
    )(x, mod, g_norm, win, wkr, gq, gkv, wuqt, wukn, wuvt, *tabs)


def _value_rows(diff):
    if diff:
        rows = 2 * A_HEAD_DIM
        return ((0, rows + V_ONES), (0, rows + V_ONES)), rows
    rows = LANES // 2
    return ((0, rows + V_ONES), (rows + V_ONES, 2 * (rows + V_ONES))), rows


def _split_queries(qt, qs_ref):
    dq = qt.shape[0]
    row = lax.broadcasted_iota(jnp.int32, qt.shape, 0)
    zero = jnp.zeros_like(qt)
    qs_ref[0] = jnp.where(row < dq // 2, qt, zero)
    qs_ref[1] = jnp.where(row >= dq // 2, qt, zero)


def _probabilities(s, m):
    return jnp.exp2((s - m).astype(BF16))


def _flash_kernel(lam_ref, gsub_ref, qt_ref, k_ref, vt_ref, o_ref,
                  qs_ref, s0_ref, s1_ref, p0_ref, p1_ref, p2_ref, a0_ref, a1_ref, a2_ref,
                  m_ref, acc_ref, fin_ref, *, diff, lam_init):
    v_rows, n_val = _value_rows(diff)
    s_bufs, p_bufs, a_bufs = (s0_ref, s1_ref), (p0_ref, p1_ref, p2_ref), (a0_ref, a1_ref, a2_ref)
    n_groups = len(KEY_GROUPS)

    def tile_rows(j):
        return slice(j * ROW_TILE, (j + 1) * ROW_TILE)

    def scores(t0, n, slot, s_ref):
        for j in range(0, n, SCORE_TILES):
            rows = min(SCORE_TILES, n - j) * ROW_TILE
            k = k_ref[0, (t0 + j) * ROW_TILE:(t0 + j) * ROW_TILE + rows, :]
            s_ref[slot, j * ROW_TILE:j * ROW_TILE + rows, :] = jnp.dot(
                k, qs_ref[slot], preferred_element_type=F32)

    def stats(n, s_ref, a_ref, first):
        for slot in range(2):
            m_prev = jnp.full((1, ROW_TILE), MASK_VALUE, F32) if first else m_ref[slot]
            strips = [s_ref[slot, c * STRIP:(c + 1) * STRIP, :] for c in range(n * ROW_TILE // STRIP)]
            m_next = jnp.maximum(
                m_prev, jnp.max(functools.reduce(jnp.maximum, strips), axis=0, keepdims=True))
            a_ref[slot] = jnp.exp2(m_prev - m_next)
            m_ref[slot] = m_next

    def probs_tile(j, s_ref, p_ref):
        for slot in range(2):
            m = m_ref[slot]
            for c in range(ROW_TILE // STRIP):
                rows = slice(j * ROW_TILE + c * STRIP, j * ROW_TILE + (c + 1) * STRIP)
                p_ref[slot, rows, :] = _probabilities(s_ref[slot, rows, :], m)

    def values_tile(t, j, p_ref, pv):
        for slot in range(2):
            lo, hi = v_rows[slot]
            pv[slot].append(jnp.dot(vt_ref[0, t, lo:hi, :], p_ref[slot, tile_rows(j), :],
                                    preferred_element_type=F32))

    def accumulate(pv, a_ref):
        for slot in range(2):
            acc_ref[slot] = a_ref[slot] * acc_ref[slot] + functools.reduce(jnp.add, pv[slot])

    def finalize(i, src_ref):
        acc_a, acc_b = src_ref[0], src_ref[1]
        ot_a = acc_a[:n_val] * (1.0 / acc_a[n_val:n_val + 1])
        ot_b = acc_b[:n_val] * (1.0 / acc_b[n_val:n_val + 1])
        if diff:
            lv = lam_ref[...]
            lam = (jnp.exp(jnp.sum(lv[0:1] * lv[1:2], axis=1, keepdims=True))
                   - jnp.exp(jnp.sum(lv[2:3] * lv[3:4], axis=1, keepdims=True)) + lam_init)
            o = (ot_a - lam * ot_b).T
            o = _rms(o, gsub_ref[...]) * (1.0 - lam_init)
        else:
            o = jnp.concatenate([ot_a, ot_b], axis=0).T
        o_ref[0, pl.ds(pl.multiple_of(i * ROW_TILE, ROW_TILE), ROW_TILE), :] = o.astype(BF16)

    acc_ref[...] = jnp.zeros(acc_ref.shape, F32)
    fin_ref[...] = jnp.ones(fin_ref.shape, F32)

    def scores_group(g, s_ref):
        for slot in range(2):
            scores(KEY_GROUPS[g][0], KEY_GROUPS[g][1], slot, s_ref)

    def probs_group(g, s_ref, p_ref, a_ref, first):
        stats(KEY_GROUPS[g][1], s_ref, a_ref, first)
        for j in range(KEY_GROUPS[g][1]):
            probs_tile(j, s_ref, p_ref)

    _split_queries(qt_ref[0, 0], qs_ref)
    for fill in range(-SCORE_LEAD, 0):
        g_prob, g_sco = fill + PROB_LEAD, fill + SCORE_LEAD
        if g_prob >= 0:
            probs_group(g_prob, s_bufs[g_prob % 2], p_bufs[g_prob % 3], a_bufs[g_prob % 3],
                        g_prob == 0)
        scores_group(g_sco, s_bufs[g_sco % 2])

    def body(i, carry):
        for g in range(n_groups):
            g_prob, g_sco = (g + PROB_LEAD) % n_groups, (g + SCORE_LEAD) % n_groups
            t_val, n_val_tiles = KEY_GROUPS[g]
            t_sco, n_sco = KEY_GROUPS[g_sco]
            n_prob = KEY_GROUPS[g_prob][1]
            s_prob, p_prob, a_prob = s_bufs[g_prob % 2], p_bufs[g_prob % 3], a_bufs[g_prob % 3]
            stats(n_prob, s_prob, a_prob, g_prob == 0)
            if g + SCORE_LEAD == n_groups:
                _split_queries(qt_ref[0, i + 1], qs_ref)
            pv = ([], [])
            for j in range(max(n_val_tiles, n_prob)):
                if j < n_prob:
                    probs_tile(j, s_prob, p_prob)
                if j < n_val_tiles:
                    values_tile(t_val + j, j, p_bufs[g % 3], pv)
                if j < 2:
                    scores(t_sco, n_sco, j, s_bufs[g_sco % 2])
            accumulate(pv, a_bufs[g % 3])
            if g == 0:
                finalize(jnp.maximum(i - 1, 0), fin_ref)
        fin_ref[...] = acc_ref[...]
        return carry
    lax.fori_loop(0, CTX_TILE, body, 0)
    finalize(CTX_TILE - 1, fin_ref)

    for slot in range(2):
        scores(CTX_TILE, 1, slot, s0_ref)
    stats(1, s0_ref, a0_ref, True)
    probs_tile(0, s0_ref, p0_ref)
    pv = ([], [])
    values_tile(CTX_TILE, 0, p0_ref, pv)
    accumulate(pv, a0_ref)
    finalize(CTX_TILE, acc_ref)


def _flash(qt, k, vt, lam, gsub, *, diff, lam_init):
    dq = k.shape[-1] // N_PAIRS
    vt_rows = vt.shape[2] // N_PAIRS
    acc_rows = _value_rows(diff)[0][0][1]
    group_rows = max(n for _, n in KEY_GROUPS) * ROW_TILE
    return pl.pallas_call(
        functools.partial(_flash_kernel, diff=diff, lam_init=lam_init),
        out_shape=jax.ShapeDtypeStruct((BATCH, T_ALL, D_MODEL), BF16),
        grid=(BATCH, N_PAIRS),
        in_specs=[pl.BlockSpec(lam.shape, lambda b, p: (0, 0)),
                  pl.BlockSpec(gsub.shape, lambda b, p: (0, 0)),
                  pl.BlockSpec((1, N_TILES, dq, ROW_TILE), lambda b, p: (b, 0, p, 0)),
                  pl.BlockSpec((1, T_ALL, dq), lambda b, p: (b, 0, p)),
                  pl.BlockSpec((1, N_TILES, vt_rows, ROW_TILE), lambda b, p: (b, 0, p, 0))],
        out_specs=pl.BlockSpec((1, T_ALL, LANES), lambda b, p: (b, 0, p)),
        scratch_shapes=[pltpu.VMEM((2, dq, ROW_TILE), BF16),
                        pltpu.VMEM((2, group_rows, ROW_TILE), F32),
                        pltpu.VMEM((2, group_rows, ROW_TILE), F32),
                        pltpu.VMEM((2, group_rows, ROW_TILE), BF16),
                        pltpu.VMEM((2, group_rows, ROW_TILE), BF16),
                        pltpu.VMEM((2, group_rows, ROW_TILE), BF16),
                        pltpu.VMEM((2, 1, ROW_TILE), F32),
                        pltpu.VMEM((2, 1, ROW_TILE), F32),
                        pltpu.VMEM((2, 1, ROW_TILE), F32),
                        pltpu.VMEM((2, 1, ROW_TILE), F32),
                        pltpu.VMEM((2, acc_rows, ROW_TILE), F32),
                        pltpu.VMEM((2, acc_rows, ROW_TILE), F32)],
        compiler_params=_params(2),
        name="flash_diff" if diff else "flash_pair",
    )(lam, gsub, qt, k, vt)


NA_Q_ROWS = ROW_TILE // GRID_W
NA_BAND_TILES = 3
NA_LAST_BAND = SEQ // ROW_TILE - NA_BAND_TILES


def _na_band_start(i):
    return jnp.clip(i - 1, 0, NA_LAST_BAND)


def _na_kernel(qt_ref, k0_ref, k1_ref, k2_ref, kc_ref, v0_ref, v1_ref, v2_ref, vc_ref,
               bias_ref, o_ref, qs_ref):
    k_refs = (k0_ref, k1_ref, k2_ref)
    v_refs = (v0_ref, v1_ref, v2_ref, vc_ref)
    v_rows, n_val = _value_rows(False)
    chains = [(b, slot) for b in range(BATCH) for slot in range(2)]
    for b in range(BATCH):
        _split_queries(qt_ref[b, 0], qs_ref.at[b])
    scores = {}
    for b, slot in chains:
        qs = qs_ref[b, slot]
        s = [jnp.dot(k_refs[j][b], qs, preferred_element_type=F32)
             + bias_ref[0, slot, j * ROW_TILE:(j + 1) * ROW_TILE, :]
             for j in range(NA_BAND_TILES)]
        s.append(jnp.dot(kc_ref[b], qs, preferred_element_type=F32))
        scores[b, slot] = s
    outs = {}
    for b, slot in chains:
        lo, hi = v_rows[slot]
        s = scores[b, slot]
        m = jnp.max(functools.reduce(jnp.maximum, s), axis=0, keepdims=True)
        acc = functools.reduce(jnp.add, [
            jnp.dot(r[b, 0, lo:hi, :], _probabilities(sj, m), preferred_element_type=F32)
            for r, sj in zip(v_refs, s)])
        outs[b, slot] = acc[:n_val] * (1.0 / acc[n_val:n_val + 1])
    for b in range(BATCH):
        o_ref[b] = jnp.concatenate([outs[b, 0], outs[b, 1]], axis=0).T.astype(BF16)


def _na_variant(i):
    return jnp.where(i == 0, 0, jnp.where(i < NA_LAST_BAND + NA_BAND_TILES - 1, 1,
                                          jnp.where(i < CTX_TILE, 2, 3)))


def _na_attention(qt, k, vt, bias):
    vt_rows = vt.shape[2] // N_PAIRS

    def kspec(j):
        return pl.BlockSpec((BATCH, ROW_TILE, LANES), lambda p, i: (0, _na_band_start(i) + j, p))

    def vspec(j):
        return pl.BlockSpec((BATCH, 1, vt_rows, ROW_TILE),
                            lambda p, i: (0, _na_band_start(i) + j, p, 0))

    return pl.pallas_call(
        _na_kernel,
        out_shape=jax.ShapeDtypeStruct((BATCH, T_ALL, D_MODEL), BF16),
        grid=(N_PAIRS, N_TILES),
        in_specs=[pl.BlockSpec((BATCH, 1, LANES, ROW_TILE), lambda p, i: (0, i, p, 0)),
                  kspec(0), kspec(1), kspec(2),
                  pl.BlockSpec((BATCH, ROW_TILE, LANES), lambda p, i: (0, CTX_TILE, p)),
                  vspec(0), vspec(1), vspec(2),
                  pl.BlockSpec((BATCH, 1, vt_rows, ROW_TILE), lambda p, i: (0, CTX_TILE, p, 0)),
                  pl.BlockSpec((1, 2, NA_BAND_TILES * ROW_TILE, ROW_TILE),
                               lambda p, i: (_na_variant(i), p, 0, 0))],
        out_specs=pl.BlockSpec((BATCH, ROW_TILE, LANES), lambda p, i: (0, i, p)),
        scratch_shapes=[pltpu.VMEM((BATCH, 2, LANES, ROW_TILE), BF16)],
        compiler_params=_params(2),
        name="na_attention",
    )(qt, k, k, k, k, vt, vt, vt, vt, bias)


def _na_bias_table(rpb):
    rows = SEQ // GRID_W
    band_rows = NA_BAND_TILES * NA_Q_ROWS
    col = np.arange(GRID_W)
    cstart = np.clip(col - NA_COLS // 2, 0, GRID_W - NA_COLS)
    ok_c = (col[None, :] >= cstart[:, None]) & (col[None, :] < cstart[:, None] + NA_COLS)
    dc = col[None, :] - col[:, None] + NA_COLS - 1
    onehot_c = ((dc[:, :, None] == np.arange(2 * NA_COLS - 1)) & ok_c[:, :, None]).astype(np.float32)
    dr = np.zeros((4, NA_Q_ROWS, band_rows), np.int32)
    ok_r = np.zeros((4, NA_Q_ROWS, band_rows), bool)
    for variant, i in enumerate((0, 1, SEQ // ROW_TILE - 1)):
        j = min(max(i - 1, 0), NA_LAST_BAND)
        r = NA_Q_ROWS * i + np.arange(NA_Q_ROWS)[:, None]
        k_row = NA_Q_ROWS * j + np.arange(band_rows)[None, :]
        rstart = np.clip(r - NA_ROWS // 2, 0, rows - NA_ROWS)
        ok_r[variant] = (k_row >= rstart) & (k_row < rstart + NA_ROWS)
        dr[variant] = np.where(ok_r[variant], k_row - r + NA_ROWS - 1, 0)
    by_col = jnp.einsum("hab,ckb->hack", rpb.astype(F32) * LOG2E, jnp.asarray(onehot_c),
                        precision=lax.Precision.HIGHEST)
    by_col = jnp.where(ok_c[None, None], by_col, MASK_VALUE)
    full = jnp.take(by_col, jnp.asarray(dr.reshape(-1)), axis=1)
    full = full.reshape(C_HEADS, 4, NA_Q_ROWS, band_rows, GRID_W, GRID_W)
    full = jnp.where(ok_r[None, :, :, :, None, None], full, MASK_VALUE)
    full = full.transpose(1, 0, 3, 5, 2, 4)
    return full.reshape(4, C_HEADS, NA_BAND_TILES * ROW_TILE, ROW_TILE)


def _post_kernel(o_ref, x_ref, mod_ref, g_ref, wo_ref, wgu_ref, wd_ref, xo_ref):
    m = mod_ref[0]
    y = jnp.dot(o_ref[0], wo_ref[...], preferred_element_type=F32)
    x1 = x_ref[0] + m[2:3] * _rms(y, g_ref[1:2])
    h2 = (_rms(x1, g_ref[2:3]) * (1.0 + m[4:5]) + m[3:4]).astype(BF16)
    gu = jnp.dot(h2, wgu_ref[...], preferred_element_type=F32)
    gate, up = gu[:, :FFN_HIDDEN], gu[:, FFN_HIDDEN:]
    a = (gate * _sigmoid(gate) * up).astype(BF16)
    f = jnp.dot(a, wd_ref[...], preferred_element_type=F32)
    xo_ref[0] = x1 + m[5:6] * _rms(f, g_ref[3:4])


def _post(o, x, mod, g_norm, wo, wgu, wd, *, n_tiles):
    return pl.pallas_call(
        _post_kernel,
        out_shape=jax.ShapeDtypeStruct((BATCH, n_tiles * ROW_TILE, D_MODEL), F32),
        grid=(BATCH, n_tiles),
        in_specs=[pl.BlockSpec((1, ROW_TILE, D_MODEL), lambda b, t: (b, t, 0))] + _row_specs() + [
            _resident((4, D_MODEL)), _resident(wo.shape), _resident(wgu.shape),
            _resident(wd.shape)],
        out_specs=pl.BlockSpec((1, ROW_TILE, D_MODEL), lambda b, t: (b, t, 0)),
        compiler_params=_params(2),
        name="post_attention",
    )(o, x, mod, g_norm, wo, wgu, wd)


def _rope_angles(rot_dim):
    n_freq = rot_dim // 4
    inv = ROPE_THETA ** (-jnp.arange(n_freq, dtype=F32) / n_freq)
    t = jnp.arange(SEQ, dtype=jnp.int32)
    row = (t // GRID_W).astype(F32)
    col = (t % GRID_W).astype(F32)
    ang = jnp.concatenate([row[:, None] * inv, col[:, None] * inv], axis=-1)
    cos = jnp.concatenate([jnp.cos(ang), jnp.ones((CTX_LEN, rot_dim // 2), F32)], axis=0)
    sin = jnp.concatenate([jnp.sin(ang), jnp.zeros((CTX_LEN, rot_dim // 2), F32)], axis=0)
    return cos, sin


def _rope_tables_diff():
    cos, sin = _rope_angles(A_HEAD_DIM)
    c_lane = jnp.tile(cos, (1, 4))
    s_lane = jnp.tile(jnp.concatenate([-sin, sin], axis=1), (1, 2))
    return c_lane, s_lane, cos.T, sin.T


def _rope_tables_mla():
    cos, sin = _rope_angles(B_ROPE)
    ones = jnp.ones((T_ALL, B_NOPE), F32)
    zeros = jnp.zeros((T_ALL, B_NOPE), F32)
    tail = LANES - B_NOPE - B_ROPE
    c_lane = jnp.concatenate([ones, cos, cos, ones[:, :tail]], axis=1)
    s_lane = jnp.concatenate([zeros, -sin, sin, zeros[:, :tail]], axis=1)
    return c_lane, s_lane, cos.T, sin.T


def kernel(x, c, ctx, c_ctx, l0_w_mod, l0_b_mod, l0_g_norm, l0_w_gu, l0_w_down, l0_a_w_qkv, l0_a_w_o, l0_a_lam, l0_a_g_sub, l1_w_mod, l1_b_mod, l1_g_norm, l1_w_gu, l1_w_down, l1_b_w_in, l1_b_g_q, l1_b_g_kv, l1_b_w_uq, l1_b_w_ukv, l1_b_w_o, l2_w_mod, l2_b_mod, l2_g_norm, l2_w_gu, l2_w_down, l2_c_w_qkv, l2_c_rpb, l2_c_w_o, l3_w_mod, l3_b_mod, l3_g_norm, l3_w_gu, l3_w_down, l3_a_w_qkv, l3_a_w_o, l3_a_lam, l3_a_g_sub):
    common = [
        (l0_w_mod, l0_b_mod, l0_g_norm, l0_w_gu, l0_w_down),
        (l1_w_mod, l1_b_mod, l1_g_norm, l1_w_gu, l1_w_down),
        (l2_w_mod, l2_b_mod, l2_g_norm, l2_w_gu, l2_w_down),
        (l3_w_mod, l3_b_mod, l3_g_norm, l3_w_gu, l3_w_down),
    ]
    diff_params = {0: (l0_a_w_qkv, l0_a_w_o, l0_a_lam, l0_a_g_sub),
                   3: (l3_a_w_qkv, l3_a_w_o, l3_a_lam, l3_a_g_sub)}

    xs = jnp.concatenate([x, ctx], axis=1)
    cc = jnp.concatenate([c, c_ctx[None, :], jnp.zeros((8 - BATCH - 1, D_MODEL), F32)], axis=0)
    tabs_a = _rope_tables_diff()
    tabs_b = _rope_tables_mla()
    dummy_lam = jnp.zeros((4, A_HEAD_DIM), F32)
    dummy_gsub = jnp.ones((1, LANES), F32)

    for i in range(DEPTH):
        w_mod, b_mod, g_norm, w_gu, w_down = common[i]
        last = i == DEPTH - 1
        mod = _modulation(cc, w_mod, b_mod).reshape(8, 6, D_MODEL)
        kind = i % 3
        if kind == 0:
            w_qkv, w_o, lam, g_sub = diff_params[i]
            lam_init = 0.8 - 0.6 * math.exp(-0.3 * i)
            wqt = w_qkv[:, :D_MODEL].T.astype(BF16)
            wk = w_qkv[:, D_MODEL:2 * D_MODEL].astype(BF16)
            wvt = w_qkv[:, 2 * D_MODEL:].T.astype(BF16)
            qt, k, vt = _proj_qkv(xs, mod, g_norm, wqt, wk, wvt, tabs_a, rope=True,
                                  qscale=A_HEAD_DIM ** -0.5 * LOG2E, v_rows=2 * A_HEAD_DIM)
            o = _flash(qt, k, vt, lam, g_sub.reshape(1, LANES), diff=True, lam_init=lam_init)
        elif kind == 1:
            w_in, g_q, g_kv, w_uq, w_ukv, w_o = (l1_b_w_in, l1_b_g_q, l1_b_g_kv, l1_b_w_uq,
                                                 l1_b_w_ukv, l1_b_w_o)
            n_lora = B_Q_LORA + B_KV_LORA
            pad = LANES - B_NOPE - B_ROPE
            win = w_in[:, :n_lora].astype(BF16)
            wkr = jnp.pad(w_in[:, n_lora:], ((0, 0), (B_NOPE, pad))).astype(BF16)
            wuq = w_uq.reshape(B_Q_LORA, B_HEADS, B_NOPE + B_ROPE)
            wuq = jnp.pad(wuq, ((0, 0), (0, 0), (0, pad)))
            wuqt = wuq.reshape(B_Q_LORA, B_HEADS * LANES).T.astype(BF16)
            wukv = w_ukv.reshape(B_KV_LORA, B_HEADS, B_NOPE + B_VDIM)
            wukn = jnp.pad(wukv[:, :, :B_NOPE], ((0, 0), (0, 0), (0, LANES - B_NOPE)))
            wukn = wukn.reshape(B_KV_LORA, B_HEADS * LANES).astype(BF16)
            wuvt = wukv[:, :, B_NOPE:].reshape(B_KV_LORA, B_HEADS * B_VDIM).T.astype(BF16)
            qt, k, vt = _proj_mla(xs, mod, g_norm, win, wkr, g_q.reshape(1, -1),
                                  g_kv.reshape(1, -1), wuqt, wukn, wuvt, tabs_b,
                                  qscale=(B_NOPE + B_ROPE) ** -0.5 * LOG2E)
            o = _flash(qt, k, vt, dummy_lam, dummy_gsub, diff=False, lam_init=0.0)
        else:
            w_qkv, rpb, w_o = l2_c_w_qkv, l2_c_rpb, l2_c_w_o
            wqt = w_qkv[:, :D_MODEL].T.astype(BF16)
            wk = w_qkv[:, D_MODEL:2 * D_MODEL].astype(BF16)
            wvt = w_qkv[:, 2 * D_MODEL:].T.astype(BF16)
            qt, k, vt = _proj_qkv(xs, mod, g_norm, wqt, wk, wvt, tabs_a, rope=False,
                                  qscale=C_HEAD_DIM ** -0.5 * LOG2E, v_rows=C_HEAD_DIM)
            o = _na_attention(qt, k, vt, _na_bias_table(rpb))
        xs = _post(o, xs, mod, g_norm, w_o.astype(BF16), w_gu.astype(BF16),
                   w_down.astype(BF16), n_tiles=N_TILES - 1 if last else N_TILES)
    return xs
```

```python
import functools
import math

import numpy as np
import jax
import jax.numpy as jnp
from jax import lax
from jax.experimental import pallas as pl
from jax.experimental.pallas import tpu as pltpu

D_MODEL = 1024
BATCH = 4
SEQ = 8192
DEPTH = 4
GRID_W = 64
CTX_LEN = 256
ROPE_THETA = 10000.0
NORM_EPS = 1e-6

A_HEAD_DIM = 64
A_HEADS = 8
B_HEADS = 16
B_NOPE = 64
B_ROPE = 32
B_VDIM = 64
B_Q_LORA = 256
B_KV_LORA = 256
C_HEAD_DIM = 64
C_HEADS = 16
NA_ROWS = 8
NA_COLS = 16
FFN_HIDDEN = 2816

LANES = 128
STRIP = 64
ROW_TILE = 256
T_ALL = SEQ + CTX_LEN
N_TILES = T_ALL // ROW_TILE
CTX_TILE = N_TILES - 1
N_PAIRS = 8
KEY_GROUPS = tuple((3 * g, 3) for g in range(9)) + tuple((27 + 2 * g, 2) for g in range(3))
SCORE_TILES = 2
PROB_LEAD = 1
SCORE_LEAD = 2
V_ONES = 16
LOG2E = math.log2(math.e)
MASK_VALUE = -1e30
VMEM_LIMIT = 60 * 1024 * 1024

F32 = jnp.float32
BF16 = jnp.bfloat16
NT_DIMS = (((1,), (1,)), ((), ()))


def _rms(x, g):
    return x * lax.rsqrt(jnp.mean(x * x, axis=-1, keepdims=True) + NORM_EPS) * g


def _sigmoid(x):
    return 1.0 / (1.0 + jnp.exp(-x))


def _params(n_axes):
    return pltpu.CompilerParams(
        dimension_semantics=("arbitrary",) * n_axes, vmem_limit_bytes=VMEM_LIMIT)


def _resident(shape):
    zeros = (0,) * len(shape)
    return pl.BlockSpec(shape, lambda *_: zeros, pipeline_mode=pl.Buffered(1))


def _mod_row(b, t):
    return jnp.where(t == CTX_TILE, BATCH, b)


def _mod_kernel(c_ref, w_ref, b_ref, o_ref):
    c = c_ref[...]
    a = (c * _sigmoid(c)).astype(BF16)
    o_ref[...] = jnp.dot(a, w_ref[...].astype(BF16), preferred_element_type=F32) + b_ref[...]


def _modulation(cc, w_mod, b_mod):
    n = w_mod.shape[1]
    blk = D_MODEL
    return pl.pallas_call(
        _mod_kernel,
        out_shape=jax.ShapeDtypeStruct((8, n), F32),
        grid=(n // blk,),
        in_specs=[pl.BlockSpec((8, D_MODEL), lambda j: (0, 0)),
                  pl.BlockSpec((D_MODEL, blk), lambda j: (0, j)),
                  pl.BlockSpec((1, blk), lambda j: (0, j))],
        out_specs=pl.BlockSpec((8, blk), lambda j: (0, j)),
        compiler_params=_params(1),
        name="modulation",
    )(cc, w_mod, b_mod.reshape(1, n))


def _rope_lanes(k, c, s, first, half):
    rot = jnp.where(first, pltpu.roll(k, LANES - half, 1), pltpu.roll(k, half, 1))
    return k * c + rot * s


def _store_values(vt_ref, vt, rows):
    ones = jnp.ones((V_ONES, ROW_TILE), BF16)
    for n in range(vt.shape[0] // rows):
        r0 = n * (rows + V_ONES)
        vt_ref[0, 0, r0:r0 + rows, :] = vt[n * rows:(n + 1) * rows].astype(BF16)
        vt_ref[0, 0, r0 + rows:r0 + rows + V_ONES, :] = ones


def _proj_qkv_kernel(x_ref, mod_ref, g_ref, wqt_ref, wk_ref, wvt_ref,
                     ck_ref, sk_ref, cqt_ref, sqt_ref,
                     qt_ref, k_ref, vt_ref, *, rope, qscale, v_rows):
    m = mod_ref[0]
    h = (_rms(x_ref[0], g_ref[0:1]) * (1.0 + m[1:2]) + m[0:1]).astype(BF16)
    qt = lax.dot_general(wqt_ref[...], h, NT_DIMS, preferred_element_type=F32) * qscale
    k = jnp.dot(h, wk_ref[...], preferred_element_type=F32)
    vt = lax.dot_general(wvt_ref[...], h, NT_DIMS, preferred_element_type=F32)
    _store_values(vt_ref, vt, v_rows)
    if not rope:
        qt_ref[0, 0] = qt.astype(BF16)
        k_ref[0] = k.astype(BF16)
        return
    half = A_HEAD_DIM // 2
    c, s = ck_ref[...], sk_ref[...]
    lane = lax.broadcasted_iota(jnp.int32, (ROW_TILE, LANES), 1)
    first = (lane % A_HEAD_DIM) < half
    for p in range(D_MODEL // LANES):
        sl = slice(p * LANES, (p + 1) * LANES)
        k_ref[0, :, sl] = _rope_lanes(k[:, sl], c, s, first, half).astype(BF16)
    ct, st = cqt_ref[...], sqt_ref[...]
    for g in range(D_MODEL // A_HEAD_DIM):
        r0 = g * A_HEAD_DIM
        t1, t2 = qt[r0:r0 + half], qt[r0 + half:r0 + 2 * half]
        qt_ref[0, 0, r0:r0 + half, :] = (t1 * ct - t2 * st).astype(BF16)
        qt_ref[0, 0, r0 + half:r0 + 2 * half, :] = (t1 * st + t2 * ct).astype(BF16)


def _row_specs():
    return [pl.BlockSpec((1, ROW_TILE, D_MODEL), lambda b, t: (b, t, 0)),
            pl.BlockSpec((1, 6, D_MODEL), lambda b, t: (_mod_row(b, t), 0, 0))]


def _rope_specs(half):
    return [pl.BlockSpec((ROW_TILE, LANES), lambda b, t: (t, 0)),
            pl.BlockSpec((ROW_TILE, LANES), lambda b, t: (t, 0)),
            pl.BlockSpec((half, ROW_TILE), lambda b, t: (0, t)),
            pl.BlockSpec((half, ROW_TILE), lambda b, t: (0, t))]


def _qkv_out(width, v_rows):
    vt_rows = D_MODEL // v_rows * (v_rows + V_ONES)
    shapes = (jax.ShapeDtypeStruct((BATCH, N_TILES, width, ROW_TILE), BF16),
              jax.ShapeDtypeStruct((BATCH, T_ALL, width), BF16),
              jax.ShapeDtypeStruct((BATCH, N_TILES, vt_rows, ROW_TILE), BF16))
    specs = (pl.BlockSpec((1, 1, width, ROW_TILE), lambda b, t: (b, t, 0, 0)),
             pl.BlockSpec((1, ROW_TILE, width), lambda b, t: (b, t, 0)),
             pl.BlockSpec((1, 1, vt_rows, ROW_TILE), lambda b, t: (b, t, 0, 0)))
    return shapes, specs


def _proj_qkv(x, mod, g_norm, wqt, wk, wvt, tabs, *, rope, qscale, v_rows):
    shapes, specs = _qkv_out(D_MODEL, v_rows)
    return pl.pallas_call(
        functools.partial(_proj_qkv_kernel, rope=rope, qscale=qscale, v_rows=v_rows),
        out_shape=shapes,
        grid=(BATCH, N_TILES),
        in_specs=_row_specs() + [
            _resident((4, D_MODEL)),
            _resident((D_MODEL, D_MODEL)), _resident((D_MODEL, D_MODEL)),
            _resident((D_MODEL, D_MODEL))] + _rope_specs(tabs[2].shape[0]),
        out_specs=specs,
        compiler_params=_params(2),
        name="proj_qkv",
    )(x, mod, g_norm, wqt, wk, wvt, *tabs)


def _proj_mla_kernel(x_ref, mod_ref, g_ref, win_ref, wkr_ref, gq_ref, gkv_ref,
                     wuqt_ref, wukn_ref, wuvt_ref, ck_ref, sk_ref, cqt_ref, sqt_ref,
                     qt_ref, k_ref, vt_ref, *, qscale):
    m = mod_ref[0]
    h = (_rms(x_ref[0], g_ref[0:1]) * (1.0 + m[1:2]) + m[0:1]).astype(BF16)
    z = jnp.dot(h, win_ref[...], preferred_element_type=F32)
    cq = _rms(z[:, :B_Q_LORA], gq_ref[...]).astype(BF16)
    ckv = _rms(z[:, B_Q_LORA:], gkv_ref[...]).astype(BF16)
    half = B_ROPE // 2

    qt = lax.dot_general(wuqt_ref[...], cq, NT_DIMS, preferred_element_type=F32) * qscale
    ct, st = cqt_ref[...], sqt_ref[...]
    for hh in range(B_HEADS):
        r0, r1 = hh * LANES, hh * LANES + B_NOPE
        qt_ref[0, 0, r0:r1, :] = qt[r0:r1].astype(BF16)
        t1, t2 = qt[r1:r1 + half], qt[r1 + half:r1 + 2 * half]
        qt_ref[0, 0, r1:r1 + half, :] = (t1 * ct - t2 * st).astype(BF16)
        qt_ref[0, 0, r1 + half:r1 + 2 * half, :] = (t1 * st + t2 * ct).astype(BF16)
        qt_ref[0, 0, r1 + B_ROPE:r0 + LANES, :] = qt[r1 + B_ROPE:r0 + LANES].astype(BF16)

    kr = jnp.dot(h, wkr_ref[...], preferred_element_type=F32)
    lane = lax.broadcasted_iota(jnp.int32, (ROW_TILE, LANES), 1)
    first = (lane >= B_NOPE) & (lane < B_NOPE + half)
    kr = _rope_lanes(kr, ck_ref[...], sk_ref[...], first, half)
    kn = jnp.dot(ckv, wukn_ref[...], preferred_element_type=F32)
    for hh in range(B_HEADS):
        sl = slice(hh * LANES, (hh + 1) * LANES)
        k_ref[0, :, sl] = (kn[:, sl] + kr).astype(BF16)

    vt = lax.dot_general(wuvt_ref[...], ckv, NT_DIMS, preferred_element_type=F32)
    _store_values(vt_ref, vt, B_VDIM)


def _proj_mla(x, mod, g_norm, win, wkr, gq, gkv, wuqt, wukn, wuvt, tabs, *, qscale):
    shapes, specs = _qkv_out(B_HEADS * LANES, B_VDIM)
    return pl.pallas_call(
        functools.partial(_proj_mla_kernel, qscale=qscale),
        out_shape=shapes,
        grid=(BATCH, N_TILES),
        in_specs=_row_specs() + [
            _resident((4, D_MODEL)),
            _resident(win.shape), _resident(wkr.shape),
            _resident((1, B_Q_LORA)), _resident((1, B_KV_LORA)),
            _resident(wuqt.shape), _resident(wukn.shape), _resident(wuvt.shape)]
            + _rope_specs(tabs[2].shape[0]),
        out_specs=specs,
        compiler_params=_params(2),
        name="proj_mla",
    )(x, mod, g_norm, win, wkr, gq, gkv, wuqt, wukn, wuvt, *tabs)


def _value_rows(diff):
    if diff:
        rows = 2 * A_HEAD_DIM
        return ((0, rows + V_ONES), (0, rows + V_ONES)), rows
    rows = LANES // 2
    return ((0, rows + V_ONES), (rows + V_ONES, 2 * (rows + V_ONES))), rows


def _split_queries(qt, qs_ref):
    dq = qt.shape[0]
    row = lax.broadcasted_iota(jnp.int32, qt.shape, 0)
    zero = jnp.zeros_like(qt)
    qs_ref[0] = jnp.where(row < dq // 2, qt, zero)
    qs_ref[1] = jnp.where(row >= dq // 2, qt, zero)


def _probabilities(s, m):
    return jnp.exp2(s - m).astype(BF16)


def _flash_kernel(lam_ref, gsub_ref, qt_ref, k_ref, vt_ref, o_ref,
                  qs_ref, s0_ref, s1_ref, p0_ref, p1_ref, p2_ref, a0_ref, a1_ref, a2_ref,
                  m_ref, acc_ref, fin_ref, *, diff, lam_init):
    v_rows, n_val = _value_rows(diff)
    s_bufs, p_bufs, a_bufs = (s0_ref, s1_ref), (p0_ref, p1_ref, p2_ref), (a0_ref, a1_ref, a2_ref)
    n_groups = len(KEY_GROUPS)

    def tile_rows(j):
        return slice(j * ROW_TILE, (j + 1) * ROW_TILE)

    def scores(t0, n, slot, s_ref):
        for j in range(0, n, SCORE_TILES):
            rows = min(SCORE_TILES, n - j) * ROW_TILE
            k = k_ref[0, (t0 + j) * ROW_TILE:(t0 + j) * ROW_TILE + rows, :]
            s_ref[slot, j * ROW_TILE:j * ROW_TILE + rows, :] = jnp.dot(
                k, qs_ref[slot], preferred_element_type=F32)

    def stats(n, s_ref, a_ref, first):
        for slot in range(2):
            m_prev = jnp.full((1, ROW_TILE), MASK_VALUE, F32) if first else m_ref[slot]
            strips = [s_ref[slot, c * STRIP:(c + 1) * STRIP, :] for c in range(n * ROW_TILE // STRIP)]
            m_next = jnp.maximum(
                m_prev, jnp.max(functools.reduce(jnp.maximum, strips), axis=0, keepdims=True))
            a_ref[slot] = jnp.exp2(m_prev - m_next)
            m_ref[slot] = m_next

    def probs_tile(j, s_ref, p_ref):
        for slot in range(2):
            m = m_ref[slot]
            for c in range(ROW_TILE // STRIP):
                rows = slice(j * ROW_TILE + c * STRIP, j * ROW_TILE + (c + 1) * STRIP)
                p_ref[slot, rows, :] = _probabilities(s_ref[slot, rows, :], m)

    def values_tile(t, j, p_ref, pv):
        for slot in range(2):
            lo, hi = v_rows[slot]
            pv[slot].append(jnp.dot(vt_ref[0, t, lo:hi, :], p_ref[slot, tile_rows(j), :],
                                    preferred_element_type=F32))

    def accumulate(pv, a_ref):
        for slot in range(2):
            acc_ref[slot] = a_ref[slot] * acc_ref[slot] + functools.reduce(jnp.add, pv[slot])

    def finalize(i, src_ref):
        acc_a, acc_b = src_ref[0], src_ref[1]
        ot_a = acc_a[:n_val] * (1.0 / acc_a[n_val:n_val + 1])
        ot_b = acc_b[:n_val] * (1.0 / acc_b[n_val:n_val + 1])
        if diff:
            lv = lam_ref[...]
            lam = (jnp.exp(jnp.sum(lv[0:1] * lv[1:2], axis=1, keepdims=True))
                   - jnp.exp(jnp.sum(lv[2:3] * lv[3:4], axis=1, keepdims=True)) + lam_init)
            o = (ot_a - lam * ot_b).T
            o = _rms(o, gsub_ref[...]) * (1.0 - lam_init)
        else:
            o = jnp.concatenate([ot_a, ot_b], axis=0).T
        o_ref[0, pl.ds(pl.multiple_of(i * ROW_TILE, ROW_TILE), ROW_TILE), :] = o.astype(BF16)

    acc_ref[...] = jnp.zeros(acc_ref.shape, F32)
    fin_ref[...] = jnp.ones(fin_ref.shape, F32)

    def scores_group(g, s_ref):
        for slot in range(2):
            scores(KEY_GROUPS[g][0], KEY_GROUPS[g][1], slot, s_ref)

    def probs_group(g, s_ref, p_ref, a_ref, first):
        stats(KEY_GROUPS[g][1], s_ref, a_ref, first)
        for j in range(KEY_GROUPS[g][1]):
            probs_tile(j, s_ref, p_ref)

    _split_queries(qt_ref[0, 0], qs_ref)
    for fill in range(-SCORE_LEAD, 0):
        g_prob, g_sco = fill + PROB_LEAD, fill + SCORE_LEAD
        if g_prob >= 0:
            probs_group(g_prob, s_bufs[g_prob % 2], p_bufs[g_prob % 3], a_bufs[g_prob % 3],
                        g_prob == 0)
        scores_group(g_sco, s_bufs[g_sco % 2])

    def body(i, carry):
        for g in range(n_groups):
            g_prob, g_sco = (g + PROB_LEAD) % n_groups, (g + SCORE_LEAD) % n_groups
            t_val, n_val_tiles = KEY_GROUPS[g]
            t_sco, n_sco = KEY_GROUPS[g_sco]
            n_prob = KEY_GROUPS[g_prob][1]
            s_prob, p_prob, a_prob = s_bufs[g_prob % 2], p_bufs[g_prob % 3], a_bufs[g_prob % 3]
            stats(n_prob, s_prob, a_prob, g_prob == 0)
            if g + SCORE_LEAD == n_groups:
                _split_queries(qt_ref[0, i + 1], qs_ref)
            pv = ([], [])
            for j in range(max(n_val_tiles, n_prob)):
                if j < n_prob:
                    probs_tile(j, s_prob, p_prob)
                if j < n_val_tiles:
                    values_tile(t_val + j, j, p_bufs[g % 3], pv)
                if j < 2:
                    scores(t_sco, n_sco, j, s_bufs[g_sco % 2])
            accumulate(pv, a_bufs[g % 3])
            if g == 0:
                finalize(jnp.maximum(i - 1, 0), fin_ref)
        fin_ref[...] = acc_ref[...]
        return carry
    lax.fori_loop(0, CTX_TILE, body, 0)
    finalize(CTX_TILE - 1, fin_ref)

    for slot in range(2):
        scores(CTX_TILE, 1, slot, s0_ref)
    stats(1, s0_ref, a0_ref, True)
    probs_tile(0, s0_ref, p0_ref)
    pv = ([], [])
    values_tile(CTX_TILE, 0, p0_ref, pv)
    accumulate(pv, a0_ref)
    finalize(CTX_TILE, acc_ref)


def _flash(qt, k, vt, lam, gsub, *, diff, lam_init):
    dq = k.shape[-1] // N_PAIRS
    vt_rows = vt.shape[2] // N_PAIRS
    acc_rows = _value_rows(diff)[0][0][1]
    group_rows = max(n for _, n in KEY_GROUPS) * ROW_TILE
    return pl.pallas_call(
        functools.partial(_flash_kernel, diff=diff, lam_init=lam_init),
        out_shape=jax.ShapeDtypeStruct((BATCH, T_ALL, D_MODEL), BF16),
        grid=(BATCH, N_PAIRS),
        in_specs=[pl.BlockSpec(lam.shape, lambda b, p: (0, 0)),
                  pl.BlockSpec(gsub.shape, lambda b, p: (0, 0)),
                  pl.BlockSpec((1, N_TILES, dq, ROW_TILE), lambda b, p: (b, 0, p, 0)),
                  pl.BlockSpec((1, T_ALL, dq), lambda b, p: (b, 0, p)),
                  pl.BlockSpec((1, N_TILES, vt_rows, ROW_TILE), lambda b, p: (b, 0, p, 0))],
        out_specs=pl.BlockSpec((1, T_ALL, LANES), lambda b, p: (b, 0, p)),
        scratch_shapes=[pltpu.VMEM((2, dq, ROW_TILE), BF16),
                        pltpu.VMEM((2, group_rows, ROW_TILE), F32),
                        pltpu.VMEM((2, group_rows, ROW_TILE), F32),
                        pltpu.VMEM((2, group_rows, ROW_TILE), BF16),
                        pltpu.VMEM((2, group_rows, ROW_TILE), BF16),
                        pltpu.VMEM((2, group_rows, ROW_TILE), BF16),
                        pltpu.VMEM((2, 1, ROW_TILE), F32),
                        pltpu.VMEM((2, 1, ROW_TILE), F32),
                        pltpu.VMEM((2, 1, ROW_TILE), F32),
                        pltpu.VMEM((2, 1, ROW_TILE), F32),
                        pltpu.VMEM((2, acc_rows, ROW_TILE), F32),
                        pltpu.VMEM((2, acc_rows, ROW_TILE), F32)],
        compiler_params=_params(2),
        name="flash_diff" if diff else "flash_pair",
    )(lam, gsub, qt, k, vt)


NA_Q_ROWS = ROW_TILE // GRID_W
NA_BAND_TILES = 3
NA_LAST_BAND = SEQ // ROW_TILE - NA_BAND_TILES


def _na_band_start(i):
    return jnp.clip(i - 1, 0, NA_LAST_BAND)


def _na_kernel(qt_ref, k0_ref, k1_ref, k2_ref, kc_ref, v0_ref, v1_ref, v2_ref, vc_ref,
               bias_ref, o_ref, qs_ref):
    k_refs = (k0_ref, k1_ref, k2_ref)
    v_refs = (v0_ref, v1_ref, v2_ref, vc_ref)
    v_rows, n_val = _value_rows(False)
    chains = [(b, slot) for b in range(BATCH) for slot in range(2)]
    for b in range(BATCH):
        _split_queries(qt_ref[b, 0], qs_ref.at[b])
    scores = {}
    for b, slot in chains:
        qs = qs_ref[b, slot]
        s = [jnp.dot(k_refs[j][b], qs, preferred_element_type=F32)
             + bias_ref[0, slot, j * ROW_TILE:(j + 1) * ROW_TILE, :]
             for j in range(NA_BAND_TILES)]
        s.append(jnp.dot(kc_ref[b], qs, preferred_element_type=F32))
        scores[b, slot] = s
    outs = {}
    for b, slot in chains:
        lo, hi = v_rows[slot]
        s = scores[b, slot]
        m = jnp.max(functools.reduce(jnp.maximum, s), axis=0, keepdims=True)
        acc = functools.reduce(jnp.add, [
            jnp.dot(r[b, 0, lo:hi, :], _probabilities(sj, m), preferred_element_type=F32)
            for r, sj in zip(v_refs, s)])
        outs[b, slot] = acc[:n_val] * (1.0 / acc[n_val:n_val + 1])
    for b in range(BATCH):
        o_ref[b] = jnp.concatenate([outs[b, 0], outs[b, 1]], axis=0).T.astype(BF16)


def _na_variant(i):
    return jnp.where(i == 0, 0, jnp.where(i < NA_LAST_BAND + NA_BAND_TILES - 1, 1,
                                          jnp.where(i < CTX_TILE, 2, 3)))


def _na_attention(qt, k, vt, bias):
    vt_rows = vt.shape[2] // N_PAIRS

    def kspec(j):
        return pl.BlockSpec((BATCH, ROW_TILE, LANES), lambda p, i: (0, _na_band_start(i) + j, p))

    def vspec(j):
        return pl.BlockSpec((BATCH, 1, vt_rows, ROW_TILE),
                            lambda p, i: (0, _na_band_start(i) + j, p, 0))

    return pl.pallas_call(
        _na_kernel,
        out_shape=jax.ShapeDtypeStruct((BATCH, T_ALL, D_MODEL), BF16),
        grid=(N_PAIRS, N_TILES),
        in_specs=[pl.BlockSpec((BATCH, 1, LANES, ROW_TILE), lambda p, i: (0, i, p, 0)),
                  kspec(0), kspec(1), kspec(2),
                  pl.BlockSpec((BATCH, ROW_TILE, LANES), lambda p, i: (0, CTX_TILE, p)),
                  vspec(0), vspec(1), vspec(2),
                  pl.BlockSpec((BATCH, 1, vt_rows, ROW_TILE), lambda p, i: (0, CTX_TILE, p, 0)),
                  pl.BlockSpec((1, 2, NA_BAND_TILES * ROW_TILE, ROW_TILE),
                               lambda p, i: (_na_variant(i), p, 0, 0))],
        out_specs=pl.BlockSpec((BATCH, ROW_TILE, LANES), lambda p, i: (0, i, p)),
        scratch_shapes=[pltpu.VMEM((BATCH, 2, LANES, ROW_TILE), BF16)],
        compiler_params=_params(2),
        name="na_attention",
    )(qt, k, k, k, k, vt, vt, vt, vt, bias)


def _na_bias_table(rpb):
    rows = SEQ // GRID_W
    band_rows = NA_BAND_TILES * NA_Q_ROWS
    col = np.arange(GRID_W)
    cstart = np.clip(col - NA_COLS // 2, 0, GRID_W - NA_COLS)
    ok_c = (col[None, :] >= cstart[:, None]) & (col[None, :] < cstart[:, None] + NA_COLS)
    dc = col[None, :] - col[:, None] + NA_COLS - 1
    onehot_c = ((dc[:, :, None] == np.arange(2 * NA_COLS - 1)) & ok_c[:, :, None]).astype(np.float32)
    dr = np.zeros((4, NA_Q_ROWS, band_rows), np.int32)
    ok_r = np.zeros((4, NA_Q_ROWS, band_rows), bool)
    for variant, i in enumerate((0, 1, SEQ // ROW_TILE - 1)):
        j = min(max(i - 1, 0), NA_LAST_BAND)
        r = NA_Q_ROWS * i + np.arange(NA_Q_ROWS)[:, None]
        k_row = NA_Q_ROWS * j + np.arange(band_rows)[None, :]
        rstart = np.clip(r - NA_ROWS // 2, 0, rows - NA_ROWS)
        ok_r[variant] = (k_row >= rstart) & (k_row < rstart + NA_ROWS)
        dr[variant] = np.where(ok_r[variant], k_row - r + NA_ROWS - 1, 0)
    by_col = jnp.einsum("hab,ckb->hack", rpb.astype(F32) * LOG2E, jnp.asarray(onehot_c),
                        precision=lax.Precision.HIGHEST)
    by_col = jnp.where(ok_c[None, None], by_col, MASK_VALUE)
    full = jnp.take(by_col, jnp.asarray(dr.reshape(-1)), axis=1)
    full = full.reshape(C_HEADS, 4, NA_Q_ROWS, band_rows, GRID_W, GRID_W)
    full = jnp.where(ok_r[None, :, :, :, None, None], full, MASK_VALUE)
    full = full.transpose(1, 0, 3, 5, 2, 4)
    return full.reshape(4, C_HEADS, NA_BAND_TILES * ROW_TILE, ROW_TILE)


def _post_kernel(o_ref, x_ref, mod_ref, g_ref, wo_ref, wgu_ref, wd_ref, xo_ref):
    m = mod_ref[0]
    y = jnp.dot(o_ref[0], wo_ref[...], preferred_element_type=F32)
    x1 = x_ref[0] + m[2:3] * _rms(y, g_ref[1:2])
    h2 = (_rms(x1, g_ref[2:3]) * (1.0 + m[4:5]) + m[3:4]).astype(BF16)
    gu = jnp.dot(h2, wgu_ref[...], preferred_element_type=F32)
    gate, up = gu[:, :FFN_HIDDEN], gu[:, FFN_HIDDEN:]
    a = (gate * _sigmoid(gate) * up).astype(BF16)
    f = jnp.dot(a, wd_ref[...], preferred_element_type=F32)
    xo_ref[0] = x1 + m[5:6] * _rms(f, g_ref[3:4])


def _post(o, x, mod, g_norm, wo, wgu, wd, *, n_tiles):
    return pl.pallas_call(
        _post_kernel,
        out_shape=jax.ShapeDtypeStruct((BATCH, n_tiles * ROW_TILE, D_MODEL), F32),
        grid=(BATCH, n_tiles),
        in_specs=[pl.BlockSpec((1, ROW_TILE, D_MODEL), lambda b, t: (b, t, 0))] + _row_specs() + [
            _resident((4, D_MODEL)), _resident(wo.shape), _resident(wgu.shape),
            _resident(wd.shape)],
        out_specs=pl.BlockSpec((1, ROW_TILE, D_MODEL), lambda b, t: (b, t, 0)),
        compiler_params=_params(2),
        name="post_attention",
    )(o, x, mod, g_norm, wo, wgu, wd)


def _rope_angles(rot_dim):
    n_freq = rot_dim // 4
    inv = ROPE_THETA ** (-jnp.arange(n_freq, dtype=F32) / n_freq)
    t = jnp.arange(SEQ, dtype=jnp.int32)
    row = (t // GRID_W).astype(F32)
    col = (t % GRID_W).astype(F32)
    ang = jnp.concatenate([row[:, None] * inv, col[:, None] * inv], axis=-1)
    cos = jnp.concatenate([jnp.cos(ang), jnp.ones((CTX_LEN, rot_dim // 2), F32)], axis=0)
    sin = jnp.concatenate([jnp.sin(ang), jnp.zeros((CTX_LEN, rot_dim // 2), F32)], axis=0)
    return cos, sin


def _rope_tables_diff():
    cos, sin = _rope_angles(A_HEAD_DIM)
    c_lane = jnp.tile(cos, (1, 4))
    s_lane = jnp.tile(jnp.concatenate([-sin, sin], axis=1), (1, 2))
    return c_lane, s_lane, cos.T, sin.T


def _rope_tables_mla():
    cos, sin = _rope_angles(B_ROPE)
    ones = jnp.ones((T_ALL, B_NOPE), F32)
    zeros = jnp.zeros((T_ALL, B_NOPE), F32)
    tail = LANES - B_NOPE - B_ROPE
    c_lane = jnp.concatenate([ones, cos, cos, ones[:, :tail]], axis=1)
    s_lane = jnp.concatenate([zeros, -sin, sin, zeros[:, :tail]], axis=1)
    return c_lane, s_lane, cos.T, sin.T


def kernel(x, c, ctx, c_ctx, l0_w_mod, l0_b_mod, l0_g_norm, l0_w_gu, l0_w_down, l0_a_w_qkv, l0_a_w_o, l0_a_lam, l0_a_g_sub, l1_w_mod, l1_b_mod, l1_g_norm, l1_w_gu, l1_w_down, l1_b_w_in, l1_b_g_q, l1_b_g_kv, l1_b_w_uq, l1_b_w_ukv, l1_b_w_o, l2_w_mod, l2_b_mod, l2_g_norm, l2_w_gu, l2_w_down, l2_c_w_qkv, l2_c_rpb, l2_c_w_o, l3_w_mod, l3_b_mod, l3_g_norm, l3_w_gu, l3_w_down, l3_a_w_qkv, l3_a_w_o, l3_a_lam, l3_a_g_sub):
    common = [
        (l0_w_mod, l0_b_mod, l0_g_norm, l0_w_gu, l0_w_down),
        (l1_w_mod, l1_b_mod, l1_g_norm, l1_w_gu, l1_w_down),
        (l2_w_mod, l2_b_mod, l2_g_norm, l2_w_gu, l2_w_down),
        (l3_w_mod, l3_b_mod, l3_g_norm, l3_w_gu, l3_w_down),
    ]
    diff_params = {0: (l0_a_w_qkv, l0_a_w_o, l0_a_lam, l0_a_g_sub),
                   3: (l3_a_w_qkv, l3_a_w_o, l3_a_lam, l3_a_g_sub)}

    xs = jnp.concatenate([x, ctx], axis=1)
    cc = jnp.concatenate([c, c_ctx[None, :], jnp.zeros((8 - BATCH - 1, D_MODEL), F32)], axis=0)
    tabs_a = _rope_tables_diff()
    tabs_b = _rope_tables_mla()
    dummy_lam = jnp.zeros((4, A_HEAD_DIM), F32)
    dummy_gsub = jnp.ones((1, LANES), F32)

    for i in range(DEPTH):
        w_mod, b_mod, g_norm, w_gu, w_down = common[i]
        last = i == DEPTH - 1
        mod = _modulation(cc, w_mod, b_mod).reshape(8, 6, D_MODEL)
        kind = i % 3
        if kind == 0:
            w_qkv, w_o, lam, g_sub = diff_params[i]
            lam_init = 0.8 - 0.6 * math.exp(-0.3 * i)
            wqt = w_qkv[:, :D_MODEL].T.astype(BF16)
            wk = w_qkv[:, D_MODEL:2 * D_MODEL].astype(BF16)
            wvt = w_qkv[:, 2 * D_MODEL:].T.astype(BF16)
            qt, k, vt = _proj_qkv(xs, mod, g_norm, wqt, wk, wvt, tabs_a, rope=True,
                                  qscale=A_HEAD_DIM ** -0.5 * LOG2E, v_rows=2 * A_HEAD_DIM)
            o = _flash(qt, k, vt, lam, g_sub.reshape(1, LANES), diff=True, lam_init=lam_init)
        elif kind == 1:
            w_in, g_q, g_kv, w_uq, w_ukv, w_o = (l1_b_w_in, l1_b_g_q, l1_b_g_kv, l1_b_w_uq,
                                                 l1_b_w_ukv, l1_b_w_o)
            n_lora = B_Q_LORA + B_KV_LORA
            pad = LANES - B_NOPE - B_ROPE
            win = w_in[:, :n_lora].astype(BF16)
            wkr = jnp.pad(w_in[:, n_lora:], ((0, 0), (B_NOPE, pad))).astype(BF16)
            wuq = w_uq.reshape(B_Q_LORA, B_HEADS, B_NOPE + B_ROPE)
            wuq = jnp.pad(wuq, ((0, 0), (0, 0), (0, pad)))
            wuqt = wuq.reshape(B_Q_LORA, B_HEADS * LANES).T.astype(BF16)
            wukv = w_ukv.reshape(B_KV_LORA, B_HEADS, B_NOPE + B_VDIM)
            wukn = jnp.pad(wukv[:, :, :B_NOPE], ((0, 0), (0, 0), (0, LANES - B_NOPE)))
            wukn = wukn.reshape(B_KV_LORA, B_HEADS * LANES).astype(BF16)
            wuvt = wukv[:, :, B_NOPE:].reshape(B_KV_LORA, B_HEADS * B_VDIM).T.astype(BF16)
            qt, k, vt = _proj_mla(xs, mod, g_norm, win, wkr, g_q.reshape(1, -1),
                                  g_kv.reshape(1, -1), wuqt, wukn, wuvt, tabs_b,
                                  qscale=(B_NOPE + B_ROPE) ** -0.5 * LOG2E)
            o = _flash(qt, k, vt, dummy_lam, dummy_gsub, diff=False, lam_init=0.0)
        else:
            w_qkv, rpb, w_o = l2_c_w_qkv, l2_c_rpb, l2_c_w_o
            wqt = w_qkv[:, :D_MODEL].T.astype(BF16)
            wk = w_qkv[:, D_MODEL:2 * D_MODEL].astype(BF16)
            wvt = w_qkv[:, 2 * D_MODEL:].T.astype(BF16)
            qt, k, vt = _proj_qkv(xs, mod, g_norm, wqt, wk, wvt, tabs_a, rope=False,
                                  qscale=C_HEAD_DIM ** -0.5 * LOG2E, v_rows=C_HEAD_DIM)
            o = _na_attention(qt, k, vt, _na_bias_table(rpb))
        xs = _post(o, xs, mod, g_norm, w_o.astype(BF16), w_gu.astype(BF16),
                   w_down.astype(BF16), n_tiles=N_TILES - 1 if last else N_TILES)
    return xs
```

```python
import functools
import math

import numpy as np
import jax
import jax.numpy as jnp
from jax import lax
from jax.experimental import pallas as pl
from jax.experimental.pallas import tpu as pltpu

D_MODEL = 1024
BATCH = 4
SEQ = 8192
DEPTH = 4
GRID_W = 64
CTX_LEN = 256
ROPE_THETA = 10000.0
NORM_EPS = 1e-6

A_HEAD_DIM = 64
A_HEADS = 8
B_HEADS = 16
B_NOPE = 64
B_ROPE = 32
B_VDIM = 64
B_Q_LORA = 256
B_KV_LORA = 256
C_HEAD_DIM = 64
C_HEADS = 16
NA_ROWS = 8
NA_COLS = 16
FFN_HIDDEN = 2816

LANES = 128
STRIP = 64
ROW_TILE = 256
T_ALL = SEQ + CTX_LEN
N_TILES = T_ALL // ROW_TILE
CTX_TILE = N_TILES - 1
N_PAIRS = 8
KEY_GROUPS = tuple((3 * g, 3) for g in range(9)) + tuple((27 + 2 * g, 2) for g in range(3))
SCORE_TILES = 2
PROB_LEAD = 1
SCORE_LEAD = 2
V_ONES = 16
LOG2E = math.log2(math.e)
MASK_VALUE = -1e30
VMEM_LIMIT = 60 * 1024 * 1024

F32 = jnp.float32
BF16 = jnp.bfloat16
NT_DIMS = (((1,), (1,)), ((), ()))


def _rms(x, g):
    return x * lax.rsqrt(jnp.mean(x * x, axis=-1, keepdims=True) + NORM_EPS) * g


def _sigmoid(x):
    return 1.0 / (1.0 + jnp.exp(-x))


def _params(n_axes):
    return pltpu.CompilerParams(
        dimension_semantics=("arbitrary",) * n_axes, vmem_limit_bytes=VMEM_LIMIT)


def _resident(shape):
    zeros = (0,) * len(shape)
    return pl.BlockSpec(shape, lambda *_: zeros, pipeline_mode=pl.Buffered(1))


def _mod_row(b, t):
    return jnp.where(t == CTX_TILE, BATCH, b)


def _mod_kernel(c_ref, w_ref, b_ref, o_ref):
    c = c_ref[...]
    a = (c * _sigmoid(c)).astype(BF16)
    o_ref[...] = jnp.dot(a, w_ref[...].astype(BF16), preferred_element_type=F32) + b_ref[...]


def _modulation(cc, w_mod, b_mod):
    n = w_mod.shape[1]
    blk = D_MODEL
    return pl.pallas_call(
        _mod_kernel,
        out_shape=jax.ShapeDtypeStruct((8, n), F32),
        grid=(n // blk,),
        in_specs=[pl.BlockSpec((8, D_MODEL), lambda j: (0, 0)),
                  pl.BlockSpec((D_MODEL, blk), lambda j: (0, j)),
                  pl.BlockSpec((1, blk), lambda j: (0, j))],
        out_specs=pl.BlockSpec((8, blk), lambda j: (0, j)),
        compiler_params=_params(1),
        name="modulation",
    )(cc, w_mod, b_mod.reshape(1, n))


def _rope_lanes(k, c, s, first, half):
    rot = jnp.where(first, pltpu.roll(k, LANES - half, 1), pltpu.roll(k, half, 1))
    return k * c + rot * s


def _store_values(vt_ref, vt, rows):
    ones = jnp.ones((V_ONES, ROW_TILE), BF16)
    for n in range(vt.shape[0] // rows):
        r0 = n * (rows + V_ONES)
        vt_ref[0, 0, r0:r0 + rows, :] = vt[n * rows:(n + 1) * rows].astype(BF16)
        vt_ref[0, 0, r0 + rows:r0 + rows + V_ONES, :] = ones


def _proj_qkv_kernel(x_ref, mod_ref, g_ref, wqt_ref, wk_ref, wvt_ref,
                     ck_ref, sk_ref, cqt_ref, sqt_ref,
                     qt_ref, k_ref, vt_ref, *, rope, qscale, v_rows):
    m = mod_ref[0]
    h = (_rms(x_ref[0], g_ref[0:1]) * (1.0 + m[1:2]) + m[0:1]).astype(BF16)
    qt = lax.dot_general(wqt_ref[...], h, NT_DIMS, preferred_element_type=F32) * qscale
    k = jnp.dot(h, wk_ref[...], preferred_element_type=F32)
    vt = lax.dot_general(wvt_ref[...], h, NT_DIMS, preferred_element_type=F32)
    _store_values(vt_ref, vt, v_rows)
    if not rope:
        qt_ref[0, 0] = qt.astype(BF16)
        k_ref[0] = k.astype(BF16)
        return
    half = A_HEAD_DIM // 2
    c, s = ck_ref[...], sk_ref[...]
    lane = lax.broadcasted_iota(jnp.int32, (ROW_TILE, LANES), 1)
    first = (lane % A_HEAD_DIM) < half
    for p in range(D_MODEL // LANES):
        sl = slice(p * LANES, (p + 1) * LANES)
        k_ref[0, :, sl] = _rope_lanes(k[:, sl], c, s, first, half).astype(BF16)
    ct, st = cqt_ref[...], sqt_ref[...]
    for g in range(D_MODEL // A_HEAD_DIM):
        r0 = g * A_HEAD_DIM
        t1, t2 = qt[r0:r0 + half], qt[r0 + half:r0 + 2 * half]
        qt_ref[0, 0, r0:r0 + half, :] = (t1 * ct - t2 * st).astype(BF16)
        qt_ref[0, 0, r0 + half:r0 + 2 * half, :] = (t1 * st + t2 * ct).astype(BF16)


def _row_specs():
    return [pl.BlockSpec((1, ROW_TILE, D_MODEL), lambda b, t: (b, t, 0)),
            pl.BlockSpec((1, 6, D_MODEL), lambda b, t: (_mod_row(b, t), 0, 0))]


def _rope_specs(half):
    return [pl.BlockSpec((ROW_TILE, LANES), lambda b, t: (t, 0)),
            pl.BlockSpec((ROW_TILE, LANES), lambda b, t: (t, 0)),
            pl.BlockSpec((half, ROW_TILE), lambda b, t: (0, t)),
            pl.BlockSpec((half, ROW_TILE), lambda b, t: (0, t))]


def _qkv_out(width, v_rows):
    vt_rows = D_MODEL // v_rows * (v_rows + V_ONES)
    shapes = (jax.ShapeDtypeStruct((BATCH, N_TILES, width, ROW_TILE), BF16),
              jax.ShapeDtypeStruct((BATCH, T_ALL, width), BF16),
              jax.ShapeDtypeStruct((BATCH, N_TILES, vt_rows, ROW_TILE), BF16))
    specs = (pl.BlockSpec((1, 1, width, ROW_TILE), lambda b, t: (b, t, 0, 0)),
             pl.BlockSpec((1, ROW_TILE, width), lambda b, t: (b, t, 0)),
             pl.BlockSpec((1, 1, vt_rows, ROW_TILE), lambda b, t: (b, t, 0, 0)))
    return shapes, specs


def _proj_qkv(x, mod, g_norm, wqt, wk, wvt, tabs, *, rope, qscale, v_rows):
    shapes, specs = _qkv_out(D_MODEL, v_rows)
    return pl.pallas_call(
        functools.partial(_proj_qkv_kernel, rope=rope, qscale=qscale, v_rows=v_rows),
        out_shape=shapes,
        grid=(BATCH, N_TILES),
        in_specs=_row_specs() + [
            _resident((4, D_MODEL)),
            _resident((D_MODEL, D_MODEL)), _resident((D_MODEL, D_MODEL)),
            _resident((D_MODEL, D_MODEL))] + _rope_specs(tabs[2].shape[0]),
        out_specs=specs,
        compiler_params=_params(2),
        name="proj_qkv",
    )(x, mod, g_norm, wqt, wk, wvt, *tabs)


def _proj_mla_kernel(x_ref, mod_ref, g_ref, win_ref, wkr_ref, gq_ref, gkv_ref,
                     wuqt_ref, wukn_ref, wuvt_ref, ck_ref, sk_ref, cqt_ref, sqt_ref,
                     qt_ref, k_ref, vt_ref, *, qscale):
    m = mod_ref[0]
    h = (_rms(x_ref[0], g_ref[0:1]) * (1.0 + m[1:2]) + m[0:1]).astype(BF16)
    z = jnp.dot(h, win_ref[...], preferred_element_type=F32)
    cq = _rms(z[:, :B_Q_LORA], gq_ref[...]).astype(BF16)
    ckv = _rms(z[:, B_Q_LORA:], gkv_ref[...]).astype(BF16)
    half = B_ROPE // 2

    qt = lax.dot_general(wuqt_ref[...], cq, NT_DIMS, preferred_element_type=F32) * qscale
    ct, st = cqt_ref[...], sqt_ref[...]
    for hh in range(B_HEADS):
        r0, r1 = hh * LANES, hh * LANES + B_NOPE
        qt_ref[0, 0, r0:r1, :] = qt[r0:r1].astype(BF16)
        t1, t2 = qt[r1:r1 + half], qt[r1 + half:r1 + 2 * half]
        qt_ref[0, 0, r1:r1 + half, :] = (t1 * ct - t2 * st).astype(BF16)
        qt_ref[0, 0, r1 + half:r1 + 2 * half, :] = (t1 * st + t2 * ct).astype(BF16)
        qt_ref[0, 0, r1 + B_ROPE:r0 + LANES, :] = qt[r1 + B_ROPE:r0 + LANES].astype(BF16)

    kr = jnp.dot(h, wkr_ref[...], preferred_element_type=F32)
    lane = lax.broadcasted_iota(jnp.int32, (ROW_TILE, LANES), 1)
    first = (lane >= B_NOPE) & (lane < B_NOPE + half)
    kr = _rope_lanes(kr, ck_ref[...], sk_ref[...], first, half)
    kn = jnp.dot(ckv, wukn_ref[...], preferred_element_type=F32)
    for hh in range(B_HEADS):
        sl = slice(hh * LANES, (hh + 1) * LANES)
        k_ref[0, :, sl] = (kn[:, sl] + kr).astype(BF16)

    vt = lax.dot_general(wuvt_ref[...], ckv, NT_DIMS, preferred_element_type=F32)
    _store_values(vt_ref, vt, B_VDIM)


def _proj_mla(x, mod, g_norm, win, wkr, gq, gkv, wuqt, wukn, wuvt, tabs, *, qscale):
    shapes, specs = _qkv_out(B_HEADS * LANES, B_VDIM)
    return pl.pallas_call(
        functools.partial(_proj_mla_kernel, qscale=qscale),
        out_shape=shapes,
        grid=(BATCH, N_TILES),
        in_specs=_row_specs() + [
            _resident((4, D_MODEL)),
            _resident(win.shape), _resident(wkr.shape),
            _resident((1, B_Q_LORA)), _resident((1, B_KV_LORA)),
            _resident(wuqt.shape), _resident(wukn.shape), _resident(wuvt.shape)]
            + _rope_specs(tabs[2].shape[0]),
        out_specs=specs,
        compiler_params=_params(2),
        name="proj_mla",
    )(x, mod, g_norm, win, wkr, gq, gkv, wuqt, wukn, wuvt, *tabs)


def _value_rows(diff):
    if diff:
        rows = 2 * A_HEAD_DIM
        return ((0, rows + V_ONES), (0, rows + V_ONES)), rows
    rows = LANES // 2
    return ((0, rows + V_ONES), (rows + V_ONES, 2 * (rows + V_ONES))), rows


def _split_queries(qt, qs_ref):
    dq = qt.shape[0]
    if dq == 2 * LANES:
        qs_ref[0] = qt[:LANES]
        qs_ref[1] = qt[LANES:]
        return
    row = lax.broadcasted_iota(jnp.int32, qt.shape, 0)
    zero = jnp.zeros_like(qt)
    qs_ref[0] = jnp.where(row < dq // 2, qt, zero)
    qs_ref[1] = jnp.where(row >= dq // 2, qt, zero)


def _slot_keys(k_ref, rows, slot):
    if k_ref.shape[-1] == 2 * LANES:
        return k_ref[0, rows, slot * LANES:(slot + 1) * LANES]
    return k_ref[0, rows, :]


def _probabilities(s, m):
    return jnp.exp2(s - m).astype(BF16)


def _flash_kernel(lam_ref, gsub_ref, qt_ref, k_ref, vt_ref, o_ref,
                  qs_ref, s0_ref, s1_ref, p0_ref, p1_ref, p2_ref, a0_ref, a1_ref, a2_ref,
                  m_ref, acc_ref, fin_ref, *, diff, lam_init):
    v_rows, n_val = _value_rows(diff)
    s_bufs, p_bufs, a_bufs = (s0_ref, s1_ref), (p0_ref, p1_ref, p2_ref), (a0_ref, a1_ref, a2_ref)
    n_groups = len(KEY_GROUPS)

    def tile_rows(j):
        return slice(j * ROW_TILE, (j + 1) * ROW_TILE)

    def scores(t0, n, slot, s_ref):
        for j in range(0, n, SCORE_TILES):
            rows = min(SCORE_TILES, n - j) * ROW_TILE
            k = _slot_keys(k_ref, slice((t0 + j) * ROW_TILE, (t0 + j) * ROW_TILE + rows), slot)
            s_ref[slot, j * ROW_TILE:j * ROW_TILE + rows, :] = jnp.dot(
                k, qs_ref[slot], preferred_element_type=F32)

    def stats(n, s_ref, a_ref, first):
        for slot in range(2):
            m_prev = jnp.full((1, ROW_TILE), MASK_VALUE, F32) if first else m_ref[slot]
            strips = [s_ref[slot, c * STRIP:(c + 1) * STRIP, :] for c in range(n * ROW_TILE // STRIP)]
            m_next = jnp.maximum(
                m_prev, jnp.max(functools.reduce(jnp.maximum, strips), axis=0, keepdims=True))
            a_ref[slot] = jnp.exp2(m_prev - m_next)
            m_ref[slot] = m_next

    def probs_tile(j, s_ref, p_ref):
        for slot in range(2):
            m = m_ref[slot]
            for c in range(ROW_TILE // STRIP):
                rows = slice(j * ROW_TILE + c * STRIP, j * ROW_TILE + (c + 1) * STRIP)
                p_ref[slot, rows, :] = _probabilities(s_ref[slot, rows, :], m)

    def values_tile(t, j, p_ref, pv):
        for slot in range(2):
            lo, hi = v_rows[slot]
            pv[slot].append(jnp.dot(vt_ref[0, t, lo:hi, :], p_ref[slot, tile_rows(j), :],
                                    preferred_element_type=F32))

    def accumulate(pv, a_ref):
        for slot in range(2):
            acc_ref[slot] = a_ref[slot] * acc_ref[slot] + functools.reduce(jnp.add, pv[slot])

    def finalize(i, src_ref):
        acc_a, acc_b = src_ref[0], src_ref[1]
        ot_a = acc_a[:n_val] * (1.0 / acc_a[n_val:n_val + 1])
        ot_b = acc_b[:n_val] * (1.0 / acc_b[n_val:n_val + 1])
        if diff:
            lv = lam_ref[...]
            lam = (jnp.exp(jnp.sum(lv[0:1] * lv[1:2], axis=1, keepdims=True))
                   - jnp.exp(jnp.sum(lv[2:3] * lv[3:4], axis=1, keepdims=True)) + lam_init)
            o = (ot_a - lam * ot_b).T
            o = _rms(o, gsub_ref[...]) * (1.0 - lam_init)
        else:
            o = jnp.concatenate([ot_a, ot_b], axis=0).T
        o_ref[0, pl.ds(pl.multiple_of(i * ROW_TILE, ROW_TILE), ROW_TILE), :] = o.astype(BF16)

    acc_ref[...] = jnp.zeros(acc_ref.shape, F32)
    fin_ref[...] = jnp.ones(fin_ref.shape, F32)

    def scores_group(g, s_ref):
        for slot in range(2):
            scores(KEY_GROUPS[g][0], KEY_GROUPS[g][1], slot, s_ref)

    def probs_group(g, s_ref, p_ref, a_ref, first):
        stats(KEY_GROUPS[g][1], s_ref, a_ref, first)
        for j in range(KEY_GROUPS[g][1]):
            probs_tile(j, s_ref, p_ref)

    _split_queries(qt_ref[0, 0], qs_ref)
    for fill in range(-SCORE_LEAD, 0):
        g_prob, g_sco = fill + PROB_LEAD, fill + SCORE_LEAD
        if g_prob >= 0:
            probs_group(g_prob, s_bufs[g_prob % 2], p_bufs[g_prob % 3], a_bufs[g_prob % 3],
                        g_prob == 0)
        scores_group(g_sco, s_bufs[g_sco % 2])

    def body(i, carry):
        for g in range(n_groups):
            g_prob, g_sco = (g + PROB_LEAD) % n_groups, (g + SCORE_LEAD) % n_groups
            t_val, n_val_tiles = KEY_GROUPS[g]
            t_sco, n_sco = KEY_GROUPS[g_sco]
            n_prob = KEY_GROUPS[g_prob][1]
            s_prob, p_prob, a_prob = s_bufs[g_prob % 2], p_bufs[g_prob % 3], a_bufs[g_prob % 3]
            stats(n_prob, s_prob, a_prob, g_prob == 0)
            if g + SCORE_LEAD == n_groups:
                _split_queries(qt_ref[0, i + 1], qs_ref)
            pv = ([], [])
            for j in range(max(n_val_tiles, n_prob)):
                if j < n_prob:
                    probs_tile(j, s_prob, p_prob)
                if j < n_val_tiles:
                    values_tile(t_val + j, j, p_bufs[g % 3], pv)
                if j < 2:
                    scores(t_sco, n_sco, j, s_bufs[g_sco % 2])
            accumulate(pv, a_bufs[g % 3])
            if g == 0:
                finalize(jnp.maximum(i - 1, 0), fin_ref)
        fin_ref[...] = acc_ref[...]
        return carry
    lax.fori_loop(0, CTX_TILE, body, 0)
    finalize(CTX_TILE - 1, fin_ref)

    for slot in range(2):
        scores(CTX_TILE, 1, slot, s0_ref)
    stats(1, s0_ref, a0_ref, True)
    probs_tile(0, s0_ref, p0_ref)
    pv = ([], [])
    values_tile(CTX_TILE, 0, p0_ref, pv)
    accumulate(pv, a0_ref)
    finalize(CTX_TILE, acc_ref)


def _flash(qt, k, vt, lam, gsub, *, diff, lam_init):
    dq = k.shape[-1] // N_PAIRS
    vt_rows = vt.shape[2] // N_PAIRS
    acc_rows = _value_rows(diff)[0][0][1]
    group_rows = max(n for _, n in KEY_GROUPS) * ROW_TILE
    return pl.pallas_call(
        functools.partial(_flash_kernel, diff=diff, lam_init=lam_init),
        out_shape=jax.ShapeDtypeStruct((BATCH, T_ALL, D_MODEL), BF16),
        grid=(BATCH, N_PAIRS),
        in_specs=[pl.BlockSpec(lam.shape, lambda b, p: (0, 0)),
                  pl.BlockSpec(gsub.shape, lambda b, p: (0, 0)),
                  pl.BlockSpec((1, N_TILES, dq, ROW_TILE), lambda b, p: (b, 0, p, 0)),
                  pl.BlockSpec((1, T_ALL, dq), lambda b, p: (b, 0, p)),
                  pl.BlockSpec((1, N_TILES, vt_rows, ROW_TILE), lambda b, p: (b, 0, p, 0))],
        out_specs=pl.BlockSpec((1, T_ALL, LANES), lambda b, p: (b, 0, p)),
        scratch_shapes=[pltpu.VMEM((2, LANES, ROW_TILE), BF16),
                        pltpu.VMEM((2, group_rows, ROW_TILE), F32),
                        pltpu.VMEM((2, group_rows, ROW_TILE), F32),
                        pltpu.VMEM((2, group_rows, ROW_TILE), BF16),
                        pltpu.VMEM((2, group_rows, ROW_TILE), BF16),
                        pltpu.VMEM((2, group_rows, ROW_TILE), BF16),
                        pltpu.VMEM((2, 1, ROW_TILE), F32),
                        pltpu.VMEM((2, 1, ROW_TILE), F32),
                        pltpu.VMEM((2, 1, ROW_TILE), F32),
                        pltpu.VMEM((2, 1, ROW_TILE), F32),
                        pltpu.VMEM((2, acc_rows, ROW_TILE), F32),
                        pltpu.VMEM((2, acc_rows, ROW_TILE), F32)],
        compiler_params=_params(2),
        name="flash_diff" if diff else "flash_pair",
    )(lam, gsub, qt, k, vt)


NA_Q_ROWS = ROW_TILE // GRID_W
NA_BAND_TILES = 3
NA_LAST_BAND = SEQ // ROW_TILE - NA_BAND_TILES


def _na_band_start(i):
    return jnp.clip(i - 1, 0, NA_LAST_BAND)


def _na_kernel(qt_ref, k0_ref, k1_ref, k2_ref, kc_ref, v0_ref, v1_ref, v2_ref, vc_ref,
               bias_ref, o_ref, qs_ref):
    k_refs = (k0_ref, k1_ref, k2_ref)
    v_refs = (v0_ref, v1_ref, v2_ref, vc_ref)
    v_rows, n_val = _value_rows(False)
    chains = [(b, slot) for b in range(BATCH) for slot in range(2)]
    for b in range(BATCH):
        _split_queries(qt_ref[b, 0], qs_ref.at[b])
    scores = {}
    for b, slot in chains:
        qs = qs_ref[b, slot]
        s = [jnp.dot(k_refs[j][b], qs, preferred_element_type=F32)
             + bias_ref[0, slot, j * ROW_TILE:(j + 1) * ROW_TILE, :]
             for j in range(NA_BAND_TILES)]
        s.append(jnp.dot(kc_ref[b], qs, preferred_element_type=F32))
        scores[b, slot] = s
    outs = {}
    for b, slot in chains:
        lo, hi = v_rows[slot]
        s = scores[b, slot]
        m = jnp.max(functools.reduce(jnp.maximum, s), axis=0, keepdims=True)
        acc = functools.reduce(jnp.add, [
            jnp.dot(r[b, 0, lo:hi, :], _probabilities(sj, m), preferred_element_type=F32)
            for r, sj in zip(v_refs, s)])
        outs[b, slot] = acc[:n_val] * (1.0 / acc[n_val:n_val + 1])
    for b in range(BATCH):
        o_ref[b] = jnp.concatenate([outs[b, 0], outs[b, 1]], axis=0).T.astype(BF16)


def _na_variant(i):
    return jnp.where(i == 0, 0, jnp.where(i < NA_LAST_BAND + NA_BAND_TILES - 1, 1,
                                          jnp.where(i < CTX_TILE, 2, 3)))


def _na_attention(qt, k, vt, bias):
    vt_rows = vt.shape[2] // N_PAIRS

    def kspec(j):
        return pl.BlockSpec((BATCH, ROW_TILE, LANES), lambda p, i: (0, _na_band_start(i) + j, p))

    def vspec(j):
        return pl.BlockSpec((BATCH, 1, vt_rows, ROW_TILE),
                            lambda p, i: (0, _na_band_start(i) + j, p, 0))

    return pl.pallas_call(
        _na_kernel,
        out_shape=jax.ShapeDtypeStruct((BATCH, T_ALL, D_MODEL), BF16),
        grid=(N_PAIRS, N_TILES),
        in_specs=[pl.BlockSpec((BATCH, 1, LANES, ROW_TILE), lambda p, i: (0, i, p, 0)),
                  kspec(0), kspec(1), kspec(2),
                  pl.BlockSpec((BATCH, ROW_TILE, LANES), lambda p, i: (0, CTX_TILE, p)),
                  vspec(0), vspec(1), vspec(2),
                  pl.BlockSpec((BATCH, 1, vt_rows, ROW_TILE), lambda p, i: (0, CTX_TILE, p, 0)),
                  pl.BlockSpec((1, 2, NA_BAND_TILES * ROW_TILE, ROW_TILE),
                               lambda p, i: (_na_variant(i), p, 0, 0))],
        out_specs=pl.BlockSpec((BATCH, ROW_TILE, LANES), lambda p, i: (0, i, p)),
        scratch_shapes=[pltpu.VMEM((BATCH, 2, LANES, ROW_TILE), BF16)],
        compiler_params=_params(2),
        name="na_attention",
    )(qt, k, k, k, k, vt, vt, vt, vt, bias)


def _na_bias_table(rpb):
    rows = SEQ // GRID_W
    band_rows = NA_BAND_TILES * NA_Q_ROWS
    col = np.arange(GRID_W)
    cstart = np.clip(col - NA_COLS // 2, 0, GRID_W - NA_COLS)
    ok_c = (col[None, :] >= cstart[:, None]) & (col[None, :] < cstart[:, None] + NA_COLS)
    dc = col[None, :] - col[:, None] + NA_COLS - 1
    onehot_c = ((dc[:, :, None] == np.arange(2 * NA_COLS - 1)) & ok_c[:, :, None]).astype(np.float32)
    dr = np.zeros((4, NA_Q_ROWS, band_rows), np.int32)
    ok_r = np.zeros((4, NA_Q_ROWS, band_rows), bool)
    for variant, i in enumerate((0, 1, SEQ // ROW_TILE - 1)):
        j = min(max(i - 1, 0), NA_LAST_BAND)
        r = NA_Q_ROWS * i + np.arange(NA_Q_ROWS)[:, None]
        k_row = NA_Q_ROWS * j + np.arange(band_rows)[None, :]
        rstart = np.clip(r - NA_ROWS // 2, 0, rows - NA_ROWS)
        ok_r[variant] = (k_row >= rstart) & (k_row < rstart + NA_ROWS)
        dr[variant] = np.where(ok_r[variant], k_row - r + NA_ROWS - 1, 0)
    by_col = jnp.einsum("hab,ckb->hack", rpb.astype(F32) * LOG2E, jnp.asarray(onehot_c),
                        precision=lax.Precision.HIGHEST)
    by_col = jnp.where(ok_c[None, None], by_col, MASK_VALUE)
    full = jnp.take(by_col, jnp.asarray(dr.reshape(-1)), axis=1)
    full = full.reshape(C_HEADS, 4, NA_Q_ROWS, band_rows, GRID_W, GRID_W)
    full = jnp.where(ok_r[None, :, :, :, None, None], full, MASK_VALUE)
    full = full.transpose(1, 0, 3, 5, 2, 4)
    return full.reshape(4, C_HEADS, NA_BAND_TILES * ROW_TILE, ROW_TILE)


def _post_kernel(o_ref, x_ref, mod_ref, g_ref, wo_ref, wgu_ref, wd_ref, xo_ref):
    m = mod_ref[0]
    y = jnp.dot(o_ref[0], wo_ref[...], preferred_element_type=F32)
    x1 = x_ref[0] + m[2:3] * _rms(y, g_ref[1:2])
    h2 = (_rms(x1, g_ref[2:3]) * (1.0 + m[4:5]) + m[3:4]).astype(BF16)
    gu = jnp.dot(h2, wgu_ref[...], preferred_element_type=F32)
    gate, up = gu[:, :FFN_HIDDEN], gu[:, FFN_HIDDEN:]
    a = (gate * _sigmoid(gate) * up).astype(BF16)
    f = jnp.dot(a, wd_ref[...], preferred_element_type=F32)
    xo_ref[0] = x1 + m[5:6] * _rms(f, g_ref[3:4])


def _post(o, x, mod, g_norm, wo, wgu, wd, *, n_tiles):
    return pl.pallas_call(
        _post_kernel,
        out_shape=jax.ShapeDtypeStruct((BATCH, n_tiles * ROW_TILE, D_MODEL), F32),
        grid=(BATCH, n_tiles),
        in_specs=[pl.BlockSpec((1, ROW_TILE, D_MODEL), lambda b, t: (b, t, 0))] + _row_specs() + [
            _resident((4, D_MODEL)), _resident(wo.shape), _resident(wgu.shape),
            _resident(wd.shape)],
        out_specs=pl.BlockSpec((1, ROW_TILE, D_MODEL), lambda b, t: (b, t, 0)),
        compiler_params=_params(2),
        name="post_attention",
    )(o, x, mod, g_norm, wo, wgu, wd)


def _rope_angles(rot_dim):
    n_freq = rot_dim // 4
    inv = ROPE_THETA ** (-jnp.arange(n_freq, dtype=F32) / n_freq)
    t = jnp.arange(SEQ, dtype=jnp.int32)
    row = (t // GRID_W).astype(F32)
    col = (t % GRID_W).astype(F32)
    ang = jnp.concatenate([row[:, None] * inv, col[:, None] * inv], axis=-1)
    cos = jnp.concatenate([jnp.cos(ang), jnp.ones((CTX_LEN, rot_dim // 2), F32)], axis=0)
    sin = jnp.concatenate([jnp.sin(ang), jnp.zeros((CTX_LEN, rot_dim // 2), F32)], axis=0)
    return cos, sin


def _rope_tables_diff():
    cos, sin = _rope_angles(A_HEAD_DIM)
    c_lane = jnp.tile(cos, (1, 4))
    s_lane = jnp.tile(jnp.concatenate([-sin, sin], axis=1), (1, 2))
    return c_lane, s_lane, cos.T, sin.T


def _rope_tables_mla():
    cos, sin = _rope_angles(B_ROPE)
    ones = jnp.ones((T_ALL, B_NOPE), F32)
    zeros = jnp.zeros((T_ALL, B_NOPE), F32)
    tail = LANES - B_NOPE - B_ROPE
    c_lane = jnp.concatenate([ones, cos, cos, ones[:, :tail]], axis=1)
    s_lane = jnp.concatenate([zeros, -sin, sin, zeros[:, :tail]], axis=1)
    return c_lane, s_lane, cos.T, sin.T


def kernel(x, c, ctx, c_ctx, l0_w_mod, l0_b_mod, l0_g_norm, l0_w_gu, l0_w_down, l0_a_w_qkv, l0_a_w_o, l0_a_lam, l0_a_g_sub, l1_w_mod, l1_b_mod, l1_g_norm, l1_w_gu, l1_w_down, l1_b_w_in, l1_b_g_q, l1_b_g_kv, l1_b_w_uq, l1_b_w_ukv, l1_b_w_o, l2_w_mod, l2_b_mod, l2_g_norm, l2_w_gu, l2_w_down, l2_c_w_qkv, l2_c_rpb, l2_c_w_o, l3_w_mod, l3_b_mod, l3_g_norm, l3_w_gu, l3_w_down, l3_a_w_qkv, l3_a_w_o, l3_a_lam, l3_a_g_sub):
    common = [
        (l0_w_mod, l0_b_mod, l0_g_norm, l0_w_gu, l0_w_down),
        (l1_w_mod, l1_b_mod, l1_g_norm, l1_w_gu, l1_w_down),
        (l2_w_mod, l2_b_mod, l2_g_norm, l2_w_gu, l2_w_down),
        (l3_w_mod, l3_b_mod, l3_g_norm, l3_w_gu, l3_w_down),
    ]
    diff_params = {0: (l0_a_w_qkv, l0_a_w_o, l0_a_lam, l0_a_g_sub),
                   3: (l3_a_w_qkv, l3_a_w_o, l3_a_lam, l3_a_g_sub)}

    xs = jnp.concatenate([x, ctx], axis=1)
    cc = jnp.concatenate([c, c_ctx[None, :], jnp.zeros((8 - BATCH - 1, D_MODEL), F32)], axis=0)
    tabs_a = _rope_tables_diff()
    tabs_b = _rope_tables_mla()
    dummy_lam = jnp.zeros((4, A_HEAD_DIM), F32)
    dummy_gsub = jnp.ones((1, LANES), F32)

    for i in range(DEPTH):
        w_mod, b_mod, g_norm, w_gu, w_down = common[i]
        last = i == DEPTH - 1
        mod = _modulation(cc, w_mod, b_mod).reshape(8, 6, D_MODEL)
        kind = i % 3
        if kind == 0:
            w_qkv, w_o, lam, g_sub = diff_params[i]
            lam_init = 0.8 - 0.6 * math.exp(-0.3 * i)
            wqt = w_qkv[:, :D_MODEL].T.astype(BF16)
            wk = w_qkv[:, D_MODEL:2 * D_MODEL].astype(BF16)
            wvt = w_qkv[:, 2 * D_MODEL:].T.astype(BF16)
            qt, k, vt = _proj_qkv(xs, mod, g_norm, wqt, wk, wvt, tabs_a, rope=True,
                                  qscale=A_HEAD_DIM ** -0.5 * LOG2E, v_rows=2 * A_HEAD_DIM)
            o = _flash(qt, k, vt, lam, g_sub.reshape(1, LANES), diff=True, lam_init=lam_init)
        elif kind == 1:
            w_in, g_q, g_kv, w_uq, w_ukv, w_o = (l1_b_w_in, l1_b_g_q, l1_b_g_kv, l1_b_w_uq,
                                                 l1_b_w_ukv, l1_b_w_o)
            n_lora = B_Q_LORA + B_KV_LORA
            pad = LANES - B_NOPE - B_ROPE
            win = w_in[:, :n_lora].astype(BF16)
            wkr = jnp.pad(w_in[:, n_lora:], ((0, 0), (B_NOPE, pad))).astype(BF16)
            wuq = w_uq.reshape(B_Q_LORA, B_HEADS, B_NOPE + B_ROPE)
            wuq = jnp.pad(wuq, ((0, 0), (0, 0), (0, pad)))
            wuqt = wuq.reshape(B_Q_LORA, B_HEADS * LANES).T.astype(BF16)
            wukv = w_ukv.reshape(B_KV_LORA, B_HEADS, B_NOPE + B_VDIM)
            wukn = jnp.pad(wukv[:, :, :B_NOPE], ((0, 0), (0, 0), (0, LANES - B_NOPE)))
            wukn = wukn.reshape(B_KV_LORA, B_HEADS * LANES).astype(BF16)
            wuvt = wukv[:, :, B_NOPE:].reshape(B_KV_LORA, B_HEADS * B_VDIM).T.astype(BF16)
            qt, k, vt = _proj_mla(xs, mod, g_norm, win, wkr, g_q.reshape(1, -1),
                                  g_kv.reshape(1, -1), wuqt, wukn, wuvt, tabs_b,
                                  qscale=(B_NOPE + B_ROPE) ** -0.5 * LOG2E)
            o = _flash(qt, k, vt, dummy_lam, dummy_gsub, diff=False, lam_init=0.0)
        else:
            w_qkv, rpb, w_o = l2_c_w_qkv, l2_c_rpb, l2_c_w_o
            wqt = w_qkv[:, :D_MODEL].T.astype(BF16)
            wk = w_qkv[:, D_MODEL:2 * D_MODEL].astype(BF16)
            wvt = w_qkv[:, 2 * D_MODEL:].T.astype(BF16)
            qt, k, vt = _proj_qkv(xs, mod, g_norm, wqt, wk, wvt, tabs_a, rope=False,
                                  qscale=C_HEAD_DIM ** -0.5 * LOG2E, v_rows=C_HEAD_DIM)
            o = _na_attention(qt, k, vt, _na_bias_table(rpb))
        xs = _post(o, xs, mod, g_norm, w_o.astype(BF16), w_gu.astype(BF16),
                   w_down.astype(BF16), n_tiles=N_TILES - 1 if last else N_TILES)
    return xs
```

```python
import functools
import math

import numpy as np
import jax
import jax.numpy as jnp
from jax import lax
from jax.experimental import pallas as pl
from jax.experimental.pallas import tpu as pltpu

D_MODEL = 1024
BATCH = 4
SEQ = 8192
DEPTH = 4
GRID_W = 64
CTX_LEN = 256
ROPE_THETA = 10000.0
NORM_EPS = 1e-6

A_HEAD_DIM = 64
A_HEADS = 8
B_HEADS = 16
B_NOPE = 64
B_ROPE = 32
B_VDIM = 64
B_Q_LORA = 256
B_KV_LORA = 256
C_HEAD_DIM = 64
C_HEADS = 16
NA_ROWS = 8
NA_COLS = 16
FFN_HIDDEN = 2816

LANES = 128
STRIP = 64
ROW_TILE = 256
T_ALL = SEQ + CTX_LEN
N_TILES = T_ALL // ROW_TILE
CTX_TILE = N_TILES - 1
N_PAIRS = 8
KEY_GROUPS = tuple((3 * g, 3) for g in range(9)) + tuple((27 + 2 * g, 2) for g in range(3))
SCORE_TILES = 2
PROB_LEAD = 1
SCORE_LEAD = 2
V_ONES = 16
LOG2E = math.log2(math.e)
MASK_VALUE = -1e30
SAFE_EXPONENT = 96.0
NORM_SLACK = 1.02
VMEM_LIMIT = 60 * 1024 * 1024

F32 = jnp.float32
BF16 = jnp.bfloat16
NT_DIMS = (((1,), (1,)), ((), ()))


def _rms(x, g):
    return x * lax.rsqrt(jnp.mean(x * x, axis=-1, keepdims=True) + NORM_EPS) * g


def _sigmoid(x):
    return 1.0 / (1.0 + jnp.exp(-x))


def _params(n_axes):
    return pltpu.CompilerParams(
        dimension_semantics=("arbitrary",) * n_axes, vmem_limit_bytes=VMEM_LIMIT)


def _resident(shape):
    zeros = (0,) * len(shape)
    return pl.BlockSpec(shape, lambda *_: zeros, pipeline_mode=pl.Buffered(1))


def _mod_row(b, t):
    return jnp.where(t == CTX_TILE, BATCH, b)


def _mod_kernel(c_ref, w_ref, b_ref, o_ref):
    c = c_ref[...]
    a = (c * _sigmoid(c)).astype(BF16)
    o_ref[...] = jnp.dot(a, w_ref[...].astype(BF16), preferred_element_type=F32) + b_ref[...]


def _modulation(cc, w_mod, b_mod):
    n = w_mod.shape[1]
    blk = D_MODEL
    return pl.pallas_call(
        _mod_kernel,
        out_shape=jax.ShapeDtypeStruct((8, n), F32),
        grid=(n // blk,),
        in_specs=[pl.BlockSpec((8, D_MODEL), lambda j: (0, 0)),
                  pl.BlockSpec((D_MODEL, blk), lambda j: (0, j)),
                  pl.BlockSpec((1, blk), lambda j: (0, j))],
        out_specs=pl.BlockSpec((8, blk), lambda j: (0, j)),
        compiler_params=_params(1),
        name="modulation",
    )(cc, w_mod, b_mod.reshape(1, n))


def _rope_lanes(k, c, s, first, half):
    rot = jnp.where(first, pltpu.roll(k, LANES - half, 1), pltpu.roll(k, half, 1))
    return k * c + rot * s


def _store_values(vt_ref, vt, rows):
    ones = jnp.ones((V_ONES, ROW_TILE), BF16)
    for n in range(vt.shape[0] // rows):
        r0 = n * (rows + V_ONES)
        vt_ref[0, 0, r0:r0 + rows, :] = vt[n * rows:(n + 1) * rows].astype(BF16)
        vt_ref[0, 0, r0 + rows:r0 + rows + V_ONES, :] = ones


def _proj_qkv_kernel(x_ref, mod_ref, g_ref, wqt_ref, wk_ref, wvt_ref,
                     ck_ref, sk_ref, cqt_ref, sqt_ref,
                     qt_ref, k_ref, vt_ref, *, rope, qscale, v_rows):
    m = mod_ref[0]
    h = (_rms(x_ref[0], g_ref[0:1]) * (1.0 + m[1:2]) + m[0:1]).astype(BF16)
    qt = lax.dot_general(wqt_ref[...], h, NT_DIMS, preferred_element_type=F32) * qscale
    k = jnp.dot(h, wk_ref[...], preferred_element_type=F32)
    vt = lax.dot_general(wvt_ref[...], h, NT_DIMS, preferred_element_type=F32)
    _store_values(vt_ref, vt, v_rows)
    if not rope:
        qt_ref[0, 0] = qt.astype(BF16)
        k_ref[0] = k.astype(BF16)
        return
    half = A_HEAD_DIM // 2
    c, s = ck_ref[...], sk_ref[...]
    lane = lax.broadcasted_iota(jnp.int32, (ROW_TILE, LANES), 1)
    first = (lane % A_HEAD_DIM) < half
    for p in range(D_MODEL // LANES):
        sl = slice(p * LANES, (p + 1) * LANES)
        k_ref[0, :, sl] = _rope_lanes(k[:, sl], c, s, first, half).astype(BF16)
    ct, st = cqt_ref[...], sqt_ref[...]
    for g in range(D_MODEL // A_HEAD_DIM):
        r0 = g * A_HEAD_DIM
        t1, t2 = qt[r0:r0 + half], qt[r0 + half:r0 + 2 * half]
        qt_ref[0, 0, r0:r0 + half, :] = (t1 * ct - t2 * st).astype(BF16)
        qt_ref[0, 0, r0 + half:r0 + 2 * half, :] = (t1 * st + t2 * ct).astype(BF16)


def _row_specs():
    return [pl.BlockSpec((1, ROW_TILE, D_MODEL), lambda b, t: (b, t, 0)),
            pl.BlockSpec((1, 6, D_MODEL), lambda b, t: (_mod_row(b, t), 0, 0))]


def _rope_specs(half):
    return [pl.BlockSpec((ROW_TILE, LANES), lambda b, t: (t, 0)),
            pl.BlockSpec((ROW_TILE, LANES), lambda b, t: (t, 0)),
            pl.BlockSpec((half, ROW_TILE), lambda b, t: (0, t)),
            pl.BlockSpec((half, ROW_TILE), lambda b, t: (0, t))]


def _qkv_out(width, v_rows):
    vt_rows = D_MODEL // v_rows * (v_rows + V_ONES)
    shapes = (jax.ShapeDtypeStruct((BATCH, N_TILES, width, ROW_TILE), BF16),
              jax.ShapeDtypeStruct((BATCH, T_ALL, width), BF16),
              jax.ShapeDtypeStruct((BATCH, N_TILES, vt_rows, ROW_TILE), BF16))
    specs = (pl.BlockSpec((1, 1, width, ROW_TILE), lambda b, t: (b, t, 0, 0)),
             pl.BlockSpec((1, ROW_TILE, width), lambda b, t: (b, t, 0)),
             pl.BlockSpec((1, 1, vt_rows, ROW_TILE), lambda b, t: (b, t, 0, 0)))
    return shapes, specs


def _proj_qkv(x, mod, g_norm, wqt, wk, wvt, tabs, *, rope, qscale, v_rows):
    shapes, specs = _qkv_out(D_MODEL, v_rows)
    return pl.pallas_call(
        functools.partial(_proj_qkv_kernel, rope=rope, qscale=qscale, v_rows=v_rows),
        out_shape=shapes,
        grid=(BATCH, N_TILES),
        in_specs=_row_specs() + [
            _resident((4, D_MODEL)),
            _resident((D_MODEL, D_MODEL)), _resident((D_MODEL, D_MODEL)),
            _resident((D_MODEL, D_MODEL))] + _rope_specs(tabs[2].shape[0]),
        out_specs=specs,
        compiler_params=_params(2),
        name="proj_qkv",
    )(x, mod, g_norm, wqt, wk, wvt, *tabs)


def _proj_mla_kernel(x_ref, mod_ref, g_ref, win_ref, wkr_ref, gq_ref, gkv_ref,
                     wuqt_ref, wukn_ref, wuvt_ref, ck_ref, sk_ref, cqt_ref, sqt_ref,
                     qt_ref, k_ref, vt_ref, *, qscale):
    m = mod_ref[0]
    h = (_rms(x_ref[0], g_ref[0:1]) * (1.0 + m[1:2]) + m[0:1]).astype(BF16)
    z = jnp.dot(h, win_ref[...], preferred_element_type=F32)
    cq = _rms(z[:, :B_Q_LORA], gq_ref[...]).astype(BF16)
    ckv = _rms(z[:, B_Q_LORA:], gkv_ref[...]).astype(BF16)
    half = B_ROPE // 2

    qt = lax.dot_general(wuqt_ref[...], cq, NT_DIMS, preferred_element_type=F32) * qscale
    ct, st = cqt_ref[...], sqt_ref[...]
    for hh in range(B_HEADS):
        r0, r1 = hh * LANES, hh * LANES + B_NOPE
        qt_ref[0, 0, r0:r1, :] = qt[r0:r1].astype(BF16)
        t1, t2 = qt[r1:r1 + half], qt[r1 + half:r1 + 2 * half]
        qt_ref[0, 0, r1:r1 + half, :] = (t1 * ct - t2 * st).astype(BF16)
        qt_ref[0, 0, r1 + half:r1 + 2 * half, :] = (t1 * st + t2 * ct).astype(BF16)
        qt_ref[0, 0, r1 + B_ROPE:r0 + LANES, :] = qt[r1 + B_ROPE:r0 + LANES].astype(BF16)

    kr = jnp.dot(h, wkr_ref[...], preferred_element_type=F32)
    lane = lax.broadcasted_iota(jnp.int32, (ROW_TILE, LANES), 1)
    first = (lane >= B_NOPE) & (lane < B_NOPE + half)
    kr = _rope_lanes(kr, ck_ref[...], sk_ref[...], first, half)
    kn = jnp.dot(ckv, wukn_ref[...], preferred_element_type=F32)
    for hh in range(B_HEADS):
        sl = slice(hh * LANES, (hh + 1) * LANES)
        k_ref[0, :, sl] = (kn[:, sl] + kr).astype(BF16)

    vt = lax.dot_general(wuvt_ref[...], ckv, NT_DIMS, preferred_element_type=F32)
    _store_values(vt_ref, vt, B_VDIM)


def _proj_mla(x, mod, g_norm, win, wkr, gq, gkv, wuqt, wukn, wuvt, tabs, *, qscale):
    shapes, specs = _qkv_out(B_HEADS * LANES, B_VDIM)
    return pl.pallas_call(
        functools.partial(_proj_mla_kernel, qscale=qscale),
        out_shape=shapes,
        grid=(BATCH, N_TILES),
        in_specs=_row_specs() + [
            _resident((4, D_MODEL)),
            _resident(win.shape), _resident(wkr.shape),
            _resident((1, B_Q_LORA)), _resident((1, B_KV_LORA)),
            _resident(wuqt.shape), _resident(wukn.shape), _resident(wuvt.shape)]
            + _rope_specs(tabs[2].shape[0]),
        out_specs=specs,
        compiler_params=_params(2),
        name="proj_mla",
    )(x, mod, g_norm, win, wkr, gq, gkv, wuqt, wukn, wuvt, *tabs)


def _value_rows(diff):
    if diff:
        rows = 2 * A_HEAD_DIM
        return ((0, rows + V_ONES), (0, rows + V_ONES)), rows
    rows = LANES // 2
    return ((0, rows + V_ONES), (rows + V_ONES, 2 * (rows + V_ONES))), rows


def _split_queries(qt, qs_ref):
    dq = qt.shape[0]
    if dq == 2 * LANES:
        qs_ref[0] = qt[:LANES]
        qs_ref[1] = qt[LANES:]
        return
    row = lax.broadcasted_iota(jnp.int32, qt.shape, 0)
    zero = jnp.zeros_like(qt)
    qs_ref[0] = jnp.where(row < dq // 2, qt, zero)
    qs_ref[1] = jnp.where(row >= dq // 2, qt, zero)


def _slot_keys(k_ref, rows, slot):
    if k_ref.shape[-1] == 2 * LANES:
        return k_ref[0, rows, slot * LANES:(slot + 1) * LANES]
    return k_ref[0, rows, :]


def _slot_queries(qt, slot):
    half = qt.shape[0] // 2
    return qt[slot * half:(slot + 1) * half]


def _probabilities(s, m):
    return jnp.exp2(s - m).astype(BF16)


def _flash_kernel(lam_ref, gsub_ref, qt_ref, k_ref, vt_ref, o_ref,
                  qs_ref, c_ref, s0_ref, s1_ref, p0_ref, p1_ref, p2_ref, a0_ref, a1_ref, a2_ref,
                  m_ref, acc_ref, fin_ref, *, diff, lam_init):
    v_rows, n_val = _value_rows(diff)
    s_bufs, p_bufs, a_bufs = (s0_ref, s1_ref), (p0_ref, p1_ref, p2_ref), (a0_ref, a1_ref, a2_ref)
    n_groups = len(KEY_GROUPS)

    def tile_rows(j):
        return slice(j * ROW_TILE, (j + 1) * ROW_TILE)

    def scores(t0, n, slot, s_ref):
        for j in range(0, n, SCORE_TILES):
            rows = min(SCORE_TILES, n - j) * ROW_TILE
            k = _slot_keys(k_ref, slice((t0 + j) * ROW_TILE, (t0 + j) * ROW_TILE + rows), slot)
            s_ref[slot, j * ROW_TILE:j * ROW_TILE + rows, :] = jnp.dot(
                k, qs_ref[slot], preferred_element_type=F32)

    def stats(n, s_ref, a_ref, first):
        for slot in range(2):
            m_prev = jnp.full((1, ROW_TILE), MASK_VALUE, F32) if first else m_ref[slot]
            strips = [s_ref[slot, c * STRIP:(c + 1) * STRIP, :] for c in range(n * ROW_TILE // STRIP)]
            m_next = jnp.maximum(
                m_prev, jnp.max(functools.reduce(jnp.maximum, strips), axis=0, keepdims=True))
            a_ref[slot] = jnp.exp2(m_prev - m_next)
            m_ref[slot] = m_next

    def probs_tile(j, s_ref, p_ref):
        for slot in range(2):
            m = m_ref[slot]
            for c in range(ROW_TILE // STRIP):
                rows = slice(j * ROW_TILE + c * STRIP, j * ROW_TILE + (c + 1) * STRIP)
                p_ref[slot, rows, :] = _probabilities(s_ref[slot, rows, :], m)

    def values_tile(t, j, p_ref, pv):
        for slot in range(2):
            lo, hi = v_rows[slot]
            pv[slot].append(jnp.dot(vt_ref[0, t, lo:hi, :], p_ref[slot, tile_rows(j), :],
                                    preferred_element_type=F32))

    def accumulate(pv, a_ref):
        for slot in range(2):
            acc_ref[slot] = a_ref[slot] * acc_ref[slot] + functools.reduce(jnp.add, pv[slot])

    def finalize(i, src_ref):
        acc_a, acc_b = src_ref[0], src_ref[1]
        ot_a = acc_a[:n_val] * (1.0 / acc_a[n_val:n_val + 1])
        ot_b = acc_b[:n_val] * (1.0 / acc_b[n_val:n_val + 1])
        if diff:
            lv = lam_ref[...]
            lam = (jnp.exp(jnp.sum(lv[0:1] * lv[1:2], axis=1, keepdims=True))
                   - jnp.exp(jnp.sum(lv[2:3] * lv[3:4], axis=1, keepdims=True)) + lam_init)
            o = (ot_a - lam * ot_b).T
            o = _rms(o, gsub_ref[...]) * (1.0 - lam_init)
        else:
            o = jnp.concatenate([ot_a, ot_b], axis=0).T
        o_ref[0, pl.ds(pl.multiple_of(i * ROW_TILE, ROW_TILE), ROW_TILE), :] = o.astype(BF16)

    def exact_path():
        acc_ref[...] = jnp.zeros(acc_ref.shape, F32)
        fin_ref[...] = jnp.ones(fin_ref.shape, F32)

        def scores_group(g, s_ref):
            for slot in range(2):
                scores(KEY_GROUPS[g][0], KEY_GROUPS[g][1], slot, s_ref)

        def probs_group(g, s_ref, p_ref, a_ref, first):
            stats(KEY_GROUPS[g][1], s_ref, a_ref, first)
            for j in range(KEY_GROUPS[g][1]):
                probs_tile(j, s_ref, p_ref)

        _split_queries(qt_ref[0, 0], qs_ref)
        for fill in range(-SCORE_LEAD, 0):
            g_prob, g_sco = fill + PROB_LEAD, fill + SCORE_LEAD
            if g_prob >= 0:
                probs_group(g_prob, s_bufs[g_prob % 2], p_bufs[g_prob % 3], a_bufs[g_prob % 3],
                            g_prob == 0)
            scores_group(g_sco, s_bufs[g_sco % 2])

        def body(i, carry):
            for g in range(n_groups):
                g_prob, g_sco = (g + PROB_LEAD) % n_groups, (g + SCORE_LEAD) % n_groups
                t_val, n_val_tiles = KEY_GROUPS[g]
                t_sco, n_sco = KEY_GROUPS[g_sco]
                n_prob = KEY_GROUPS[g_prob][1]
                s_prob, p_prob, a_prob = s_bufs[g_prob % 2], p_bufs[g_prob % 3], a_bufs[g_prob % 3]
                stats(n_prob, s_prob, a_prob, g_prob == 0)
                if g + SCORE_LEAD == n_groups:
                    _split_queries(qt_ref[0, i + 1], qs_ref)
                pv = ([], [])
                for j in range(max(n_val_tiles, n_prob)):
                    if j < n_prob:
                        probs_tile(j, s_prob, p_prob)
                    if j < n_val_tiles:
                        values_tile(t_val + j, j, p_bufs[g % 3], pv)
                    if j < 2:
                        scores(t_sco, n_sco, j, s_bufs[g_sco % 2])
                accumulate(pv, a_bufs[g % 3])
                if g == 0:
                    finalize(jnp.maximum(i - 1, 0), fin_ref)
            fin_ref[...] = acc_ref[...]
            return carry
        lax.fori_loop(0, CTX_TILE, body, 0)
        finalize(CTX_TILE - 1, fin_ref)

        for slot in range(2):
            scores(CTX_TILE, 1, slot, s0_ref)
        stats(1, s0_ref, a0_ref, True)
        probs_tile(0, s0_ref, p0_ref)
        pv = ([], [])
        values_tile(CTX_TILE, 0, p0_ref, pv)
        accumulate(pv, a0_ref)
        finalize(CTX_TILE, acc_ref)

    def key_norm(slot):
        ones = jnp.ones((LANES, LANES), BF16)

        def step(t, mx):
            rows = pl.ds(pl.multiple_of(t * ROW_TILE, ROW_TILE), ROW_TILE)
            k = _slot_keys(k_ref, rows, slot).astype(F32)
            return jnp.maximum(mx, jnp.dot((k * k).astype(BF16), ones, preferred_element_type=F32))
        mx = lax.fori_loop(0, N_TILES, step, jnp.zeros((ROW_TILE, LANES), F32), unroll=3)
        return jnp.sqrt(jnp.max(mx[:, 0:1], axis=0, keepdims=True) * NORM_SLACK)

    def query_norm(slot):
        def step(t, mx):
            q = _slot_queries(qt_ref[0, t], slot).astype(F32)
            return jnp.maximum(mx, jnp.sum(q * q, axis=0, keepdims=True))
        mx = lax.fori_loop(0, N_TILES, step, jnp.zeros((1, ROW_TILE), F32), unroll=3)
        return jnp.sqrt(jnp.max(mx, axis=1, keepdims=True))

    def bounded_path(k_norms):
        def split(i):
            _split_queries(qt_ref[0, i], qs_ref)
            for slot in range(2):
                q = qs_ref[slot].astype(F32)
                c_ref[slot] = jnp.sqrt(jnp.sum(q * q, axis=0, keepdims=True)) * k_norms[slot] + 1.0

        def probabilities(t0, n, slot, p_ref):
            c = c_ref[slot]
            for j in range(0, n, SCORE_TILES):
                rows = min(SCORE_TILES, n - j) * ROW_TILE
                k = _slot_keys(k_ref, slice((t0 + j) * ROW_TILE, (t0 + j) * ROW_TILE + rows), slot)
                s = jnp.dot(k, qs_ref[slot], preferred_element_type=F32)
                p_ref[slot, j * ROW_TILE:j * ROW_TILE + rows, :] = _probabilities(s, c)

        def step(g, p_ref, next_probabilities):
            pv = ([], [])
            for j in range(KEY_GROUPS[g][1]):
                values_tile(KEY_GROUPS[g][0] + j, j, p_ref, pv)
                if j < len(next_probabilities):
                    next_probabilities[j]()
            for slot in range(2):
                total = functools.reduce(jnp.add, pv[slot])
                acc_ref[slot] = total if g == 0 else acc_ref[slot] + total

        fin_ref[...] = jnp.ones(fin_ref.shape, F32)
        split(0)
        for slot in range(2):
            probabilities(KEY_GROUPS[0][0], KEY_GROUPS[0][1], slot, p_bufs[0])

        def body(i, carry):
            for g in range(n_groups):
                g_next = (g + 1) % n_groups
                if g_next == 0:
                    split(i + 1)
                step(g, p_bufs[g % 2],
                     [functools.partial(probabilities, KEY_GROUPS[g_next][0], KEY_GROUPS[g_next][1],
                                        slot, p_bufs[g_next % 2]) for slot in range(2)])
                if g == 0:
                    finalize(jnp.maximum(i - 1, 0), fin_ref)
            fin_ref[...] = acc_ref[...]
            return carry
        lax.fori_loop(0, CTX_TILE, body, 0)
        finalize(CTX_TILE - 1, fin_ref)

        for slot in range(2):
            probabilities(CTX_TILE, 1, slot, p_bufs[0])
        pv = ([], [])
        values_tile(CTX_TILE, 0, p_bufs[0], pv)
        for slot in range(2):
            acc_ref[slot] = pv[slot][0]
        finalize(CTX_TILE, acc_ref)

    shared_keys = k_ref.shape[-1] == LANES
    k_norms = [key_norm(0)]
    k_norms.append(k_norms[0] if shared_keys else key_norm(1))
    widest = functools.reduce(jnp.maximum, [2.0 * (k_norms[slot] * query_norm(slot) + 1.0)
                                            for slot in range(2)])[0, 0]

    @pl.when(widest <= SAFE_EXPONENT)
    def _():
        bounded_path(k_norms)

    @pl.when(jnp.logical_not(widest <= SAFE_EXPONENT))
    def _():
        exact_path()


def _flash(qt, k, vt, lam, gsub, *, diff, lam_init):
    dq = k.shape[-1] // N_PAIRS
    vt_rows = vt.shape[2] // N_PAIRS
    acc_rows = _value_rows(diff)[0][0][1]
    group_rows = max(n for _, n in KEY_GROUPS) * ROW_TILE
    return pl.pallas_call(
        functools.partial(_flash_kernel, diff=diff, lam_init=lam_init),
        out_shape=jax.ShapeDtypeStruct((BATCH, T_ALL, D_MODEL), BF16),
        grid=(BATCH, N_PAIRS),
        in_specs=[pl.BlockSpec(lam.shape, lambda b, p: (0, 0)),
                  pl.BlockSpec(gsub.shape, lambda b, p: (0, 0)),
                  pl.BlockSpec((1, N_TILES, dq, ROW_TILE), lambda b, p: (b, 0, p, 0)),
                  pl.BlockSpec((1, T_ALL, dq), lambda b, p: (b, 0, p)),
                  pl.BlockSpec((1, N_TILES, vt_rows, ROW_TILE), lambda b, p: (b, 0, p, 0))],
        out_specs=pl.BlockSpec((1, T_ALL, LANES), lambda b, p: (b, 0, p)),
        scratch_shapes=[pltpu.VMEM((2, LANES, ROW_TILE), BF16),
                        pltpu.VMEM((2, 1, ROW_TILE), F32),
                        pltpu.VMEM((2, group_rows, ROW_TILE), F32),
                        pltpu.VMEM((2, group_rows, ROW_TILE), F32),
                        pltpu.VMEM((2, group_rows, ROW_TILE), BF16),
                        pltpu.VMEM((2, group_rows, ROW_TILE), BF16),
                        pltpu.VMEM((2, group_rows, ROW_TILE), BF16),
                        pltpu.VMEM((2, 1, ROW_TILE), F32),
                        pltpu.VMEM((2, 1, ROW_TILE), F32),
                        pltpu.VMEM((2, 1, ROW_TILE), F32),
                        pltpu.VMEM((2, 1, ROW_TILE), F32),
                        pltpu.VMEM((2, acc_rows, ROW_TILE), F32),
                        pltpu.VMEM((2, acc_rows, ROW_TILE), F32)],
        compiler_params=_params(2),
        name="flash_diff" if diff else "flash_pair",
    )(lam, gsub, qt, k, vt)


NA_Q_ROWS = ROW_TILE // GRID_W
NA_BAND_TILES = 3
NA_LAST_BAND = SEQ // ROW_TILE - NA_BAND_TILES


def _na_band_start(i):
    return jnp.clip(i - 1, 0, NA_LAST_BAND)


def _na_kernel(qt_ref, k0_ref, k1_ref, k2_ref, kc_ref, v0_ref, v1_ref, v2_ref, vc_ref,
               bias_ref, o_ref, qs_ref):
    k_refs = (k0_ref, k1_ref, k2_ref)
    v_refs = (v0_ref, v1_ref, v2_ref, vc_ref)
    v_rows, n_val = _value_rows(False)
    chains = [(b, slot) for b in range(BATCH) for slot in range(2)]
    for b in range(BATCH):
        _split_queries(qt_ref[b, 0], qs_ref.at[b])
    scores = {}
    for b, slot in chains:
        qs = qs_ref[b, slot]
        s = [jnp.dot(k_refs[j][b], qs, preferred_element_type=F32)
             + bias_ref[0, slot, j * ROW_TILE:(j + 1) * ROW_TILE, :]
             for j in range(NA_BAND_TILES)]
        s.append(jnp.dot(kc_ref[b], qs, preferred_element_type=F32))
        scores[b, slot] = s
    outs = {}
    for b, slot in chains:
        lo, hi = v_rows[slot]
        s = scores[b, slot]
        m = jnp.max(functools.reduce(jnp.maximum, s), axis=0, keepdims=True)
        acc = functools.reduce(jnp.add, [
            jnp.dot(r[b, 0, lo:hi, :], _probabilities(sj, m), preferred_element_type=F32)
            for r, sj in zip(v_refs, s)])
        outs[b, slot] = acc[:n_val] * (1.0 / acc[n_val:n_val + 1])
    for b in range(BATCH):
        o_ref[b] = jnp.concatenate([outs[b, 0], outs[b, 1]], axis=0).T.astype(BF16)


def _na_variant(i):
    return jnp.where(i == 0, 0, jnp.where(i < NA_LAST_BAND + NA_BAND_TILES - 1, 1,
                                          jnp.where(i < CTX_TILE, 2, 3)))


def _na_attention(qt, k, vt, bias):
    vt_rows = vt.shape[2] // N_PAIRS

    def kspec(j):
        return pl.BlockSpec((BATCH, ROW_TILE, LANES), lambda p, i: (0, _na_band_start(i) + j, p))

    def vspec(j):
        return pl.BlockSpec((BATCH, 1, vt_rows, ROW_TILE),
                            lambda p, i: (0, _na_band_start(i) + j, p, 0))

    return pl.pallas_call(
        _na_kernel,
        out_shape=jax.ShapeDtypeStruct((BATCH, T_ALL, D_MODEL), BF16),
        grid=(N_PAIRS, N_TILES),
        in_specs=[pl.BlockSpec((BATCH, 1, LANES, ROW_TILE), lambda p, i: (0, i, p, 0)),
                  kspec(0), kspec(1), kspec(2),
                  pl.BlockSpec((BATCH, ROW_TILE, LANES), lambda p, i: (0, CTX_TILE, p)),
                  vspec(0), vspec(1), vspec(2),
                  pl.BlockSpec((BATCH, 1, vt_rows, ROW_TILE), lambda p, i: (0, CTX_TILE, p, 0)),
                  pl.BlockSpec((1, 2, NA_BAND_TILES * ROW_TILE, ROW_TILE),
                               lambda p, i: (_na_variant(i), p, 0, 0))],
        out_specs=pl.BlockSpec((BATCH, ROW_TILE, LANES), lambda p, i: (0, i, p)),
        scratch_shapes=[pltpu.VMEM((BATCH, 2, LANES, ROW_TILE), BF16)],
        compiler_params=_params(2),
        name="na_attention",
    )(qt, k, k, k, k, vt, vt, vt, vt, bias)


def _na_bias_table(rpb):
    rows = SEQ // GRID_W
    band_rows = NA_BAND_TILES * NA_Q_ROWS
    col = np.arange(GRID_W)
    cstart = np.clip(col - NA_COLS // 2, 0, GRID_W - NA_COLS)
    ok_c = (col[None, :] >= cstart[:, None]) & (col[None, :] < cstart[:, None] + NA_COLS)
    dc = col[None, :] - col[:, None] + NA_COLS - 1
    onehot_c = ((dc[:, :, None] == np.arange(2 * NA_COLS - 1)) & ok_c[:, :, None]).astype(np.float32)
    dr = np.zeros((4, NA_Q_ROWS, band_rows), np.int32)
    ok_r = np.zeros((4, NA_Q_ROWS, band_rows), bool)
    for variant, i in enumerate((0, 1, SEQ // ROW_TILE - 1)):
        j = min(max(i - 1, 0), NA_LAST_BAND)
        r = NA_Q_ROWS * i + np.arange(NA_Q_ROWS)[:, None]
        k_row = NA_Q_ROWS * j + np.arange(band_rows)[None, :]
        rstart = np.clip(r - NA_ROWS // 2, 0, rows - NA_ROWS)
        ok_r[variant] = (k_row >= rstart) & (k_row < rstart + NA_ROWS)
        dr[variant] = np.where(ok_r[variant], k_row - r + NA_ROWS - 1, 0)
    by_col = jnp.einsum("hab,ckb->hack", rpb.astype(F32) * LOG2E, jnp.asarray(onehot_c),
                        precision=lax.Precision.HIGHEST)
    by_col = jnp.where(ok_c[None, None], by_col, MASK_VALUE)
    full = jnp.take(by_col, jnp.asarray(dr.reshape(-1)), axis=1)
    full = full.reshape(C_HEADS, 4, NA_Q_ROWS, band_rows, GRID_W, GRID_W)
    full = jnp.where(ok_r[None, :, :, :, None, None], full, MASK_VALUE)
    full = full.transpose(1, 0, 3, 5, 2, 4)
    return full.reshape(4, C_HEADS, NA_BAND_TILES * ROW_TILE, ROW_TILE)


def _post_kernel(o_ref, x_ref, mod_ref, g_ref, wo_ref, wgu_ref, wd_ref, xo_ref):
    m = mod_ref[0]
    y = jnp.dot(o_ref[0], wo_ref[...], preferred_element_type=F32)
    x1 = x_ref[0] + m[2:3] * _rms(y, g_ref[1:2])
    h2 = (_rms(x1, g_ref[2:3]) * (1.0 + m[4:5]) + m[3:4]).astype(BF16)
    gu = jnp.dot(h2, wgu_ref[...], preferred_element_type=F32)
    gate, up = gu[:, :FFN_HIDDEN], gu[:, FFN_HIDDEN:]
    a = (gate * _sigmoid(gate) * up).astype(BF16)
    f = jnp.dot(a, wd_ref[...], preferred_element_type=F32)
    xo_ref[0] = x1 + m[5:6] * _rms(f, g_ref[3:4])


def _post(o, x, mod, g_norm, wo, wgu, wd, *, n_tiles):
    return pl.pallas_call(
        _post_kernel,
        out_shape=jax.ShapeDtypeStruct((BATCH, n_tiles * ROW_TILE, D_MODEL), F32),
        grid=(BATCH, n_tiles),
        in_specs=[pl.BlockSpec((1, ROW_TILE, D_MODEL), lambda b, t: (b, t, 0))] + _row_specs() + [
            _resident((4, D_MODEL)), _resident(wo.shape), _resident(wgu.shape),
            _resident(wd.shape)],
        out_specs=pl.BlockSpec((1, ROW_TILE, D_MODEL), lambda b, t: (b, t, 0)),
        compiler_params=_params(2),
        name="post_attention",
    )(o, x, mod, g_norm, wo, wgu, wd)


def _rope_angles(rot_dim):
    n_freq = rot_dim // 4
    inv = ROPE_THETA ** (-jnp.arange(n_freq, dtype=F32) / n_freq)
    t = jnp.arange(SEQ, dtype=jnp.int32)
    row = (t // GRID_W).astype(F32)
    col = (t % GRID_W).astype(F32)
    ang = jnp.concatenate([row[:, None] * inv, col[:, None] * inv], axis=-1)
    cos = jnp.concatenate([jnp.cos(ang), jnp.ones((CTX_LEN, rot_dim // 2), F32)], axis=0)
    sin = jnp.concatenate([jnp.sin(ang), jnp.zeros((CTX_LEN, rot_dim // 2), F32)], axis=0)
    return cos, sin


def _rope_tables_diff():
    cos, sin = _rope_angles(A_HEAD_DIM)
    c_lane = jnp.tile(cos, (1, 4))
    s_lane = jnp.tile(jnp.concatenate([-sin, sin], axis=1), (1, 2))
    return c_lane, s_lane, cos.T, sin.T


def _rope_tables_mla():
    cos, sin = _rope_angles(B_ROPE)
    ones = jnp.ones((T_ALL, B_NOPE), F32)
    zeros = jnp.zeros((T_ALL, B_NOPE), F32)
    tail = LANES - B_NOPE - B_ROPE
    c_lane = jnp.concatenate([ones, cos, cos, ones[:, :tail]], axis=1)
    s_lane = jnp.concatenate([zeros, -sin, sin, zeros[:, :tail]], axis=1)
    return c_lane, s_lane, cos.T, sin.T


def kernel(x, c, ctx, c_ctx, l0_w_mod, l0_b_mod, l0_g_norm, l0_w_gu, l0_w_down, l0_a_w_qkv, l0_a_w_o, l0_a_lam, l0_a_g_sub, l1_w_mod, l1_b_mod, l1_g_norm, l1_w_gu, l1_w_down, l1_b_w_in, l1_b_g_q, l1_b_g_kv, l1_b_w_uq, l1_b_w_ukv, l1_b_w_o, l2_w_mod, l2_b_mod, l2_g_norm, l2_w_gu, l2_w_down, l2_c_w_qkv, l2_c_rpb, l2_c_w_o, l3_w_mod, l3_b_mod, l3_g_norm, l3_w_gu, l3_w_down, l3_a_w_qkv, l3_a_w_o, l3_a_lam, l3_a_g_sub):
    common = [
        (l0_w_mod, l0_b_mod, l0_g_norm, l0_w_gu, l0_w_down),
        (l1_w_mod, l1_b_mod, l1_g_norm, l1_w_gu, l1_w_down),
        (l2_w_mod, l2_b_mod, l2_g_norm, l2_w_gu, l2_w_down),
        (l3_w_mod, l3_b_mod, l3_g_norm, l3_w_gu, l3_w_down),
    ]
    diff_params = {0: (l0_a_w_qkv, l0_a_w_o, l0_a_lam, l0_a_g_sub),
                   3: (l3_a_w_qkv, l3_a_w_o, l3_a_lam, l3_a_g_sub)}

    xs = jnp.concatenate([x, ctx], axis=1)
    cc = jnp.concatenate([c, c_ctx[None, :], jnp.zeros((8 - BATCH - 1, D_MODEL), F32)], axis=0)
    tabs_a = _rope_tables_diff()
    tabs_b = _rope_tables_mla()
    dummy_lam = jnp.zeros((4, A_HEAD_DIM), F32)
    dummy_gsub = jnp.ones((1, LANES), F32)

    for i in range(DEPTH):
        w_mod, b_mod, g_norm, w_gu, w_down = common[i]
        last = i == DEPTH - 1
        mod = _modulation(cc, w_mod, b_mod).reshape(8, 6, D_MODEL)
        kind = i % 3
        if kind == 0:
            w_qkv, w_o, lam, g_sub = diff_params[i]
            lam_init = 0.8 - 0.6 * math.exp(-0.3 * i)
            wqt = w_qkv[:, :D_MODEL].T.astype(BF16)
            wk = w_qkv[:, D_MODEL:2 * D_MODEL].astype(BF16)
            wvt = w_qkv[:, 2 * D_MODEL:].T.astype(BF16)
            qt, k, vt = _proj_qkv(xs, mod, g_norm, wqt, wk, wvt, tabs_a, rope=True,
                                  qscale=A_HEAD_DIM ** -0.5 * LOG2E, v_rows=2 * A_HEAD_DIM)
            o = _flash(qt, k, vt, lam, g_sub.reshape(1, LANES), diff=True, lam_init=lam_init)
        elif kind == 1:
            w_in, g_q, g_kv, w_uq, w_ukv, w_o = (l1_b_w_in, l1_b_g_q, l1_b_g_kv, l1_b_w_uq,
                                                 l1_b_w_ukv, l1_b_w_o)
            n_lora = B_Q_LORA + B_KV_LORA
            pad = LANES - B_NOPE - B_ROPE
            win = w_in[:, :n_lora].astype(BF16)
            wkr = jnp.pad(w_in[:, n_lora:], ((0, 0), (B_NOPE, pad))).astype(BF16)
            wuq = w_uq.reshape(B_Q_LORA, B_HEADS, B_NOPE + B_ROPE)
            wuq = jnp.pad(wuq, ((0, 0), (0, 0), (0, pad)))
            wuqt = wuq.reshape(B_Q_LORA, B_HEADS * LANES).T.astype(BF16)
            wukv = w_ukv.reshape(B_KV_LORA, B_HEADS, B_NOPE + B_VDIM)
            wukn = jnp.pad(wukv[:, :, :B_NOPE], ((0, 0), (0, 0), (0, LANES - B_NOPE)))
            wukn = wukn.reshape(B_KV_LORA, B_HEADS * LANES).astype(BF16)
            wuvt = wukv[:, :, B_NOPE:].reshape(B_KV_LORA, B_HEADS * B_VDIM).T.astype(BF16)
            qt, k, vt = _proj_mla(xs, mod, g_norm, win, wkr, g_q.reshape(1, -1),
                                  g_kv.reshape(1, -1), wuqt, wukn, wuvt, tabs_b,
                                  qscale=(B_NOPE + B_ROPE) ** -0.5 * LOG2E)
            o = _flash(qt, k, vt, dummy_lam, dummy_gsub, diff=False, lam_init=0.0)
        else:
            w_qkv, rpb, w_o = l2_c_w_qkv, l2_c_rpb, l2_c_w_o
            wqt = w_qkv[:, :D_MODEL].T.astype(BF16)
            wk = w_qkv[:, D_MODEL:2 * D_MODEL].astype(BF16)
            wvt = w_qkv[:, 2 * D_MODEL:].T.astype(BF16)
            qt, k, vt = _proj_qkv(xs, mod, g_norm, wqt, wk, wvt, tabs_a, rope=False,
                                  qscale=C_HEAD_DIM ** -0.5 * LOG2E, v_rows=C_HEAD_DIM)
            o = _na_attention(qt, k, vt, _na_bias_table(rpb))
        xs = _post(o, xs, mod, g_norm, w_o.astype(BF16), w_gu.astype(BF16),
                   w_down.astype(BF16), n_tiles=N_TILES - 1 if last else N_TILES)
    return xs
```

```python
import functools
import math

import numpy as np
import jax
import jax.numpy as jnp
from jax import lax
from jax.experimental import pallas as pl
from jax.experimental.pallas import tpu as pltpu

D_MODEL = 1024
BATCH = 4
SEQ = 8192
DEPTH = 4
GRID_W = 64
CTX_LEN = 256
ROPE_THETA = 10000.0
NORM_EPS = 1e-6

A_HEAD_DIM = 64
A_HEADS = 8
B_HEADS = 16
B_NOPE = 64
B_ROPE = 32
B_VDIM = 64
B_Q_LORA = 256
B_KV_LORA = 256
C_HEAD_DIM = 64
C_HEADS = 16
NA_ROWS = 8
NA_COLS = 16
FFN_HIDDEN = 2816

LANES = 128
STRIP = 64
ROW_TILE = 256
T_ALL = SEQ + CTX_LEN
N_TILES = T_ALL // ROW_TILE
CTX_TILE = N_TILES - 1
N_PAIRS = 8
KEY_GROUPS = tuple((3 * g, 3) for g in range(9)) + tuple((27 + 2 * g, 2) for g in range(3))
SCORE_TILES = 2
PROB_LEAD = 1
SCORE_LEAD = 2
V_ONES = 16
LOG2E = math.log2(math.e)
MASK_VALUE = -1e30
SAFE_EXPONENT = 96.0
NORM_SLACK = 1.02
VMEM_LIMIT = 60 * 1024 * 1024

F32 = jnp.float32
BF16 = jnp.bfloat16
NT_DIMS = (((1,), (1,)), ((), ()))


def _rms(x, g):
    return x * lax.rsqrt(jnp.mean(x * x, axis=-1, keepdims=True) + NORM_EPS) * g


def _sigmoid(x):
    return 1.0 / (1.0 + jnp.exp(-x))


def _params(n_axes):
    return pltpu.CompilerParams(
        dimension_semantics=("arbitrary",) * n_axes, vmem_limit_bytes=VMEM_LIMIT)


def _resident(shape):
    zeros = (0,) * len(shape)
    return pl.BlockSpec(shape, lambda *_: zeros, pipeline_mode=pl.Buffered(1))


def _mod_row(b, t):
    return jnp.where(t == CTX_TILE, BATCH, b)


def _mod_kernel(c_ref, w_ref, b_ref, o_ref):
    c = c_ref[...]
    a = (c * _sigmoid(c)).astype(BF16)
    o_ref[...] = jnp.dot(a, w_ref[...].astype(BF16), preferred_element_type=F32) + b_ref[...]


def _modulation(cc, w_mod, b_mod):
    n = w_mod.shape[1]
    blk = D_MODEL
    return pl.pallas_call(
        _mod_kernel,
        out_shape=jax.ShapeDtypeStruct((8, n), F32),
        grid=(n // blk,),
        in_specs=[pl.BlockSpec((8, D_MODEL), lambda j: (0, 0)),
                  pl.BlockSpec((D_MODEL, blk), lambda j: (0, j)),
                  pl.BlockSpec((1, blk), lambda j: (0, j))],
        out_specs=pl.BlockSpec((8, blk), lambda j: (0, j)),
        compiler_params=_params(1),
        name="modulation",
    )(cc, w_mod, b_mod.reshape(1, n))


def _rope_lanes(k, c, s, first, half):
    rot = jnp.where(first, pltpu.roll(k, LANES - half, 1), pltpu.roll(k, half, 1))
    return k * c + rot * s


def _store_values(vt_ref, vt, rows):
    ones = jnp.ones((V_ONES, ROW_TILE), BF16)
    for n in range(vt.shape[0] // rows):
        r0 = n * (rows + V_ONES)
        vt_ref[0, 0, r0:r0 + rows, :] = vt[n * rows:(n + 1) * rows].astype(BF16)
        vt_ref[0, 0, r0 + rows:r0 + rows + V_ONES, :] = ones


def _proj_qkv_kernel(x_ref, mod_ref, g_ref, wqt_ref, wk_ref, wvt_ref,
                     ck_ref, sk_ref, cqt_ref, sqt_ref,
                     qt_ref, k_ref, vt_ref, *, rope, qscale, v_rows):
    m = mod_ref[0]
    h = (_rms(x_ref[0], g_ref[0:1]) * (1.0 + m[1:2]) + m[0:1]).astype(BF16)
    qt = lax.dot_general(wqt_ref[...], h, NT_DIMS, preferred_element_type=F32) * qscale
    k = jnp.dot(h, wk_ref[...], preferred_element_type=F32)
    vt = lax.dot_general(wvt_ref[...], h, NT_DIMS, preferred_element_type=F32)
    _store_values(vt_ref, vt, v_rows)
    if not rope:
        qt_ref[0, 0] = qt.astype(BF16)
        k_ref[0] = k.astype(BF16)
        return
    half = A_HEAD_DIM // 2
    c, s = ck_ref[...], sk_ref[...]
    lane = lax.broadcasted_iota(jnp.int32, (ROW_TILE, LANES), 1)
    first = (lane % A_HEAD_DIM) < half
    for p in range(D_MODEL // LANES):
        sl = slice(p * LANES, (p + 1) * LANES)
        k_ref[0, :, sl] = _rope_lanes(k[:, sl], c, s, first, half).astype(BF16)
    ct, st = cqt_ref[...], sqt_ref[...]
    for g in range(D_MODEL // A_HEAD_DIM):
        r0 = g * A_HEAD_DIM
        t1, t2 = qt[r0:r0 + half], qt[r0 + half:r0 + 2 * half]
        qt_ref[0, 0, r0:r0 + half, :] = (t1 * ct - t2 * st).astype(BF16)
        qt_ref[0, 0, r0 + half:r0 + 2 * half, :] = (t1 * st + t2 * ct).astype(BF16)


def _row_specs():
    return [pl.BlockSpec((1, ROW_TILE, D_MODEL), lambda b, t: (b, t, 0)),
            pl.BlockSpec((1, 6, D_MODEL), lambda b, t: (_mod_row(b, t), 0, 0))]


def _rope_specs(half):
    return [pl.BlockSpec((ROW_TILE, LANES), lambda b, t: (t, 0)),
            pl.BlockSpec((ROW_TILE, LANES), lambda b, t: (t, 0)),
            pl.BlockSpec((half, ROW_TILE), lambda b, t: (0, t)),
            pl.BlockSpec((half, ROW_TILE), lambda b, t: (0, t))]


def _qkv_out(width, v_rows):
    vt_rows = D_MODEL // v_rows * (v_rows + V_ONES)
    shapes = (jax.ShapeDtypeStruct((BATCH, N_TILES, width, ROW_TILE), BF16),
              jax.ShapeDtypeStruct((BATCH, T_ALL, width), BF16),
              jax.ShapeDtypeStruct((BATCH, N_TILES, vt_rows, ROW_TILE), BF16))
    specs = (pl.BlockSpec((1, 1, width, ROW_TILE), lambda b, t: (b, t, 0, 0)),
             pl.BlockSpec((1, ROW_TILE, width), lambda b, t: (b, t, 0)),
             pl.BlockSpec((1, 1, vt_rows, ROW_TILE), lambda b, t: (b, t, 0, 0)))
    return shapes, specs


def _proj_qkv(x, mod, g_norm, wqt, wk, wvt, tabs, *, rope, qscale, v_rows):
    shapes, specs = _qkv_out(D_MODEL, v_rows)
    return pl.pallas_call(
        functools.partial(_proj_qkv_kernel, rope=rope, qscale=qscale, v_rows=v_rows),
        out_shape=shapes,
        grid=(BATCH, N_TILES),
        in_specs=_row_specs() + [
            _resident((4, D_MODEL)),
            _resident((D_MODEL, D_MODEL)), _resident((D_MODEL, D_MODEL)),
            _resident((D_MODEL, D_MODEL))] + _rope_specs(tabs[2].shape[0]),
        out_specs=specs,
        compiler_params=_params(2),
        name="proj_qkv",
    )(x, mod, g_norm, wqt, wk, wvt, *tabs)


def _proj_mla_kernel(x_ref, mod_ref, g_ref, win_ref, wkr_ref, gq_ref, gkv_ref,
                     wuqt_ref, wukn_ref, wuvt_ref, ck_ref, sk_ref, cqt_ref, sqt_ref,
                     qt_ref, k_ref, vt_ref, *, qscale):
    m = mod_ref[0]
    h = (_rms(x_ref[0], g_ref[0:1]) * (1.0 + m[1:2]) + m[0:1]).astype(BF16)
    z = jnp.dot(h, win_ref[...], preferred_element_type=F32)
    cq = _rms(z[:, :B_Q_LORA], gq_ref[...]).astype(BF16)
    ckv = _rms(z[:, B_Q_LORA:], gkv_ref[...]).astype(BF16)
    half = B_ROPE // 2

    qt = lax.dot_general(wuqt_ref[...], cq, NT_DIMS, preferred_element_type=F32) * qscale
    ct, st = cqt_ref[...], sqt_ref[...]
    for hh in range(B_HEADS):
        r0, r1 = hh * LANES, hh * LANES + B_NOPE
        qt_ref[0, 0, r0:r1, :] = qt[r0:r1].astype(BF16)
        t1, t2 = qt[r1:r1 + half], qt[r1 + half:r1 + 2 * half]
        qt_ref[0, 0, r1:r1 + half, :] = (t1 * ct - t2 * st).astype(BF16)
        qt_ref[0, 0, r1 + half:r1 + 2 * half, :] = (t1 * st + t2 * ct).astype(BF16)
        qt_ref[0, 0, r1 + B_ROPE:r0 + LANES, :] = qt[r1 + B_ROPE:r0 + LANES].astype(BF16)

    kr = jnp.dot(h, wkr_ref[...], preferred_element_type=F32)
    lane = lax.broadcasted_iota(jnp.int32, (ROW_TILE, LANES), 1)
    first = (lane >= B_NOPE) & (lane < B_NOPE + half)
    kr = _rope_lanes(kr, ck_ref[...], sk_ref[...], first, half)
    kn = jnp.dot(ckv, wukn_ref[...], preferred_element_type=F32)
    for hh in range(B_HEADS):
        sl = slice(hh * LANES, (hh + 1) * LANES)
        k_ref[0, :, sl] = (kn[:, sl] + kr).astype(BF16)

    vt = lax.dot_general(wuvt_ref[...], ckv, NT_DIMS, preferred_element_type=F32)
    _store_values(vt_ref, vt, B_VDIM)


def _proj_mla(x, mod, g_norm, win, wkr, gq, gkv, wuqt, wukn, wuvt, tabs, *, qscale):
    shapes, specs = _qkv_out(B_HEADS * LANES, B_VDIM)
    return pl.pallas_call(
        functools.partial(_proj_mla_kernel, qscale=qscale),
        out_shape=shapes,
        grid=(BATCH, N_TILES),
        in_specs=_row_specs() + [
            _resident((4, D_MODEL)),
            _resident(win.shape), _resident(wkr.shape),
            _resident((1, B_Q_LORA)), _resident((1, B_KV_LORA)),
            _resident(wuqt.shape), _resident(wukn.shape), _resident(wuvt.shape)]
            + _rope_specs(tabs[2].shape[0]),
        out_specs=specs,
        compiler_params=_params(2),
        name="proj_mla",
    )(x, mod, g_norm, win, wkr, gq, gkv, wuqt, wukn, wuvt, *tabs)


def _value_rows(diff):
    if diff:
        rows = 2 * A_HEAD_DIM
        return ((0, rows + V_ONES), (0, rows + V_ONES)), rows
    rows = LANES // 2
    return ((0, rows + V_ONES), (rows + V_ONES, 2 * (rows + V_ONES))), rows


def _split_queries(qt, qs_ref):
    dq = qt.shape[0]
    if dq == 2 * LANES:
        qs_ref[0] = qt[:LANES]
        qs_ref[1] = qt[LANES:]
        return
    row = lax.broadcasted_iota(jnp.int32, qt.shape, 0)
    zero = jnp.zeros_like(qt)
    qs_ref[0] = jnp.where(row < dq // 2, qt, zero)
    qs_ref[1] = jnp.where(row >= dq // 2, qt, zero)


def _slot_keys(k_ref, rows, slot):
    if k_ref.shape[-1] == 2 * LANES:
        return k_ref[0, rows, slot * LANES:(slot + 1) * LANES]
    return k_ref[0, rows, :]


def _slot_queries(qt, slot):
    half = qt.shape[0] // 2
    return qt[slot * half:(slot + 1) * half]


def _probabilities(s, m):
    return jnp.exp2(s - m).astype(BF16)


def _flash_kernel(lam_ref, gsub_ref, qt_ref, k_ref, vt_ref, o_ref,
                  qs_ref, c_ref, s0_ref, s1_ref, p0_ref, p1_ref, p2_ref, a0_ref, a1_ref, a2_ref,
                  m_ref, acc_ref, fin_ref, *, diff, lam_init):
    v_rows, n_val = _value_rows(diff)
    s_bufs, p_bufs, a_bufs = (s0_ref, s1_ref), (p0_ref, p1_ref, p2_ref), (a0_ref, a1_ref, a2_ref)
    n_groups = len(KEY_GROUPS)

    def tile_rows(j):
        return slice(j * ROW_TILE, (j + 1) * ROW_TILE)

    def scores(t0, n, slot, s_ref):
        for j in range(0, n, SCORE_TILES):
            rows = min(SCORE_TILES, n - j) * ROW_TILE
            k = _slot_keys(k_ref, slice((t0 + j) * ROW_TILE, (t0 + j) * ROW_TILE + rows), slot)
            s_ref[slot, j * ROW_TILE:j * ROW_TILE + rows, :] = jnp.dot(
                k, qs_ref[slot], preferred_element_type=F32)

    def stats(n, s_ref, a_ref, first):
        for slot in range(2):
            m_prev = jnp.full((1, ROW_TILE), MASK_VALUE, F32) if first else m_ref[slot]
            strips = [s_ref[slot, c * STRIP:(c + 1) * STRIP, :] for c in range(n * ROW_TILE // STRIP)]
            m_next = jnp.maximum(
                m_prev, jnp.max(functools.reduce(jnp.maximum, strips), axis=0, keepdims=True))
            a_ref[slot] = jnp.exp2(m_prev - m_next)
            m_ref[slot] = m_next

    def probs_tile(j, s_ref, p_ref):
        for slot in range(2):
            m = m_ref[slot]
            for c in range(ROW_TILE // STRIP):
                rows = slice(j * ROW_TILE + c * STRIP, j * ROW_TILE + (c + 1) * STRIP)
                p_ref[slot, rows, :] = _probabilities(s_ref[slot, rows, :], m)

    def values_tile(t, j, p_ref, pv):
        for slot in range(2):
            lo, hi = v_rows[slot]
            pv[slot].append(jnp.dot(vt_ref[0, t, lo:hi, :], p_ref[slot, tile_rows(j), :],
                                    preferred_element_type=F32))

    def accumulate(pv, a_ref):
        for slot in range(2):
            acc_ref[slot] = a_ref[slot] * acc_ref[slot] + functools.reduce(jnp.add, pv[slot])

    def finalize(i, src_ref):
        acc_a, acc_b = src_ref[0], src_ref[1]
        ot_a = acc_a[:n_val] * (1.0 / acc_a[n_val:n_val + 1])
        ot_b = acc_b[:n_val] * (1.0 / acc_b[n_val:n_val + 1])
        if diff:
            lv = lam_ref[...]
            lam = (jnp.exp(jnp.sum(lv[0:1] * lv[1:2], axis=1, keepdims=True))
                   - jnp.exp(jnp.sum(lv[2:3] * lv[3:4], axis=1, keepdims=True)) + lam_init)
            o = (ot_a - lam * ot_b).T
            o = _rms(o, gsub_ref[...]) * (1.0 - lam_init)
        else:
            o = jnp.concatenate([ot_a, ot_b], axis=0).T
        o_ref[0, pl.ds(pl.multiple_of(i * ROW_TILE, ROW_TILE), ROW_TILE), :] = o.astype(BF16)

    def exact_path():
        acc_ref[...] = jnp.zeros(acc_ref.shape, F32)
        fin_ref[...] = jnp.ones(fin_ref.shape, F32)

        def scores_group(g, s_ref):
            for slot in range(2):
                scores(KEY_GROUPS[g][0], KEY_GROUPS[g][1], slot, s_ref)

        def probs_group(g, s_ref, p_ref, a_ref, first):
            stats(KEY_GROUPS[g][1], s_ref, a_ref, first)
            for j in range(KEY_GROUPS[g][1]):
                probs_tile(j, s_ref, p_ref)

        _split_queries(qt_ref[0, 0], qs_ref)
        for fill in range(-SCORE_LEAD, 0):
            g_prob, g_sco = fill + PROB_LEAD, fill + SCORE_LEAD
            if g_prob >= 0:
                probs_group(g_prob, s_bufs[g_prob % 2], p_bufs[g_prob % 3], a_bufs[g_prob % 3],
                            g_prob == 0)
            scores_group(g_sco, s_bufs[g_sco % 2])

        def body(i, carry):
            for g in range(n_groups):
                g_prob, g_sco = (g + PROB_LEAD) % n_groups, (g + SCORE_LEAD) % n_groups
                t_val, n_val_tiles = KEY_GROUPS[g]
                t_sco, n_sco = KEY_GROUPS[g_sco]
                n_prob = KEY_GROUPS[g_prob][1]
                s_prob, p_prob, a_prob = s_bufs[g_prob % 2], p_bufs[g_prob % 3], a_bufs[g_prob % 3]
                stats(n_prob, s_prob, a_prob, g_prob == 0)
                if g + SCORE_LEAD == n_groups:
                    _split_queries(qt_ref[0, i + 1], qs_ref)
                pv = ([], [])
                for j in range(max(n_val_tiles, n_prob)):
                    if j < n_prob:
                        probs_tile(j, s_prob, p_prob)
                    if j < n_val_tiles:
                        values_tile(t_val + j, j, p_bufs[g % 3], pv)
                    if j < 2:
                        scores(t_sco, n_sco, j, s_bufs[g_sco % 2])
                accumulate(pv, a_bufs[g % 3])
                if g == 0:
                    finalize(jnp.maximum(i - 1, 0), fin_ref)
            fin_ref[...] = acc_ref[...]
            return carry
        lax.fori_loop(0, CTX_TILE, body, 0)
        finalize(CTX_TILE - 1, fin_ref)

        for slot in range(2):
            scores(CTX_TILE, 1, slot, s0_ref)
        stats(1, s0_ref, a0_ref, True)
        probs_tile(0, s0_ref, p0_ref)
        pv = ([], [])
        values_tile(CTX_TILE, 0, p0_ref, pv)
        accumulate(pv, a0_ref)
        finalize(CTX_TILE, acc_ref)

    def key_norms(slot, heads):
        width = LANES // heads
        r = lax.broadcasted_iota(jnp.int32, (LANES, LANES), 0)
        c = lax.broadcasted_iota(jnp.int32, (LANES, LANES), 1)
        same_head = ((r // width) == (c // width)).astype(BF16)

        def step(t, mx):
            rows = pl.ds(pl.multiple_of(t * ROW_TILE, ROW_TILE), ROW_TILE)
            k = _slot_keys(k_ref, rows, slot).astype(F32)
            return jnp.maximum(
                mx, jnp.dot((k * k).astype(BF16), same_head, preferred_element_type=F32))
        mx = lax.fori_loop(0, N_TILES, step, jnp.zeros((ROW_TILE, LANES), F32), unroll=3)
        return [jnp.sqrt(jnp.max(mx[:, h * width:h * width + 1], axis=0, keepdims=True)
                         * NORM_SLACK) for h in range(heads)]

    def query_norm(slot):
        def step(t, mx):
            q = _slot_queries(qt_ref[0, t], slot).astype(F32)
            return jnp.maximum(mx, jnp.sum(q * q, axis=0, keepdims=True))
        mx = lax.fori_loop(0, N_TILES, step, jnp.zeros((1, ROW_TILE), F32), unroll=3)
        return jnp.sqrt(jnp.max(mx, axis=1, keepdims=True))

    def bounded_path(k_norms):
        def split(i):
            _split_queries(qt_ref[0, i], qs_ref)
            for slot in range(2):
                q = qs_ref[slot].astype(F32)
                c_ref[slot] = jnp.sqrt(jnp.sum(q * q, axis=0, keepdims=True)) * k_norms[slot] + 1.0

        def probabilities(t0, n, slot, p_ref):
            c = c_ref[slot]
            for j in range(0, n, SCORE_TILES):
                rows = min(SCORE_TILES, n - j) * ROW_TILE
                k = _slot_keys(k_ref, slice((t0 + j) * ROW_TILE, (t0 + j) * ROW_TILE + rows), slot)
                s = jnp.dot(k, qs_ref[slot], preferred_element_type=F32)
                p_ref[slot, j * ROW_TILE:j * ROW_TILE + rows, :] = _probabilities(s, c)

        def step(g, p_ref, next_probabilities):
            pv = ([], [])
            for j in range(KEY_GROUPS[g][1]):
                values_tile(KEY_GROUPS[g][0] + j, j, p_ref, pv)
                if j < len(next_probabilities):
                    next_probabilities[j]()
            for slot in range(2):
                total = functools.reduce(jnp.add, pv[slot])
                acc_ref[slot] = total if g == 0 else acc_ref[slot] + total

        fin_ref[...] = jnp.ones(fin_ref.shape, F32)
        split(0)
        for slot in range(2):
            probabilities(KEY_GROUPS[0][0], KEY_GROUPS[0][1], slot, p_bufs[0])

        def body(i, carry):
            for g in range(n_groups):
                g_next = (g + 1) % n_groups
                if g_next == 0:
                    split(i + 1)
                step(g, p_bufs[g % 2],
                     [functools.partial(probabilities, KEY_GROUPS[g_next][0], KEY_GROUPS[g_next][1],
                                        slot, p_bufs[g_next % 2]) for slot in range(2)])
                if g == 0:
                    finalize(jnp.maximum(i - 1, 0), fin_ref)
            fin_ref[...] = acc_ref[...]
            return carry
        lax.fori_loop(0, CTX_TILE, body, 0)
        finalize(CTX_TILE - 1, fin_ref)

        for slot in range(2):
            probabilities(CTX_TILE, 1, slot, p_bufs[0])
        pv = ([], [])
        values_tile(CTX_TILE, 0, p_bufs[0], pv)
        for slot in range(2):
            acc_ref[slot] = pv[slot][0]
        finalize(CTX_TILE, acc_ref)

    if k_ref.shape[-1] == LANES:
        k_norms = key_norms(0, 2)
    else:
        k_norms = [key_norms(slot, 1)[0] for slot in range(2)]
    widest = functools.reduce(jnp.maximum, [2.0 * (k_norms[slot] * query_norm(slot) + 1.0)
                                            for slot in range(2)])[0, 0]

    @pl.when(widest <= SAFE_EXPONENT)
    def _():
        bounded_path(k_norms)

    @pl.when(jnp.logical_not(widest <= SAFE_EXPONENT))
    def _():
        exact_path()


def _flash(qt, k, vt, lam, gsub, *, diff, lam_init):
    dq = k.shape[-1] // N_PAIRS
    vt_rows = vt.shape[2] // N_PAIRS
    acc_rows = _value_rows(diff)[0][0][1]
    group_rows = max(n for _, n in KEY_GROUPS) * ROW_TILE
    return pl.pallas_call(
        functools.partial(_flash_kernel, diff=diff, lam_init=lam_init),
        out_shape=jax.ShapeDtypeStruct((BATCH, T_ALL, D_MODEL), BF16),
        grid=(BATCH, N_PAIRS),
        in_specs=[pl.BlockSpec(lam.shape, lambda b, p: (0, 0)),
                  pl.BlockSpec(gsub.shape, lambda b, p: (0, 0)),
                  pl.BlockSpec((1, N_TILES, dq, ROW_TILE), lambda b, p: (b, 0, p, 0)),
                  pl.BlockSpec((1, T_ALL, dq), lambda b, p: (b, 0, p)),
                  pl.BlockSpec((1, N_TILES, vt_rows, ROW_TILE), lambda b, p: (b, 0, p, 0))],
        out_specs=pl.BlockSpec((1, T_ALL, LANES), lambda b, p: (b, 0, p)),
        scratch_shapes=[pltpu.VMEM((2, LANES, ROW_TILE), BF16),
                        pltpu.VMEM((2, 1, ROW_TILE), F32),
                        pltpu.VMEM((2, group_rows, ROW_TILE), F32),
                        pltpu.VMEM((2, group_rows, ROW_TILE), F32),
                        pltpu.VMEM((2, group_rows, ROW_TILE), BF16),
                        pltpu.VMEM((2, group_rows, ROW_TILE), BF16),
                        pltpu.VMEM((2, group_rows, ROW_TILE), BF16),
                        pltpu.VMEM((2, 1, ROW_TILE), F32),
                        pltpu.VMEM((2, 1, ROW_TILE), F32),
                        pltpu.VMEM((2, 1, ROW_TILE), F32),
                        pltpu.VMEM((2, 1, ROW_TILE), F32),
                        pltpu.VMEM((2, acc_rows, ROW_TILE), F32),
                        pltpu.VMEM((2, acc_rows, ROW_TILE), F32)],
        compiler_params=_params(2),
        name="flash_diff" if diff else "flash_pair",
    )(lam, gsub, qt, k, vt)


NA_Q_ROWS = ROW_TILE // GRID_W
NA_BAND_TILES = 3
NA_LAST_BAND = SEQ // ROW_TILE - NA_BAND_TILES


def _na_band_start(i):
    return jnp.clip(i - 1, 0, NA_LAST_BAND)


def _na_kernel(qt_ref, k0_ref, k1_ref, k2_ref, kc_ref, v0_ref, v1_ref, v2_ref, vc_ref,
               bias_ref, o_ref, qs_ref):
    k_refs = (k0_ref, k1_ref, k2_ref)
    v_refs = (v0_ref, v1_ref, v2_ref, vc_ref)
    v_rows, n_val = _value_rows(False)
    chains = [(b, slot) for b in range(BATCH) for slot in range(2)]
    for b in range(BATCH):
        _split_queries(qt_ref[b, 0], qs_ref.at[b])
    scores = {}
    for b, slot in chains:
        qs = qs_ref[b, slot]
        s = [jnp.dot(k_refs[j][b], qs, preferred_element_type=F32)
             + bias_ref[0, slot, j * ROW_TILE:(j + 1) * ROW_TILE, :]
             for j in range(NA_BAND_TILES)]
        s.append(jnp.dot(kc_ref[b], qs, preferred_element_type=F32))
        scores[b, slot] = s
    outs = {}
    for b, slot in chains:
        lo, hi = v_rows[slot]
        s = scores[b, slot]
        m = jnp.max(functools.reduce(jnp.maximum, s), axis=0, keepdims=True)
        acc = functools.reduce(jnp.add, [
            jnp.dot(r[b, 0, lo:hi, :], _probabilities(sj, m), preferred_element_type=F32)
            for r, sj in zip(v_refs, s)])
        outs[b, slot] = acc[:n_val] * (1.0 / acc[n_val:n_val + 1])
    for b in range(BATCH):
        o_ref[b] = jnp.concatenate([outs[b, 0], outs[b, 1]], axis=0).T.astype(BF16)


def _na_variant(i):
    return jnp.where(i == 0, 0, jnp.where(i < NA_LAST_BAND + NA_BAND_TILES - 1, 1,
                                          jnp.where(i < CTX_TILE, 2, 3)))


def _na_attention(qt, k, vt, bias):
    vt_rows = vt.shape[2] // N_PAIRS

    def kspec(j):
        return pl.BlockSpec((BATCH, ROW_TILE, LANES), lambda p, i: (0, _na_band_start(i) + j, p))

    def vspec(j):
        return pl.BlockSpec((BATCH, 1, vt_rows, ROW_TILE),
                            lambda p, i: (0, _na_band_start(i) + j, p, 0))

    return pl.pallas_call(
        _na_kernel,
        out_shape=jax.ShapeDtypeStruct((BATCH, T_ALL, D_MODEL), BF16),
        grid=(N_PAIRS, N_TILES),
        in_specs=[pl.BlockSpec((BATCH, 1, LANES, ROW_TILE), lambda p, i: (0, i, p, 0)),
                  kspec(0), kspec(1), kspec(2),
                  pl.BlockSpec((BATCH, ROW_TILE, LANES), lambda p, i: (0, CTX_TILE, p)),
                  vspec(0), vspec(1), vspec(2),
                  pl.BlockSpec((BATCH, 1, vt_rows, ROW_TILE), lambda p, i: (0, CTX_TILE, p, 0)),
                  pl.BlockSpec((1, 2, NA_BAND_TILES * ROW_TILE, ROW_TILE),
                               lambda p, i: (_na_variant(i), p, 0, 0))],
        out_specs=pl.BlockSpec((BATCH, ROW_TILE, LANES), lambda p, i: (0, i, p)),
        scratch_shapes=[pltpu.VMEM((BATCH, 2, LANES, ROW_TILE), BF16)],
        compiler_params=_params(2),
        name="na_attention",
    )(qt, k, k, k, k, vt, vt, vt, vt, bias)


def _na_bias_table(rpb):
    rows = SEQ // GRID_W
    band_rows = NA_BAND_TILES * NA_Q_ROWS
    col = np.arange(GRID_W)
    cstart = np.clip(col - NA_COLS // 2, 0, GRID_W - NA_COLS)
    ok_c = (col[None, :] >= cstart[:, None]) & (col[None, :] < cstart[:, None] + NA_COLS)
    dc = col[None, :] - col[:, None] + NA_COLS - 1
    onehot_c = ((dc[:, :, None] == np.arange(2 * NA_COLS - 1)) & ok_c[:, :, None]).astype(np.float32)
    dr = np.zeros((4, NA_Q_ROWS, band_rows), np.int32)
    ok_r = np.zeros((4, NA_Q_ROWS, band_rows), bool)
    for variant, i in enumerate((0, 1, SEQ // ROW_TILE - 1)):
        j = min(max(i - 1, 0), NA_LAST_BAND)
        r = NA_Q_ROWS * i + np.arange(NA_Q_ROWS)[:, None]
        k_row = NA_Q_ROWS * j + np.arange(band_rows)[None, :]
        rstart = np.clip(r - NA_ROWS // 2, 0, rows - NA_ROWS)
        ok_r[variant] = (k_row >= rstart) & (k_row < rstart + NA_ROWS)
        dr[variant] = np.where(ok_r[variant], k_row - r + NA_ROWS - 1, 0)
    by_col = jnp.einsum("hab,ckb->hack", rpb.astype(F32) * LOG2E, jnp.asarray(onehot_c),
                        precision=lax.Precision.HIGHEST)
    by_col = jnp.where(ok_c[None, None], by_col, MASK_VALUE)
    full = jnp.take(by_col, jnp.asarray(dr.reshape(-1)), axis=1)
    full = full.reshape(C_HEADS, 4, NA_Q_ROWS, band_rows, GRID_W, GRID_W)
    full = jnp.where(ok_r[None, :, :, :, None, None], full, MASK_VALUE)
    full = full.transpose(1, 0, 3, 5, 2, 4)
    return full.reshape(4, C_HEADS, NA_BAND_TILES * ROW_TILE, ROW_TILE)


def _post_kernel(o_ref, x_ref, mod_ref, g_ref, wo_ref, wgu_ref, wd_ref, xo_ref):
    m = mod_ref[0]
    y = jnp.dot(o_ref[0], wo_ref[...], preferred_element_type=F32)
    x1 = x_ref[0] + m[2:3] * _rms(y, g_ref[1:2])
    h2 = (_rms(x1, g_ref[2:3]) * (1.0 + m[4:5]) + m[3:4]).astype(BF16)
    gu = jnp.dot(h2, wgu_ref[...], preferred_element_type=F32)
    gate, up = gu[:, :FFN_HIDDEN], gu[:, FFN_HIDDEN:]
    a = (gate * _sigmoid(gate) * up).astype(BF16)
    f = jnp.dot(a, wd_ref[...], preferred_element_type=F32)
    xo_ref[0] = x1 + m[5:6] * _rms(f, g_ref[3:4])


def _post(o, x, mod, g_norm, wo, wgu, wd, *, n_tiles):
    return pl.pallas_call(
        _post_kernel,
        out_shape=jax.ShapeDtypeStruct((BATCH, n_tiles * ROW_TILE, D_MODEL), F32),
        grid=(BATCH, n_tiles),
        in_specs=[pl.BlockSpec((1, ROW_TILE, D_MODEL), lambda b, t: (b, t, 0))] + _row_specs() + [
            _resident((4, D_MODEL)), _resident(wo.shape), _resident(wgu.shape),
            _resident(wd.shape)],
        out_specs=pl.BlockSpec((1, ROW_TILE, D_MODEL), lambda b, t: (b, t, 0)),
        compiler_params=_params(2),
        name="post_attention",
    )(o, x, mod, g_norm, wo, wgu, wd)


def _rope_angles(rot_dim):
    n_freq = rot_dim // 4
    inv = ROPE_THETA ** (-jnp.arange(n_freq, dtype=F32) / n_freq)
    t = jnp.arange(SEQ, dtype=jnp.int32)
    row = (t // GRID_W).astype(F32)
    col = (t % GRID_W).astype(F32)
    ang = jnp.concatenate([row[:, None] * inv, col[:, None] * inv], axis=-1)
    cos = jnp.concatenate([jnp.cos(ang), jnp.ones((CTX_LEN, rot_dim // 2), F32)], axis=0)
    sin = jnp.concatenate([jnp.sin(ang), jnp.zeros((CTX_LEN, rot_dim // 2), F32)], axis=0)
    return cos, sin


def _rope_tables_diff():
    cos, sin = _rope_angles(A_HEAD_DIM)
    c_lane = jnp.tile(cos, (1, 4))
    s_lane = jnp.tile(jnp.concatenate([-sin, sin], axis=1), (1, 2))
    return c_lane, s_lane, cos.T, sin.T


def _rope_tables_mla():
    cos, sin = _rope_angles(B_ROPE)
    ones = jnp.ones((T_ALL, B_NOPE), F32)
    zeros = jnp.zeros((T_ALL, B_NOPE), F32)
    tail = LANES - B_NOPE - B_ROPE
    c_lane = jnp.concatenate([ones, cos, cos, ones[:, :tail]], axis=1)
    s_lane = jnp.concatenate([zeros, -sin, sin, zeros[:, :tail]], axis=1)
    return c_lane, s_lane, cos.T, sin.T


def kernel(x, c, ctx, c_ctx, l0_w_mod, l0_b_mod, l0_g_norm, l0_w_gu, l0_w_down, l0_a_w_qkv, l0_a_w_o, l0_a_lam, l0_a_g_sub, l1_w_mod, l1_b_mod, l1_g_norm, l1_w_gu, l1_w_down, l1_b_w_in, l1_b_g_q, l1_b_g_kv, l1_b_w_uq, l1_b_w_ukv, l1_b_w_o, l2_w_mod, l2_b_mod, l2_g_norm, l2_w_gu, l2_w_down, l2_c_w_qkv, l2_c_rpb, l2_c_w_o, l3_w_mod, l3_b_mod, l3_g_norm, l3_w_gu, l3_w_down, l3_a_w_qkv, l3_a_w_o, l3_a_lam, l3_a_g_sub):
    common = [
        (l0_w_mod, l0_b_mod, l0_g_norm, l0_w_gu, l0_w_down),
        (l1_w_mod, l1_b_mod, l1_g_norm, l1_w_gu, l1_w_down),
        (l2_w_mod, l2_b_mod, l2_g_norm, l2_w_gu, l2_w_down),
        (l3_w_mod, l3_b_mod, l3_g_norm, l3_w_gu, l3_w_down),
    ]
    diff_params = {0: (l0_a_w_qkv, l0_a_w_o, l0_a_lam, l0_a_g_sub),
                   3: (l3_a_w_qkv, l3_a_w_o, l3_a_lam, l3_a_g_sub)}

    xs = jnp.concatenate([x, ctx], axis=1)
    cc = jnp.concatenate([c, c_ctx[None, :], jnp.zeros((8 - BATCH - 1, D_MODEL), F32)], axis=0)
    tabs_a = _rope_tables_diff()
    tabs_b = _rope_tables_mla()
    dummy_lam = jnp.zeros((4, A_HEAD_DIM), F32)
    dummy_gsub = jnp.ones((1, LANES), F32)

    for i in range(DEPTH):
        w_mod, b_mod, g_norm, w_gu, w_down = common[i]
        last = i == DEPTH - 1
        mod = _modulation(cc, w_mod, b_mod).reshape(8, 6, D_MODEL)
        kind = i % 3
        if kind == 0:
            w_qkv, w_o, lam, g_sub = diff_params[i]
            lam_init = 0.8 - 0.6 * math.exp(-0.3 * i)
            wqt = w_qkv[:, :D_MODEL].T.astype(BF16)
            wk = w_qkv[:, D_MODEL:2 * D_MODEL].astype(BF16)
            wvt = w_qkv[:, 2 * D_MODEL:].T.astype(BF16)
            qt, k, vt = _proj_qkv(xs, mod, g_norm, wqt, wk, wvt, tabs_a, rope=True,
                                  qscale=A_HEAD_DIM ** -0.5 * LOG2E, v_rows=2 * A_HEAD_DIM)
            o = _flash(qt, k, vt, lam, g_sub.reshape(1, LANES), diff=True, lam_init=lam_init)
        elif kind == 1:
            w_in, g_q, g_kv, w_uq, w_ukv, w_o = (l1_b_w_in, l1_b_g_q, l1_b_g_kv, l1_b_w_uq,
                                                 l1_b_w_ukv, l1_b_w_o)
            n_lora = B_Q_LORA + B_KV_LORA
            pad = LANES - B_NOPE - B_ROPE
            win = w_in[:, :n_lora].astype(BF16)
            wkr = jnp.pad(w_in[:, n_lora:], ((0, 0), (B_NOPE, pad))).astype(BF16)
            wuq = w_uq.reshape(B_Q_LORA, B_HEADS, B_NOPE + B_ROPE)
            wuq = jnp.pad(wuq, ((0, 0), (0, 0), (0, pad)))
            wuqt = wuq.reshape(B_Q_LORA, B_HEADS * LANES).T.astype(BF16)
            wukv = w_ukv.reshape(B_KV_LORA, B_HEADS, B_NOPE + B_VDIM)
            wukn = jnp.pad(wukv[:, :, :B_NOPE], ((0, 0), (0, 0), (0, LANES - B_NOPE)))
            wukn = wukn.reshape(B_KV_LORA, B_HEADS * LANES).astype(BF16)
            wuvt = wukv[:, :, B_NOPE:].reshape(B_KV_LORA, B_HEADS * B_VDIM).T.astype(BF16)
            qt, k, vt = _proj_mla(xs, mod, g_norm, win, wkr, g_q.reshape(1, -1),
                                  g_kv.reshape(1, -1), wuqt, wukn, wuvt, tabs_b,
                                  qscale=(B_NOPE + B_ROPE) ** -0.5 * LOG2E)
            o = _flash(qt, k, vt, dummy_lam, dummy_gsub, diff=False, lam_init=0.0)
        else:
            w_qkv, rpb, w_o = l2_c_w_qkv, l2_c_rpb, l2_c_w_o
            wqt = w_qkv[:, :D_MODEL].T.astype(BF16)
            wk = w_qkv[:, D_MODEL:2 * D_MODEL].astype(BF16)
            wvt = w_qkv[:, 2 * D_MODEL:].T.astype(BF16)
            qt, k, vt = _proj_qkv(xs, mod, g_norm, wqt, wk, wvt, tabs_a, rope=False,
                                  qscale=C_HEAD_DIM ** -0.5 * LOG2E, v_rows=C_HEAD_DIM)
            o = _na_attention(qt, k, vt, _na_bias_table(rpb))
        xs = _post(o, xs, mod, g_norm, w_o.astype(BF16), w_gu.astype(BF16),
                   w_down.astype(BF16), n_tiles=N_TILES - 1 if last else N_TILES)
    return xs
```

```python
import functools
import math

import numpy as np
import jax
import jax.numpy as jnp
from jax import lax
from jax.experimental import pallas as pl
from jax.experimental.pallas import tpu as pltpu

D_MODEL = 1024
BATCH = 4
SEQ = 8192
DEPTH = 4
GRID_W = 64
CTX_LEN = 256
ROPE_THETA = 10000.0
NORM_EPS = 1e-6

A_HEAD_DIM = 64
A_HEADS = 8
B_HEADS = 16
B_NOPE = 64
B_ROPE = 32
B_VDIM = 64
B_Q_LORA = 256
B_KV_LORA = 256
C_HEAD_DIM = 64
C_HEADS = 16
NA_ROWS = 8
NA_COLS = 16
FFN_HIDDEN = 2816

LANES = 128
STRIP = 64
ROW_TILE = 256
T_ALL = SEQ + CTX_LEN
N_TILES = T_ALL // ROW_TILE
CTX_TILE = N_TILES - 1
N_PAIRS = 8
KEY_GROUPS = tuple((3 * g, 3) for g in range(9)) + tuple((27 + 2 * g, 2) for g in range(3))
SCORE_TILES = 2
PROB_LEAD = 1
SCORE_LEAD = 2
V_ONES = 16
LOG2E = math.log2(math.e)
MASK_VALUE = -1e30
SAFE_EXPONENT = 96.0
NORM_SLACK = 1.02
VMEM_LIMIT = 60 * 1024 * 1024

F32 = jnp.float32
BF16 = jnp.bfloat16
NT_DIMS = (((1,), (1,)), ((), ()))


def _rms(x, g):
    return x * lax.rsqrt(jnp.mean(x * x, axis=-1, keepdims=True) + NORM_EPS) * g


def _sigmoid(x):
    return 1.0 / (1.0 + jnp.exp(-x))


def _params(n_axes):
    return pltpu.CompilerParams(
        dimension_semantics=("arbitrary",) * n_axes, vmem_limit_bytes=VMEM_LIMIT)


def _resident(shape):
    zeros = (0,) * len(shape)
    return pl.BlockSpec(shape, lambda *_: zeros, pipeline_mode=pl.Buffered(1))


def _mod_row(b, t):
    return jnp.where(t == CTX_TILE, BATCH, b)


def _mod_kernel(c_ref, w_ref, b_ref, o_ref):
    c = c_ref[...]
    a = (c * _sigmoid(c)).astype(BF16)
    o_ref[...] = jnp.dot(a, w_ref[...].astype(BF16), preferred_element_type=F32) + b_ref[...]


def _modulation(cc, w_mod, b_mod):
    n = w_mod.shape[1]
    blk = D_MODEL
    return pl.pallas_call(
        _mod_kernel,
        out_shape=jax.ShapeDtypeStruct((8, n), F32),
        grid=(n // blk,),
        in_specs=[pl.BlockSpec((8, D_MODEL), lambda j: (0, 0)),
                  pl.BlockSpec((D_MODEL, blk), lambda j: (0, j)),
                  pl.BlockSpec((1, blk), lambda j: (0, j))],
        out_specs=pl.BlockSpec((8, blk), lambda j: (0, j)),
        compiler_params=_params(1),
        name="modulation",
    )(cc, w_mod, b_mod.reshape(1, n))


def _rope_lanes(k, c, s, first, half):
    rot = jnp.where(first, pltpu.roll(k, LANES - half, 1), pltpu.roll(k, half, 1))
    return k * c + rot * s


def _store_values(vt_ref, vt, rows):
    ones = jnp.ones((V_ONES, ROW_TILE), BF16)
    for n in range(vt.shape[0] // rows):
        r0 = n * (rows + V_ONES)
        vt_ref[0, 0, r0:r0 + rows, :] = vt[n * rows:(n + 1) * rows].astype(BF16)
        vt_ref[0, 0, r0 + rows:r0 + rows + V_ONES, :] = ones


def _proj_qkv_kernel(x_ref, mod_ref, g_ref, wqt_ref, wk_ref, wvt_ref,
                     ck_ref, sk_ref, cqt_ref, sqt_ref,
                     qt_ref, k_ref, vt_ref, *, rope, qscale, v_rows):
    m = mod_ref[0]
    h = (_rms(x_ref[0], g_ref[0:1]) * (1.0 + m[1:2]) + m[0:1]).astype(BF16)
    qt = lax.dot_general(wqt_ref[...], h, NT_DIMS, preferred_element_type=F32) * qscale
    k = jnp.dot(h, wk_ref[...], preferred_element_type=F32)
    vt = lax.dot_general(wvt_ref[...], h, NT_DIMS, preferred_element_type=F32)
    _store_values(vt_ref, vt, v_rows)
    if not rope:
        qt_ref[0, 0] = qt.astype(BF16)
        k_ref[0] = k.astype(BF16)
        return
    half = A_HEAD_DIM // 2
    c, s = ck_ref[...], sk_ref[...]
    lane = lax.broadcasted_iota(jnp.int32, (ROW_TILE, LANES), 1)
    first = (lane % A_HEAD_DIM) < half
    for p in range(D_MODEL // LANES):
        sl = slice(p * LANES, (p + 1) * LANES)
        k_ref[0, :, sl] = _rope_lanes(k[:, sl], c, s, first, half).astype(BF16)
    ct, st = cqt_ref[...], sqt_ref[...]
    for g in range(D_MODEL // A_HEAD_DIM):
        r0 = g * A_HEAD_DIM
        t1, t2 = qt[r0:r0 + half], qt[r0 + half:r0 + 2 * half]
        qt_ref[0, 0, r0:r0 + half, :] = (t1 * ct - t2 * st).astype(BF16)
        qt_ref[0, 0, r0 + half:r0 + 2 * half, :] = (t1 * st + t2 * ct).astype(BF16)


def _row_specs():
    return [pl.BlockSpec((1, ROW_TILE, D_MODEL), lambda b, t: (b, t, 0)),
            pl.BlockSpec((1, 6, D_MODEL), lambda b, t: (_mod_row(b, t), 0, 0))]


def _rope_specs(half):
    return [pl.BlockSpec((ROW_TILE, LANES), lambda b, t: (t, 0)),
            pl.BlockSpec((ROW_TILE, LANES), lambda b, t: (t, 0)),
            pl.BlockSpec((half, ROW_TILE), lambda b, t: (0, t)),
            pl.BlockSpec((half, ROW_TILE), lambda b, t: (0, t))]


def _qkv_out(width, v_rows):
    vt_rows = D_MODEL // v_rows * (v_rows + V_ONES)
    shapes = (jax.ShapeDtypeStruct((BATCH, N_TILES, width, ROW_TILE), BF16),
              jax.ShapeDtypeStruct((BATCH, T_ALL, width), BF16),
              jax.ShapeDtypeStruct((BATCH, N_TILES, vt_rows, ROW_TILE), BF16))
    specs = (pl.BlockSpec((1, 1, width, ROW_TILE), lambda b, t: (b, t, 0, 0)),
             pl.BlockSpec((1, ROW_TILE, width), lambda b, t: (b, t, 0)),
             pl.BlockSpec((1, 1, vt_rows, ROW_TILE), lambda b, t: (b, t, 0, 0)))
    return shapes, specs


def _proj_qkv(x, mod, g_norm, wqt, wk, wvt, tabs, *, rope, qscale, v_rows):
    shapes, specs = _qkv_out(D_MODEL, v_rows)
    return pl.pallas_call(
        functools.partial(_proj_qkv_kernel, rope=rope, qscale=qscale, v_rows=v_rows),
        out_shape=shapes,
        grid=(BATCH, N_TILES),
        in_specs=_row_specs() + [
            _resident((4, D_MODEL)),
            _resident((D_MODEL, D_MODEL)), _resident((D_MODEL, D_MODEL)),
            _resident((D_MODEL, D_MODEL))] + _rope_specs(tabs[2].shape[0]),
        out_specs=specs,
        compiler_params=_params(2),
        name="proj_qkv",
    )(x, mod, g_norm, wqt, wk, wvt, *tabs)


def _proj_mla_kernel(x_ref, mod_ref, g_ref, win_ref, wkr_ref, gq_ref, gkv_ref,
                     wuqt_ref, wukn_ref, wuvt_ref, ck_ref, sk_ref, cqt_ref, sqt_ref,
                     qt_ref, k_ref, vt_ref, *, qscale):
    m = mod_ref[0]
    h = (_rms(x_ref[0], g_ref[0:1]) * (1.0 + m[1:2]) + m[0:1]).astype(BF16)
    z = jnp.dot(h, win_ref[...], preferred_element_type=F32)
    cq = _rms(z[:, :B_Q_LORA], gq_ref[...]).astype(BF16)
    ckv = _rms(z[:, B_Q_LORA:], gkv_ref[...]).astype(BF16)
    half = B_ROPE // 2

    qt = lax.dot_general(wuqt_ref[...], cq, NT_DIMS, preferred_element_type=F32) * qscale
    ct, st = cqt_ref[...], sqt_ref[...]
    for hh in range(B_HEADS):
        r0, r1 = hh * LANES, hh * LANES + B_NOPE
        qt_ref[0, 0, r0:r1, :] = qt[r0:r1].astype(BF16)
        t1, t2 = qt[r1:r1 + half], qt[r1 + half:r1 + 2 * half]
        qt_ref[0, 0, r1:r1 + half, :] = (t1 * ct - t2 * st).astype(BF16)
        qt_ref[0, 0, r1 + half:r1 + 2 * half, :] = (t1 * st + t2 * ct).astype(BF16)
        qt_ref[0, 0, r1 + B_ROPE:r0 + LANES, :] = qt[r1 + B_ROPE:r0 + LANES].astype(BF16)

    kr = jnp.dot(h, wkr_ref[...], preferred_element_type=F32)
    lane = lax.broadcasted_iota(jnp.int32, (ROW_TILE, LANES), 1)
    first = (lane >= B_NOPE) & (lane < B_NOPE + half)
    kr = _rope_lanes(kr, ck_ref[...], sk_ref[...], first, half)
    kn = jnp.dot(ckv, wukn_ref[...], preferred_element_type=F32)
    for hh in range(B_HEADS):
        sl = slice(hh * LANES, (hh + 1) * LANES)
        k_ref[0, :, sl] = (kn[:, sl] + kr).astype(BF16)

    vt = lax.dot_general(wuvt_ref[...], ckv, NT_DIMS, preferred_element_type=F32)
    _store_values(vt_ref, vt, B_VDIM)


def _proj_mla(x, mod, g_norm, win, wkr, gq, gkv, wuqt, wukn, wuvt, tabs, *, qscale):
    shapes, specs = _qkv_out(B_HEADS * LANES, B_VDIM)
    return pl.pallas_call(
        functools.partial(_proj_mla_kernel, qscale=qscale),
        out_shape=shapes,
        grid=(BATCH, N_TILES),
        in_specs=_row_specs() + [
            _resident((4, D_MODEL)),
            _resident(win.shape), _resident(wkr.shape),
            _resident((1, B_Q_LORA)), _resident((1, B_KV_LORA)),
            _resident(wuqt.shape), _resident(wukn.shape), _resident(wuvt.shape)]
            + _rope_specs(tabs[2].shape[0]),
        out_specs=specs,
        compiler_params=_params(2),
        name="proj_mla",
    )(x, mod, g_norm, win, wkr, gq, gkv, wuqt, wukn, wuvt, *tabs)


def _value_rows(diff):
    if diff:
        rows = 2 * A_HEAD_DIM
        return ((0, rows + V_ONES), (0, rows + V_ONES)), rows
    rows = LANES // 2
    return ((0, rows + V_ONES), (rows + V_ONES, 2 * (rows + V_ONES))), rows


def _split_queries(qt, qs_ref):
    dq = qt.shape[0]
    if dq == 2 * LANES:
        qs_ref[0] = qt[:LANES]
        qs_ref[1] = qt[LANES:]
        return
    row = lax.broadcasted_iota(jnp.int32, qt.shape, 0)
    zero = jnp.zeros_like(qt)
    qs_ref[0] = jnp.where(row < dq // 2, qt, zero)
    qs_ref[1] = jnp.where(row >= dq // 2, qt, zero)


def _slot_keys(k_ref, rows, slot):
    if k_ref.shape[-1] == 2 * LANES:
        return k_ref[0, rows, slot * LANES:(slot + 1) * LANES]
    return k_ref[0, rows, :]


def _slot_queries(qt, slot):
    half = qt.shape[0] // 2
    return qt[slot * half:(slot + 1) * half]


def _probabilities(s, m):
    return jnp.exp2(s - m).astype(BF16)


def _flash_kernel(lam_ref, gsub_ref, qt_ref, k_ref, vt_ref, o_ref,
                  qs_ref, c_ref, s0_ref, s1_ref, p0_ref, p1_ref, p2_ref, a0_ref, a1_ref, a2_ref,
                  m_ref, acc_ref, fin_ref, *, diff, lam_init):
    v_rows, n_val = _value_rows(diff)
    s_bufs, p_bufs, a_bufs = (s0_ref, s1_ref), (p0_ref, p1_ref, p2_ref), (a0_ref, a1_ref, a2_ref)
    n_groups = len(KEY_GROUPS)

    def tile_rows(j):
        return slice(j * ROW_TILE, (j + 1) * ROW_TILE)

    def scores(t0, n, slot, s_ref):
        for j in range(0, n, SCORE_TILES):
            rows = min(SCORE_TILES, n - j) * ROW_TILE
            k = _slot_keys(k_ref, slice((t0 + j) * ROW_TILE, (t0 + j) * ROW_TILE + rows), slot)
            s_ref[slot, j * ROW_TILE:j * ROW_TILE + rows, :] = jnp.dot(
                k, qs_ref[slot], preferred_element_type=F32)

    def stats(n, s_ref, a_ref, first):
        for slot in range(2):
            m_prev = jnp.full((1, ROW_TILE), MASK_VALUE, F32) if first else m_ref[slot]
            strips = [s_ref[slot, c * STRIP:(c + 1) * STRIP, :] for c in range(n * ROW_TILE // STRIP)]
            m_next = jnp.maximum(
                m_prev, jnp.max(functools.reduce(jnp.maximum, strips), axis=0, keepdims=True))
            a_ref[slot] = jnp.exp2(m_prev - m_next)
            m_ref[slot] = m_next

    def probs_tile(j, s_ref, p_ref):
        for slot in range(2):
            m = m_ref[slot]
            for c in range(ROW_TILE // STRIP):
                rows = slice(j * ROW_TILE + c * STRIP, j * ROW_TILE + (c + 1) * STRIP)
                p_ref[slot, rows, :] = _probabilities(s_ref[slot, rows, :], m)

    def values_tile(t, j, p_ref, pv):
        for slot in range(2):
            lo, hi = v_rows[slot]
            pv[slot].append(jnp.dot(vt_ref[0, t, lo:hi, :], p_ref[slot, tile_rows(j), :],
                                    preferred_element_type=F32))

    def accumulate(pv, a_ref):
        for slot in range(2):
            acc_ref[slot] = a_ref[slot] * acc_ref[slot] + functools.reduce(jnp.add, pv[slot])

    def finalize(i, src_ref):
        acc_a, acc_b = src_ref[0], src_ref[1]
        ot_a = acc_a[:n_val] * (1.0 / acc_a[n_val:n_val + 1])
        ot_b = acc_b[:n_val] * (1.0 / acc_b[n_val:n_val + 1])
        if diff:
            lv = lam_ref[...]
            lam = (jnp.exp(jnp.sum(lv[0:1] * lv[1:2], axis=1, keepdims=True))
                   - jnp.exp(jnp.sum(lv[2:3] * lv[3:4], axis=1, keepdims=True)) + lam_init)
            o = (ot_a - lam * ot_b).T
            o = _rms(o, gsub_ref[...]) * (1.0 - lam_init)
        else:
            o = jnp.concatenate([ot_a, ot_b], axis=0).T
        o_ref[0, pl.ds(pl.multiple_of(i * ROW_TILE, ROW_TILE), ROW_TILE), :] = o.astype(BF16)

    def exact_path():
        acc_ref[...] = jnp.zeros(acc_ref.shape, F32)
        fin_ref[...] = jnp.ones(fin_ref.shape, F32)

        def scores_group(g, s_ref):
            for slot in range(2):
                scores(KEY_GROUPS[g][0], KEY_GROUPS[g][1], slot, s_ref)

        def probs_group(g, s_ref, p_ref, a_ref, first):
            stats(KEY_GROUPS[g][1], s_ref, a_ref, first)
            for j in range(KEY_GROUPS[g][1]):
                probs_tile(j, s_ref, p_ref)

        _split_queries(qt_ref[0, 0], qs_ref)
        for fill in range(-SCORE_LEAD, 0):
            g_prob, g_sco = fill + PROB_LEAD, fill + SCORE_LEAD
            if g_prob >= 0:
                probs_group(g_prob, s_bufs[g_prob % 2], p_bufs[g_prob % 3], a_bufs[g_prob % 3],
                            g_prob == 0)
            scores_group(g_sco, s_bufs[g_sco % 2])

        def body(i, carry):
            for g in range(n_groups):
                g_prob, g_sco = (g + PROB_LEAD) % n_groups, (g + SCORE_LEAD) % n_groups
                t_val, n_val_tiles = KEY_GROUPS[g]
                t_sco, n_sco = KEY_GROUPS[g_sco]
                n_prob = KEY_GROUPS[g_prob][1]
                s_prob, p_prob, a_prob = s_bufs[g_prob % 2], p_bufs[g_prob % 3], a_bufs[g_prob % 3]
                stats(n_prob, s_prob, a_prob, g_prob == 0)
                if g + SCORE_LEAD == n_groups:
                    _split_queries(qt_ref[0, i + 1], qs_ref)
                pv = ([], [])
                for j in range(max(n_val_tiles, n_prob)):
                    if j < n_prob:
                        probs_tile(j, s_prob, p_prob)
                    if j < n_val_tiles:
                        values_tile(t_val + j, j, p_bufs[g % 3], pv)
                    if j < 2:
                        scores(t_sco, n_sco, j, s_bufs[g_sco % 2])
                accumulate(pv, a_bufs[g % 3])
                if g == 0:
                    finalize(jnp.maximum(i - 1, 0), fin_ref)
            fin_ref[...] = acc_ref[...]
            return carry
        lax.fori_loop(0, CTX_TILE, body, 0)
        finalize(CTX_TILE - 1, fin_ref)

        for slot in range(2):
            scores(CTX_TILE, 1, slot, s0_ref)
        stats(1, s0_ref, a0_ref, True)
        probs_tile(0, s0_ref, p0_ref)
        pv = ([], [])
        values_tile(CTX_TILE, 0, p0_ref, pv)
        accumulate(pv, a0_ref)
        finalize(CTX_TILE, acc_ref)

    def key_norms(slot, heads):
        width = LANES // heads
        r = lax.broadcasted_iota(jnp.int32, (LANES, LANES), 0)
        c = lax.broadcasted_iota(jnp.int32, (LANES, LANES), 1)
        same_head = ((r // width) == (c // width)).astype(BF16)

        def step(t, mx):
            rows = pl.ds(pl.multiple_of(t * ROW_TILE, ROW_TILE), ROW_TILE)
            k = _slot_keys(k_ref, rows, slot).astype(F32)
            return jnp.maximum(
                mx, jnp.dot((k * k).astype(BF16), same_head, preferred_element_type=F32))
        mx = lax.fori_loop(0, N_TILES, step, jnp.zeros((ROW_TILE, LANES), F32), unroll=3)
        return [jnp.sqrt(jnp.max(mx[:, h * width:h * width + 1], axis=0, keepdims=True)
                         * NORM_SLACK) for h in range(heads)]

    def query_norm(slot):
        def step(t, mx):
            q = _slot_queries(qt_ref[0, t], slot).astype(F32)
            return jnp.maximum(mx, jnp.sum(q * q, axis=0, keepdims=True))
        mx = lax.fori_loop(0, N_TILES, step, jnp.zeros((1, ROW_TILE), F32), unroll=3)
        return jnp.sqrt(jnp.max(mx, axis=1, keepdims=True))

    def bounded_path(k_norms):
        def split(i):
            _split_queries(qt_ref[0, i], qs_ref)
            for slot in range(2):
                q = qs_ref[slot].astype(F32)
                c_ref[slot] = jnp.sqrt(jnp.sum(q * q, axis=0, keepdims=True)) * k_norms[slot] + 1.0

        def probabilities(t0, n, slot, p_ref):
            c = c_ref[slot]
            for j in range(0, n, SCORE_TILES):
                rows = min(SCORE_TILES, n - j) * ROW_TILE
                k = _slot_keys(k_ref, slice((t0 + j) * ROW_TILE, (t0 + j) * ROW_TILE + rows), slot)
                s = jnp.dot(k, qs_ref[slot], preferred_element_type=F32)
                p_ref[slot, j * ROW_TILE:j * ROW_TILE + rows, :] = _probabilities(s, c)

        def step(g, p_ref, next_probabilities):
            pv = ([], [])
            for j in range(KEY_GROUPS[g][1]):
                if j < len(next_probabilities):
                    next_probabilities[j]()
                values_tile(KEY_GROUPS[g][0] + j, j, p_ref, pv)
            for slot in range(2):
                total = functools.reduce(jnp.add, pv[slot])
                acc_ref[slot] = total if g == 0 else acc_ref[slot] + total

        fin_ref[...] = jnp.ones(fin_ref.shape, F32)
        split(0)
        for slot in range(2):
            probabilities(KEY_GROUPS[0][0], KEY_GROUPS[0][1], slot, p_bufs[0])

        def body(i, carry):
            for g in range(n_groups):
                g_next = (g + 1) % n_groups
                if g_next == 0:
                    split(i + 1)
                step(g, p_bufs[g % 2],
                     [functools.partial(probabilities, KEY_GROUPS[g_next][0], KEY_GROUPS[g_next][1],
                                        slot, p_bufs[g_next % 2]) for slot in range(2)])
                if g == 0:
                    finalize(jnp.maximum(i - 1, 0), fin_ref)
            fin_ref[...] = acc_ref[...]
            return carry
        lax.fori_loop(0, CTX_TILE, body, 0)
        finalize(CTX_TILE - 1, fin_ref)

        for slot in range(2):
            probabilities(CTX_TILE, 1, slot, p_bufs[0])
        pv = ([], [])
        values_tile(CTX_TILE, 0, p_bufs[0], pv)
        for slot in range(2):
            acc_ref[slot] = pv[slot][0]
        finalize(CTX_TILE, acc_ref)

    if k_ref.shape[-1] == LANES:
        k_norms = key_norms(0, 2)
    else:
        k_norms = [key_norms(slot, 1)[0] for slot in range(2)]
    widest = functools.reduce(jnp.maximum, [2.0 * (k_norms[slot] * query_norm(slot) + 1.0)
                                            for slot in range(2)])[0, 0]

    @pl.when(widest <= SAFE_EXPONENT)
    def _():
        bounded_path(k_norms)

    @pl.when(jnp.logical_not(widest <= SAFE_EXPONENT))
    def _():
        exact_path()


def _flash(qt, k, vt, lam, gsub, *, diff, lam_init):
    dq = k.shape[-1] // N_PAIRS
    vt_rows = vt.shape[2] // N_PAIRS
    acc_rows = _value_rows(diff)[0][0][1]
    group_rows = max(n for _, n in KEY_GROUPS) * ROW_TILE
    return pl.pallas_call(
        functools.partial(_flash_kernel, diff=diff, lam_init=lam_init),
        out_shape=jax.ShapeDtypeStruct((BATCH, T_ALL, D_MODEL), BF16),
        grid=(BATCH, N_PAIRS),
        in_specs=[pl.BlockSpec(lam.shape, lambda b, p: (0, 0)),
                  pl.BlockSpec(gsub.shape, lambda b, p: (0, 0)),
                  pl.BlockSpec((1, N_TILES, dq, ROW_TILE), lambda b, p: (b, 0, p, 0)),
                  pl.BlockSpec((1, T_ALL, dq), lambda b, p: (b, 0, p)),
                  pl.BlockSpec((1, N_TILES, vt_rows, ROW_TILE), lambda b, p: (b, 0, p, 0))],
        out_specs=pl.BlockSpec((1, T_ALL, LANES), lambda b, p: (b, 0, p)),
        scratch_shapes=[pltpu.VMEM((2, LANES, ROW_TILE), BF16),
                        pltpu.VMEM((2, 1, ROW_TILE), F32),
                        pltpu.VMEM((2, group_rows, ROW_TILE), F32),
                        pltpu.VMEM((2, group_rows, ROW_TILE), F32),
                        pltpu.VMEM((2, group_rows, ROW_TILE), BF16),
                        pltpu.VMEM((2, group_rows, ROW_TILE), BF16),
                        pltpu.VMEM((2, group_rows, ROW_TILE), BF16),
                        pltpu.VMEM((2, 1, ROW_TILE), F32),
                        pltpu.VMEM((2, 1, ROW_TILE), F32),
                        pltpu.VMEM((2, 1, ROW_TILE), F32),
                        pltpu.VMEM((2, 1, ROW_TILE), F32),
                        pltpu.VMEM((2, acc_rows, ROW_TILE), F32),
                        pltpu.VMEM((2, acc_rows, ROW_TILE), F32)],
        compiler_params=_params(2),
        name="flash_diff" if diff else "flash_pair",
    )(lam, gsub, qt, k, vt)


NA_Q_ROWS = ROW_TILE // GRID_W
NA_BAND_TILES = 3
NA_LAST_BAND = SEQ // ROW_TILE - NA_BAND_TILES


def _na_band_start(i):
    return jnp.clip(i - 1, 0, NA_LAST_BAND)


def _na_kernel(qt_ref, k0_ref, k1_ref, k2_ref, kc_ref, v0_ref, v1_ref, v2_ref, vc_ref,
               bias_ref, o_ref, qs_ref):
    k_refs = (k0_ref, k1_ref, k2_ref)
    v_refs = (v0_ref, v1_ref, v2_ref, vc_ref)
    v_rows, n_val = _value_rows(False)
    chains = [(b, slot) for b in range(BATCH) for slot in range(2)]
    for b in range(BATCH):
        _split_queries(qt_ref[b, 0], qs_ref.at[b])
    def chain_scores(b, slot):
        qs = qs_ref[b, slot]
        s = [jnp.dot(k_refs[j][b], qs, preferred_element_type=F32)
             + bias_ref[0, slot, j * ROW_TILE:(j + 1) * ROW_TILE, :]
             for j in range(NA_BAND_TILES)]
        s.append(jnp.dot(kc_ref[b], qs, preferred_element_type=F32))
        return s

    outs = {}
    s_next = chain_scores(*chains[0])
    for n, (b, slot) in enumerate(chains):
        s = s_next
        if n + 1 < len(chains):
            s_next = chain_scores(*chains[n + 1])
        lo, hi = v_rows[slot]
        m = jnp.max(functools.reduce(jnp.maximum, s), axis=0, keepdims=True)
        acc = functools.reduce(jnp.add, [
            jnp.dot(r[b, 0, lo:hi, :], _probabilities(sj, m), preferred_element_type=F32)
            for r, sj in zip(v_refs, s)])
        outs[b, slot] = acc[:n_val] * (1.0 / acc[n_val:n_val + 1])
    for b in range(BATCH):
        o_ref[b] = jnp.concatenate([outs[b, 0], outs[b, 1]], axis=0).T.astype(BF16)


def _na_variant(i):
    return jnp.where(i == 0, 0, jnp.where(i < NA_LAST_BAND + NA_BAND_TILES - 1, 1,
                                          jnp.where(i < CTX_TILE, 2, 3)))


def _na_attention(qt, k, vt, bias):
    vt_rows = vt.shape[2] // N_PAIRS

    def kspec(j):
        return pl.BlockSpec((BATCH, ROW_TILE, LANES), lambda p, i: (0, _na_band_start(i) + j, p))

    def vspec(j):
        return pl.BlockSpec((BATCH, 1, vt_rows, ROW_TILE),
                            lambda p, i: (0, _na_band_start(i) + j, p, 0))

    return pl.pallas_call(
        _na_kernel,
        out_shape=jax.ShapeDtypeStruct((BATCH, T_ALL, D_MODEL), BF16),
        grid=(N_PAIRS, N_TILES),
        in_specs=[pl.BlockSpec((BATCH, 1, LANES, ROW_TILE), lambda p, i: (0, i, p, 0)),
                  kspec(0), kspec(1), kspec(2),
                  pl.BlockSpec((BATCH, ROW_TILE, LANES), lambda p, i: (0, CTX_TILE, p)),
                  vspec(0), vspec(1), vspec(2),
                  pl.BlockSpec((BATCH, 1, vt_rows, ROW_TILE), lambda p, i: (0, CTX_TILE, p, 0)),
                  pl.BlockSpec((1, 2, NA_BAND_TILES * ROW_TILE, ROW_TILE),
                               lambda p, i: (_na_variant(i), p, 0, 0))],
        out_specs=pl.BlockSpec((BATCH, ROW_TILE, LANES), lambda p, i: (0, i, p)),
        scratch_shapes=[pltpu.VMEM((BATCH, 2, LANES, ROW_TILE), BF16)],
        compiler_params=_params(2),
        name="na_attention",
    )(qt, k, k, k, k, vt, vt, vt, vt, bias)


def _na_bias_table(rpb):
    rows = SEQ // GRID_W
    band_rows = NA_BAND_TILES * NA_Q_ROWS
    col = np.arange(GRID_W)
    cstart = np.clip(col - NA_COLS // 2, 0, GRID_W - NA_COLS)
    ok_c = (col[None, :] >= cstart[:, None]) & (col[None, :] < cstart[:, None] + NA_COLS)
    dc = col[None, :] - col[:, None] + NA_COLS - 1
    onehot_c = ((dc[:, :, None] == np.arange(2 * NA_COLS - 1)) & ok_c[:, :, None]).astype(np.float32)
    dr = np.zeros((4, NA_Q_ROWS, band_rows), np.int32)
    ok_r = np.zeros((4, NA_Q_ROWS, band_rows), bool)
    for variant, i in enumerate((0, 1, SEQ // ROW_TILE - 1)):
        j = min(max(i - 1, 0), NA_LAST_BAND)
        r = NA_Q_ROWS * i + np.arange(NA_Q_ROWS)[:, None]
        k_row = NA_Q_ROWS * j + np.arange(band_rows)[None, :]
        rstart = np.clip(r - NA_ROWS // 2, 0, rows - NA_ROWS)
        ok_r[variant] = (k_row >= rstart) & (k_row < rstart + NA_ROWS)
        dr[variant] = np.where(ok_r[variant], k_row - r + NA_ROWS - 1, 0)
    by_col = jnp.einsum("hab,ckb->hack", rpb.astype(F32) * LOG2E, jnp.asarray(onehot_c),
                        precision=lax.Precision.HIGHEST)
    by_col = jnp.where(ok_c[None, None], by_col, MASK_VALUE)
    full = jnp.take(by_col, jnp.asarray(dr.reshape(-1)), axis=1)
    full = full.reshape(C_HEADS, 4, NA_Q_ROWS, band_rows, GRID_W, GRID_W)
    full = jnp.where(ok_r[None, :, :, :, None, None], full, MASK_VALUE)
    full = full.transpose(1, 0, 3, 5, 2, 4)
    return full.reshape(4, C_HEADS, NA_BAND_TILES * ROW_TILE, ROW_TILE)


def _post_kernel(o_ref, x_ref, mod_ref, g_ref, wo_ref, wgu_ref, wd_ref, xo_ref):
    m = mod_ref[0]
    y = jnp.dot(o_ref[0], wo_ref[...], preferred_element_type=F32)
    x1 = x_ref[0] + m[2:3] * _rms(y, g_ref[1:2])
    h2 = (_rms(x1, g_ref[2:3]) * (1.0 + m[4:5]) + m[3:4]).astype(BF16)
    gu = jnp.dot(h2, wgu_ref[...], preferred_element_type=F32)
    gate, up = gu[:, :FFN_HIDDEN], gu[:, FFN_HIDDEN:]
    a = (gate * _sigmoid(gate) * up).astype(BF16)
    f = jnp.dot(a, wd_ref[...], preferred_element_type=F32)
    xo_ref[0] = x1 + m[5:6] * _rms(f, g_ref[3:4])


def _post(o, x, mod, g_norm, wo, wgu, wd, *, n_tiles):
    return pl.pallas_call(
        _post_kernel,
        out_shape=jax.ShapeDtypeStruct((BATCH, n_tiles * ROW_TILE, D_MODEL), F32),
        grid=(BATCH, n_tiles),
        in_specs=[pl.BlockSpec((1, ROW_TILE, D_MODEL), lambda b, t: (b, t, 0))] + _row_specs() + [
            _resident((4, D_MODEL)), _resident(wo.shape), _resident(wgu.shape),
            _resident(wd.shape)],
        out_specs=pl.BlockSpec((1, ROW_TILE, D_MODEL), lambda b, t: (b, t, 0)),
        compiler_params=_params(2),
        name="post_attention",
    )(o, x, mod, g_norm, wo, wgu, wd)


def _rope_angles(rot_dim):
    n_freq = rot_dim // 4
    inv = ROPE_THETA ** (-jnp.arange(n_freq, dtype=F32) / n_freq)
    t = jnp.arange(SEQ, dtype=jnp.int32)
    row = (t // GRID_W).astype(F32)
    col = (t % GRID_W).astype(F32)
    ang = jnp.concatenate([row[:, None] * inv, col[:, None] * inv], axis=-1)
    cos = jnp.concatenate([jnp.cos(ang), jnp.ones((CTX_LEN, rot_dim // 2), F32)], axis=0)
    sin = jnp.concatenate([jnp.sin(ang), jnp.zeros((CTX_LEN, rot_dim // 2), F32)], axis=0)
    return cos, sin


def _rope_tables_diff():
    cos, sin = _rope_angles(A_HEAD_DIM)
    c_lane = jnp.tile(cos, (1, 4))
    s_lane = jnp.tile(jnp.concatenate([-sin, sin], axis=1), (1, 2))
    return c_lane, s_lane, cos.T, sin.T


def _rope_tables_mla():
    cos, sin = _rope_angles(B_ROPE)
    ones = jnp.ones((T_ALL, B_NOPE), F32)
    zeros = jnp.zeros((T_ALL, B_NOPE), F32)
    tail = LANES - B_NOPE - B_ROPE
    c_lane = jnp.concatenate([ones, cos, cos, ones[:, :tail]], axis=1)
    s_lane = jnp.concatenate([zeros, -sin, sin, zeros[:, :tail]], axis=1)
    return c_lane, s_lane, cos.T, sin.T


def kernel(x, c, ctx, c_ctx, l0_w_mod, l0_b_mod, l0_g_norm, l0_w_gu, l0_w_down, l0_a_w_qkv, l0_a_w_o, l0_a_lam, l0_a_g_sub, l1_w_mod, l1_b_mod, l1_g_norm, l1_w_gu, l1_w_down, l1_b_w_in, l1_b_g_q, l1_b_g_kv, l1_b_w_uq, l1_b_w_ukv, l1_b_w_o, l2_w_mod, l2_b_mod, l2_g_norm, l2_w_gu, l2_w_down, l2_c_w_qkv, l2_c_rpb, l2_c_w_o, l3_w_mod, l3_b_mod, l3_g_norm, l3_w_gu, l3_w_down, l3_a_w_qkv, l3_a_w_o, l3_a_lam, l3_a_g_sub):
    common = [
        (l0_w_mod, l0_b_mod, l0_g_norm, l0_w_gu, l0_w_down),
        (l1_w_mod, l1_b_mod, l1_g_norm, l1_w_gu, l1_w_down),
        (l2_w_mod, l2_b_mod, l2_g_norm, l2_w_gu, l2_w_down),
        (l3_w_mod, l3_b_mod, l3_g_norm, l3_w_gu, l3_w_down),
    ]
    diff_params = {0: (l0_a_w_qkv, l0_a_w_o, l0_a_lam, l0_a_g_sub),
                   3: (l3_a_w_qkv, l3_a_w_o, l3_a_lam, l3_a_g_sub)}

    xs = jnp.concatenate([x, ctx], axis=1)
    cc = jnp.concatenate([c, c_ctx[None, :], jnp.zeros((8 - BATCH - 1, D_MODEL), F32)], axis=0)
    tabs_a = _rope_tables_diff()
    tabs_b = _rope_tables_mla()
    dummy_lam = jnp.zeros((4, A_HEAD_DIM), F32)
    dummy_gsub = jnp.ones((1, LANES), F32)

    for i in range(DEPTH):
        w_mod, b_mod, g_norm, w_gu, w_down = common[i]
        last = i == DEPTH - 1
        mod = _modulation(cc, w_mod, b_mod).reshape(8, 6, D_MODEL)
        kind = i % 3
        if kind == 0:
            w_qkv, w_o, lam, g_sub = diff_params[i]
            lam_init = 0.8 - 0.6 * math.exp(-0.3 * i)
            wqt = w_qkv[:, :D_MODEL].T.astype(BF16)
            wk = w_qkv[:, D_MODEL:2 * D_MODEL].astype(BF16)
            wvt = w_qkv[:, 2 * D_MODEL:].T.astype(BF16)
            qt, k, vt = _proj_qkv(xs, mod, g_norm, wqt, wk, wvt, tabs_a, rope=True,
                                  qscale=A_HEAD_DIM ** -0.5 * LOG2E, v_rows=2 * A_HEAD_DIM)
            o = _flash(qt, k, vt, lam, g_sub.reshape(1, LANES), diff=True, lam_init=lam_init)
        elif kind == 1:
            w_in, g_q, g_kv, w_uq, w_ukv, w_o = (l1_b_w_in, l1_b_g_q, l1_b_g_kv, l1_b_w_uq,
                                                 l1_b_w_ukv, l1_b_w_o)
            n_lora = B_Q_LORA + B_KV_LORA
            pad = LANES - B_NOPE - B_ROPE
            win = w_in[:, :n_lora].astype(BF16)
            wkr = jnp.pad(w_in[:, n_lora:], ((0, 0), (B_NOPE, pad))).astype(BF16)
            wuq = w_uq.reshape(B_Q_LORA, B_HEADS, B_NOPE + B_ROPE)
            wuq = jnp.pad(wuq, ((0, 0), (0, 0), (0, pad)))
            wuqt = wuq.reshape(B_Q_LORA, B_HEADS * LANES).T.astype(BF16)
            wukv = w_ukv.reshape(B_KV_LORA, B_HEADS, B_NOPE + B_VDIM)
            wukn = jnp.pad(wukv[:, :, :B_NOPE], ((0, 0), (0, 0), (0, LANES - B_NOPE)))
            wukn = wukn.reshape(B_KV_LORA, B_HEADS * LANES).astype(BF16)
            wuvt = wukv[:, :, B_NOPE:].reshape(B_KV_LORA, B_HEADS * B_VDIM).T.astype(BF16)
            qt, k, vt = _proj_mla(xs, mod, g_norm, win, wkr, g_q.reshape(1, -1),
                                  g_kv.reshape(1, -1), wuqt, wukn, wuvt, tabs_b,
                                  qscale=(B_NOPE + B_ROPE) ** -0.5 * LOG2E)
            o = _flash(qt, k, vt, dummy_lam, dummy_gsub, diff=False, lam_init=0.0)
        else:
            w_qkv, rpb, w_o = l2_c_w_qkv, l2_c_rpb, l2_c_w_o
            wqt = w_qkv[:, :D_MODEL].T.astype(BF16)
            wk = w_qkv[:, D_MODEL:2 * D_MODEL].astype(BF16)
            wvt = w_qkv[:, 2 * D_MODEL:].T.astype(BF16)
            qt, k, vt = _proj_qkv(xs, mod, g_norm, wqt, wk, wvt, tabs_a, rope=False,
                                  qscale=C_HEAD_DIM ** -0.5 * LOG2E, v_rows=C_HEAD_DIM)
            o = _na_attention(qt, k, vt, _na_bias_table(rpb))
        xs = _post(o, xs, mod, g_norm, w_o.astype(BF16), w_gu.astype(BF16),
                   w_down.astype(BF16), n_tiles=N_TILES - 1 if last else N_TILES)
    return xs
```

```python
import functools
import math

import numpy as np
import jax
import jax.numpy as jnp
from jax import lax
from jax.experimental import pallas as pl
from jax.experimental.pallas import tpu as pltpu

D_MODEL = 1024
BATCH = 4
SEQ = 8192
DEPTH = 4
GRID_W = 64
CTX_LEN = 256
ROPE_THETA = 10000.0
NORM_EPS = 1e-6

A_HEAD_DIM = 64
A_HEADS = 8
B_HEADS = 16
B_NOPE = 64
B_ROPE = 32
B_VDIM = 64
B_Q_LORA = 256
B_KV_LORA = 256
C_HEAD_DIM = 64
C_HEADS = 16
NA_ROWS = 8
NA_COLS = 16
FFN_HIDDEN = 2816

LANES = 128
STRIP = 64
ROW_TILE = 256
T_ALL = SEQ + CTX_LEN
N_TILES = T_ALL // ROW_TILE
CTX_TILE = N_TILES - 1
N_PAIRS = 8
KEY_GROUPS = tuple((3 * g, 3) for g in range(9)) + tuple((27 + 2 * g, 2) for g in range(3))
SCORE_TILES = 2
PROB_LEAD = 1
SCORE_LEAD = 2
V_ONES = 16
LOG2E = math.log2(math.e)
MASK_VALUE = -1e30
SAFE_EXPONENT = 96.0
NORM_SLACK = 1.02
VMEM_LIMIT = 60 * 1024 * 1024

F32 = jnp.float32
BF16 = jnp.bfloat16
NT_DIMS = (((1,), (1,)), ((), ()))


def _rms(x, g):
    return x * lax.rsqrt(jnp.mean(x * x, axis=-1, keepdims=True) + NORM_EPS) * g


def _sigmoid(x):
    return 1.0 / (1.0 + jnp.exp(-x))


def _params(n_axes):
    return pltpu.CompilerParams(
        dimension_semantics=("arbitrary",) * n_axes, vmem_limit_bytes=VMEM_LIMIT)


def _resident(shape):
    zeros = (0,) * len(shape)
    return pl.BlockSpec(shape, lambda *_: zeros, pipeline_mode=pl.Buffered(1))


def _mod_row(b, t):
    return jnp.where(t == CTX_TILE, BATCH, b)


def _mod_kernel(c_ref, w_ref, b_ref, o_ref):
    c = c_ref[...]
    a = (c * _sigmoid(c)).astype(BF16)
    o_ref[...] = jnp.dot(a, w_ref[...].astype(BF16), preferred_element_type=F32) + b_ref[...]


def _modulation(cc, w_mod, b_mod):
    n = w_mod.shape[1]
    blk = D_MODEL
    return pl.pallas_call(
        _mod_kernel,
        out_shape=jax.ShapeDtypeStruct((8, n), F32),
        grid=(n // blk,),
        in_specs=[pl.BlockSpec((8, D_MODEL), lambda j: (0, 0)),
                  pl.BlockSpec((D_MODEL, blk), lambda j: (0, j)),
                  pl.BlockSpec((1, blk), lambda j: (0, j))],
        out_specs=pl.BlockSpec((8, blk), lambda j: (0, j)),
        compiler_params=_params(1),
        name="modulation",
    )(cc, w_mod, b_mod.reshape(1, n))


def _rope_lanes(k, c, s, first, half):
    rot = jnp.where(first, pltpu.roll(k, LANES - half, 1), pltpu.roll(k, half, 1))
    return k * c + rot * s


def _store_values(vt_ref, vt, rows):
    ones = jnp.ones((V_ONES, ROW_TILE), BF16)
    for n in range(vt.shape[0] // rows):
        r0 = n * (rows + V_ONES)
        vt_ref[0, 0, r0:r0 + rows, :] = vt[n * rows:(n + 1) * rows].astype(BF16)
        vt_ref[0, 0, r0 + rows:r0 + rows + V_ONES, :] = ones


def _proj_qkv_kernel(x_ref, mod_ref, g_ref, wqt_ref, wk_ref, wvt_ref,
                     ck_ref, sk_ref, cqt_ref, sqt_ref,
                     qt_ref, k_ref, vt_ref, *, rope, qscale, v_rows):
    m = mod_ref[0]
    h = (_rms(x_ref[0], g_ref[0:1]) * (1.0 + m[1:2]) + m[0:1]).astype(BF16)
    qt = lax.dot_general(wqt_ref[...], h, NT_DIMS, preferred_element_type=F32) * qscale
    k = jnp.dot(h, wk_ref[...], preferred_element_type=F32)
    vt = lax.dot_general(wvt_ref[...], h, NT_DIMS, preferred_element_type=F32)
    _store_values(vt_ref, vt, v_rows)
    if not rope:
        qt_ref[0, 0] = qt.astype(BF16)
        k_ref[0] = k.astype(BF16)
        return
    half = A_HEAD_DIM // 2
    c, s = ck_ref[...], sk_ref[...]
    lane = lax.broadcasted_iota(jnp.int32, (ROW_TILE, LANES), 1)
    first = (lane % A_HEAD_DIM) < half
    for p in range(D_MODEL // LANES):
        sl = slice(p * LANES, (p + 1) * LANES)
        k_ref[0, :, sl] = _rope_lanes(k[:, sl], c, s, first, half).astype(BF16)
    ct, st = cqt_ref[...], sqt_ref[...]
    for g in range(D_MODEL // A_HEAD_DIM):
        r0 = g * A_HEAD_DIM
        t1, t2 = qt[r0:r0 + half], qt[r0 + half:r0 + 2 * half]
        qt_ref[0, 0, r0:r0 + half, :] = (t1 * ct - t2 * st).astype(BF16)
        qt_ref[0, 0, r0 + half:r0 + 2 * half, :] = (t1 * st + t2 * ct).astype(BF16)


def _row_specs():
    return [pl.BlockSpec((1, ROW_TILE, D_MODEL), lambda b, t: (b, t, 0)),
            pl.BlockSpec((1, 6, D_MODEL), lambda b, t: (_mod_row(b, t), 0, 0))]


def _rope_specs(half):
    return [pl.BlockSpec((ROW_TILE, LANES), lambda b, t: (t, 0)),
            pl.BlockSpec((ROW_TILE, LANES), lambda b, t: (t, 0)),
            pl.BlockSpec((half, ROW_TILE), lambda b, t: (0, t)),
            pl.BlockSpec((half, ROW_TILE), lambda b, t: (0, t))]


def _qkv_out(width, v_rows):
    vt_rows = D_MODEL // v_rows * (v_rows + V_ONES)
    shapes = (jax.ShapeDtypeStruct((BATCH, N_TILES, width, ROW_TILE), BF16),
              jax.ShapeDtypeStruct((BATCH, T_ALL, width), BF16),
              jax.ShapeDtypeStruct((BATCH, N_TILES, vt_rows, ROW_TILE), BF16))
    specs = (pl.BlockSpec((1, 1, width, ROW_TILE), lambda b, t: (b, t, 0, 0)),
             pl.BlockSpec((1, ROW_TILE, width), lambda b, t: (b, t, 0)),
             pl.BlockSpec((1, 1, vt_rows, ROW_TILE), lambda b, t: (b, t, 0, 0)))
    return shapes, specs


def _proj_qkv(x, mod, g_norm, wqt, wk, wvt, tabs, *, rope, qscale, v_rows):
    shapes, specs = _qkv_out(D_MODEL, v_rows)
    return pl.pallas_call(
        functools.partial(_proj_qkv_kernel, rope=rope, qscale=qscale, v_rows=v_rows),
        out_shape=shapes,
        grid=(BATCH, N_TILES),
        in_specs=_row_specs() + [
            _resident((4, D_MODEL)),
            _resident((D_MODEL, D_MODEL)), _resident((D_MODEL, D_MODEL)),
            _resident((D_MODEL, D_MODEL))] + _rope_specs(tabs[2].shape[0]),
        out_specs=specs,
        compiler_params=_params(2),
        name="proj_qkv",
    )(x, mod, g_norm, wqt, wk, wvt, *tabs)


def _proj_mla_kernel(x_ref, mod_ref, g_ref, win_ref, wkr_ref, gq_ref, gkv_ref,
                     wuqt_ref, wukn_ref, wuvt_ref, ck_ref, sk_ref, cqt_ref, sqt_ref,
                     qt_ref, k_ref, vt_ref, *, qscale):
    m = mod_ref[0]
    h = (_rms(x_ref[0], g_ref[0:1]) * (1.0 + m[1:2]) + m[0:1]).astype(BF16)
    z = jnp.dot(h, win_ref[...], preferred_element_type=F32)
    cq = _rms(z[:, :B_Q_LORA], gq_ref[...]).astype(BF16)
    ckv = _rms(z[:, B_Q_LORA:], gkv_ref[...]).astype(BF16)
    half = B_ROPE // 2

    qt = lax.dot_general(wuqt_ref[...], cq, NT_DIMS, preferred_element_type=F32) * qscale
    ct, st = cqt_ref[...], sqt_ref[...]
    for hh in range(B_HEADS):
        r0, r1 = hh * LANES, hh * LANES + B_NOPE
        qt_ref[0, 0, r0:r1, :] = qt[r0:r1].astype(BF16)
        t1, t2 = qt[r1:r1 + half], qt[r1 + half:r1 + 2 * half]
        qt_ref[0, 0, r1:r1 + half, :] = (t1 * ct - t2 * st).astype(BF16)
        qt_ref[0, 0, r1 + half:r1 + 2 * half, :] = (t1 * st + t2 * ct).astype(BF16)
        qt_ref[0, 0, r1 + B_ROPE:r0 + LANES, :] = qt[r1 + B_ROPE:r0 + LANES].astype(BF16)

    kr = jnp.dot(h, wkr_ref[...], preferred_element_type=F32)
    lane = lax.broadcasted_iota(jnp.int32, (ROW_TILE, LANES), 1)
    first = (lane >= B_NOPE) & (lane < B_NOPE + half)
    kr = _rope_lanes(kr, ck_ref[...], sk_ref[...], first, half)
    kn = jnp.dot(ckv, wukn_ref[...], preferred_element_type=F32)
    for hh in range(B_HEADS):
        sl = slice(hh * LANES, (hh + 1) * LANES)
        k_ref[0, :, sl] = (kn[:, sl] + kr).astype(BF16)

    vt = lax.dot_general(wuvt_ref[...], ckv, NT_DIMS, preferred_element_type=F32)
    _store_values(vt_ref, vt, B_VDIM)


def _proj_mla(x, mod, g_norm, win, wkr, gq, gkv, wuqt, wukn, wuvt, tabs, *, qscale):
    shapes, specs = _qkv_out(B_HEADS * LANES, B_VDIM)
    return pl.pallas_call(
        functools.partial(_proj_mla_kernel, qscale=qscale),
        out_shape=shapes,
        grid=(BATCH, N_TILES),
        in_specs=_row_specs() + [
            _resident((4, D_MODEL)),
            _resident(win.shape), _resident(wkr.shape),
            _resident((1, B_Q_LORA)), _resident((1, B_KV_LORA)),
            _resident(wuqt.shape), _resident(wukn.shape), _resident(wuvt.shape)]
            + _rope_specs(tabs[2].shape[0]),
        out_specs=specs,
        compiler_params=_params(2),
        name="proj_mla",
    )(x, mod, g_norm, win, wkr, gq, gkv, wuqt, wukn, wuvt, *tabs)


def _value_rows(diff):
    if diff:
        rows = 2 * A_HEAD_DIM
        return ((0, rows + V_ONES), (0, rows + V_ONES)), rows
    rows = LANES // 2
    return ((0, rows + V_ONES), (rows + V_ONES, 2 * (rows + V_ONES))), rows


def _split_queries(qt, qs_ref):
    dq = qt.shape[0]
    if dq == 2 * LANES:
        qs_ref[0] = qt[:LANES]
        qs_ref[1] = qt[LANES:]
        return
    row = lax.broadcasted_iota(jnp.int32, qt.shape, 0)
    zero = jnp.zeros_like(qt)
    qs_ref[0] = jnp.where(row < dq // 2, qt, zero)
    qs_ref[1] = jnp.where(row >= dq // 2, qt, zero)


def _slot_keys(k_ref, rows, slot):
    if k_ref.shape[-1] == 2 * LANES:
        return k_ref[0, rows, slot * LANES:(slot + 1) * LANES]
    return k_ref[0, rows, :]


def _slot_queries(qt, slot):
    half = qt.shape[0] // 2
    return qt[slot * half:(slot + 1) * half]


def _probabilities(s, m):
    return jnp.exp2(s - m).astype(BF16)


def _flash_kernel(lam_ref, gsub_ref, qt_ref, k_ref, vt_ref, o_ref,
                  qs_ref, c_ref, s0_ref, s1_ref, p0_ref, p1_ref, p2_ref, a0_ref, a1_ref, a2_ref,
                  m_ref, acc_ref, fin_ref, *, diff, lam_init):
    v_rows, n_val = _value_rows(diff)
    s_bufs, p_bufs, a_bufs = (s0_ref, s1_ref), (p0_ref, p1_ref, p2_ref), (a0_ref, a1_ref, a2_ref)
    n_groups = len(KEY_GROUPS)

    def tile_rows(j):
        return slice(j * ROW_TILE, (j + 1) * ROW_TILE)

    def scores(t0, n, slot, s_ref):
        for j in range(0, n, SCORE_TILES):
            rows = min(SCORE_TILES, n - j) * ROW_TILE
            k = _slot_keys(k_ref, slice((t0 + j) * ROW_TILE, (t0 + j) * ROW_TILE + rows), slot)
            s_ref[slot, j * ROW_TILE:j * ROW_TILE + rows, :] = jnp.dot(
                k, qs_ref[slot], preferred_element_type=F32)

    def stats(n, s_ref, a_ref, first):
        for slot in range(2):
            m_prev = jnp.full((1, ROW_TILE), MASK_VALUE, F32) if first else m_ref[slot]
            strips = [s_ref[slot, c * STRIP:(c + 1) * STRIP, :] for c in range(n * ROW_TILE // STRIP)]
            m_next = jnp.maximum(
                m_prev, jnp.max(functools.reduce(jnp.maximum, strips), axis=0, keepdims=True))
            a_ref[slot] = jnp.exp2(m_prev - m_next)
            m_ref[slot] = m_next

    def probs_tile(j, s_ref, p_ref):
        for slot in range(2):
            m = m_ref[slot]
            for c in range(ROW_TILE // STRIP):
                rows = slice(j * ROW_TILE + c * STRIP, j * ROW_TILE + (c + 1) * STRIP)
                p_ref[slot, rows, :] = _probabilities(s_ref[slot, rows, :], m)

    def values_tile(t, j, p_ref, pv):
        for slot in range(2):
            lo, hi = v_rows[slot]
            pv[slot].append(jnp.dot(vt_ref[0, t, lo:hi, :], p_ref[slot, tile_rows(j), :],
                                    preferred_element_type=F32))

    def accumulate(pv, a_ref):
        for slot in range(2):
            acc_ref[slot] = a_ref[slot] * acc_ref[slot] + functools.reduce(jnp.add, pv[slot])

    def finalize(i, src_ref):
        acc_a, acc_b = src_ref[0], src_ref[1]
        ot_a = acc_a[:n_val] * (1.0 / acc_a[n_val:n_val + 1])
        ot_b = acc_b[:n_val] * (1.0 / acc_b[n_val:n_val + 1])
        if diff:
            lv = lam_ref[...]
            lam = (jnp.exp(jnp.sum(lv[0:1] * lv[1:2], axis=1, keepdims=True))
                   - jnp.exp(jnp.sum(lv[2:3] * lv[3:4], axis=1, keepdims=True)) + lam_init)
            o = (ot_a - lam * ot_b).T
            o = _rms(o, gsub_ref[...]) * (1.0 - lam_init)
        else:
            o = jnp.concatenate([ot_a, ot_b], axis=0).T
        o_ref[0, pl.ds(pl.multiple_of(i * ROW_TILE, ROW_TILE), ROW_TILE), :] = o.astype(BF16)

    def exact_path():
        acc_ref[...] = jnp.zeros(acc_ref.shape, F32)
        fin_ref[...] = jnp.ones(fin_ref.shape, F32)

        def scores_group(g, s_ref):
            for slot in range(2):
                scores(KEY_GROUPS[g][0], KEY_GROUPS[g][1], slot, s_ref)

        def probs_group(g, s_ref, p_ref, a_ref, first):
            stats(KEY_GROUPS[g][1], s_ref, a_ref, first)
            for j in range(KEY_GROUPS[g][1]):
                probs_tile(j, s_ref, p_ref)

        _split_queries(qt_ref[0, 0], qs_ref)
        for fill in range(-SCORE_LEAD, 0):
            g_prob, g_sco = fill + PROB_LEAD, fill + SCORE_LEAD
            if g_prob >= 0:
                probs_group(g_prob, s_bufs[g_prob % 2], p_bufs[g_prob % 3], a_bufs[g_prob % 3],
                            g_prob == 0)
            scores_group(g_sco, s_bufs[g_sco % 2])

        def body(i, carry):
            for g in range(n_groups):
                g_prob, g_sco = (g + PROB_LEAD) % n_groups, (g + SCORE_LEAD) % n_groups
                t_val, n_val_tiles = KEY_GROUPS[g]
                t_sco, n_sco = KEY_GROUPS[g_sco]
                n_prob = KEY_GROUPS[g_prob][1]
                s_prob, p_prob, a_prob = s_bufs[g_prob % 2], p_bufs[g_prob % 3], a_bufs[g_prob % 3]
                stats(n_prob, s_prob, a_prob, g_prob == 0)
                if g + SCORE_LEAD == n_groups:
                    _split_queries(qt_ref[0, i + 1], qs_ref)
                pv = ([], [])
                for j in range(max(n_val_tiles, n_prob)):
                    if j < n_prob:
                        probs_tile(j, s_prob, p_prob)
                    if j < n_val_tiles:
                        values_tile(t_val + j, j, p_bufs[g % 3], pv)
                    if j < 2:
                        scores(t_sco, n_sco, j, s_bufs[g_sco % 2])
                accumulate(pv, a_bufs[g % 3])
                if g == 0:
                    finalize(jnp.maximum(i - 1, 0), fin_ref)
            fin_ref[...] = acc_ref[...]
            return carry
        lax.fori_loop(0, CTX_TILE, body, 0)
        finalize(CTX_TILE - 1, fin_ref)

        for slot in range(2):
            scores(CTX_TILE, 1, slot, s0_ref)
        stats(1, s0_ref, a0_ref, True)
        probs_tile(0, s0_ref, p0_ref)
        pv = ([], [])
        values_tile(CTX_TILE, 0, p0_ref, pv)
        accumulate(pv, a0_ref)
        finalize(CTX_TILE, acc_ref)

    def key_norms(slot, heads):
        width = LANES // heads
        r = lax.broadcasted_iota(jnp.int32, (LANES, LANES), 0)
        c = lax.broadcasted_iota(jnp.int32, (LANES, LANES), 1)
        same_head = ((r // width) == (c // width)).astype(BF16)

        def step(t, mx):
            rows = pl.ds(pl.multiple_of(t * ROW_TILE, ROW_TILE), ROW_TILE)
            k = _slot_keys(k_ref, rows, slot).astype(F32)
            return jnp.maximum(
                mx, jnp.dot((k * k).astype(BF16), same_head, preferred_element_type=F32))
        mx = lax.fori_loop(0, N_TILES, step, jnp.zeros((ROW_TILE, LANES), F32), unroll=3)
        return [jnp.sqrt(jnp.max(mx[:, h * width:h * width + 1], axis=0, keepdims=True)
                         * NORM_SLACK) for h in range(heads)]

    def query_norm(slot):
        def step(t, mx):
            q = _slot_queries(qt_ref[0, t], slot).astype(F32)
            return jnp.maximum(mx, jnp.sum(q * q, axis=0, keepdims=True))
        mx = lax.fori_loop(0, N_TILES, step, jnp.zeros((1, ROW_TILE), F32), unroll=3)
        return jnp.sqrt(jnp.max(mx, axis=1, keepdims=True))

    def bounded_path(k_norms):
        def split(i):
            _split_queries(qt_ref[0, i], qs_ref)
            for slot in range(2):
                q = qs_ref[slot].astype(F32)
                c_ref[slot] = jnp.sqrt(jnp.sum(q * q, axis=0, keepdims=True)) * k_norms[slot] + 1.0

        def probabilities(t0, n, slot, p_ref):
            c = c_ref[slot]
            for j in range(0, n, SCORE_TILES):
                rows = min(SCORE_TILES, n - j) * ROW_TILE
                k = _slot_keys(k_ref, slice((t0 + j) * ROW_TILE, (t0 + j) * ROW_TILE + rows), slot)
                s = jnp.dot(k, qs_ref[slot], preferred_element_type=F32)
                p_ref[slot, j * ROW_TILE:j * ROW_TILE + rows, :] = _probabilities(s, c)

        def step(g, p_ref, next_probabilities):
            pv = ([], [])
            scores_first = k_ref.shape[-1] == LANES
            for j in range(KEY_GROUPS[g][1]):
                if scores_first and j < len(next_probabilities):
                    next_probabilities[j]()
                values_tile(KEY_GROUPS[g][0] + j, j, p_ref, pv)
                if not scores_first and j < len(next_probabilities):
                    next_probabilities[j]()
            for slot in range(2):
                total = functools.reduce(jnp.add, pv[slot])
                acc_ref[slot] = total if g == 0 else acc_ref[slot] + total

        fin_ref[...] = jnp.ones(fin_ref.shape, F32)
        split(0)
        for slot in range(2):
            probabilities(KEY_GROUPS[0][0], KEY_GROUPS[0][1], slot, p_bufs[0])

        def body(i, carry):
            for g in range(n_groups):
                g_next = (g + 1) % n_groups
                if g_next == 0:
                    split(i + 1)
                step(g, p_bufs[g % 2],
                     [functools.partial(probabilities, KEY_GROUPS[g_next][0], KEY_GROUPS[g_next][1],
                                        slot, p_bufs[g_next % 2]) for slot in range(2)])
                if g == 0:
                    finalize(jnp.maximum(i - 1, 0), fin_ref)
            fin_ref[...] = acc_ref[...]
            return carry
        lax.fori_loop(0, CTX_TILE, body, 0)
        finalize(CTX_TILE - 1, fin_ref)

        for slot in range(2):
            probabilities(CTX_TILE, 1, slot, p_bufs[0])
        pv = ([], [])
        values_tile(CTX_TILE, 0, p_bufs[0], pv)
        for slot in range(2):
            acc_ref[slot] = pv[slot][0]
        finalize(CTX_TILE, acc_ref)

    if k_ref.shape[-1] == LANES:
        k_norms = key_norms(0, 2)
    else:
        k_norms = [key_norms(slot, 1)[0] for slot in range(2)]
    widest = functools.reduce(jnp.maximum, [2.0 * (k_norms[slot] * query_norm(slot) + 1.0)
                                            for slot in range(2)])[0, 0]

    @pl.when(widest <= SAFE_EXPONENT)
    def _():
        bounded_path(k_norms)

    @pl.when(jnp.logical_not(widest <= SAFE_EXPONENT))
    def _():
        exact_path()


def _flash(qt, k, vt, lam, gsub, *, diff, lam_init):
    dq = k.shape[-1] // N_PAIRS
    vt_rows = vt.shape[2] // N_PAIRS
    acc_rows = _value_rows(diff)[0][0][1]
    group_rows = max(n for _, n in KEY_GROUPS) * ROW_TILE
    return pl.pallas_call(
        functools.partial(_flash_kernel, diff=diff, lam_init=lam_init),
        out_shape=jax.ShapeDtypeStruct((BATCH, T_ALL, D_MODEL), BF16),
        grid=(BATCH, N_PAIRS),
        in_specs=[pl.BlockSpec(lam.shape, lambda b, p: (0, 0)),
                  pl.BlockSpec(gsub.shape, lambda b, p: (0, 0)),
                  pl.BlockSpec((1, N_TILES, dq, ROW_TILE), lambda b, p: (b, 0, p, 0)),
                  pl.BlockSpec((1, T_ALL, dq), lambda b, p: (b, 0, p)),
                  pl.BlockSpec((1, N_TILES, vt_rows, ROW_TILE), lambda b, p: (b, 0, p, 0))],
        out_specs=pl.BlockSpec((1, T_ALL, LANES), lambda b, p: (b, 0, p)),
        scratch_shapes=[pltpu.VMEM((2, LANES, ROW_TILE), BF16),
                        pltpu.VMEM((2, 1, ROW_TILE), F32),
                        pltpu.VMEM((2, group_rows, ROW_TILE), F32),
                        pltpu.VMEM((2, group_rows, ROW_TILE), F32),
                        pltpu.VMEM((2, group_rows, ROW_TILE), BF16),
                        pltpu.VMEM((2, group_rows, ROW_TILE), BF16),
                        pltpu.VMEM((2, group_rows, ROW_TILE), BF16),
                        pltpu.VMEM((2, 1, ROW_TILE), F32),
                        pltpu.VMEM((2, 1, ROW_TILE), F32),
                        pltpu.VMEM((2, 1, ROW_TILE), F32),
                        pltpu.VMEM((2, 1, ROW_TILE), F32),
                        pltpu.VMEM((2, acc_rows, ROW_TILE), F32),
                        pltpu.VMEM((2, acc_rows, ROW_TILE), F32)],
        compiler_params=_params(2),
        name="flash_diff" if diff else "flash_pair",
    )(lam, gsub, qt, k, vt)


NA_Q_ROWS = ROW_TILE // GRID_W
NA_BAND_TILES = 3
NA_LAST_BAND = SEQ // ROW_TILE - NA_BAND_TILES


def _na_band_start(i):
    return jnp.clip(i - 1, 0, NA_LAST_BAND)


def _na_kernel(qt_ref, k0_ref, k1_ref, k2_ref, kc_ref, v0_ref, v1_ref, v2_ref, vc_ref,
               bias_ref, o_ref, qs_ref):
    k_refs = (k0_ref, k1_ref, k2_ref)
    v_refs = (v0_ref, v1_ref, v2_ref, vc_ref)
    v_rows, n_val = _value_rows(False)
    chains = [(b, slot) for b in range(BATCH) for slot in range(2)]
    for b in range(BATCH):
        _split_queries(qt_ref[b, 0], qs_ref.at[b])
    def chain_scores(b, slot):
        qs = qs_ref[b, slot]
        s = [jnp.dot(k_refs[j][b], qs, preferred_element_type=F32)
             + bias_ref[0, slot, j * ROW_TILE:(j + 1) * ROW_TILE, :]
             for j in range(NA_BAND_TILES)]
        s.append(jnp.dot(kc_ref[b], qs, preferred_element_type=F32))
        return s

    outs = {}
    s_next = chain_scores(*chains[0])
    for n, (b, slot) in enumerate(chains):
        s = s_next
        if n + 1 < len(chains):
            s_next = chain_scores(*chains[n + 1])
        lo, hi = v_rows[slot]
        m = jnp.max(functools.reduce(jnp.maximum, s), axis=0, keepdims=True)
        acc = functools.reduce(jnp.add, [
            jnp.dot(r[b, 0, lo:hi, :], _probabilities(sj, m), preferred_element_type=F32)
            for r, sj in zip(v_refs, s)])
        outs[b, slot] = acc[:n_val] * (1.0 / acc[n_val:n_val + 1])
    for b in range(BATCH):
        o_ref[b] = jnp.concatenate([outs[b, 0], outs[b, 1]], axis=0).T.astype(BF16)


def _na_variant(i):
    return jnp.where(i == 0, 0, jnp.where(i < NA_LAST_BAND + NA_BAND_TILES - 1, 1,
                                          jnp.where(i < CTX_TILE, 2, 3)))


def _na_attention(qt, k, vt, bias):
    vt_rows = vt.shape[2] // N_PAIRS

    def kspec(j):
        return pl.BlockSpec((BATCH, ROW_TILE, LANES), lambda p, i: (0, _na_band_start(i) + j, p))

    def vspec(j):
        return pl.BlockSpec((BATCH, 1, vt_rows, ROW_TILE),
                            lambda p, i: (0, _na_band_start(i) + j, p, 0))

    return pl.pallas_call(
        _na_kernel,
        out_shape=jax.ShapeDtypeStruct((BATCH, T_ALL, D_MODEL), BF16),
        grid=(N_PAIRS, N_TILES),
        in_specs=[pl.BlockSpec((BATCH, 1, LANES, ROW_TILE), lambda p, i: (0, i, p, 0)),
                  kspec(0), kspec(1), kspec(2),
                  pl.BlockSpec((BATCH, ROW_TILE, LANES), lambda p, i: (0, CTX_TILE, p)),
                  vspec(0), vspec(1), vspec(2),
                  pl.BlockSpec((BATCH, 1, vt_rows, ROW_TILE), lambda p, i: (0, CTX_TILE, p, 0)),
                  pl.BlockSpec((1, 2, NA_BAND_TILES * ROW_TILE, ROW_TILE),
                               lambda p, i: (_na_variant(i), p, 0, 0))],
        out_specs=pl.BlockSpec((BATCH, ROW_TILE, LANES), lambda p, i: (0, i, p)),
        scratch_shapes=[pltpu.VMEM((BATCH, 2, LANES, ROW_TILE), BF16)],
        compiler_params=_params(2),
        name="na_attention",
    )(qt, k, k, k, k, vt, vt, vt, vt, bias)


def _na_bias_table(rpb):
    rows = SEQ // GRID_W
    band_rows = NA_BAND_TILES * NA_Q_ROWS
    col = np.arange(GRID_W)
    cstart = np.clip(col - NA_COLS // 2, 0, GRID_W - NA_COLS)
    ok_c = (col[None, :] >= cstart[:, None]) & (col[None, :] < cstart[:, None] + NA_COLS)
    dc = col[None, :] - col[:, None] + NA_COLS - 1
    onehot_c = ((dc[:, :, None] == np.arange(2 * NA_COLS - 1)) & ok_c[:, :, None]).astype(np.float32)
    dr = np.zeros((4, NA_Q_ROWS, band_rows), np.int32)
    ok_r = np.zeros((4, NA_Q_ROWS, band_rows), bool)
    for variant, i in enumerate((0, 1, SEQ // ROW_TILE - 1)):
        j = min(max(i - 1, 0), NA_LAST_BAND)
        r = NA_Q_ROWS * i + np.arange(NA_Q_ROWS)[:, None]
        k_row = NA_Q_ROWS * j + np.arange(band_rows)[None, :]
        rstart = np.clip(r - NA_ROWS // 2, 0, rows - NA_ROWS)
        ok_r[variant] = (k_row >= rstart) & (k_row < rstart + NA_ROWS)
        dr[variant] = np.where(ok_r[variant], k_row - r + NA_ROWS - 1, 0)
    by_col = jnp.einsum("hab,ckb->hack", rpb.astype(F32) * LOG2E, jnp.asarray(onehot_c),
                        precision=lax.Precision.HIGHEST)
    by_col = jnp.where(ok_c[None, None], by_col, MASK_VALUE)
    full = jnp.take(by_col, jnp.asarray(dr.reshape(-1)), axis=1)
    full = full.reshape(C_HEADS, 4, NA_Q_ROWS, band_rows, GRID_W, GRID_W)
    full = jnp.where(ok_r[None, :, :, :, None, None], full, MASK_VALUE)
    full = full.transpose(1, 0, 3, 5, 2, 4)
    return full.reshape(4, C_HEADS, NA_BAND_TILES * ROW_TILE, ROW_TILE)


def _post_kernel(o_ref, x_ref, mod_ref, g_ref, wo_ref, wgu_ref, wd_ref, xo_ref):
    m = mod_ref[0]
    y = jnp.dot(o_ref[0], wo_ref[...], preferred_element_type=F32)
    x1 = x_ref[0] + m[2:3] * _rms(y, g_ref[1:2])
    h2 = (_rms(x1, g_ref[2:3]) * (1.0 + m[4:5]) + m[3:4]).astype(BF16)
    gu = jnp.dot(h2, wgu_ref[...], preferred_element_type=F32)
    gate, up = gu[:, :FFN_HIDDEN], gu[:, FFN_HIDDEN:]
    a = (gate * _sigmoid(gate) * up).astype(BF16)
    f = jnp.dot(a, wd_ref[...], preferred_element_type=F32)
    xo_ref[0] = x1 + m[5:6] * _rms(f, g_ref[3:4])


def _post(o, x, mod, g_norm, wo, wgu, wd, *, n_tiles):
    return pl.pallas_call(
        _post_kernel,
        out_shape=jax.ShapeDtypeStruct((BATCH, n_tiles * ROW_TILE, D_MODEL), F32),
        grid=(BATCH, n_tiles),
        in_specs=[pl.BlockSpec((1, ROW_TILE, D_MODEL), lambda b, t: (b, t, 0))] + _row_specs() + [
            _resident((4, D_MODEL)), _resident(wo.shape), _resident(wgu.shape),
            _resident(wd.shape)],
        out_specs=pl.BlockSpec((1, ROW_TILE, D_MODEL), lambda b, t: (b, t, 0)),
        compiler_params=_params(2),
        name="post_attention",
    )(o, x, mod, g_norm, wo, wgu, wd)


def _rope_angles(rot_dim):
    n_freq = rot_dim // 4
    inv = ROPE_THETA ** (-jnp.arange(n_freq, dtype=F32) / n_freq)
    t = jnp.arange(SEQ, dtype=jnp.int32)
    row = (t // GRID_W).astype(F32)
    col = (t % GRID_W).astype(F32)
    ang = jnp.concatenate([row[:, None] * inv, col[:, None] * inv], axis=-1)
    cos = jnp.concatenate([jnp.cos(ang), jnp.ones((CTX_LEN, rot_dim // 2), F32)], axis=0)
    sin = jnp.concatenate([jnp.sin(ang), jnp.zeros((CTX_LEN, rot_dim // 2), F32)], axis=0)
    return cos, sin


def _rope_tables_diff():
    cos, sin = _rope_angles(A_HEAD_DIM)
    c_lane = jnp.tile(cos, (1, 4))
    s_lane = jnp.tile(jnp.concatenate([-sin, sin], axis=1), (1, 2))
    return c_lane, s_lane, cos.T, sin.T


def _rope_tables_mla():
    cos, sin = _rope_angles(B_ROPE)
    ones = jnp.ones((T_ALL, B_NOPE), F32)
    zeros = jnp.zeros((T_ALL, B_NOPE), F32)
    tail = LANES - B_NOPE - B_ROPE
    c_lane = jnp.concatenate([ones, cos, cos, ones[:, :tail]], axis=1)
    s_lane = jnp.concatenate([zeros, -sin, sin, zeros[:, :tail]], axis=1)
    return c_lane, s_lane, cos.T, sin.T


def kernel(x, c, ctx, c_ctx, l0_w_mod, l0_b_mod, l0_g_norm, l0_w_gu, l0_w_down, l0_a_w_qkv, l0_a_w_o, l0_a_lam, l0_a_g_sub, l1_w_mod, l1_b_mod, l1_g_norm, l1_w_gu, l1_w_down, l1_b_w_in, l1_b_g_q, l1_b_g_kv, l1_b_w_uq, l1_b_w_ukv, l1_b_w_o, l2_w_mod, l2_b_mod, l2_g_norm, l2_w_gu, l2_w_down, l2_c_w_qkv, l2_c_rpb, l2_c_w_o, l3_w_mod, l3_b_mod, l3_g_norm, l3_w_gu, l3_w_down, l3_a_w_qkv, l3_a_w_o, l3_a_lam, l3_a_g_sub):
    common = [
        (l0_w_mod, l0_b_mod, l0_g_norm, l0_w_gu, l0_w_down),
        (l1_w_mod, l1_b_mod, l1_g_norm, l1_w_gu, l1_w_down),
        (l2_w_mod, l2_b_mod, l2_g_norm, l2_w_gu, l2_w_down),
        (l3_w_mod, l3_b_mod, l3_g_norm, l3_w_gu, l3_w_down),
    ]
    diff_params = {0: (l0_a_w_qkv, l0_a_w_o, l0_a_lam, l0_a_g_sub),
                   3: (l3_a_w_qkv, l3_a_w_o, l3_a_lam, l3_a_g_sub)}

    xs = jnp.concatenate([x, ctx], axis=1)
    cc = jnp.concatenate([c, c_ctx[None, :], jnp.zeros((8 - BATCH - 1, D_MODEL), F32)], axis=0)
    tabs_a = _rope_tables_diff()
    tabs_b = _rope_tables_mla()
    dummy_lam = jnp.zeros((4, A_HEAD_DIM), F32)
    dummy_gsub = jnp.ones((1, LANES), F32)

    for i in range(DEPTH):
        w_mod, b_mod, g_norm, w_gu, w_down = common[i]
        last = i == DEPTH - 1
        mod = _modulation(cc, w_mod, b_mod).reshape(8, 6, D_MODEL)
        kind = i % 3
        if kind == 0:
            w_qkv, w_o, lam, g_sub = diff_params[i]
            lam_init = 0.8 - 0.6 * math.exp(-0.3 * i)
            wqt = w_qkv[:, :D_MODEL].T.astype(BF16)
            wk = w_qkv[:, D_MODEL:2 * D_MODEL].astype(BF16)
            wvt = w_qkv[:, 2 * D_MODEL:].T.astype(BF16)
            qt, k, vt = _proj_qkv(xs, mod, g_norm, wqt, wk, wvt, tabs_a, rope=True,
                                  qscale=A_HEAD_DIM ** -0.5 * LOG2E, v_rows=2 * A_HEAD_DIM)
            o = _flash(qt, k, vt, lam, g_sub.reshape(1, LANES), diff=True, lam_init=lam_init)
        elif kind == 1:
            w_in, g_q, g_kv, w_uq, w_ukv, w_o = (l1_b_w_in, l1_b_g_q, l1_b_g_kv, l1_b_w_uq,
                                                 l1_b_w_ukv, l1_b_w_o)
            n_lora = B_Q_LORA + B_KV_LORA
            pad = LANES - B_NOPE - B_ROPE
            win = w_in[:, :n_lora].astype(BF16)
            wkr = jnp.pad(w_in[:, n_lora:], ((0, 0), (B_NOPE, pad))).astype(BF16)
            wuq = w_uq.reshape(B_Q_LORA, B_HEADS, B_NOPE + B_ROPE)
            wuq = jnp.pad(wuq, ((0, 0), (0, 0), (0, pad)))
            wuqt = wuq.reshape(B_Q_LORA, B_HEADS * LANES).T.astype(BF16)
            wukv = w_ukv.reshape(B_KV_LORA, B_HEADS, B_NOPE + B_VDIM)
            wukn = jnp.pad(wukv[:, :, :B_NOPE], ((0, 0), (0, 0), (0, LANES - B_NOPE)))
            wukn = wukn.reshape(B_KV_LORA, B_HEADS * LANES).astype(BF16)
            wuvt = wukv[:, :, B_NOPE:].reshape(B_KV_LORA, B_HEADS * B_VDIM).T.astype(BF16)
            qt, k, vt = _proj_mla(xs, mod, g_norm, win, wkr, g_q.reshape(1, -1),
                                  g_kv.reshape(1, -1), wuqt, wukn, wuvt, tabs_b,
                                  qscale=(B_NOPE + B_ROPE) ** -0.5 * LOG2E)
            o = _flash(qt, k, vt, dummy_lam, dummy_gsub, diff=False, lam_init=0.0)
        else:
            w_qkv, rpb, w_o = l2_c_w_qkv, l2_c_rpb, l2_c_w_o
            wqt = w_qkv[:, :D_MODEL].T.astype(BF16)
            wk = w_qkv[:, D_MODEL:2 * D_MODEL].astype(BF16)
            wvt = w_qkv[:, 2 * D_MODEL:].T.astype(BF16)
            qt, k, vt = _proj_qkv(xs, mod, g_norm, wqt, wk, wvt, tabs_a, rope=False,
                                  qscale=C_HEAD_DIM ** -0.5 * LOG2E, v_rows=C_HEAD_DIM)
            o = _na_attention(qt, k, vt, _na_bias_table(rpb))
        xs = _post(o, xs, mod, g_norm, w_o.astype(BF16), w_gu.astype(BF16),
                   w_down.astype(BF16), n_tiles=N_TILES - 1 if last else N_TILES)
    return xs
```

```python
import functools
import math

import numpy as np
import jax
import jax.numpy as jnp
from jax import lax
from jax.experimental import pallas as pl
from jax.experimental.pallas import tpu as pltpu

D_MODEL = 1024
BATCH = 4
SEQ = 8192
DEPTH = 4
GRID_W = 64
CTX_LEN = 256
ROPE_THETA = 10000.0
NORM_EPS = 1e-6

A_HEAD_DIM = 64
A_HEADS = 8
B_HEADS = 16
B_NOPE = 64
B_ROPE = 32
B_VDIM = 64
B_Q_LORA = 256
B_KV_LORA = 256
C_HEAD_DIM = 64
C_HEADS = 16
NA_ROWS = 8
NA_COLS = 16
FFN_HIDDEN = 2816

LANES = 128
STRIP = 64
ROW_TILE = 256
T_ALL = SEQ + CTX_LEN
N_TILES = T_ALL // ROW_TILE
CTX_TILE = N_TILES - 1
N_PAIRS = 8
KEY_GROUPS = tuple((3 * g, 3) for g in range(9)) + tuple((27 + 2 * g, 2) for g in range(3))
SCORE_TILES = 2
PROB_LEAD = 1
SCORE_LEAD = 2
V_ONES = 16
LOG2E = math.log2(math.e)
MASK_VALUE = -1e30
SAFE_EXPONENT = 96.0
NORM_SLACK = 1.02
VMEM_LIMIT = 60 * 1024 * 1024

F32 = jnp.float32
BF16 = jnp.bfloat16
NT_DIMS = (((1,), (1,)), ((), ()))


def _rms(x, g):
    return x * lax.rsqrt(jnp.mean(x * x, axis=-1, keepdims=True) + NORM_EPS) * g


def _sigmoid(x):
    return 1.0 / (1.0 + jnp.exp(-x))


def _params(n_axes):
    return pltpu.CompilerParams(
        dimension_semantics=("arbitrary",) * n_axes, vmem_limit_bytes=VMEM_LIMIT)


def _resident(shape):
    zeros = (0,) * len(shape)
    return pl.BlockSpec(shape, lambda *_: zeros, pipeline_mode=pl.Buffered(1))


def _mod_row(b, t):
    return jnp.where(t == CTX_TILE, BATCH, b)


def _mod_kernel(c_ref, w_ref, b_ref, o_ref):
    c = c_ref[...]
    a = (c * _sigmoid(c)).astype(BF16)
    o_ref[...] = jnp.dot(a, w_ref[...].astype(BF16), preferred_element_type=F32) + b_ref[...]


def _modulation(cc, w_mod, b_mod):
    n = w_mod.shape[1]
    blk = D_MODEL
    return pl.pallas_call(
        _mod_kernel,
        out_shape=jax.ShapeDtypeStruct((8, n), F32),
        grid=(n // blk,),
        in_specs=[pl.BlockSpec((8, D_MODEL), lambda j: (0, 0)),
                  pl.BlockSpec((D_MODEL, blk), lambda j: (0, j)),
                  pl.BlockSpec((1, blk), lambda j: (0, j))],
        out_specs=pl.BlockSpec((8, blk), lambda j: (0, j)),
        compiler_params=_params(1),
        name="modulation",
    )(cc, w_mod, b_mod.reshape(1, n))


def _rope_lanes(k, c, s, first, half):
    rot = jnp.where(first, pltpu.roll(k, LANES - half, 1), pltpu.roll(k, half, 1))
    return k * c + rot * s


def _store_values(vt_ref, vt, rows):
    ones = jnp.ones((V_ONES, ROW_TILE), BF16)
    for n in range(vt.shape[0] // rows):
        r0 = n * (rows + V_ONES)
        vt_ref[0, 0, r0:r0 + rows, :] = vt[n * rows:(n + 1) * rows].astype(BF16)
        vt_ref[0, 0, r0 + rows:r0 + rows + V_ONES, :] = ones


def _proj_qkv_kernel(x_ref, xc_ref, mod_ref, g_ref, wqt_ref, wk_ref, wvt_ref,
                     ck_ref, sk_ref, cqt_ref, sqt_ref,
                     qt_ref, k_ref, vt_ref, *, rope, qscale, v_rows):
    m = mod_ref[0]
    h = (_rms(_stream_tile(x_ref, xc_ref), g_ref[0:1]) * (1.0 + m[1:2]) + m[0:1]).astype(BF16)
    qt = lax.dot_general(wqt_ref[...], h, NT_DIMS, preferred_element_type=F32) * qscale
    k = jnp.dot(h, wk_ref[...], preferred_element_type=F32)
    vt = lax.dot_general(wvt_ref[...], h, NT_DIMS, preferred_element_type=F32)
    _store_values(vt_ref, vt, v_rows)
    if not rope:
        qt_ref[0, 0] = qt.astype(BF16)
        k_ref[0] = k.astype(BF16)
        return
    half = A_HEAD_DIM // 2
    c, s = ck_ref[...], sk_ref[...]
    lane = lax.broadcasted_iota(jnp.int32, (ROW_TILE, LANES), 1)
    first = (lane % A_HEAD_DIM) < half
    for p in range(D_MODEL // LANES):
        sl = slice(p * LANES, (p + 1) * LANES)
        k_ref[0, :, sl] = _rope_lanes(k[:, sl], c, s, first, half).astype(BF16)
    ct, st = cqt_ref[...], sqt_ref[...]
    for g in range(D_MODEL // A_HEAD_DIM):
        r0 = g * A_HEAD_DIM
        t1, t2 = qt[r0:r0 + half], qt[r0 + half:r0 + 2 * half]
        qt_ref[0, 0, r0:r0 + half, :] = (t1 * ct - t2 * st).astype(BF16)
        qt_ref[0, 0, r0 + half:r0 + 2 * half, :] = (t1 * st + t2 * ct).astype(BF16)


def _row_specs(stream):
    last_lat, ctx_tile = (a.shape[1] // ROW_TILE - 1 for a in stream)
    return [pl.BlockSpec((1, ROW_TILE, D_MODEL), lambda b, t: (b, jnp.minimum(t, last_lat), 0)),
            pl.BlockSpec((1, ROW_TILE, D_MODEL), lambda b, t: (b, ctx_tile, 0)),
            pl.BlockSpec((1, 6, D_MODEL), lambda b, t: (_mod_row(b, t), 0, 0))]


def _stream_tile(x_ref, xc_ref):
    return jnp.where(pl.program_id(1) == CTX_TILE, xc_ref[0], x_ref[0])


def _rope_specs(half):
    return [pl.BlockSpec((ROW_TILE, LANES), lambda b, t: (t, 0)),
            pl.BlockSpec((ROW_TILE, LANES), lambda b, t: (t, 0)),
            pl.BlockSpec((half, ROW_TILE), lambda b, t: (0, t)),
            pl.BlockSpec((half, ROW_TILE), lambda b, t: (0, t))]


def _qkv_out(width, v_rows):
    vt_rows = D_MODEL // v_rows * (v_rows + V_ONES)
    shapes = (jax.ShapeDtypeStruct((BATCH, N_TILES, width, ROW_TILE), BF16),
              jax.ShapeDtypeStruct((BATCH, T_ALL, width), BF16),
              jax.ShapeDtypeStruct((BATCH, N_TILES, vt_rows, ROW_TILE), BF16))
    specs = (pl.BlockSpec((1, 1, width, ROW_TILE), lambda b, t: (b, t, 0, 0)),
             pl.BlockSpec((1, ROW_TILE, width), lambda b, t: (b, t, 0)),
             pl.BlockSpec((1, 1, vt_rows, ROW_TILE), lambda b, t: (b, t, 0, 0)))
    return shapes, specs


def _proj_qkv(stream, mod, g_norm, wqt, wk, wvt, tabs, *, rope, qscale, v_rows):
    shapes, specs = _qkv_out(D_MODEL, v_rows)
    return pl.pallas_call(
        functools.partial(_proj_qkv_kernel, rope=rope, qscale=qscale, v_rows=v_rows),
        out_shape=shapes,
        grid=(BATCH, N_TILES),
        in_specs=_row_specs(stream) + [
            _resident((4, D_MODEL)),
            _resident((D_MODEL, D_MODEL)), _resident((D_MODEL, D_MODEL)),
            _resident((D_MODEL, D_MODEL))] + _rope_specs(tabs[2].shape[0]),
        out_specs=specs,
        compiler_params=_params(2),
        name="proj_qkv",
    )(*stream, mod, g_norm, wqt, wk, wvt, *tabs)


def _proj_mla_kernel(x_ref, xc_ref, mod_ref, g_ref, win_ref, wkr_ref, gq_ref, gkv_ref,
                     wuqt_ref, wukn_ref, wuvt_ref, ck_ref, sk_ref, cqt_ref, sqt_ref,
                     qt_ref, k_ref, vt_ref, *, qscale):
    m = mod_ref[0]
    h = (_rms(_stream_tile(x_ref, xc_ref), g_ref[0:1]) * (1.0 + m[1:2]) + m[0:1]).astype(BF16)
    z = jnp.dot(h, win_ref[...], preferred_element_type=F32)
    cq = _rms(z[:, :B_Q_LORA], gq_ref[...]).astype(BF16)
    ckv = _rms(z[:, B_Q_LORA:], gkv_ref[...]).astype(BF16)
    half = B_ROPE // 2

    qt = lax.dot_general(wuqt_ref[...], cq, NT_DIMS, preferred_element_type=F32) * qscale
    ct, st = cqt_ref[...], sqt_ref[...]
    for hh in range(B_HEADS):
        r0, r1 = hh * LANES, hh * LANES + B_NOPE
        qt_ref[0, 0, r0:r1, :] = qt[r0:r1].astype(BF16)
        t1, t2 = qt[r1:r1 + half], qt[r1 + half:r1 + 2 * half]
        qt_ref[0, 0, r1:r1 + half, :] = (t1 * ct - t2 * st).astype(BF16)
        qt_ref[0, 0, r1 + half:r1 + 2 * half, :] = (t1 * st + t2 * ct).astype(BF16)
        qt_ref[0, 0, r1 + B_ROPE:r0 + LANES, :] = qt[r1 + B_ROPE:r0 + LANES].astype(BF16)

    kr = jnp.dot(h, wkr_ref[...], preferred_element_type=F32)
    lane = lax.broadcasted_iota(jnp.int32, (ROW_TILE, LANES), 1)
    first = (lane >= B_NOPE) & (lane < B_NOPE + half)
    kr = _rope_lanes(kr, ck_ref[...], sk_ref[...], first, half)
    kn = jnp.dot(ckv, wukn_ref[...], preferred_element_type=F32)
    for hh in range(B_HEADS):
        sl = slice(hh * LANES, (hh + 1) * LANES)
        k_ref[0, :, sl] = (kn[:, sl] + kr).astype(BF16)

    vt = lax.dot_general(wuvt_ref[...], ckv, NT_DIMS, preferred_element_type=F32)
    _store_values(vt_ref, vt, B_VDIM)


def _proj_mla(stream, mod, g_norm, win, wkr, gq, gkv, wuqt, wukn, wuvt, tabs, *, qscale):
    shapes, specs = _qkv_out(B_HEADS * LANES, B_VDIM)
    return pl.pallas_call(
        functools.partial(_proj_mla_kernel, qscale=qscale),
        out_shape=shapes,
        grid=(BATCH, N_TILES),
        in_specs=_row_specs(stream) + [
            _resident((4, D_MODEL)),
            _resident(win.shape), _resident(wkr.shape),
            _resident((1, B_Q_LORA)), _resident((1, B_KV_LORA)),
            _resident(wuqt.shape), _resident(wukn.shape), _resident(wuvt.shape)]
            + _rope_specs(tabs[2].shape[0]),
        out_specs=specs,
        compiler_params=_params(2),
        name="proj_mla",
    )(*stream, mod, g_norm, win, wkr, gq, gkv, wuqt, wukn, wuvt, *tabs)


def _value_rows(diff):
    if diff:
        rows = 2 * A_HEAD_DIM
        return ((0, rows + V_ONES), (0, rows + V_ONES)), rows
    rows = LANES // 2
    return ((0, rows + V_ONES), (rows + V_ONES, 2 * (rows + V_ONES))), rows


def _split_queries(qt, qs_ref):
    dq = qt.shape[0]
    if dq == 2 * LANES:
        qs_ref[0] = qt[:LANES]
        qs_ref[1] = qt[LANES:]
        return
    row = lax.broadcasted_iota(jnp.int32, qt.shape, 0)
    zero = jnp.zeros_like(qt)
    qs_ref[0] = jnp.where(row < dq // 2, qt, zero)
    qs_ref[1] = jnp.where(row >= dq // 2, qt, zero)


def _slot_keys(k_ref, rows, slot):
    if k_ref.shape[-1] == 2 * LANES:
        return k_ref[0, rows, slot * LANES:(slot + 1) * LANES]
    return k_ref[0, rows, :]


def _slot_queries(qt, slot):
    half = qt.shape[0] // 2
    return qt[slot * half:(slot + 1) * half]


def _probabilities(s, m):
    return jnp.exp2(s - m).astype(BF16)


def _flash_kernel(lam_ref, gsub_ref, qt_ref, k_ref, vt_ref, o_ref,
                  qs_ref, c_ref, s0_ref, s1_ref, p0_ref, p1_ref, p2_ref, a0_ref, a1_ref, a2_ref,
                  m_ref, acc_ref, fin_ref, *, diff, lam_init):
    v_rows, n_val = _value_rows(diff)
    s_bufs, p_bufs, a_bufs = (s0_ref, s1_ref), (p0_ref, p1_ref, p2_ref), (a0_ref, a1_ref, a2_ref)
    n_groups = len(KEY_GROUPS)

    def tile_rows(j):
        return slice(j * ROW_TILE, (j + 1) * ROW_TILE)

    def scores(t0, n, slot, s_ref):
        for j in range(0, n, SCORE_TILES):
            rows = min(SCORE_TILES, n - j) * ROW_TILE
            k = _slot_keys(k_ref, slice((t0 + j) * ROW_TILE, (t0 + j) * ROW_TILE + rows), slot)
            s_ref[slot, j * ROW_TILE:j * ROW_TILE + rows, :] = jnp.dot(
                k, qs_ref[slot], preferred_element_type=F32)

    def stats(n, s_ref, a_ref, first):
        for slot in range(2):
            m_prev = jnp.full((1, ROW_TILE), MASK_VALUE, F32) if first else m_ref[slot]
            strips = [s_ref[slot, c * STRIP:(c + 1) * STRIP, :] for c in range(n * ROW_TILE // STRIP)]
            m_next = jnp.maximum(
                m_prev, jnp.max(functools.reduce(jnp.maximum, strips), axis=0, keepdims=True))
            a_ref[slot] = jnp.exp2(m_prev - m_next)
            m_ref[slot] = m_next

    def probs_tile(j, s_ref, p_ref):
        for slot in range(2):
            m = m_ref[slot]
            for c in range(ROW_TILE // STRIP):
                rows = slice(j * ROW_TILE + c * STRIP, j * ROW_TILE + (c + 1) * STRIP)
                p_ref[slot, rows, :] = _probabilities(s_ref[slot, rows, :], m)

    def values_tile(t, j, p_ref, pv):
        for slot in range(2):
            lo, hi = v_rows[slot]
            pv[slot].append(jnp.dot(vt_ref[0, t, lo:hi, :], p_ref[slot, tile_rows(j), :],
                                    preferred_element_type=F32))

    def accumulate(pv, a_ref):
        for slot in range(2):
            acc_ref[slot] = a_ref[slot] * acc_ref[slot] + functools.reduce(jnp.add, pv[slot])

    def finalize(i, src_ref):
        acc_a, acc_b = src_ref[0], src_ref[1]
        ot_a = acc_a[:n_val] * (1.0 / acc_a[n_val:n_val + 1])
        ot_b = acc_b[:n_val] * (1.0 / acc_b[n_val:n_val + 1])
        if diff:
            lv = lam_ref[...]
            lam = (jnp.exp(jnp.sum(lv[0:1] * lv[1:2], axis=1, keepdims=True))
                   - jnp.exp(jnp.sum(lv[2:3] * lv[3:4], axis=1, keepdims=True)) + lam_init)
            o = (ot_a - lam * ot_b).T
            o = _rms(o, gsub_ref[...]) * (1.0 - lam_init)
        else:
            o = jnp.concatenate([ot_a, ot_b], axis=0).T
        o_ref[0, pl.ds(pl.multiple_of(i * ROW_TILE, ROW_TILE), ROW_TILE), :] = o.astype(BF16)

    def exact_path():
        acc_ref[...] = jnp.zeros(acc_ref.shape, F32)
        fin_ref[...] = jnp.ones(fin_ref.shape, F32)

        def scores_group(g, s_ref):
            for slot in range(2):
                scores(KEY_GROUPS[g][0], KEY_GROUPS[g][1], slot, s_ref)

        def probs_group(g, s_ref, p_ref, a_ref, first):
            stats(KEY_GROUPS[g][1], s_ref, a_ref, first)
            for j in range(KEY_GROUPS[g][1]):
                probs_tile(j, s_ref, p_ref)

        _split_queries(qt_ref[0, 0], qs_ref)
        for fill in range(-SCORE_LEAD, 0):
            g_prob, g_sco = fill + PROB_LEAD, fill + SCORE_LEAD
            if g_prob >= 0:
                probs_group(g_prob, s_bufs[g_prob % 2], p_bufs[g_prob % 3], a_bufs[g_prob % 3],
                            g_prob == 0)
            scores_group(g_sco, s_bufs[g_sco % 2])

        def body(i, carry):
            for g in range(n_groups):
                g_prob, g_sco = (g + PROB_LEAD) % n_groups, (g + SCORE_LEAD) % n_groups
                t_val, n_val_tiles = KEY_GROUPS[g]
                t_sco, n_sco = KEY_GROUPS[g_sco]
                n_prob = KEY_GROUPS[g_prob][1]
                s_prob, p_prob, a_prob = s_bufs[g_prob % 2], p_bufs[g_prob % 3], a_bufs[g_prob % 3]
                stats(n_prob, s_prob, a_prob, g_prob == 0)
                if g + SCORE_LEAD == n_groups:
                    _split_queries(qt_ref[0, i + 1], qs_ref)
                pv = ([], [])
                for j in range(max(n_val_tiles, n_prob)):
                    if j < n_prob:
                        probs_tile(j, s_prob, p_prob)
                    if j < n_val_tiles:
                        values_tile(t_val + j, j, p_bufs[g % 3], pv)
                    if j < 2:
                        scores(t_sco, n_sco, j, s_bufs[g_sco % 2])
                accumulate(pv, a_bufs[g % 3])
                if g == 0:
                    finalize(jnp.maximum(i - 1, 0), fin_ref)
            fin_ref[...] = acc_ref[...]
            return carry
        lax.fori_loop(0, CTX_TILE, body, 0)
        finalize(CTX_TILE - 1, fin_ref)

        for slot in range(2):
            scores(CTX_TILE, 1, slot, s0_ref)
        stats(1, s0_ref, a0_ref, True)
        probs_tile(0, s0_ref, p0_ref)
        pv = ([], [])
        values_tile(CTX_TILE, 0, p0_ref, pv)
        accumulate(pv, a0_ref)
        finalize(CTX_TILE, acc_ref)

    def key_norms(slot, heads):
        width = LANES // heads
        r = lax.broadcasted_iota(jnp.int32, (LANES, LANES), 0)
        c = lax.broadcasted_iota(jnp.int32, (LANES, LANES), 1)
        same_head = ((r // width) == (c // width)).astype(BF16)

        def step(t, mx):
            rows = pl.ds(pl.multiple_of(t * ROW_TILE, ROW_TILE), ROW_TILE)
            k = _slot_keys(k_ref, rows, slot).astype(F32)
            return jnp.maximum(
                mx, jnp.dot((k * k).astype(BF16), same_head, preferred_element_type=F32))
        mx = lax.fori_loop(0, N_TILES, step, jnp.zeros((ROW_TILE, LANES), F32), unroll=3)
        return [jnp.sqrt(jnp.max(mx[:, h * width:h * width + 1], axis=0, keepdims=True)
                         * NORM_SLACK) for h in range(heads)]

    def query_norm(slot):
        def step(t, mx):
            q = _slot_queries(qt_ref[0, t], slot).astype(F32)
            return jnp.maximum(mx, jnp.sum(q * q, axis=0, keepdims=True))
        mx = lax.fori_loop(0, N_TILES, step, jnp.zeros((1, ROW_TILE), F32), unroll=3)
        return jnp.sqrt(jnp.max(mx, axis=1, keepdims=True))

    def bounded_path(k_norms):
        def split(i):
            _split_queries(qt_ref[0, i], qs_ref)
            for slot in range(2):
                q = qs_ref[slot].astype(F32)
                c_ref[slot] = jnp.sqrt(jnp.sum(q * q, axis=0, keepdims=True)) * k_norms[slot] + 1.0

        def probabilities(t0, n, slot, p_ref):
            c = c_ref[slot]
            for j in range(0, n, SCORE_TILES):
                rows = min(SCORE_TILES, n - j) * ROW_TILE
                k = _slot_keys(k_ref, slice((t0 + j) * ROW_TILE, (t0 + j) * ROW_TILE + rows), slot)
                s = jnp.dot(k, qs_ref[slot], preferred_element_type=F32)
                p_ref[slot, j * ROW_TILE:j * ROW_TILE + rows, :] = _probabilities(s, c)

        def step(g, p_ref, next_probabilities):
            pv = ([], [])
            scores_first = k_ref.shape[-1] == LANES
            for j in range(KEY_GROUPS[g][1]):
                if scores_first and j < len(next_probabilities):
                    next_probabilities[j]()
                values_tile(KEY_GROUPS[g][0] + j, j, p_ref, pv)
                if not scores_first and j < len(next_probabilities):
                    next_probabilities[j]()
            for slot in range(2):
                total = functools.reduce(jnp.add, pv[slot])
                acc_ref[slot] = total if g == 0 else acc_ref[slot] + total

        fin_ref[...] = jnp.ones(fin_ref.shape, F32)
        split(0)
        for slot in range(2):
            probabilities(KEY_GROUPS[0][0], KEY_GROUPS[0][1], slot, p_bufs[0])

        def body(i, carry):
            for g in range(n_groups):
                g_next = (g + 1) % n_groups
                if g_next == 0:
                    split(i + 1)
                step(g, p_bufs[g % 2],
                     [functools.partial(probabilities, KEY_GROUPS[g_next][0], KEY_GROUPS[g_next][1],
                                        slot, p_bufs[g_next % 2]) for slot in range(2)])
                if g == 0:
                    finalize(jnp.maximum(i - 1, 0), fin_ref)
            fin_ref[...] = acc_ref[...]
            return carry
        lax.fori_loop(0, CTX_TILE, body, 0)
        finalize(CTX_TILE - 1, fin_ref)

        for slot in range(2):
            probabilities(CTX_TILE, 1, slot, p_bufs[0])
        pv = ([], [])
        values_tile(CTX_TILE, 0, p_bufs[0], pv)
        for slot in range(2):
            acc_ref[slot] = pv[slot][0]
        finalize(CTX_TILE, acc_ref)

    if k_ref.shape[-1] == LANES:
        k_norms = key_norms(0, 2)
    else:
        k_norms = [key_norms(slot, 1)[0] for slot in range(2)]
    widest = functools.reduce(jnp.maximum, [2.0 * (k_norms[slot] * query_norm(slot) + 1.0)
                                            for slot in range(2)])[0, 0]

    @pl.when(widest <= SAFE_EXPONENT)
    def _():
        bounded_path(k_norms)

    @pl.when(jnp.logical_not(widest <= SAFE_EXPONENT))
    def _():
        exact_path()


def _flash(qt, k, vt, lam, gsub, *, diff, lam_init):
    dq = k.shape[-1] // N_PAIRS
    vt_rows = vt.shape[2] // N_PAIRS
    acc_rows = _value_rows(diff)[0][0][1]
    group_rows = max(n for _, n in KEY_GROUPS) * ROW_TILE
    return pl.pallas_call(
        functools.partial(_flash_kernel, diff=diff, lam_init=lam_init),
        out_shape=jax.ShapeDtypeStruct((BATCH, T_ALL, D_MODEL), BF16),
        grid=(BATCH, N_PAIRS),
        in_specs=[pl.BlockSpec(lam.shape, lambda b, p: (0, 0)),
                  pl.BlockSpec(gsub.shape, lambda b, p: (0, 0)),
                  pl.BlockSpec((1, N_TILES, dq, ROW_TILE), lambda b, p: (b, 0, p, 0)),
                  pl.BlockSpec((1, T_ALL, dq), lambda b, p: (b, 0, p)),
                  pl.BlockSpec((1, N_TILES, vt_rows, ROW_TILE), lambda b, p: (b, 0, p, 0))],
        out_specs=pl.BlockSpec((1, T_ALL, LANES), lambda b, p: (b, 0, p)),
        scratch_shapes=[pltpu.VMEM((2, LANES, ROW_TILE), BF16),
                        pltpu.VMEM((2, 1, ROW_TILE), F32),
                        pltpu.VMEM((2, group_rows, ROW_TILE), F32),
                        pltpu.VMEM((2, group_rows, ROW_TILE), F32),
                        pltpu.VMEM((2, group_rows, ROW_TILE), BF16),
                        pltpu.VMEM((2, group_rows, ROW_TILE), BF16),
                        pltpu.VMEM((2, group_rows, ROW_TILE), BF16),
                        pltpu.VMEM((2, 1, ROW_TILE), F32),
                        pltpu.VMEM((2, 1, ROW_TILE), F32),
                        pltpu.VMEM((2, 1, ROW_TILE), F32),
                        pltpu.VMEM((2, 1, ROW_TILE), F32),
                        pltpu.VMEM((2, acc_rows, ROW_TILE), F32),
                        pltpu.VMEM((2, acc_rows, ROW_TILE), F32)],
        compiler_params=_params(2),
        name="flash_diff" if diff else "flash_pair",
    )(lam, gsub, qt, k, vt)


NA_Q_ROWS = ROW_TILE // GRID_W
NA_BAND_TILES = 3
NA_LAST_BAND = SEQ // ROW_TILE - NA_BAND_TILES


def _na_band_start(i):
    return jnp.clip(i - 1, 0, NA_LAST_BAND)


def _na_kernel(qt_ref, k0_ref, k1_ref, k2_ref, kc_ref, v0_ref, v1_ref, v2_ref, vc_ref,
               bias_ref, o_ref, qs_ref):
    k_refs = (k0_ref, k1_ref, k2_ref)
    v_refs = (v0_ref, v1_ref, v2_ref, vc_ref)
    v_rows, n_val = _value_rows(False)
    chains = [(b, slot) for b in range(BATCH) for slot in range(2)]
    for b in range(BATCH):
        _split_queries(qt_ref[b, 0], qs_ref.at[b])
    def chain_scores(b, slot):
        qs = qs_ref[b, slot]
        s = [jnp.dot(k_refs[j][b], qs, preferred_element_type=F32)
             + bias_ref[0, slot, j * ROW_TILE:(j + 1) * ROW_TILE, :]
             for j in range(NA_BAND_TILES)]
        s.append(jnp.dot(kc_ref[b], qs, preferred_element_type=F32))
        return s

    outs = {}
    s_next = chain_scores(*chains[0])
    for n, (b, slot) in enumerate(chains):
        s = s_next
        if n + 1 < len(chains):
            s_next = chain_scores(*chains[n + 1])
        lo, hi = v_rows[slot]
        m = jnp.max(functools.reduce(jnp.maximum, s), axis=0, keepdims=True)
        acc = functools.reduce(jnp.add, [
            jnp.dot(r[b, 0, lo:hi, :], _probabilities(sj, m), preferred_element_type=F32)
            for r, sj in zip(v_refs, s)])
        outs[b, slot] = acc[:n_val] * (1.0 / acc[n_val:n_val + 1])
    for b in range(BATCH):
        o_ref[b] = jnp.concatenate([outs[b, 0], outs[b, 1]], axis=0).T.astype(BF16)


def _na_variant(i):
    return jnp.where(i == 0, 0, jnp.where(i < NA_LAST_BAND + NA_BAND_TILES - 1, 1,
                                          jnp.where(i < CTX_TILE, 2, 3)))


def _na_attention(qt, k, vt, bias):
    vt_rows = vt.shape[2] // N_PAIRS

    def kspec(j):
        return pl.BlockSpec((BATCH, ROW_TILE, LANES), lambda p, i: (0, _na_band_start(i) + j, p))

    def vspec(j):
        return pl.BlockSpec((BATCH, 1, vt_rows, ROW_TILE),
                            lambda p, i: (0, _na_band_start(i) + j, p, 0))

    return pl.pallas_call(
        _na_kernel,
        out_shape=jax.ShapeDtypeStruct((BATCH, T_ALL, D_MODEL), BF16),
        grid=(N_PAIRS, N_TILES),
        in_specs=[pl.BlockSpec((BATCH, 1, LANES, ROW_TILE), lambda p, i: (0, i, p, 0)),
                  kspec(0), kspec(1), kspec(2),
                  pl.BlockSpec((BATCH, ROW_TILE, LANES), lambda p, i: (0, CTX_TILE, p)),
                  vspec(0), vspec(1), vspec(2),
                  pl.BlockSpec((BATCH, 1, vt_rows, ROW_TILE), lambda p, i: (0, CTX_TILE, p, 0)),
                  pl.BlockSpec((1, 2, NA_BAND_TILES * ROW_TILE, ROW_TILE),
                               lambda p, i: (_na_variant(i), p, 0, 0))],
        out_specs=pl.BlockSpec((BATCH, ROW_TILE, LANES), lambda p, i: (0, i, p)),
        scratch_shapes=[pltpu.VMEM((BATCH, 2, LANES, ROW_TILE), BF16)],
        compiler_params=_params(2),
        name="na_attention",
    )(qt, k, k, k, k, vt, vt, vt, vt, bias)


def _na_bias_table(rpb):
    rows = SEQ // GRID_W
    band_rows = NA_BAND_TILES * NA_Q_ROWS
    col = np.arange(GRID_W)
    cstart = np.clip(col - NA_COLS // 2, 0, GRID_W - NA_COLS)
    ok_c = (col[None, :] >= cstart[:, None]) & (col[None, :] < cstart[:, None] + NA_COLS)
    dc = col[None, :] - col[:, None] + NA_COLS - 1
    onehot_c = ((dc[:, :, None] == np.arange(2 * NA_COLS - 1)) & ok_c[:, :, None]).astype(np.float32)
    dr = np.zeros((4, NA_Q_ROWS, band_rows), np.int32)
    ok_r = np.zeros((4, NA_Q_ROWS, band_rows), bool)
    for variant, i in enumerate((0, 1, SEQ // ROW_TILE - 1)):
        j = min(max(i - 1, 0), NA_LAST_BAND)
        r = NA_Q_ROWS * i + np.arange(NA_Q_ROWS)[:, None]
        k_row = NA_Q_ROWS * j + np.arange(band_rows)[None, :]
        rstart = np.clip(r - NA_ROWS // 2, 0, rows - NA_ROWS)
        ok_r[variant] = (k_row >= rstart) & (k_row < rstart + NA_ROWS)
        dr[variant] = np.where(ok_r[variant], k_row - r + NA_ROWS - 1, 0)
    by_col = jnp.einsum("hab,ckb->hack", rpb.astype(F32) * LOG2E, jnp.asarray(onehot_c),
                        precision=lax.Precision.HIGHEST)
    by_col = jnp.where(ok_c[None, None], by_col, MASK_VALUE)
    full = jnp.take(by_col, jnp.asarray(dr.reshape(-1)), axis=1)
    full = full.reshape(C_HEADS, 4, NA_Q_ROWS, band_rows, GRID_W, GRID_W)
    full = jnp.where(ok_r[None, :, :, :, None, None], full, MASK_VALUE)
    full = full.transpose(1, 0, 3, 5, 2, 4)
    return full.reshape(4, C_HEADS, NA_BAND_TILES * ROW_TILE, ROW_TILE)


def _post_kernel(o_ref, x_ref, xc_ref, mod_ref, g_ref, wo_ref, wgu_ref, wd_ref, xo_ref):
    m = mod_ref[0]
    y = jnp.dot(o_ref[0], wo_ref[...], preferred_element_type=F32)
    x1 = _stream_tile(x_ref, xc_ref) + m[2:3] * _rms(y, g_ref[1:2])
    h2 = (_rms(x1, g_ref[2:3]) * (1.0 + m[4:5]) + m[3:4]).astype(BF16)
    gu = jnp.dot(h2, wgu_ref[...], preferred_element_type=F32)
    gate, up = gu[:, :FFN_HIDDEN], gu[:, FFN_HIDDEN:]
    a = (gate * _sigmoid(gate) * up).astype(BF16)
    f = jnp.dot(a, wd_ref[...], preferred_element_type=F32)
    xo_ref[0] = x1 + m[5:6] * _rms(f, g_ref[3:4])


def _post(o, stream, mod, g_norm, wo, wgu, wd, *, n_tiles):
    return pl.pallas_call(
        _post_kernel,
        out_shape=jax.ShapeDtypeStruct((BATCH, n_tiles * ROW_TILE, D_MODEL), F32),
        grid=(BATCH, n_tiles),
        in_specs=[pl.BlockSpec((1, ROW_TILE, D_MODEL), lambda b, t: (b, t, 0))] + _row_specs(stream) + [
            _resident((4, D_MODEL)), _resident(wo.shape), _resident(wgu.shape),
            _resident(wd.shape)],
        out_specs=pl.BlockSpec((1, ROW_TILE, D_MODEL), lambda b, t: (b, t, 0)),
        compiler_params=_params(2),
        name="post_attention",
    )(o, *stream, mod, g_norm, wo, wgu, wd)


def _rope_angles(rot_dim):
    n_freq = rot_dim // 4
    inv = ROPE_THETA ** (-jnp.arange(n_freq, dtype=F32) / n_freq)
    t = jnp.arange(SEQ, dtype=jnp.int32)
    row = (t // GRID_W).astype(F32)
    col = (t % GRID_W).astype(F32)
    ang = jnp.concatenate([row[:, None] * inv, col[:, None] * inv], axis=-1)
    cos = jnp.concatenate([jnp.cos(ang), jnp.ones((CTX_LEN, rot_dim // 2), F32)], axis=0)
    sin = jnp.concatenate([jnp.sin(ang), jnp.zeros((CTX_LEN, rot_dim // 2), F32)], axis=0)
    return cos, sin


def _rope_tables_diff():
    cos, sin = _rope_angles(A_HEAD_DIM)
    c_lane = jnp.tile(cos, (1, 4))
    s_lane = jnp.tile(jnp.concatenate([-sin, sin], axis=1), (1, 2))
    return c_lane, s_lane, cos.T, sin.T


def _rope_tables_mla():
    cos, sin = _rope_angles(B_ROPE)
    ones = jnp.ones((T_ALL, B_NOPE), F32)
    zeros = jnp.zeros((T_ALL, B_NOPE), F32)
    tail = LANES - B_NOPE - B_ROPE
    c_lane = jnp.concatenate([ones, cos, cos, ones[:, :tail]], axis=1)
    s_lane = jnp.concatenate([zeros, -sin, sin, zeros[:, :tail]], axis=1)
    return c_lane, s_lane, cos.T, sin.T


def kernel(x, c, ctx, c_ctx, l0_w_mod, l0_b_mod, l0_g_norm, l0_w_gu, l0_w_down, l0_a_w_qkv, l0_a_w_o, l0_a_lam, l0_a_g_sub, l1_w_mod, l1_b_mod, l1_g_norm, l1_w_gu, l1_w_down, l1_b_w_in, l1_b_g_q, l1_b_g_kv, l1_b_w_uq, l1_b_w_ukv, l1_b_w_o, l2_w_mod, l2_b_mod, l2_g_norm, l2_w_gu, l2_w_down, l2_c_w_qkv, l2_c_rpb, l2_c_w_o, l3_w_mod, l3_b_mod, l3_g_norm, l3_w_gu, l3_w_down, l3_a_w_qkv, l3_a_w_o, l3_a_lam, l3_a_g_sub):
    common = [
        (l0_w_mod, l0_b_mod, l0_g_norm, l0_w_gu, l0_w_down),
        (l1_w_mod, l1_b_mod, l1_g_norm, l1_w_gu, l1_w_down),
        (l2_w_mod, l2_b_mod, l2_g_norm, l2_w_gu, l2_w_down),
        (l3_w_mod, l3_b_mod, l3_g_norm, l3_w_gu, l3_w_down),
    ]
    diff_params = {0: (l0_a_w_qkv, l0_a_w_o, l0_a_lam, l0_a_g_sub),
                   3: (l3_a_w_qkv, l3_a_w_o, l3_a_lam, l3_a_g_sub)}

    stream = (x, ctx)
    cc = jnp.concatenate([c, c_ctx[None, :], jnp.zeros((8 - BATCH - 1, D_MODEL), F32)], axis=0)
    tabs_a = _rope_tables_diff()
    tabs_b = _rope_tables_mla()
    dummy_lam = jnp.zeros((4, A_HEAD_DIM), F32)
    dummy_gsub = jnp.ones((1, LANES), F32)

    for i in range(DEPTH):
        w_mod, b_mod, g_norm, w_gu, w_down = common[i]
        last = i == DEPTH - 1
        mod = _modulation(cc, w_mod, b_mod).reshape(8, 6, D_MODEL)
        kind = i % 3
        if kind == 0:
            w_qkv, w_o, lam, g_sub = diff_params[i]
            lam_init = 0.8 - 0.6 * math.exp(-0.3 * i)
            wqt = w_qkv[:, :D_MODEL].T.astype(BF16)
            wk = w_qkv[:, D_MODEL:2 * D_MODEL].astype(BF16)
            wvt = w_qkv[:, 2 * D_MODEL:].T.astype(BF16)
            qt, k, vt = _proj_qkv(stream, mod, g_norm, wqt, wk, wvt, tabs_a, rope=True,
                                  qscale=A_HEAD_DIM ** -0.5 * LOG2E, v_rows=2 * A_HEAD_DIM)
            o = _flash(qt, k, vt, lam, g_sub.reshape(1, LANES), diff=True, lam_init=lam_init)
        elif kind == 1:
            w_in, g_q, g_kv, w_uq, w_ukv, w_o = (l1_b_w_in, l1_b_g_q, l1_b_g_kv, l1_b_w_uq,
                                                 l1_b_w_ukv, l1_b_w_o)
            n_lora = B_Q_LORA + B_KV_LORA
            pad = LANES - B_NOPE - B_ROPE
            win = w_in[:, :n_lora].astype(BF16)
            wkr = jnp.pad(w_in[:, n_lora:], ((0, 0), (B_NOPE, pad))).astype(BF16)
            wuq = w_uq.reshape(B_Q_LORA, B_HEADS, B_NOPE + B_ROPE)
            wuq = jnp.pad(wuq, ((0, 0), (0, 0), (0, pad)))
            wuqt = wuq.reshape(B_Q_LORA, B_HEADS * LANES).T.astype(BF16)
            wukv = w_ukv.reshape(B_KV_LORA, B_HEADS, B_NOPE + B_VDIM)
            wukn = jnp.pad(wukv[:, :, :B_NOPE], ((0, 0), (0, 0), (0, LANES - B_NOPE)))
            wukn = wukn.reshape(B_KV_LORA, B_HEADS * LANES).astype(BF16)
            wuvt = wukv[:, :, B_NOPE:].reshape(B_KV_LORA, B_HEADS * B_VDIM).T.astype(BF16)
            qt, k, vt = _proj_mla(stream, mod, g_norm, win, wkr, g_q.reshape(1, -1),
                                  g_kv.reshape(1, -1), wuqt, wukn, wuvt, tabs_b,
                                  qscale=(B_NOPE + B_ROPE) ** -0.5 * LOG2E)
            o = _flash(qt, k, vt, dummy_lam, dummy_gsub, diff=False, lam_init=0.0)
        else:
            w_qkv, rpb, w_o = l2_c_w_qkv, l2_c_rpb, l2_c_w_o
            wqt = w_qkv[:, :D_MODEL].T.astype(BF16)
            wk = w_qkv[:, D_MODEL:2 * D_MODEL].astype(BF16)
            wvt = w_qkv[:, 2 * D_MODEL:].T.astype(BF16)
            qt, k, vt = _proj_qkv(stream, mod, g_norm, wqt, wk, wvt, tabs_a, rope=False,
                                  qscale=C_HEAD_DIM ** -0.5 * LOG2E, v_rows=C_HEAD_DIM)
            o = _na_attention(qt, k, vt, _na_bias_table(rpb))
        xs = _post(o, stream, mod, g_norm, w_o.astype(BF16), w_gu.astype(BF16),
                   w_down.astype(BF16), n_tiles=N_TILES - 1 if last else N_TILES)
        stream = (xs, xs)
    return xs
```

```python
import functools
import math

import numpy as np
import jax
import jax.numpy as jnp
from jax import lax
from jax.experimental import pallas as pl
from jax.experimental.pallas import tpu as pltpu

D_MODEL = 1024
BATCH = 4
SEQ = 8192
DEPTH = 4
GRID_W = 64
CTX_LEN = 256
ROPE_THETA = 10000.0
NORM_EPS = 1e-6

A_HEAD_DIM = 64
A_HEADS = 8
B_HEADS = 16
B_NOPE = 64
B_ROPE = 32
B_VDIM = 64
B_Q_LORA = 256
B_KV_LORA = 256
C_HEAD_DIM = 64
C_HEADS = 16
NA_ROWS = 8
NA_COLS = 16
FFN_HIDDEN = 2816

LANES = 128
STRIP = 64
ROW_TILE = 256
T_ALL = SEQ + CTX_LEN
N_TILES = T_ALL // ROW_TILE
CTX_TILE = N_TILES - 1
N_PAIRS = 8
KEY_GROUPS = tuple((3 * g, 3) for g in range(9)) + tuple((27 + 2 * g, 2) for g in range(3))
SCORE_TILES = 2
PROB_LEAD = 1
SCORE_LEAD = 2
V_ONES = 16
LOG2E = math.log2(math.e)
MASK_VALUE = -1e30
SAFE_EXPONENT = 96.0
NORM_SLACK = 1.02
VMEM_LIMIT = 60 * 1024 * 1024

F32 = jnp.float32
BF16 = jnp.bfloat16
NT_DIMS = (((1,), (1,)), ((), ()))


def _rms(x, g):
    return x * lax.rsqrt(jnp.mean(x * x, axis=-1, keepdims=True) + NORM_EPS) * g


def _sigmoid(x):
    return 1.0 / (1.0 + jnp.exp(-x))


def _params(n_axes):
    return pltpu.CompilerParams(
        dimension_semantics=("arbitrary",) * n_axes, vmem_limit_bytes=VMEM_LIMIT)


def _resident(shape):
    zeros = (0,) * len(shape)
    return pl.BlockSpec(shape, lambda *_: zeros, pipeline_mode=pl.Buffered(1))


def _mod_row(b, t):
    return jnp.where(t == CTX_TILE, BATCH, b)


def _mod_kernel(c_ref, w_ref, b_ref, o_ref):
    c = c_ref[...]
    a = (c * _sigmoid(c)).astype(BF16)
    o_ref[...] = jnp.dot(a, w_ref[...].astype(BF16), preferred_element_type=F32) + b_ref[...]


def _modulation(cc, w_mod, b_mod):
    n = w_mod.shape[1]
    blk = D_MODEL
    return pl.pallas_call(
        _mod_kernel,
        out_shape=jax.ShapeDtypeStruct((8, n), F32),
        grid=(n // blk,),
        in_specs=[pl.BlockSpec((8, D_MODEL), lambda j: (0, 0)),
                  pl.BlockSpec((D_MODEL, blk), lambda j: (0, j)),
                  pl.BlockSpec((1, blk), lambda j: (0, j))],
        out_specs=pl.BlockSpec((8, blk), lambda j: (0, j)),
        compiler_params=_params(1),
        name="modulation",
    )(cc, w_mod, b_mod.reshape(1, n))


def _rope_lanes(k, c, s, first, half):
    rot = jnp.where(first, pltpu.roll(k, LANES - half, 1), pltpu.roll(k, half, 1))
    return k * c + rot * s


def _store_values(vt_ref, vt, rows):
    ones = jnp.ones((V_ONES, ROW_TILE), BF16)
    for n in range(vt.shape[0] // rows):
        r0 = n * (rows + V_ONES)
        vt_ref[0, 0, r0:r0 + rows, :] = vt[n * rows:(n + 1) * rows].astype(BF16)
        vt_ref[0, 0, r0 + rows:r0 + rows + V_ONES, :] = ones


def _proj_qkv_kernel(x_ref, xc_ref, mod_ref, g_ref, wqt_ref, wk_ref, wvt_ref,
                     ck_ref, sk_ref, cqt_ref, sqt_ref,
                     qt_ref, k_ref, vt_ref, *, rope, qscale, v_rows):
    m = mod_ref[0]
    h = (_rms(_stream_tile(x_ref, xc_ref), g_ref[0:1]) * (1.0 + m[1:2]) + m[0:1]).astype(BF16)
    qt = lax.dot_general(wqt_ref[...], h, NT_DIMS, preferred_element_type=F32) * qscale
    k = jnp.dot(h, wk_ref[...], preferred_element_type=F32)
    vt = lax.dot_general(wvt_ref[...], h, NT_DIMS, preferred_element_type=F32)
    _store_values(vt_ref, vt, v_rows)
    if not rope:
        qt_ref[0, 0] = qt.astype(BF16)
        k_ref[0] = k.astype(BF16)
        return
    half = A_HEAD_DIM // 2
    c, s = ck_ref[...], sk_ref[...]
    lane = lax.broadcasted_iota(jnp.int32, (ROW_TILE, LANES), 1)
    first = (lane % A_HEAD_DIM) < half
    for p in range(D_MODEL // LANES):
        sl = slice(p * LANES, (p + 1) * LANES)
        k_ref[0, :, sl] = _rope_lanes(k[:, sl], c, s, first, half).astype(BF16)
    ct, st = cqt_ref[...], sqt_ref[...]
    for g in range(D_MODEL // A_HEAD_DIM):
        r0 = g * A_HEAD_DIM
        t1, t2 = qt[r0:r0 + half], qt[r0 + half:r0 + 2 * half]
        qt_ref[0, 0, r0:r0 + half, :] = (t1 * ct - t2 * st).astype(BF16)
        qt_ref[0, 0, r0 + half:r0 + 2 * half, :] = (t1 * st + t2 * ct).astype(BF16)


def _row_specs(stream):
    last_lat, ctx_tile = (a.shape[1] // ROW_TILE - 1 for a in stream)
    return [pl.BlockSpec((1, ROW_TILE, D_MODEL), lambda b, t: (b, jnp.minimum(t, last_lat), 0)),
            pl.BlockSpec((1, ROW_TILE, D_MODEL), lambda b, t: (b, ctx_tile, 0)),
            pl.BlockSpec((1, 6, D_MODEL), lambda b, t: (_mod_row(b, t), 0, 0))]


def _stream_tile(x_ref, xc_ref):
    return jnp.where(pl.program_id(1) == CTX_TILE, xc_ref[0], x_ref[0])


def _rope_specs(half):
    return [pl.BlockSpec((ROW_TILE, LANES), lambda b, t: (t, 0)),
            pl.BlockSpec((ROW_TILE, LANES), lambda b, t: (t, 0)),
            pl.BlockSpec((half, ROW_TILE), lambda b, t: (0, t)),
            pl.BlockSpec((half, ROW_TILE), lambda b, t: (0, t))]


def _qkv_out(width, v_rows):
    vt_rows = D_MODEL // v_rows * (v_rows + V_ONES)
    shapes = (jax.ShapeDtypeStruct((BATCH, N_TILES, width, ROW_TILE), BF16),
              jax.ShapeDtypeStruct((BATCH, T_ALL, width), BF16),
              jax.ShapeDtypeStruct((BATCH, N_TILES, vt_rows, ROW_TILE), BF16))
    specs = (pl.BlockSpec((1, 1, width, ROW_TILE), lambda b, t: (b, t, 0, 0)),
             pl.BlockSpec((1, ROW_TILE, width), lambda b, t: (b, t, 0)),
             pl.BlockSpec((1, 1, vt_rows, ROW_TILE), lambda b, t: (b, t, 0, 0)))
    return shapes, specs


def _proj_qkv(stream, mod, g_norm, wqt, wk, wvt, tabs, *, rope, qscale, v_rows):
    shapes, specs = _qkv_out(D_MODEL, v_rows)
    return pl.pallas_call(
        functools.partial(_proj_qkv_kernel, rope=rope, qscale=qscale, v_rows=v_rows),
        out_shape=shapes,
        grid=(BATCH, N_TILES),
        in_specs=_row_specs(stream) + [
            _resident((4, D_MODEL)),
            _resident((D_MODEL, D_MODEL)), _resident((D_MODEL, D_MODEL)),
            _resident((D_MODEL, D_MODEL))] + _rope_specs(tabs[2].shape[0]),
        out_specs=specs,
        compiler_params=_params(2),
        name="proj_qkv",
    )(*stream, mod, g_norm, wqt, wk, wvt, *tabs)


def _proj_mla_kernel(x_ref, xc_ref, mod_ref, g_ref, win_ref, wkr_ref, gq_ref, gkv_ref,
                     wuqt_ref, wukn_ref, wuvt_ref, ck_ref, sk_ref, cqt_ref, sqt_ref,
                     qt_ref, k_ref, vt_ref, *, qscale):
    m = mod_ref[0]
    h = (_rms(_stream_tile(x_ref, xc_ref), g_ref[0:1]) * (1.0 + m[1:2]) + m[0:1]).astype(BF16)
    z = jnp.dot(h, win_ref[...], preferred_element_type=F32)
    cq = _rms(z[:, :B_Q_LORA], gq_ref[...]).astype(BF16)
    ckv = _rms(z[:, B_Q_LORA:], gkv_ref[...]).astype(BF16)
    half = B_ROPE // 2

    qt = lax.dot_general(wuqt_ref[...], cq, NT_DIMS, preferred_element_type=F32) * qscale
    ct, st = cqt_ref[...], sqt_ref[...]
    for hh in range(B_HEADS):
        r0, r1 = hh * LANES, hh * LANES + B_NOPE
        qt_ref[0, 0, r0:r1, :] = qt[r0:r1].astype(BF16)
        t1, t2 = qt[r1:r1 + half], qt[r1 + half:r1 + 2 * half]
        qt_ref[0, 0, r1:r1 + half, :] = (t1 * ct - t2 * st).astype(BF16)
        qt_ref[0, 0, r1 + half:r1 + 2 * half, :] = (t1 * st + t2 * ct).astype(BF16)
        qt_ref[0, 0, r1 + B_ROPE:r0 + LANES, :] = qt[r1 + B_ROPE:r0 + LANES].astype(BF16)

    kr = jnp.dot(h, wkr_ref[...], preferred_element_type=F32)
    lane = lax.broadcasted_iota(jnp.int32, (ROW_TILE, LANES), 1)
    first = (lane >= B_NOPE) & (lane < B_NOPE + half)
    kr = _rope_lanes(kr, ck_ref[...], sk_ref[...], first, half)
    kn = jnp.dot(ckv, wukn_ref[...], preferred_element_type=F32)
    for hh in range(B_HEADS):
        sl = slice(hh * LANES, (hh + 1) * LANES)
        k_ref[0, :, sl] = (kn[:, sl] + kr).astype(BF16)

    vt = lax.dot_general(wuvt_ref[...], ckv, NT_DIMS, preferred_element_type=F32)
    _store_values(vt_ref, vt, B_VDIM)


def _proj_mla(stream, mod, g_norm, win, wkr, gq, gkv, wuqt, wukn, wuvt, tabs, *, qscale):
    shapes, specs = _qkv_out(B_HEADS * LANES, B_VDIM)
    return pl.pallas_call(
        functools.partial(_proj_mla_kernel, qscale=qscale),
        out_shape=shapes,
        grid=(BATCH, N_TILES),
        in_specs=_row_specs(stream) + [
            _resident((4, D_MODEL)),
            _resident(win.shape), _resident(wkr.shape),
            _resident((1, B_Q_LORA)), _resident((1, B_KV_LORA)),
            _resident(wuqt.shape), _resident(wukn.shape), _resident(wuvt.shape)]
            + _rope_specs(tabs[2].shape[0]),
        out_specs=specs,
        compiler_params=_params(2),
        name="proj_mla",
    )(*stream, mod, g_norm, win, wkr, gq, gkv, wuqt, wukn, wuvt, *tabs)


def _value_rows(diff):
    if diff:
        rows = 2 * A_HEAD_DIM
        return ((0, rows + V_ONES), (0, rows + V_ONES)), rows
    rows = LANES // 2
    return ((0, rows + V_ONES), (rows + V_ONES, 2 * (rows + V_ONES))), rows


def _split_queries(qt, qs_ref):
    dq = qt.shape[0]
    if dq == 2 * LANES:
        qs_ref[0] = qt[:LANES]
        qs_ref[1] = qt[LANES:]
        return
    row = lax.broadcasted_iota(jnp.int32, qt.shape, 0)
    zero = jnp.zeros_like(qt)
    qs_ref[0] = jnp.where(row < dq // 2, qt, zero)
    qs_ref[1] = jnp.where(row >= dq // 2, qt, zero)


def _slot_keys(k_ref, rows, slot):
    if k_ref.shape[-1] == 2 * LANES:
        return k_ref[0, rows, slot * LANES:(slot + 1) * LANES]
    return k_ref[0, rows, :]


def _slot_queries(qt, slot):
    half = qt.shape[0] // 2
    return qt[slot * half:(slot + 1) * half]


def _probabilities(s, m):
    return jnp.exp2(s - m).astype(BF16)


def _flash_kernel(lam_ref, gsub_ref, qt_ref, k_ref, vt_ref, o_ref,
                  qs_ref, c_ref, s0_ref, s1_ref, p0_ref, p1_ref, p2_ref, a0_ref, a1_ref, a2_ref,
                  m_ref, acc_ref, fin_ref, *, diff, lam_init):
    v_rows, n_val = _value_rows(diff)
    s_bufs, p_bufs, a_bufs = (s0_ref, s1_ref), (p0_ref, p1_ref, p2_ref), (a0_ref, a1_ref, a2_ref)
    n_groups = len(KEY_GROUPS)

    def tile_rows(j):
        return slice(j * ROW_TILE, (j + 1) * ROW_TILE)

    def scores(t0, n, slot, s_ref):
        for j in range(0, n, SCORE_TILES):
            rows = min(SCORE_TILES, n - j) * ROW_TILE
            k = _slot_keys(k_ref, slice((t0 + j) * ROW_TILE, (t0 + j) * ROW_TILE + rows), slot)
            s_ref[slot, j * ROW_TILE:j * ROW_TILE + rows, :] = jnp.dot(
                k, qs_ref[slot], preferred_element_type=F32)

    def stats(n, s_ref, a_ref, first):
        for slot in range(2):
            m_prev = jnp.full((1, ROW_TILE), MASK_VALUE, F32) if first else m_ref[slot]
            strips = [s_ref[slot, c * STRIP:(c + 1) * STRIP, :] for c in range(n * ROW_TILE // STRIP)]
            m_next = jnp.maximum(
                m_prev, jnp.max(functools.reduce(jnp.maximum, strips), axis=0, keepdims=True))
            a_ref[slot] = jnp.exp2(m_prev - m_next)
            m_ref[slot] = m_next

    def probs_tile(j, s_ref, p_ref):
        for slot in range(2):
            m = m_ref[slot]
            for c in range(ROW_TILE // STRIP):
                rows = slice(j * ROW_TILE + c * STRIP, j * ROW_TILE + (c + 1) * STRIP)
                p_ref[slot, rows, :] = _probabilities(s_ref[slot, rows, :], m)

    def values_tile(t, j, p_ref, pv):
        for slot in range(2):
            lo, hi = v_rows[slot]
            pv[slot].append(jnp.dot(vt_ref[0, t, lo:hi, :], p_ref[slot, tile_rows(j), :],
                                    preferred_element_type=F32))

    def accumulate(pv, a_ref):
        for slot in range(2):
            acc_ref[slot] = a_ref[slot] * acc_ref[slot] + functools.reduce(jnp.add, pv[slot])

    def finalize(i, src_ref):
        acc_a, acc_b = src_ref[0], src_ref[1]
        ot_a = acc_a[:n_val] * (1.0 / acc_a[n_val:n_val + 1])
        ot_b = acc_b[:n_val] * (1.0 / acc_b[n_val:n_val + 1])
        if diff:
            lv = lam_ref[...]
            lam = (jnp.exp(jnp.sum(lv[0:1] * lv[1:2], axis=1, keepdims=True))
                   - jnp.exp(jnp.sum(lv[2:3] * lv[3:4], axis=1, keepdims=True)) + lam_init)
            o = (ot_a - lam * ot_b).T
            o = _rms(o, gsub_ref[...]) * (1.0 - lam_init)
        else:
            o = jnp.concatenate([ot_a, ot_b], axis=0).T
        o_ref[0, pl.ds(pl.multiple_of(i * ROW_TILE, ROW_TILE), ROW_TILE), :] = o.astype(BF16)

    def exact_path():
        acc_ref[...] = jnp.zeros(acc_ref.shape, F32)
        fin_ref[...] = jnp.ones(fin_ref.shape, F32)

        def scores_group(g, s_ref):
            for slot in range(2):
                scores(KEY_GROUPS[g][0], KEY_GROUPS[g][1], slot, s_ref)

        def probs_group(g, s_ref, p_ref, a_ref, first):
            stats(KEY_GROUPS[g][1], s_ref, a_ref, first)
            for j in range(KEY_GROUPS[g][1]):
                probs_tile(j, s_ref, p_ref)

        _split_queries(qt_ref[0, 0], qs_ref)
        for fill in range(-SCORE_LEAD, 0):
            g_prob, g_sco = fill + PROB_LEAD, fill + SCORE_LEAD
            if g_prob >= 0:
                probs_group(g_prob, s_bufs[g_prob % 2], p_bufs[g_prob % 3], a_bufs[g_prob % 3],
                            g_prob == 0)
            scores_group(g_sco, s_bufs[g_sco % 2])

        def body(i, carry):
            for g in range(n_groups):
                g_prob, g_sco = (g + PROB_LEAD) % n_groups, (g + SCORE_LEAD) % n_groups
                t_val, n_val_tiles = KEY_GROUPS[g]
                t_sco, n_sco = KEY_GROUPS[g_sco]
                n_prob = KEY_GROUPS[g_prob][1]
                s_prob, p_prob, a_prob = s_bufs[g_prob % 2], p_bufs[g_prob % 3], a_bufs[g_prob % 3]
                stats(n_prob, s_prob, a_prob, g_prob == 0)
                if g + SCORE_LEAD == n_groups:
                    _split_queries(qt_ref[0, i + 1], qs_ref)
                pv = ([], [])
                for j in range(max(n_val_tiles, n_prob)):
                    if j < n_prob:
                        probs_tile(j, s_prob, p_prob)
                    if j < n_val_tiles:
                        values_tile(t_val + j, j, p_bufs[g % 3], pv)
                    if j < 2:
                        scores(t_sco, n_sco, j, s_bufs[g_sco % 2])
                accumulate(pv, a_bufs[g % 3])
                if g == 0:
                    finalize(jnp.maximum(i - 1, 0), fin_ref)
            fin_ref[...] = acc_ref[...]
            return carry
        lax.fori_loop(0, CTX_TILE, body, 0)
        finalize(CTX_TILE - 1, fin_ref)

        for slot in range(2):
            scores(CTX_TILE, 1, slot, s0_ref)
        stats(1, s0_ref, a0_ref, True)
        probs_tile(0, s0_ref, p0_ref)
        pv = ([], [])
        values_tile(CTX_TILE, 0, p0_ref, pv)
        accumulate(pv, a0_ref)
        finalize(CTX_TILE, acc_ref)

    def key_norms(slot, heads):
        width = LANES // heads
        r = lax.broadcasted_iota(jnp.int32, (LANES, LANES), 0)
        c = lax.broadcasted_iota(jnp.int32, (LANES, LANES), 1)
        same_head = ((r // width) == (c // width)).astype(BF16)

        def step(t, mx):
            rows = pl.ds(pl.multiple_of(t * ROW_TILE, ROW_TILE), ROW_TILE)
            k = _slot_keys(k_ref, rows, slot).astype(F32)
            return jnp.maximum(
                mx, jnp.dot((k * k).astype(BF16), same_head, preferred_element_type=F32))
        mx = lax.fori_loop(0, N_TILES, step, jnp.zeros((ROW_TILE, LANES), F32), unroll=3)
        return [jnp.sqrt(jnp.max(mx[:, h * width:h * width + 1], axis=0, keepdims=True)
                         * NORM_SLACK) for h in range(heads)]

    def query_norm(slot):
        def step(t, mx):
            q = _slot_queries(qt_ref[0, t], slot).astype(F32)
            return jnp.maximum(mx, jnp.sum(q * q, axis=0, keepdims=True))
        mx = lax.fori_loop(0, N_TILES, step, jnp.zeros((1, ROW_TILE), F32), unroll=3)
        return jnp.sqrt(jnp.max(mx, axis=1, keepdims=True))

    def bounded_path(k_norms):
        def split(i):
            _split_queries(qt_ref[0, i], qs_ref)
            for slot in range(2):
                q = qs_ref[slot].astype(F32)
                c_ref[slot] = jnp.sqrt(jnp.sum(q * q, axis=0, keepdims=True)) * k_norms[slot] + 1.0

        def probabilities(t0, n, slot, p_ref):
            c = c_ref[slot]
            for j in range(0, n, SCORE_TILES):
                rows = min(SCORE_TILES, n - j) * ROW_TILE
                k = _slot_keys(k_ref, slice((t0 + j) * ROW_TILE, (t0 + j) * ROW_TILE + rows), slot)
                s = jnp.dot(k, qs_ref[slot], preferred_element_type=F32)
                p_ref[slot, j * ROW_TILE:j * ROW_TILE + rows, :] = _probabilities(s, c)

        def step(g, p_ref, next_probabilities):
            pv = ([], [])
            scores_first = k_ref.shape[-1] == LANES
            for j in range(KEY_GROUPS[g][1]):
                if scores_first and j < len(next_probabilities):
                    next_probabilities[j]()
                values_tile(KEY_GROUPS[g][0] + j, j, p_ref, pv)
                if not scores_first and j < len(next_probabilities):
                    next_probabilities[j]()
            for slot in range(2):
                total = functools.reduce(jnp.add, pv[slot])
                acc_ref[slot] = total if g == 0 else acc_ref[slot] + total

        fin_ref[...] = jnp.ones(fin_ref.shape, F32)
        split(0)
        for slot in range(2):
            probabilities(KEY_GROUPS[0][0], KEY_GROUPS[0][1], slot, p_bufs[0])

        def body(i, carry):
            for g in range(n_groups):
                g_next = (g + 1) % n_groups
                if g_next == 0:
                    split(i + 1)
                step(g, p_bufs[g % 2],
                     [functools.partial(probabilities, KEY_GROUPS[g_next][0], KEY_GROUPS[g_next][1],
                                        slot, p_bufs[g_next % 2]) for slot in range(2)])
                if g == 0:
                    finalize(jnp.maximum(i - 1, 0), fin_ref)
            fin_ref[...] = acc_ref[...]
            return carry
        lax.fori_loop(0, CTX_TILE, body, 0)
        finalize(CTX_TILE - 1, fin_ref)

        for slot in range(2):
            probabilities(CTX_TILE, 1, slot, p_bufs[0])
        pv = ([], [])
        values_tile(CTX_TILE, 0, p_bufs[0], pv)
        for slot in range(2):
            acc_ref[slot] = pv[slot][0]
        finalize(CTX_TILE, acc_ref)

    if k_ref.shape[-1] == LANES:
        k_norms = key_norms(0, 2)
    else:
        k_norms = [key_norms(slot, 1)[0] for slot in range(2)]
    widest = functools.reduce(jnp.maximum, [2.0 * (k_norms[slot] * query_norm(slot) + 1.0)
                                            for slot in range(2)])[0, 0]

    @pl.when(widest <= SAFE_EXPONENT)
    def _():
        bounded_path(k_norms)

    @pl.when(jnp.logical_not(widest <= SAFE_EXPONENT))
    def _():
        exact_path()


def _flash(qt, k, vt, lam, gsub, *, diff, lam_init):
    dq = k.shape[-1] // N_PAIRS
    vt_rows = vt.shape[2] // N_PAIRS
    acc_rows = _value_rows(diff)[0][0][1]
    group_rows = max(n for _, n in KEY_GROUPS) * ROW_TILE
    return pl.pallas_call(
        functools.partial(_flash_kernel, diff=diff, lam_init=lam_init),
        out_shape=jax.ShapeDtypeStruct((BATCH, T_ALL, D_MODEL), BF16),
        grid=(BATCH, N_PAIRS),
        in_specs=[pl.BlockSpec(lam.shape, lambda b, p: (0, 0)),
                  pl.BlockSpec(gsub.shape, lambda b, p: (0, 0)),
                  pl.BlockSpec((1, N_TILES, dq, ROW_TILE), lambda b, p: (b, 0, p, 0)),
                  pl.BlockSpec((1, T_ALL, dq), lambda b, p: (b, 0, p)),
                  pl.BlockSpec((1, N_TILES, vt_rows, ROW_TILE), lambda b, p: (b, 0, p, 0))],
        out_specs=pl.BlockSpec((1, T_ALL, LANES), lambda b, p: (b, 0, p)),
        scratch_shapes=[pltpu.VMEM((2, LANES, ROW_TILE), BF16),
                        pltpu.VMEM((2, 1, ROW_TILE), F32),
                        pltpu.VMEM((2, group_rows, ROW_TILE), F32),
                        pltpu.VMEM((2, group_rows, ROW_TILE), F32),
                        pltpu.VMEM((2, group_rows, ROW_TILE), BF16),
                        pltpu.VMEM((2, group_rows, ROW_TILE), BF16),
                        pltpu.VMEM((2, group_rows, ROW_TILE), BF16),
                        pltpu.VMEM((2, 1, ROW_TILE), F32),
                        pltpu.VMEM((2, 1, ROW_TILE), F32),
                        pltpu.VMEM((2, 1, ROW_TILE), F32),
                        pltpu.VMEM((2, 1, ROW_TILE), F32),
                        pltpu.VMEM((2, acc_rows, ROW_TILE), F32),
                        pltpu.VMEM((2, acc_rows, ROW_TILE), F32)],
        compiler_params=_params(2),
        name="flash_diff" if diff else "flash_pair",
    )(lam, gsub, qt, k, vt)


NA_Q_ROWS = ROW_TILE // GRID_W
NA_BAND_TILES = 3
NA_LAST_BAND = SEQ // ROW_TILE - NA_BAND_TILES


def _na_band_start(i):
    return jnp.clip(i - 1, 0, NA_LAST_BAND)


def _na_kernel(qt_ref, k0_ref, k1_ref, k2_ref, kc_ref, v0_ref, v1_ref, v2_ref, vc_ref,
               bias_ref, o_ref, qs_ref):
    k_refs = (k0_ref, k1_ref, k2_ref)
    v_refs = (v0_ref, v1_ref, v2_ref, vc_ref)
    v_rows, n_val = _value_rows(False)
    chains = [(b, slot) for b in range(BATCH) for slot in range(2)]
    for b in range(BATCH):
        _split_queries(qt_ref[b, 0], qs_ref.at[b])
    def chain_scores(b, slot):
        qs = qs_ref[b, slot]
        s = [jnp.dot(k_refs[j][b], qs, preferred_element_type=F32)
             + bias_ref[0, slot, j * ROW_TILE:(j + 1) * ROW_TILE, :]
             for j in range(NA_BAND_TILES)]
        s.append(jnp.dot(kc_ref[b], qs, preferred_element_type=F32))
        return s

    outs = {}
    s_next = chain_scores(*chains[0])
    for n, (b, slot) in enumerate(chains):
        s = s_next
        if n + 1 < len(chains):
            s_next = chain_scores(*chains[n + 1])
        lo, hi = v_rows[slot]
        m = jnp.max(functools.reduce(jnp.maximum, s), axis=0, keepdims=True)
        acc = functools.reduce(jnp.add, [
            jnp.dot(r[b, 0, lo:hi, :], _probabilities(sj, m), preferred_element_type=F32)
            for r, sj in zip(v_refs, s)])
        outs[b, slot] = acc[:n_val] * (1.0 / acc[n_val:n_val + 1])
    for b in range(BATCH):
        o_ref[b] = jnp.concatenate([outs[b, 0], outs[b, 1]], axis=0).T.astype(BF16)


def _na_variant(i):
    return jnp.where(i == 0, 0, jnp.where(i < NA_LAST_BAND + NA_BAND_TILES - 1, 1,
                                          jnp.where(i < CTX_TILE, 2, 3)))


def _na_attention(qt, k, vt, bias):
    vt_rows = vt.shape[2] // N_PAIRS

    def kspec(j):
        return pl.BlockSpec((BATCH, ROW_TILE, LANES), lambda p, i: (0, _na_band_start(i) + j, p))

    def vspec(j):
        return pl.BlockSpec((BATCH, 1, vt_rows, ROW_TILE),
                            lambda p, i: (0, _na_band_start(i) + j, p, 0))

    return pl.pallas_call(
        _na_kernel,
        out_shape=jax.ShapeDtypeStruct((BATCH, T_ALL, D_MODEL), BF16),
        grid=(N_PAIRS, N_TILES),
        in_specs=[pl.BlockSpec((BATCH, 1, LANES, ROW_TILE), lambda p, i: (0, i, p, 0)),
                  kspec(0), kspec(1), kspec(2),
                  pl.BlockSpec((BATCH, ROW_TILE, LANES), lambda p, i: (0, CTX_TILE, p)),
                  vspec(0), vspec(1), vspec(2),
                  pl.BlockSpec((BATCH, 1, vt_rows, ROW_TILE), lambda p, i: (0, CTX_TILE, p, 0)),
                  pl.BlockSpec((1, 2, NA_BAND_TILES * ROW_TILE, ROW_TILE),
                               lambda p, i: (_na_variant(i), p, 0, 0))],
        out_specs=pl.BlockSpec((BATCH, ROW_TILE, LANES), lambda p, i: (0, i, p)),
        scratch_shapes=[pltpu.VMEM((BATCH, 2, LANES, ROW_TILE), BF16)],
        compiler_params=_params(2),
        name="na_attention",
    )(qt, k, k, k, k, vt, vt, vt, vt, bias)


def _na_bias_table(rpb):
    rows = SEQ // GRID_W
    band_rows = NA_BAND_TILES * NA_Q_ROWS
    col = np.arange(GRID_W)
    cstart = np.clip(col - NA_COLS // 2, 0, GRID_W - NA_COLS)
    ok_c = (col[None, :] >= cstart[:, None]) & (col[None, :] < cstart[:, None] + NA_COLS)
    dc = col[None, :] - col[:, None] + NA_COLS - 1
    onehot_c = ((dc[:, :, None] == np.arange(2 * NA_COLS - 1)) & ok_c[:, :, None]).astype(np.float32)
    dr = np.full((4, NA_Q_ROWS, band_rows), 2 * NA_ROWS - 1, np.int32)
    ok_r = np.zeros((4, NA_Q_ROWS, band_rows), bool)
    for variant, i in enumerate((0, 1, SEQ // ROW_TILE - 1)):
        j = min(max(i - 1, 0), NA_LAST_BAND)
        r = NA_Q_ROWS * i + np.arange(NA_Q_ROWS)[:, None]
        k_row = NA_Q_ROWS * j + np.arange(band_rows)[None, :]
        rstart = np.clip(r - NA_ROWS // 2, 0, rows - NA_ROWS)
        ok_r[variant] = (k_row >= rstart) & (k_row < rstart + NA_ROWS)
        dr[variant] = np.where(ok_r[variant], k_row - r + NA_ROWS - 1, 2 * NA_ROWS - 1)
    by_col = jnp.einsum("hab,ckb->hack", rpb.astype(F32) * LOG2E, jnp.asarray(onehot_c),
                        precision=lax.Precision.HIGHEST)
    by_col = jnp.where(ok_c[None, None], by_col, MASK_VALUE)
    by_col = jnp.concatenate(
        [by_col, jnp.full((C_HEADS, 1, GRID_W, GRID_W), MASK_VALUE, F32)], axis=1)
    full = jnp.take(by_col, jnp.asarray(dr.reshape(-1)), axis=1)
    full = full.reshape(C_HEADS, 4, NA_Q_ROWS, band_rows, GRID_W, GRID_W)
    full = full.transpose(1, 0, 3, 5, 2, 4)
    return full.reshape(4, C_HEADS, NA_BAND_TILES * ROW_TILE, ROW_TILE)


def _post_kernel(o_ref, x_ref, xc_ref, mod_ref, g_ref, wo_ref, wgu_ref, wd_ref, xo_ref):
    m = mod_ref[0]
    y = jnp.dot(o_ref[0], wo_ref[...], preferred_element_type=F32)
    x1 = _stream_tile(x_ref, xc_ref) + m[2:3] * _rms(y, g_ref[1:2])
    h2 = (_rms(x1, g_ref[2:3]) * (1.0 + m[4:5]) + m[3:4]).astype(BF16)
    gu = jnp.dot(h2, wgu_ref[...], preferred_element_type=F32)
    gate, up = gu[:, :FFN_HIDDEN], gu[:, FFN_HIDDEN:]
    a = (gate * _sigmoid(gate) * up).astype(BF16)
    f = jnp.dot(a, wd_ref[...], preferred_element_type=F32)
    xo_ref[0] = x1 + m[5:6] * _rms(f, g_ref[3:4])


def _post(o, stream, mod, g_norm, wo, wgu, wd, *, n_tiles):
    return pl.pallas_call(
        _post_kernel,
        out_shape=jax.ShapeDtypeStruct((BATCH, n_tiles * ROW_TILE, D_MODEL), F32),
        grid=(BATCH, n_tiles),
        in_specs=[pl.BlockSpec((1, ROW_TILE, D_MODEL), lambda b, t: (b, t, 0))] + _row_specs(stream) + [
            _resident((4, D_MODEL)), _resident(wo.shape), _resident(wgu.shape),
            _resident(wd.shape)],
        out_specs=pl.BlockSpec((1, ROW_TILE, D_MODEL), lambda b, t: (b, t, 0)),
        compiler_params=_params(2),
        name="post_attention",
    )(o, *stream, mod, g_norm, wo, wgu, wd)


def _rope_angles(rot_dim):
    n_freq = rot_dim // 4
    inv = ROPE_THETA ** (-jnp.arange(n_freq, dtype=F32) / n_freq)
    t = jnp.arange(SEQ, dtype=jnp.int32)
    row = (t // GRID_W).astype(F32)
    col = (t % GRID_W).astype(F32)
    ang = jnp.concatenate([row[:, None] * inv, col[:, None] * inv], axis=-1)
    cos = jnp.concatenate([jnp.cos(ang), jnp.ones((CTX_LEN, rot_dim // 2), F32)], axis=0)
    sin = jnp.concatenate([jnp.sin(ang), jnp.zeros((CTX_LEN, rot_dim // 2), F32)], axis=0)
    return cos, sin


def _rope_tables_diff():
    cos, sin = _rope_angles(A_HEAD_DIM)
    c_lane = jnp.tile(cos, (1, 4))
    s_lane = jnp.tile(jnp.concatenate([-sin, sin], axis=1), (1, 2))
    return c_lane, s_lane, cos.T, sin.T


def _rope_tables_mla():
    cos, sin = _rope_angles(B_ROPE)
    ones = jnp.ones((T_ALL, B_NOPE), F32)
    zeros = jnp.zeros((T_ALL, B_NOPE), F32)
    tail = LANES - B_NOPE - B_ROPE
    c_lane = jnp.concatenate([ones, cos, cos, ones[:, :tail]], axis=1)
    s_lane = jnp.concatenate([zeros, -sin, sin, zeros[:, :tail]], axis=1)
    return c_lane, s_lane, cos.T, sin.T


def kernel(x, c, ctx, c_ctx, l0_w_mod, l0_b_mod, l0_g_norm, l0_w_gu, l0_w_down, l0_a_w_qkv, l0_a_w_o, l0_a_lam, l0_a_g_sub, l1_w_mod, l1_b_mod, l1_g_norm, l1_w_gu, l1_w_down, l1_b_w_in, l1_b_g_q, l1_b_g_kv, l1_b_w_uq, l1_b_w_ukv, l1_b_w_o, l2_w_mod, l2_b_mod, l2_g_norm, l2_w_gu, l2_w_down, l2_c_w_qkv, l2_c_rpb, l2_c_w_o, l3_w_mod, l3_b_mod, l3_g_norm, l3_w_gu, l3_w_down, l3_a_w_qkv, l3_a_w_o, l3_a_lam, l3_a_g_sub):
    common = [
        (l0_w_mod, l0_b_mod, l0_g_norm, l0_w_gu, l0_w_down),
        (l1_w_mod, l1_b_mod, l1_g_norm, l1_w_gu, l1_w_down),
        (l2_w_mod, l2_b_mod, l2_g_norm, l2_w_gu, l2_w_down),
        (l3_w_mod, l3_b_mod, l3_g_norm, l3_w_gu, l3_w_down),
    ]
    diff_params = {0: (l0_a_w_qkv, l0_a_w_o, l0_a_lam, l0_a_g_sub),
                   3: (l3_a_w_qkv, l3_a_w_o, l3_a_lam, l3_a_g_sub)}

    stream = (x, ctx)
    cc = jnp.concatenate([c, c_ctx[None, :], jnp.zeros((8 - BATCH - 1, D_MODEL), F32)], axis=0)
    tabs_a = _rope_tables_diff()
    tabs_b = _rope_tables_mla()
    dummy_lam = jnp.zeros((4, A_HEAD_DIM), F32)
    dummy_gsub = jnp.ones((1, LANES), F32)

    for i in range(DEPTH):
        w_mod, b_mod, g_norm, w_gu, w_down = common[i]
        last = i == DEPTH - 1
        mod = _modulation(cc, w_mod, b_mod).reshape(8, 6, D_MODEL)
        kind = i % 3
        if kind == 0:
            w_qkv, w_o, lam, g_sub = diff_params[i]
            lam_init = 0.8 - 0.6 * math.exp(-0.3 * i)
            wqt = w_qkv[:, :D_MODEL].T.astype(BF16)
            wk = w_qkv[:, D_MODEL:2 * D_MODEL].astype(BF16)
            wvt = w_qkv[:, 2 * D_MODEL:].T.astype(BF16)
            qt, k, vt = _proj_qkv(stream, mod, g_norm, wqt, wk, wvt, tabs_a, rope=True,
                                  qscale=A_HEAD_DIM ** -0.5 * LOG2E, v_rows=2 * A_HEAD_DIM)
            o = _flash(qt, k, vt, lam, g_sub.reshape(1, LANES), diff=True, lam_init=lam_init)
        elif kind == 1:
            w_in, g_q, g_kv, w_uq, w_ukv, w_o = (l1_b_w_in, l1_b_g_q, l1_b_g_kv, l1_b_w_uq,
                                                 l1_b_w_ukv, l1_b_w_o)
            n_lora = B_Q_LORA + B_KV_LORA
            pad = LANES - B_NOPE - B_ROPE
            win = w_in[:, :n_lora].astype(BF16)
            wkr = jnp.pad(w_in[:, n_lora:], ((0, 0), (B_NOPE, pad))).astype(BF16)
            wuq = w_uq.reshape(B_Q_LORA, B_HEADS, B_NOPE + B_ROPE)
            wuq = jnp.pad(wuq, ((0, 0), (0, 0), (0, pad)))
            wuqt = wuq.reshape(B_Q_LORA, B_HEADS * LANES).T.astype(BF16)
            wukv = w_ukv.reshape(B_KV_LORA, B_HEADS, B_NOPE + B_VDIM)
            wukn = jnp.pad(wukv[:, :, :B_NOPE], ((0, 0), (0, 0), (0, LANES - B_NOPE)))
            wukn = wukn.reshape(B_KV_LORA, B_HEADS * LANES).astype(BF16)
            wuvt = wukv[:, :, B_NOPE:].reshape(B_KV_LORA, B_HEADS * B_VDIM).T.astype(BF16)
            qt, k, vt = _proj_mla(stream, mod, g_norm, win, wkr, g_q.reshape(1, -1),
                                  g_kv.reshape(1, -1), wuqt, wukn, wuvt, tabs_b,
                                  qscale=(B_NOPE + B_ROPE) ** -0.5 * LOG2E)
            o = _flash(qt, k, vt, dummy_lam, dummy_gsub, diff=False, lam_init=0.0)
        else:
            w_qkv, rpb, w_o = l2_c_w_qkv, l2_c_rpb, l2_c_w_o
            wqt = w_qkv[:, :D_MODEL].T.astype(BF16)
            wk = w_qkv[:, D_MODEL:2 * D_MODEL].astype(BF16)
            wvt = w_qkv[:, 2 * D_MODEL:].T.astype(BF16)
            qt, k, vt = _proj_qkv(stream, mod, g_norm, wqt, wk, wvt, tabs_a, rope=False,
                                  qscale=C_HEAD_DIM ** -0.5 * LOG2E, v_rows=C_HEAD_DIM)
            o = _na_attention(qt, k, vt, _na_bias_table(rpb))
        xs = _post(o, stream, mod, g_norm, w_o.astype(BF16), w_gu.astype(BF16),
                   w_down.astype(BF16), n_tiles=N_TILES - 1 if last else N_TILES)
        stream = (xs, xs)
    return xs
```

```python
import functools
import math

import numpy as np
import jax
import jax.numpy as jnp
from jax import lax
from jax.experimental import pallas as pl
from jax.experimental.pallas import tpu as pltpu

D_MODEL = 1024
BATCH = 4
SEQ = 8192
DEPTH = 4
GRID_W = 64
CTX_LEN = 256
ROPE_THETA = 10000.0
NORM_EPS = 1e-6

A_HEAD_DIM = 64
A_HEADS = 8
B_HEADS = 16
B_NOPE = 64
B_ROPE = 32
B_VDIM = 64
B_Q_LORA = 256
B_KV_LORA = 256
C_HEAD_DIM = 64
C_HEADS = 16
NA_ROWS = 8
NA_COLS = 16
FFN_HIDDEN = 2816

LANES = 128
STRIP = 64
ROW_TILE = 256
T_ALL = SEQ + CTX_LEN
N_TILES = T_ALL // ROW_TILE
CTX_TILE = N_TILES - 1
N_PAIRS = 8
KEY_GROUPS = tuple((3 * g, 3) for g in range(9)) + tuple((27 + 2 * g, 2) for g in range(3))
SCORE_TILES = 2
PROB_LEAD = 1
SCORE_LEAD = 2
V_ONES = 16
LOG2E = math.log2(math.e)
MASK_VALUE = -1e30
SAFE_EXPONENT = 96.0
NORM_SLACK = 1.02
VMEM_LIMIT = 60 * 1024 * 1024

F32 = jnp.float32
BF16 = jnp.bfloat16
NT_DIMS = (((1,), (1,)), ((), ()))


def _rms(x, g):
    return x * lax.rsqrt(jnp.mean(x * x, axis=-1, keepdims=True) + NORM_EPS) * g


def _sigmoid(x):
    return 1.0 / (1.0 + jnp.exp(-x))


def _params(n_axes):
    return pltpu.CompilerParams(
        dimension_semantics=("arbitrary",) * n_axes, vmem_limit_bytes=VMEM_LIMIT)


def _resident(shape):
    zeros = (0,) * len(shape)
    return pl.BlockSpec(shape, lambda *_: zeros, pipeline_mode=pl.Buffered(1))


def _mod_row(b, t):
    return jnp.where(t == CTX_TILE, BATCH, b)


def _mod_kernel(c_ref, w_ref, b_ref, o_ref):
    c = c_ref[...]
    a = (c * _sigmoid(c)).astype(BF16)
    o_ref[...] = jnp.dot(a, w_ref[...].astype(BF16), preferred_element_type=F32) + b_ref[...]


def _modulation(cc, w_mod, b_mod):
    n = w_mod.shape[1]
    blk = D_MODEL
    return pl.pallas_call(
        _mod_kernel,
        out_shape=jax.ShapeDtypeStruct((8, n), F32),
        grid=(n // blk,),
        in_specs=[pl.BlockSpec((8, D_MODEL), lambda j: (0, 0)),
                  pl.BlockSpec((D_MODEL, blk), lambda j: (0, j)),
                  pl.BlockSpec((1, blk), lambda j: (0, j))],
        out_specs=pl.BlockSpec((8, blk), lambda j: (0, j)),
        compiler_params=_params(1),
        name="modulation",
    )(cc, w_mod, b_mod.reshape(1, n))


def _rope_lanes(k, c, s, first, half):
    rot = jnp.where(first, pltpu.roll(k, LANES - half, 1), pltpu.roll(k, half, 1))
    return k * c + rot * s


def _store_values(vt_ref, vt, rows):
    ones = jnp.ones((V_ONES, ROW_TILE), BF16)
    for n in range(vt.shape[0] // rows):
        r0 = n * (rows + V_ONES)
        vt_ref[0, 0, r0:r0 + rows, :] = vt[n * rows:(n + 1) * rows].astype(BF16)
        vt_ref[0, 0, r0 + rows:r0 + rows + V_ONES, :] = ones


def _proj_qkv_kernel(x_ref, xc_ref, mod_ref, g_ref, wqt_ref, wk_ref, wvt_ref,
                     ck_ref, sk_ref, cqt_ref, sqt_ref,
                     qt_ref, k_ref, vt_ref, *, rope, qscale, v_rows):
    m = mod_ref[0]
    h = (_rms(_stream_tile(x_ref, xc_ref), g_ref[0:1]) * (1.0 + m[1:2]) + m[0:1]).astype(BF16)
    qt = lax.dot_general(wqt_ref[...], h, NT_DIMS, preferred_element_type=F32) * qscale
    k = jnp.dot(h, wk_ref[...], preferred_element_type=F32)
    vt = lax.dot_general(wvt_ref[...], h, NT_DIMS, preferred_element_type=F32)
    _store_values(vt_ref, vt, v_rows)
    if not rope:
        qt_ref[0, 0] = qt.astype(BF16)
        k_ref[0] = k.astype(BF16)
        return
    half = A_HEAD_DIM // 2
    c, s = ck_ref[...], sk_ref[...]
    lane = lax.broadcasted_iota(jnp.int32, (ROW_TILE, LANES), 1)
    first = (lane % A_HEAD_DIM) < half
    for p in range(D_MODEL // LANES):
        sl = slice(p * LANES, (p + 1) * LANES)
        k_ref[0, :, sl] = _rope_lanes(k[:, sl], c, s, first, half).astype(BF16)
    ct, st = cqt_ref[...], sqt_ref[...]
    for g in range(D_MODEL // A_HEAD_DIM):
        r0 = g * A_HEAD_DIM
        t1, t2 = qt[r0:r0 + half], qt[r0 + half:r0 + 2 * half]
        qt_ref[0, 0, r0:r0 + half, :] = (t1 * ct - t2 * st).astype(BF16)
        qt_ref[0, 0, r0 + half:r0 + 2 * half, :] = (t1 * st + t2 * ct).astype(BF16)


def _row_specs(stream):
    last_lat, ctx_tile = (a.shape[1] // ROW_TILE - 1 for a in stream)
    return [pl.BlockSpec((1, ROW_TILE, D_MODEL), lambda b, t: (b, jnp.minimum(t, last_lat), 0)),
            pl.BlockSpec((1, ROW_TILE, D_MODEL), lambda b, t: (b, ctx_tile, 0)),
            pl.BlockSpec((1, 6, D_MODEL), lambda b, t: (_mod_row(b, t), 0, 0))]


def _stream_tile(x_ref, xc_ref):
    return jnp.where(pl.program_id(1) == CTX_TILE, xc_ref[0], x_ref[0])


def _rope_specs(half):
    return [pl.BlockSpec((ROW_TILE, LANES), lambda b, t: (t, 0)),
            pl.BlockSpec((ROW_TILE, LANES), lambda b, t: (t, 0)),
            pl.BlockSpec((half, ROW_TILE), lambda b, t: (0, t)),
            pl.BlockSpec((half, ROW_TILE), lambda b, t: (0, t))]


def _qkv_out(width, v_rows):
    vt_rows = D_MODEL // v_rows * (v_rows + V_ONES)
    shapes = (jax.ShapeDtypeStruct((BATCH, N_TILES, width, ROW_TILE), BF16),
              jax.ShapeDtypeStruct((BATCH, T_ALL, width), BF16),
              jax.ShapeDtypeStruct((BATCH, N_TILES, vt_rows, ROW_TILE), BF16))
    specs = (pl.BlockSpec((1, 1, width, ROW_TILE), lambda b, t: (b, t, 0, 0)),
             pl.BlockSpec((1, ROW_TILE, width), lambda b, t: (b, t, 0)),
             pl.BlockSpec((1, 1, vt_rows, ROW_TILE), lambda b, t: (b, t, 0, 0)))
    return shapes, specs


def _proj_qkv(stream, mod, g_norm, wqt, wk, wvt, tabs, *, rope, qscale, v_rows):
    shapes, specs = _qkv_out(D_MODEL, v_rows)
    return pl.pallas_call(
        functools.partial(_proj_qkv_kernel, rope=rope, qscale=qscale, v_rows=v_rows),
        out_shape=shapes,
        grid=(BATCH, N_TILES),
        in_specs=_row_specs(stream) + [
            _resident((4, D_MODEL)),
            _resident((D_MODEL, D_MODEL)), _resident((D_MODEL, D_MODEL)),
            _resident((D_MODEL, D_MODEL))] + _rope_specs(tabs[2].shape[0]),
        out_specs=specs,
        compiler_params=_params(2),
        name="proj_qkv",
    )(*stream, mod, g_norm, wqt, wk, wvt, *tabs)


def _proj_mla_kernel(x_ref, xc_ref, mod_ref, g_ref, win_ref, wkr_ref, gq_ref, gkv_ref,
                     wuqt_ref, wukn_ref, wuvt_ref, ck_ref, sk_ref, cqt_ref, sqt_ref,
                     qt_ref, k_ref, vt_ref, *, qscale):
    m = mod_ref[0]
    h = (_rms(_stream_tile(x_ref, xc_ref), g_ref[0:1]) * (1.0 + m[1:2]) + m[0:1]).astype(BF16)
    z = jnp.dot(h, win_ref[...], preferred_element_type=F32)
    cq = _rms(z[:, :B_Q_LORA], gq_ref[...]).astype(BF16)
    ckv = _rms(z[:, B_Q_LORA:], gkv_ref[...]).astype(BF16)
    half = B_ROPE // 2

    qt = lax.dot_general(wuqt_ref[...], cq, NT_DIMS, preferred_element_type=F32) * qscale
    ct, st = cqt_ref[...], sqt_ref[...]
    for hh in range(B_HEADS):
        r0, r1 = hh * LANES, hh * LANES + B_NOPE
        qt_ref[0, 0, r0:r1, :] = qt[r0:r1].astype(BF16)
        t1, t2 = qt[r1:r1 + half], qt[r1 + half:r1 + 2 * half]
        qt_ref[0, 0, r1:r1 + half, :] = (t1 * ct - t2 * st).astype(BF16)
        qt_ref[0, 0, r1 + half:r1 + 2 * half, :] = (t1 * st + t2 * ct).astype(BF16)
        qt_ref[0, 0, r1 + B_ROPE:r0 + LANES, :] = qt[r1 + B_ROPE:r0 + LANES].astype(BF16)

    kr = jnp.dot(h, wkr_ref[...], preferred_element_type=F32)
    lane = lax.broadcasted_iota(jnp.int32, (ROW_TILE, LANES), 1)
    first = (lane >= B_NOPE) & (lane < B_NOPE + half)
    kr = _rope_lanes(kr, ck_ref[...], sk_ref[...], first, half)
    kn = jnp.dot(ckv, wukn_ref[...], preferred_element_type=F32)
    for hh in range(B_HEADS):
        sl = slice(hh * LANES, (hh + 1) * LANES)
        k_ref[0, :, sl] = (kn[:, sl] + kr).astype(BF16)

    vt = lax.dot_general(wuvt_ref[...], ckv, NT_DIMS, preferred_element_type=F32)
    _store_values(vt_ref, vt, B_VDIM)


def _proj_mla(stream, mod, g_norm, win, wkr, gq, gkv, wuqt, wukn, wuvt, tabs, *, qscale):
    shapes, specs = _qkv_out(B_HEADS * LANES, B_VDIM)
    return pl.pallas_call(
        functools.partial(_proj_mla_kernel, qscale=qscale),
        out_shape=shapes,
        grid=(BATCH, N_TILES),
        in_specs=_row_specs(stream) + [
            _resident((4, D_MODEL)),
            _resident(win.shape), _resident(wkr.shape),
            _resident((1, B_Q_LORA)), _resident((1, B_KV_LORA)),
            _resident(wuqt.shape), _resident(wukn.shape), _resident(wuvt.shape)]
            + _rope_specs(tabs[2].shape[0]),
        out_specs=specs,
        compiler_params=_params(2),
        name="proj_mla",
    )(*stream, mod, g_norm, win, wkr, gq, gkv, wuqt, wukn, wuvt, *tabs)


def _value_rows(diff):
    if diff:
        rows = 2 * A_HEAD_DIM
        return ((0, rows + V_ONES), (0, rows + V_ONES)), rows
    rows = LANES // 2
    return ((0, rows + V_ONES), (rows + V_ONES, 2 * (rows + V_ONES))), rows


def _split_queries(qt, qs_ref):
    dq = qt.shape[0]
    if dq == 2 * LANES:
        qs_ref[0] = qt[:LANES]
        qs_ref[1] = qt[LANES:]
        return
    row = lax.broadcasted_iota(jnp.int32, qt.shape, 0)
    zero = jnp.zeros_like(qt)
    qs_ref[0] = jnp.where(row < dq // 2, qt, zero)
    qs_ref[1] = jnp.where(row >= dq // 2, qt, zero)


def _slot_keys(k_ref, rows, slot):
    if k_ref.shape[-1] == 2 * LANES:
        return k_ref[0, rows, slot * LANES:(slot + 1) * LANES]
    return k_ref[0, rows, :]


def _slot_queries(qt, slot):
    half = qt.shape[0] // 2
    return qt[slot * half:(slot + 1) * half]


def _probabilities(s, m):
    return jnp.exp2(s - m).astype(BF16)


def _flash_kernel(lam_ref, gsub_ref, qt_ref, k_ref, vt_ref, o_ref,
                  qs_ref, c_ref, s0_ref, s1_ref, p0_ref, p1_ref, p2_ref, a0_ref, a1_ref, a2_ref,
                  m_ref, acc_ref, fin_ref, *, diff, lam_init):
    v_rows, n_val = _value_rows(diff)
    s_bufs, p_bufs, a_bufs = (s0_ref, s1_ref), (p0_ref, p1_ref, p2_ref), (a0_ref, a1_ref, a2_ref)
    n_groups = len(KEY_GROUPS)

    def tile_rows(j):
        return slice(j * ROW_TILE, (j + 1) * ROW_TILE)

    def scores(t0, n, slot, s_ref):
        for j in range(0, n, SCORE_TILES):
            rows = min(SCORE_TILES, n - j) * ROW_TILE
            k = _slot_keys(k_ref, slice((t0 + j) * ROW_TILE, (t0 + j) * ROW_TILE + rows), slot)
            s_ref[slot, j * ROW_TILE:j * ROW_TILE + rows, :] = jnp.dot(
                k, qs_ref[slot], preferred_element_type=F32)

    def stats(n, s_ref, a_ref, first):
        for slot in range(2):
            m_prev = jnp.full((1, ROW_TILE), MASK_VALUE, F32) if first else m_ref[slot]
            strips = [s_ref[slot, c * STRIP:(c + 1) * STRIP, :] for c in range(n * ROW_TILE // STRIP)]
            m_next = jnp.maximum(
                m_prev, jnp.max(functools.reduce(jnp.maximum, strips), axis=0, keepdims=True))
            a_ref[slot] = jnp.exp2(m_prev - m_next)
            m_ref[slot] = m_next

    def probs_tile(j, s_ref, p_ref):
        for slot in range(2):
            m = m_ref[slot]
            for c in range(ROW_TILE // STRIP):
                rows = slice(j * ROW_TILE + c * STRIP, j * ROW_TILE + (c + 1) * STRIP)
                p_ref[slot, rows, :] = _probabilities(s_ref[slot, rows, :], m)

    def values_tile(t, j, p_ref, pv):
        for slot in range(2):
            lo, hi = v_rows[slot]
            pv[slot].append(jnp.dot(vt_ref[0, t, lo:hi, :], p_ref[slot, tile_rows(j), :],
                                    preferred_element_type=F32))

    def accumulate(pv, a_ref):
        for slot in range(2):
            acc_ref[slot] = a_ref[slot] * acc_ref[slot] + functools.reduce(jnp.add, pv[slot])

    def finalize(i, src_ref):
        acc_a, acc_b = src_ref[0], src_ref[1]
        ot_a = acc_a[:n_val] * (1.0 / acc_a[n_val:n_val + 1])
        ot_b = acc_b[:n_val] * (1.0 / acc_b[n_val:n_val + 1])
        if diff:
            lv = lam_ref[...]
            lam = (jnp.exp(jnp.sum(lv[0:1] * lv[1:2], axis=1, keepdims=True))
                   - jnp.exp(jnp.sum(lv[2:3] * lv[3:4], axis=1, keepdims=True)) + lam_init)
            o = (ot_a - lam * ot_b).T
            o = _rms(o, gsub_ref[...]) * (1.0 - lam_init)
        else:
            o = jnp.concatenate([ot_a, ot_b], axis=0).T
        o_ref[0, pl.ds(pl.multiple_of(i * ROW_TILE, ROW_TILE), ROW_TILE), :] = o.astype(BF16)

    def exact_path():
        acc_ref[...] = jnp.zeros(acc_ref.shape, F32)
        fin_ref[...] = jnp.ones(fin_ref.shape, F32)

        def scores_group(g, s_ref):
            for slot in range(2):
                scores(KEY_GROUPS[g][0], KEY_GROUPS[g][1], slot, s_ref)

        def probs_group(g, s_ref, p_ref, a_ref, first):
            stats(KEY_GROUPS[g][1], s_ref, a_ref, first)
            for j in range(KEY_GROUPS[g][1]):
                probs_tile(j, s_ref, p_ref)

        _split_queries(qt_ref[0, 0], qs_ref)
        for fill in range(-SCORE_LEAD, 0):
            g_prob, g_sco = fill + PROB_LEAD, fill + SCORE_LEAD
            if g_prob >= 0:
                probs_group(g_prob, s_bufs[g_prob % 2], p_bufs[g_prob % 3], a_bufs[g_prob % 3],
                            g_prob == 0)
            scores_group(g_sco, s_bufs[g_sco % 2])

        def body(i, carry):
            for g in range(n_groups):
                g_prob, g_sco = (g + PROB_LEAD) % n_groups, (g + SCORE_LEAD) % n_groups
                t_val, n_val_tiles = KEY_GROUPS[g]
                t_sco, n_sco = KEY_GROUPS[g_sco]
                n_prob = KEY_GROUPS[g_prob][1]
                s_prob, p_prob, a_prob = s_bufs[g_prob % 2], p_bufs[g_prob % 3], a_bufs[g_prob % 3]
                stats(n_prob, s_prob, a_prob, g_prob == 0)
                if g + SCORE_LEAD == n_groups:
                    _split_queries(qt_ref[0, i + 1], qs_ref)
                pv = ([], [])
                for j in range(max(n_val_tiles, n_prob)):
                    if j < n_prob:
                        probs_tile(j, s_prob, p_prob)
                    if j < n_val_tiles:
                        values_tile(t_val + j, j, p_bufs[g % 3], pv)
                    if j < 2:
                        scores(t_sco, n_sco, j, s_bufs[g_sco % 2])
                accumulate(pv, a_bufs[g % 3])
                if g == 0:
                    finalize(jnp.maximum(i - 1, 0), fin_ref)
            fin_ref[...] = acc_ref[...]
            return carry
        lax.fori_loop(0, CTX_TILE, body, 0)
        finalize(CTX_TILE - 1, fin_ref)

        for slot in range(2):
            scores(CTX_TILE, 1, slot, s0_ref)
        stats(1, s0_ref, a0_ref, True)
        probs_tile(0, s0_ref, p0_ref)
        pv = ([], [])
        values_tile(CTX_TILE, 0, p0_ref, pv)
        accumulate(pv, a0_ref)
        finalize(CTX_TILE, acc_ref)

    def key_norms(slot, heads):
        width = LANES // heads
        r = lax.broadcasted_iota(jnp.int32, (LANES, LANES), 0)
        c = lax.broadcasted_iota(jnp.int32, (LANES, LANES), 1)
        same_head = ((r // width) == (c // width)).astype(BF16)

        def step(t, mx):
            rows = pl.ds(pl.multiple_of(t * ROW_TILE, ROW_TILE), ROW_TILE)
            k = _slot_keys(k_ref, rows, slot).astype(F32)
            return jnp.maximum(
                mx, jnp.dot((k * k).astype(BF16), same_head, preferred_element_type=F32))
        mx = lax.fori_loop(0, N_TILES, step, jnp.zeros((ROW_TILE, LANES), F32), unroll=3)
        return [jnp.sqrt(jnp.max(mx[:, h * width:h * width + 1], axis=0, keepdims=True)
                         * NORM_SLACK) for h in range(heads)]

    def query_norm(slot):
        def step(t, mx):
            q = _slot_queries(qt_ref[0, t], slot).astype(F32)
            return jnp.maximum(mx, jnp.sum(q * q, axis=0, keepdims=True))
        mx = lax.fori_loop(0, N_TILES, step, jnp.zeros((1, ROW_TILE), F32), unroll=3)
        return jnp.sqrt(jnp.max(mx, axis=1, keepdims=True))

    def bounded_path(k_norms):
        def split(i):
            _split_queries(qt_ref[0, i], qs_ref)
            for slot in range(2):
                q = qs_ref[slot].astype(F32)
                c_ref[slot] = jnp.sqrt(jnp.sum(q * q, axis=0, keepdims=True)) * k_norms[slot] + 1.0

        def probabilities(t0, n, slot, p_ref):
            c = c_ref[slot]
            for j in range(0, n, SCORE_TILES):
                rows = min(SCORE_TILES, n - j) * ROW_TILE
                k = _slot_keys(k_ref, slice((t0 + j) * ROW_TILE, (t0 + j) * ROW_TILE + rows), slot)
                s = jnp.dot(k, qs_ref[slot], preferred_element_type=F32)
                p_ref[slot, j * ROW_TILE:j * ROW_TILE + rows, :] = _probabilities(s, c)

        def step(g, p_ref, next_probabilities):
            pv = ([], [])
            scores_first = k_ref.shape[-1] == LANES
            for j in range(KEY_GROUPS[g][1]):
                if scores_first and j < len(next_probabilities):
                    next_probabilities[j]()
                values_tile(KEY_GROUPS[g][0] + j, j, p_ref, pv)
                if not scores_first and j < len(next_probabilities):
                    next_probabilities[j]()
            for slot in range(2):
                total = functools.reduce(jnp.add, pv[slot])
                acc_ref[slot] = total if g == 0 else acc_ref[slot] + total

        fin_ref[...] = jnp.ones(fin_ref.shape, F32)
        split(0)
        for slot in range(2):
            probabilities(KEY_GROUPS[0][0], KEY_GROUPS[0][1], slot, p_bufs[0])

        def body(i, carry):
            for g in range(n_groups):
                g_next = (g + 1) % n_groups
                if g_next == 0:
                    split(i + 1)
                step(g, p_bufs[g % 2],
                     [functools.partial(probabilities, KEY_GROUPS[g_next][0], KEY_GROUPS[g_next][1],
                                        slot, p_bufs[g_next % 2]) for slot in range(2)])
                if g == 0:
                    finalize(jnp.maximum(i - 1, 0), fin_ref)
            fin_ref[...] = acc_ref[...]
            return carry
        lax.fori_loop(0, CTX_TILE, body, 0)
        finalize(CTX_TILE - 1, fin_ref)

        for slot in range(2):
            probabilities(CTX_TILE, 1, slot, p_bufs[0])
        pv = ([], [])
        values_tile(CTX_TILE, 0, p_bufs[0], pv)
        for slot in range(2):
            acc_ref[slot] = pv[slot][0]
        finalize(CTX_TILE, acc_ref)

    if k_ref.shape[-1] == LANES:
        k_norms = key_norms(0, 2)
    else:
        k_norms = [key_norms(slot, 1)[0] for slot in range(2)]
    widest = functools.reduce(jnp.maximum, [2.0 * (k_norms[slot] * query_norm(slot) + 1.0)
                                            for slot in range(2)])[0, 0]

    @pl.when(widest <= SAFE_EXPONENT)
    def _():
        bounded_path(k_norms)

    @pl.when(jnp.logical_not(widest <= SAFE_EXPONENT))
    def _():
        exact_path()


def _flash(qt, k, vt, lam, gsub, *, diff, lam_init):
    dq = k.shape[-1] // N_PAIRS
    vt_rows = vt.shape[2] // N_PAIRS
    acc_rows = _value_rows(diff)[0][0][1]
    group_rows = max(n for _, n in KEY_GROUPS) * ROW_TILE
    return pl.pallas_call(
        functools.partial(_flash_kernel, diff=diff, lam_init=lam_init),
        out_shape=jax.ShapeDtypeStruct((BATCH, T_ALL, D_MODEL), BF16),
        grid=(BATCH, N_PAIRS),
        in_specs=[pl.BlockSpec(lam.shape, lambda b, p: (0, 0)),
                  pl.BlockSpec(gsub.shape, lambda b, p: (0, 0)),
                  pl.BlockSpec((1, N_TILES, dq, ROW_TILE), lambda b, p: (b, 0, p, 0)),
                  pl.BlockSpec((1, T_ALL, dq), lambda b, p: (b, 0, p)),
                  pl.BlockSpec((1, N_TILES, vt_rows, ROW_TILE), lambda b, p: (b, 0, p, 0))],
        out_specs=pl.BlockSpec((1, T_ALL, LANES), lambda b, p: (b, 0, p)),
        scratch_shapes=[pltpu.VMEM((2, LANES, ROW_TILE), BF16),
                        pltpu.VMEM((2, 1, ROW_TILE), F32),
                        pltpu.VMEM((2, group_rows, ROW_TILE), F32),
                        pltpu.VMEM((2, group_rows, ROW_TILE), F32),
                        pltpu.VMEM((2, group_rows, ROW_TILE), BF16),
                        pltpu.VMEM((2, group_rows, ROW_TILE), BF16),
                        pltpu.VMEM((2, group_rows, ROW_TILE), BF16),
                        pltpu.VMEM((2, 1, ROW_TILE), F32),
                        pltpu.VMEM((2, 1, ROW_TILE), F32),
                        pltpu.VMEM((2, 1, ROW_TILE), F32),
                        pltpu.VMEM((2, 1, ROW_TILE), F32),
                        pltpu.VMEM((2, acc_rows, ROW_TILE), F32),
                        pltpu.VMEM((2, acc_rows, ROW_TILE), F32)],
        compiler_params=_params(2),
        name="flash_diff" if diff else "flash_pair",
    )(lam, gsub, qt, k, vt)


NA_Q_ROWS = ROW_TILE // GRID_W
NA_BAND_TILES = 3
NA_LAST_BAND = SEQ // ROW_TILE - NA_BAND_TILES


def _na_band_start(i):
    return jnp.clip(i - 1, 0, NA_LAST_BAND)


def _na_kernel(qt_ref, k0_ref, k1_ref, k2_ref, kc_ref, v0_ref, v1_ref, v2_ref, vc_ref,
               bias_ref, o_ref, qs_ref):
    k_refs = (k0_ref, k1_ref, k2_ref)
    v_refs = (v0_ref, v1_ref, v2_ref, vc_ref)
    v_rows, n_val = _value_rows(False)
    chains = [(b, slot) for b in range(BATCH) for slot in range(2)]
    for b in range(BATCH):
        _split_queries(qt_ref[b, 0], qs_ref.at[b])
    def chain_scores(b, slot):
        qs = qs_ref[b, slot]
        s = [jnp.dot(k_refs[j][b], qs, preferred_element_type=F32)
             + bias_ref[0, slot, j * ROW_TILE:(j + 1) * ROW_TILE, :]
             for j in range(NA_BAND_TILES)]
        s.append(jnp.dot(kc_ref[b], qs, preferred_element_type=F32))
        return s

    outs = {}
    s_next = chain_scores(*chains[0])
    for n, (b, slot) in enumerate(chains):
        s = s_next
        if n + 1 < len(chains):
            s_next = chain_scores(*chains[n + 1])
        lo, hi = v_rows[slot]
        m = jnp.max(functools.reduce(jnp.maximum, s), axis=0, keepdims=True)
        acc = functools.reduce(jnp.add, [
            jnp.dot(r[b, 0, lo:hi, :], _probabilities(sj, m), preferred_element_type=F32)
            for r, sj in zip(v_refs, s)])
        outs[b, slot] = acc[:n_val] * (1.0 / acc[n_val:n_val + 1])
    for b in range(BATCH):
        o_ref[b] = jnp.concatenate([outs[b, 0], outs[b, 1]], axis=0).T.astype(BF16)


def _na_variant(i):
    return jnp.where(i == 0, 0, jnp.where(i < NA_LAST_BAND + NA_BAND_TILES - 1, 1,
                                          jnp.where(i < CTX_TILE, 2, 3)))


def _na_attention(qt, k, vt, bias):
    vt_rows = vt.shape[2] // N_PAIRS

    def kspec(j):
        return pl.BlockSpec((BATCH, ROW_TILE, LANES), lambda p, i: (0, _na_band_start(i) + j, p))

    def vspec(j):
        return pl.BlockSpec((BATCH, 1, vt_rows, ROW_TILE),
                            lambda p, i: (0, _na_band_start(i) + j, p, 0))

    return pl.pallas_call(
        _na_kernel,
        out_shape=jax.ShapeDtypeStruct((BATCH, T_ALL, D_MODEL), BF16),
        grid=(N_PAIRS, N_TILES),
        in_specs=[pl.BlockSpec((BATCH, 1, LANES, ROW_TILE), lambda p, i: (0, i, p, 0)),
                  kspec(0), kspec(1), kspec(2),
                  pl.BlockSpec((BATCH, ROW_TILE, LANES), lambda p, i: (0, CTX_TILE, p)),
                  vspec(0), vspec(1), vspec(2),
                  pl.BlockSpec((BATCH, 1, vt_rows, ROW_TILE), lambda p, i: (0, CTX_TILE, p, 0)),
                  pl.BlockSpec((1, 2, NA_BAND_TILES * ROW_TILE, ROW_TILE),
                               lambda p, i: (_na_variant(i), p, 0, 0))],
        out_specs=pl.BlockSpec((BATCH, ROW_TILE, LANES), lambda p, i: (0, i, p)),
        scratch_shapes=[pltpu.VMEM((BATCH, 2, LANES, ROW_TILE), BF16)],
        compiler_params=_params(2),
        name="na_attention",
    )(qt, k, k, k, k, vt, vt, vt, vt, bias)


def _na_bias_table(rpb):
    rows = SEQ // GRID_W
    band_rows = NA_BAND_TILES * NA_Q_ROWS
    col = np.arange(GRID_W)
    cstart = np.clip(col - NA_COLS // 2, 0, GRID_W - NA_COLS)
    ok_c = (col[None, :] >= cstart[:, None]) & (col[None, :] < cstart[:, None] + NA_COLS)
    dc = col[None, :] - col[:, None] + NA_COLS - 1
    onehot_c = ((dc[:, :, None] == np.arange(2 * NA_COLS - 1)) & ok_c[:, :, None]).astype(np.float32)
    dr = np.full((4, NA_Q_ROWS, band_rows), 2 * NA_ROWS - 1, np.int32)
    ok_r = np.zeros((4, NA_Q_ROWS, band_rows), bool)
    for variant, i in enumerate((0, 1, SEQ // ROW_TILE - 1)):
        j = min(max(i - 1, 0), NA_LAST_BAND)
        r = NA_Q_ROWS * i + np.arange(NA_Q_ROWS)[:, None]
        k_row = NA_Q_ROWS * j + np.arange(band_rows)[None, :]
        rstart = np.clip(r - NA_ROWS // 2, 0, rows - NA_ROWS)
        ok_r[variant] = (k_row >= rstart) & (k_row < rstart + NA_ROWS)
        dr[variant] = np.where(ok_r[variant], k_row - r + NA_ROWS - 1, 2 * NA_ROWS - 1)
    by_col = jnp.einsum("hab,ckb->hack", rpb.astype(F32) * LOG2E, jnp.asarray(onehot_c),
                        precision=lax.Precision.HIGHEST)
    by_col = jnp.where(ok_c[None, None], by_col, MASK_VALUE)
    by_col = jnp.concatenate(
        [by_col, jnp.full((C_HEADS, 1, GRID_W, GRID_W), MASK_VALUE, F32)], axis=1)
    full = jnp.take(by_col, jnp.asarray(dr.reshape(-1)), axis=1)
    full = full.reshape(C_HEADS, 4, NA_Q_ROWS, band_rows, GRID_W, GRID_W)
    full = full.transpose(1, 0, 3, 5, 2, 4)
    return full.reshape(4, C_HEADS, NA_BAND_TILES * ROW_TILE, ROW_TILE)


def _post_kernel(o_ref, x_ref, xc_ref, mod_ref, g_ref, wo_ref, wgu_ref, wd_ref, xo_ref):
    m = mod_ref[0]

    def by_row_halves(fn, *xs):
        half = ROW_TILE // 2
        return jnp.concatenate([fn(*(x[r:r + half] for x in xs)) for r in (0, half)], axis=0)

    y = jnp.dot(o_ref[0], wo_ref[...], preferred_element_type=F32)
    x1 = by_row_halves(lambda xi, yi: xi + m[2:3] * _rms(yi, g_ref[1:2]),
                       _stream_tile(x_ref, xc_ref), y)
    h2 = by_row_halves(
        lambda xi: (_rms(xi, g_ref[2:3]) * (1.0 + m[4:5]) + m[3:4]).astype(BF16), x1)
    gu = jnp.dot(h2, wgu_ref[...], preferred_element_type=F32)
    gate, up = gu[:, :FFN_HIDDEN], gu[:, FFN_HIDDEN:]
    a = (gate * _sigmoid(gate) * up).astype(BF16)
    f = jnp.dot(a, wd_ref[...], preferred_element_type=F32)
    xo_ref[0] = by_row_halves(lambda xi, fi: xi + m[5:6] * _rms(fi, g_ref[3:4]), x1, f)


def _post(o, stream, mod, g_norm, wo, wgu, wd, *, n_tiles):
    return pl.pallas_call(
        _post_kernel,
        out_shape=jax.ShapeDtypeStruct((BATCH, n_tiles * ROW_TILE, D_MODEL), F32),
        grid=(BATCH, n_tiles),
        in_specs=[pl.BlockSpec((1, ROW_TILE, D_MODEL), lambda b, t: (b, t, 0))] + _row_specs(stream) + [
            _resident((4, D_MODEL)), _resident(wo.shape), _resident(wgu.shape),
            _resident(wd.shape)],
        out_specs=pl.BlockSpec((1, ROW_TILE, D_MODEL), lambda b, t: (b, t, 0)),
        compiler_params=_params(2),
        name="post_attention",
    )(o, *stream, mod, g_norm, wo, wgu, wd)


def _rope_angles(rot_dim):
    n_freq = rot_dim // 4
    inv = ROPE_THETA ** (-jnp.arange(n_freq, dtype=F32) / n_freq)
    t = jnp.arange(SEQ, dtype=jnp.int32)
    row = (t // GRID_W).astype(F32)
    col = (t % GRID_W).astype(F32)
    ang = jnp.concatenate([row[:, None] * inv, col[:, None] * inv], axis=-1)
    cos = jnp.concatenate([jnp.cos(ang), jnp.ones((CTX_LEN, rot_dim // 2), F32)], axis=0)
    sin = jnp.concatenate([jnp.sin(ang), jnp.zeros((CTX_LEN, rot_dim // 2), F32)], axis=0)
    return cos, sin


def _rope_tables_diff():
    cos, sin = _rope_angles(A_HEAD_DIM)
    c_lane = jnp.tile(cos, (1, 4))
    s_lane = jnp.tile(jnp.concatenate([-sin, sin], axis=1), (1, 2))
    return c_lane, s_lane, cos.T, sin.T


def _rope_tables_mla():
    cos, sin = _rope_angles(B_ROPE)
    ones = jnp.ones((T_ALL, B_NOPE), F32)
    zeros = jnp.zeros((T_ALL, B_NOPE), F32)
    tail = LANES - B_NOPE - B_ROPE
    c_lane = jnp.concatenate([ones, cos, cos, ones[:, :tail]], axis=1)
    s_lane = jnp.concatenate([zeros, -sin, sin, zeros[:, :tail]], axis=1)
    return c_lane, s_lane, cos.T, sin.T


def kernel(x, c, ctx, c_ctx, l0_w_mod, l0_b_mod, l0_g_norm, l0_w_gu, l0_w_down, l0_a_w_qkv, l0_a_w_o, l0_a_lam, l0_a_g_sub, l1_w_mod, l1_b_mod, l1_g_norm, l1_w_gu, l1_w_down, l1_b_w_in, l1_b_g_q, l1_b_g_kv, l1_b_w_uq, l1_b_w_ukv, l1_b_w_o, l2_w_mod, l2_b_mod, l2_g_norm, l2_w_gu, l2_w_down, l2_c_w_qkv, l2_c_rpb, l2_c_w_o, l3_w_mod, l3_b_mod, l3_g_norm, l3_w_gu, l3_w_down, l3_a_w_qkv, l3_a_w_o, l3_a_lam, l3_a_g_sub):
    common = [
        (l0_w_mod, l0_b_mod, l0_g_norm, l0_w_gu, l0_w_down),
        (l1_w_mod, l1_b_mod, l1_g_norm, l1_w_gu, l1_w_down),
        (l2_w_mod, l2_b_mod, l2_g_norm, l2_w_gu, l2_w_down),
        (l3_w_mod, l3_b_mod, l3_g_norm, l3_w_gu, l3_w_down),
    ]
    diff_params = {0: (l0_a_w_qkv, l0_a_w_o, l0_a_lam, l0_a_g_sub),
                   3: (l3_a_w_qkv, l3_a_w_o, l3_a_lam, l3_a_g_sub)}

    stream = (x, ctx)
    cc = jnp.concatenate([c, c_ctx[None, :], jnp.zeros((8 - BATCH - 1, D_MODEL), F32)], axis=0)
    tabs_a = _rope_tables_diff()
    tabs_b = _rope_tables_mla()
    dummy_lam = jnp.zeros((4, A_HEAD_DIM), F32)
    dummy_gsub = jnp.ones((1, LANES), F32)

    for i in range(DEPTH):
        w_mod, b_mod, g_norm, w_gu, w_down = common[i]
        last = i == DEPTH - 1
        mod = _modulation(cc, w_mod, b_mod).reshape(8, 6, D_MODEL)
        kind = i % 3
        if kind == 0:
            w_qkv, w_o, lam, g_sub = diff_params[i]
            lam_init = 0.8 - 0.6 * math.exp(-0.3 * i)
            wqt = w_qkv[:, :D_MODEL].T.astype(BF16)
            wk = w_qkv[:, D_MODEL:2 * D_MODEL].astype(BF16)
            wvt = w_qkv[:, 2 * D_MODEL:].T.astype(BF16)
            qt, k, vt = _proj_qkv(stream, mod, g_norm, wqt, wk, wvt, tabs_a, rope=True,
                                  qscale=A_HEAD_DIM ** -0.5 * LOG2E, v_rows=2 * A_HEAD_DIM)
            o = _flash(qt, k, vt, lam, g_sub.reshape(1, LANES), diff=True, lam_init=lam_init)
        elif kind == 1:
            w_in, g_q, g_kv, w_uq, w_ukv, w_o = (l1_b_w_in, l1_b_g_q, l1_b_g_kv, l1_b_w_uq,
                                                 l1_b_w_ukv, l1_b_w_o)
            n_lora = B_Q_LORA + B_KV_LORA
            pad = LANES - B_NOPE - B_ROPE
            win = w_in[:, :n_lora].astype(BF16)
            wkr = jnp.pad(w_in[:, n_lora:], ((0, 0), (B_NOPE, pad))).astype(BF16)
            wuq = w_uq.reshape(B_Q_LORA, B_HEADS, B_NOPE + B_ROPE)
            wuq = jnp.pad(wuq, ((0, 0), (0, 0), (0, pad)))
            wuqt = wuq.reshape(B_Q_LORA, B_HEADS * LANES).T.astype(BF16)
            wukv = w_ukv.reshape(B_KV_LORA, B_HEADS, B_NOPE + B_VDIM)
            wukn = jnp.pad(wukv[:, :, :B_NOPE], ((0, 0), (0, 0), (0, LANES - B_NOPE)))
            wukn = wukn.reshape(B_KV_LORA, B_HEADS * LANES).astype(BF16)
            wuvt = wukv[:, :, B_NOPE:].reshape(B_KV_LORA, B_HEADS * B_VDIM).T.astype(BF16)
            qt, k, vt = _proj_mla(stream, mod, g_norm, win, wkr, g_q.reshape(1, -1),
                                  g_kv.reshape(1, -1), wuqt, wukn, wuvt, tabs_b,
                                  qscale=(B_NOPE + B_ROPE) ** -0.5 * LOG2E)
            o = _flash(qt, k, vt, dummy_lam, dummy_gsub, diff=False, lam_init=0.0)
        else:
            w_qkv, rpb, w_o = l2_c_w_qkv, l2_c_rpb, l2_c_w_o
            wqt = w_qkv[:, :D_MODEL].T.astype(BF16)
            wk = w_qkv[:, D_MODEL:2 * D_MODEL].astype(BF16)
            wvt = w_qkv[:, 2 * D_MODEL:].T.astype(BF16)
            qt, k, vt = _proj_qkv(stream, mod, g_norm, wqt, wk, wvt, tabs_a, rope=False,
                                  qscale=C_HEAD_DIM ** -0.5 * LOG2E, v_rows=C_HEAD_DIM)
            o = _na_attention(qt, k, vt, _na_bias_table(rpb))
        xs = _post(o, stream, mod, g_norm, w_o.astype(BF16), w_gu.astype(BF16),
                   w_down.astype(BF16), n_tiles=N_TILES - 1 if last else N_TILES)
        stream = (xs, xs)
    return xs
```
